```python
import jax, jax.numpy as jnp
from jax import lax
import numpy as np

D_MODEL = 1024
BATCH = 8
SEQ = 2048
DEPTH = 1
DEC_BATCH = 2
DEC_SEQ = 16384
PAST_LEN = 128

D_CONV = D_MODEL // 2
CONV_WIDTH = 31
CONV_PAD = CONV_WIDTH // 2
D_FNET = D_MODEL // 2
FNET_GROUPS = 4
FNET_GROUP_DIM = D_FNET // FNET_GROUPS
D_PLE = 256
N_EXPERTS = 32
TOP_K = 4
D_FF = D_MODEL
SWIGLU_ALPHA = 1.702
SWIGLU_LIMIT = 7.0
MOE_BLOCK = 128
LN_EPS = 1e-5
DEEPNORM_ALPHA = (2 * DEPTH) ** 0.25
DEEPNORM_BETA = (8 * DEPTH) ** -0.25
D_IN = 2 * D_CONV + D_FNET + 2 * D_MODEL

kernel_name = 'gated_conv_fnet_moe_encoder'


def layer_norm(x, g, b):
    xf = x.astype(jnp.float32)
    mu = jnp.mean(xf, axis=-1, keepdims=True)
    var = jnp.mean(jnp.square(xf - mu), axis=-1, keepdims=True)
    y = (xf - mu) * lax.rsqrt(var + LN_EPS) * g.astype(jnp.float32) + b.astype(jnp.float32)
    return y.astype(x.dtype)


def post_norm(x, s, g, b):
    z = DEEPNORM_ALPHA * x.astype(jnp.float32) + s.astype(jnp.float32)
    return layer_norm(z, g, b).astype(x.dtype)


def depthwise_conv(u, w, b):
    y = lax.conv_general_dilated(
        u, w[:, None, :].astype(u.dtype), window_strides=(1,), padding=[(CONV_PAD, CONV_PAD)],
        dimension_numbers=('NWC', 'WIO', 'NWC'), feature_group_count=u.shape[-1])
    return y + b


def fourier_mix(f):
    B, S, _ = f.shape
    fg = f.astype(jnp.float32).reshape(B, S, FNET_GROUPS, FNET_GROUP_DIM)
    fr = jnp.fft.fft2(fg, axes=(1, 3), norm='ortho').real
    return fr.reshape(B, S, D_FNET).astype(f.dtype)


def token_mixer(h, w_in, b_in, conv_w, conv_b, lnc_g, lnc_b, w_conv_out, w_fnet_out, w_o, b_o):
    z = h @ w_in + b_in
    c0, c1, c2, c3 = D_CONV, 2 * D_CONV, 2 * D_CONV + D_FNET, 2 * D_CONV + D_FNET + D_MODEL
    glu_v, glu_g, f_in, g_conv, g_fnet = z[..., :c0], z[..., c0:c1], z[..., c1:c2], z[..., c2:c3], z[..., c3:]
    u = glu_v * jax.nn.sigmoid(glu_g)
    u = depthwise_conv(u, conv_w, conv_b)
    u = jax.nn.silu(layer_norm(u, lnc_g, lnc_b))
    y_a = u @ w_conv_out
    y_b = fourier_mix(f_in) @ w_fnet_out
    m = jax.nn.sigmoid(g_conv) * y_a + jax.nn.sigmoid(g_fnet) * y_b
    return m @ w_o + b_o


def moe(h, w_router, b_router, w_gu, b_gu, w_down, b_down):
    T = h.shape[0]
    A = T * TOP_K
    logits = h.astype(jnp.float32) @ w_router.astype(jnp.float32) + b_router.astype(jnp.float32)
    top_vals, top_idx = lax.top_k(logits, TOP_K)
    gates = jax.nn.softmax(top_vals, axis=-1)
    flat_e = top_idx.reshape(-1).astype(jnp.int32)
    flat_tok = jnp.repeat(jnp.arange(T, dtype=jnp.int32), TOP_K)
    flat_w = gates.reshape(-1)
    order = jnp.argsort(flat_e, stable=True)
    sorted_e = flat_e[order]
    sorted_tok = flat_tok[order]
    sorted_w = flat_w[order]
    counts = jnp.bincount(flat_e, length=N_EXPERTS)
    padded = ((counts + MOE_BLOCK - 1) // MOE_BLOCK) * MOE_BLOCK
    ends_p = jnp.cumsum(padded)
    start_p = ends_p - padded
    start = jnp.cumsum(counts) - counts
    rank = jnp.arange(A, dtype=jnp.int32) - start[sorted_e]
    dest = start_p[sorted_e] + rank
    n_blocks = (A + MOE_BLOCK - 1) // MOE_BLOCK + N_EXPERTS
    P = n_blocks * MOE_BLOCK
    buf_tok = jnp.full((P,), T, dtype=jnp.int32).at[dest].set(sorted_tok)
    block_expert = jnp.clip(
        jnp.searchsorted(ends_p, jnp.arange(n_blocks, dtype=jnp.int32) * MOE_BLOCK, side='right'),
        0, N_EXPERTS - 1)
    h_pad = jnp.concatenate([h, jnp.zeros((1, h.shape[1]), h.dtype)], axis=0)
    xb = h_pad[buf_tok].reshape(n_blocks, MOE_BLOCK, h.shape[1])

    def run_block(args):
        xblk, e = args
        gu = xblk @ w_gu[e] + b_gu[e]
        g, u = gu[:, :D_FF], gu[:, D_FF:]
        g = jnp.minimum(g, SWIGLU_LIMIT)
        u = jnp.clip(u, -SWIGLU_LIMIT, SWIGLU_LIMIT)
        act = (u + 1.0) * (g * jax.nn.sigmoid(g * SWIGLU_ALPHA))
        return act @ w_down[e] + b_down[e]

    yb = lax.map(run_block, (xb, block_expert)).reshape(P, h.shape[1])
    y_assign = yb[dest] * sorted_w[:, None].astype(h.dtype)
    return jax.ops.segment_sum(y_assign, sorted_tok, num_segments=T)


def encode(x, p, ln0_g, ln0_b, w_in, b_in, conv_w, conv_b, lnc_g, lnc_b, w_conv_out, w_fnet_out,
           w_o, b_o, ln1_g, ln1_b, w_router, b_router, w_gu, b_gu, w_down, b_down,
           w_pg, b_pg, w_pp, ln2_g, ln2_b):
    B, S, _ = x.shape
    h = layer_norm(x, ln0_g, ln0_b)
    for i in range(DEPTH):
        mix = token_mixer(h, w_in[i], b_in[i], conv_w[i], conv_b[i], lnc_g[i], lnc_b[i],
                          w_conv_out[i], w_fnet_out[i], w_o[i], b_o[i])
        h = post_norm(h, mix, ln1_g[i], ln1_b[i])
        ffn = moe(h.reshape(B * S, D_MODEL), w_router[i], b_router[i], w_gu[i], b_gu[i],
                  w_down[i], b_down[i]).reshape(B, S, D_MODEL)
        ple = jax.nn.sigmoid(h @ w_pg[i] + b_pg[i]) * (p[i] @ w_pp[i])
        h = post_norm(h, ffn + ple, ln2_g[i], ln2_b[i])
    return h


def setup_inputs(seed: int = 0) -> dict:
    key = jax.random.key(seed)
    ks = jax.random.split(key, 32)
    f32 = jnp.float32
    nrm = lambda k, shape, s: jax.random.normal(k, shape, f32) * s
    gain = lambda k, shape: 1.0 + 0.02 * jax.random.normal(k, shape, f32)
    L = DEPTH
    return {
        'x_prompt': jax.random.normal(ks[0], (BATCH, SEQ, D_MODEL), f32),
        'x_sample': jax.random.normal(ks[1], (DEC_BATCH, DEC_SEQ, D_MODEL), f32),
        'p_prompt': jax.random.normal(ks[2], (DEPTH, BATCH, SEQ, D_PLE), f32),
        'p_sample': jax.random.normal(ks[3], (DEPTH, DEC_BATCH, DEC_SEQ, D_PLE), f32),
        'ln0_g': gain(ks[4], (D_MODEL,)),
        'ln0_b': nrm(ks[5], (D_MODEL,), 0.02),
        'w_in': nrm(ks[6], (L, D_MODEL, D_IN), D_MODEL ** -0.5),
        'b_in': nrm(ks[7], (L, D_IN), 0.02),
        'conv_w': nrm(ks[8], (L, CONV_WIDTH, D_CONV), CONV_WIDTH ** -0.5),
        'conv_b': nrm(ks[9], (L, D_CONV), 0.02),
        'lnc_g': gain(ks[10], (L, D_CONV)),
        'lnc_b': nrm(ks[11], (L, D_CONV), 0.02),
        'w_conv_out': nrm(ks[12], (L, D_CONV, D_MODEL), D_CONV ** -0.5),
        'w_fnet_out': nrm(ks[13], (L, D_FNET, D_MODEL), D_FNET ** -0.5),
        'w_o': nrm(ks[14], (L, D_MODEL, D_MODEL), D_MODEL ** -0.5 * DEEPNORM_BETA),
        'b_o': nrm(ks[15], (L, D_MODEL), 0.02),
        'ln1_g': gain(ks[16], (L, D_MODEL)),
        'ln1_b': nrm(ks[17], (L, D_MODEL), 0.02),
        'w_router': nrm(ks[18], (L, D_MODEL, N_EXPERTS), D_MODEL ** -0.5),
        'b_router': nrm(ks[19], (L, N_EXPERTS), 0.01),
        'w_gu': nrm(ks[20], (L, N_EXPERTS, D_MODEL, 2 * D_FF), D_MODEL ** -0.5),
        'b_gu': nrm(ks[21], (L, N_EXPERTS, 2 * D_FF), 0.02),
        'w_down': nrm(ks[22], (L, N_EXPERTS, D_FF, D_MODEL), D_FF ** -0.5 * DEEPNORM_BETA),
        'b_down': nrm(ks[23], (L, N_EXPERTS, D_MODEL), 0.02),
        'w_pg': nrm(ks[24], (L, D_MODEL, D_MODEL), D_MODEL ** -0.5),
        'b_pg': nrm(ks[25], (L, D_MODEL), 0.02),
        'w_pp': nrm(ks[26], (L, D_PLE, D_MODEL), D_PLE ** -0.5 * DEEPNORM_BETA),
        'ln2_g': gain(ks[27], (L, D_MODEL)),
        'ln2_b': nrm(ks[28], (L, D_MODEL), 0.02),
    }


def reference(x_prompt, x_sample, p_prompt, p_sample, ln0_g, ln0_b, w_in, b_in, conv_w, conv_b,
              lnc_g, lnc_b, w_conv_out, w_fnet_out, w_o, b_o, ln1_g, ln1_b, w_router, b_router,
              w_gu, b_gu, w_down, b_down, w_pg, b_pg, w_pp, ln2_g, ln2_b):
    y_prompt = encode(x_prompt, p_prompt, ln0_g, ln0_b, w_in, b_in, conv_w, conv_b, lnc_g, lnc_b,
                      w_conv_out, w_fnet_out, w_o, b_o, ln1_g, ln1_b, w_router, b_router,
                      w_gu, b_gu, w_down, b_down, w_pg, b_pg, w_pp, ln2_g, ln2_b)
    y_sample = encode(x_sample, p_sample, ln0_g, ln0_b, w_in, b_in, conv_w, conv_b, lnc_g, lnc_b,
                      w_conv_out, w_fnet_out, w_o, b_o, ln1_g, ln1_b, w_router, b_router,
                      w_gu, b_gu, w_down, b_down, w_pg, b_pg, w_pp, ln2_g, ln2_b)
    return (y_prompt, y_sample)
```

```python
import functools

import jax
import jax.numpy as jnp
from jax import lax
from jax.experimental import pallas as pl
from jax.experimental.pallas import tpu as pltpu

F32 = jnp.float32
BF16 = jnp.bfloat16
I32 = jnp.int32

D_MODEL = 1024
D_CONV = D_MODEL // 2
CONV_WIDTH = 31
CONV_PAD = CONV_WIDTH // 2
D_FNET = D_MODEL // 2
FNET_GROUPS = 4
FNET_GROUP_DIM = D_FNET // FNET_GROUPS
D_PLE = 256
N_EXPERTS = 32
TOP_K = 4
D_FF = D_MODEL
SWIGLU_ALPHA = 1.702
SWIGLU_LIMIT = 7.0
LN_EPS = 1e-5
DEPTH = 1
DEEPNORM_ALPHA = (2 * DEPTH) ** 0.25

LANES = 128
SUBLANES = 8
DFT_N1 = 128
HALO = 16

TM_INPROJ = 512
TS_CONV = 512
RC_CONV = 32
CB_FFT1 = 4096
TM_MIX = 256
TM_ROWS = 256
BM_MOE = 512
FF_CHUNK = 512
VMEM_LIMIT = 48 * 1024 * 1024


def _cparams(*sem):
    return pltpu.CompilerParams(dimension_semantics=sem, vmem_limit_bytes=VMEM_LIMIT)


def _layer_norm(x, g, b):
    mu = jnp.mean(x, axis=-1, keepdims=True)
    xc = x - mu
    var = jnp.mean(xc * xc, axis=-1, keepdims=True)
    return xc * lax.rsqrt(var + LN_EPS) * g + b


def _full(shape):
    return pl.BlockSpec(shape, lambda *_: (0,) * len(shape))


def _inproj_kernel(x_ref, g_ref, b_ref, w_ref, bias_ref, u_ref, f_ref):
    h = _layer_norm(x_ref[...], g_ref[...], b_ref[...])
    z = jnp.dot(h.astype(BF16), w_ref[...], preferred_element_type=F32) + bias_ref[...]
    u_ref[...] = (z[:, :D_CONV] * jax.nn.sigmoid(z[:, D_CONV:2 * D_CONV])).astype(BF16)
    f_ref[...] = z[:, 2 * D_CONV:].astype(BF16)


def _inproj(x, ln_g, ln_b, w, bias):
    t = x.shape[0]
    n_out = w.shape[1]
    return pl.pallas_call(
        _inproj_kernel,
        grid=(t // TM_INPROJ,),
        in_specs=[
            pl.BlockSpec((TM_INPROJ, D_MODEL), lambda i: (i, 0)),
            _full((1, D_MODEL)), _full((1, D_MODEL)),
            _full((D_MODEL, n_out)), _full((1, n_out)),
        ],
        out_specs=[
            pl.BlockSpec((TM_INPROJ, D_CONV), lambda i: (i, 0)),
            pl.BlockSpec((TM_INPROJ, D_FNET), lambda i: (i, 0)),
        ],
        out_shape=[jax.ShapeDtypeStruct((t, D_CONV), BF16), jax.ShapeDtypeStruct((t, D_FNET), BF16)],
        compiler_params=_cparams("parallel"),
        name="inproj",
    )(x, ln_g, ln_b, w, bias)


def _conv_kernel(prev_ref, cur_ref, next_ref, cw_ref, cb_ref, g_ref, b_ref, wout_ref, ya_ref,
                 ext_ref, act_ref):
    i = pl.program_id(1)
    last = pl.num_programs(1) - 1
    ts = cur_ref.shape[1]
    ext_rows = ts + 2 * HALO
    zero = jnp.zeros((HALO, D_CONV), F32)
    ext_ref[0, 0:HALO, :] = jnp.where(i > 0, prev_ref[0].astype(F32), zero)
    ext_ref[0, HALO:HALO + ts, :] = cur_ref[0].astype(F32)
    ext_ref[0, HALO + ts:ext_rows, :] = jnp.where(i < last, next_ref[0].astype(F32), zero)
    for r in range(1, SUBLANES):
        ext_ref[r, 0:ext_rows - SUBLANES, :] = ext_ref[0, r:r + ext_rows - SUBLANES, :]

    cb = cb_ref[...]
    g = g_ref[...]
    b = b_ref[...]

    def chunk(c, carry):
        r0 = pl.multiple_of(c * RC_CONV, RC_CONV)
        acc = jnp.zeros((RC_CONV, D_CONV), F32) + cb
        for j in range(CONV_WIDTH):
            off = j + HALO - CONV_PAD
            q, r = divmod(off, SUBLANES)
            acc = acc + cw_ref[j:j + 1, :] * ext_ref[r, pl.ds(r0 + q * SUBLANES, RC_CONV), :]
        y = _layer_norm(acc, g, b)
        act_ref[pl.ds(r0, RC_CONV), :] = (y * jax.nn.sigmoid(y)).astype(BF16)
        return carry

    lax.fori_loop(0, ts // RC_CONV, chunk, 0)
    ya_ref[0] = jnp.dot(act_ref[...], wout_ref[...], preferred_element_type=F32).astype(BF16)


def _conv_branch(u, conv_w, conv_b, ln_g, ln_b, w_out):
    bsz, s, _ = u.shape
    ts = TS_CONV
    nb = s // ts
    hb = ts // HALO
    return pl.pallas_call(
        _conv_kernel,
        grid=(bsz, nb),
        in_specs=[
            pl.BlockSpec((1, HALO, D_CONV), lambda b, i: (b, jnp.maximum(i * hb - 1, 0), 0)),
            pl.BlockSpec((1, ts, D_CONV), lambda b, i: (b, i, 0)),
            pl.BlockSpec((1, HALO, D_CONV), lambda b, i: (b, jnp.minimum((i + 1) * hb, s // HALO - 1), 0)),
            _full((CONV_WIDTH, D_CONV)), _full((1, D_CONV)), _full((1, D_CONV)), _full((1, D_CONV)),
            _full((D_CONV, D_MODEL)),
        ],
        out_specs=pl.BlockSpec((1, ts, D_MODEL), lambda b, i: (b, i, 0)),
        out_shape=jax.ShapeDtypeStruct((bsz, s, D_MODEL), BF16),
        scratch_shapes=[
            pltpu.VMEM((SUBLANES, ts + 2 * HALO, D_CONV), F32),
            pltpu.VMEM((ts, D_CONV), BF16),
        ],
        compiler_params=_cparams("parallel", "parallel"),
        name="conv_branch",
    )(u, u, u, conv_w, conv_b, ln_g, ln_b, w_out)


def _dft_tables(s):
    n2_len = s // DFT_N1
    k1_per_tile = DFT_N1 // n2_len
    two_pi = 2.0 * jnp.pi
    a = jnp.arange(DFT_N1, dtype=I32)
    ang1 = two_pi * ((a[:, None] * a[None, :]) % DFT_N1).astype(F32) / DFT_N1
    f1 = jnp.concatenate([jnp.cos(ang1), -jnp.sin(ang1)], axis=0).astype(BF16)
    cs = jnp.stack([jnp.cos(ang1), jnp.sin(ang1)]).astype(BF16)
    n_tiles = s // DFT_N1
    t = jnp.arange(n_tiles, dtype=I32)[:, None, None]
    row = jnp.arange(DFT_N1, dtype=I32)[None, :, None]
    col = jnp.arange(DFT_N1, dtype=I32)[None, None, :]
    k2, k1l_out = row // k1_per_tile, row % k1_per_tile
    k1l_in, n2 = col // n2_len, col % n2_len
    k = t * k1_per_tile + k1l_out + DFT_N1 * k2
    ang2 = two_pi * ((n2 * k) % s).astype(F32) / s
    hit = k1l_in == k1l_out
    gr = jnp.where(hit, jnp.cos(ang2), 0.0)
    gi = jnp.where(hit, -jnp.sin(ang2), 0.0)
    g = jnp.concatenate([jnp.concatenate([gr, -gi], axis=2),
                         jnp.concatenate([gi, gr], axis=2)], axis=1).astype(BF16)
    return f1, g, cs


def _fft1_kernel(x_ref, f_ref, y_ref):
    r = jnp.dot(f_ref[...], x_ref[0], preferred_element_type=F32)
    y_ref[0, 0] = r[:DFT_N1].astype(BF16)
    y_ref[0, 1] = r[DFT_N1:].astype(BF16)


def _fft2_kernel(y_ref, g_ref, cs_ref, w_ref, o_ref, *, scale):
    yb = y_ref[0].reshape(2 * DFT_N1, D_FNET)
    z = jnp.dot(g_ref[0], yb, preferred_element_type=F32)
    zr = z[:DFT_N1].astype(BF16)
    zi = z[DFT_N1:].astype(BF16)
    parts = []
    for grp in range(FNET_GROUPS):
        sl = slice(grp * FNET_GROUP_DIM, (grp + 1) * FNET_GROUP_DIM)
        parts.append(jnp.dot(zr[:, sl], cs_ref[0], preferred_element_type=F32)
                     + jnp.dot(zi[:, sl], cs_ref[1], preferred_element_type=F32))
    fm = (jnp.concatenate(parts, axis=1) * scale).astype(BF16)
    out = jnp.dot(fm, w_ref[...], preferred_element_type=F32)
    o_ref[...] = out.reshape(o_ref.shape).astype(o_ref.dtype)


def _fourier_branch(f_in, w_out):
    bsz, s, _ = f_in.shape
    n2_len = s // DFT_N1
    k1_per_tile = DFT_N1 // n2_len
    cols = n2_len * D_FNET
    cb = min(CB_FFT1, cols)
    f1, g, cs = _dft_tables(s)
    y = pl.pallas_call(
        _fft1_kernel,
        grid=(bsz, cols // cb),
        in_specs=[pl.BlockSpec((1, DFT_N1, cb), lambda b, j: (b, 0, j)), _full((2 * DFT_N1, DFT_N1))],
        out_specs=pl.BlockSpec((1, 2, DFT_N1, cb), lambda b, j: (b, 0, 0, j)),
        out_shape=jax.ShapeDtypeStruct((bsz, 2, DFT_N1, cols), BF16),
        compiler_params=_cparams("parallel", "parallel"),
        name="fft_stage1",
    )(f_in.reshape(bsz, DFT_N1, cols), f1)
    y = y.reshape(bsz, 2, s, D_FNET)
    scale = float((s * FNET_GROUP_DIM) ** -0.5)
    if k1_per_tile == 1:
        out_shape = (bsz, n2_len, DFT_N1 * D_MODEL)
        out_spec = pl.BlockSpec((1, n2_len, D_MODEL), lambda b, t: (b, 0, t))
    else:
        out_shape = (bsz, n2_len, DFT_N1 // k1_per_tile, k1_per_tile, D_MODEL)
        out_spec = pl.BlockSpec((1, n2_len, 1, k1_per_tile, D_MODEL), lambda b, t: (b, 0, t, 0, 0))
    yb = pl.pallas_call(
        functools.partial(_fft2_kernel, scale=scale),
        grid=(bsz, s // DFT_N1),
        in_specs=[
            pl.BlockSpec((1, 2, DFT_N1, D_FNET), lambda b, t: (b, 0, t, 0)),
            pl.BlockSpec((1, 2 * DFT_N1, 2 * DFT_N1), lambda b, t: (t, 0, 0)),
            _full((2, DFT_N1, DFT_N1)),
            _full((D_FNET, D_MODEL)),
        ],
        out_specs=out_spec,
        out_shape=jax.ShapeDtypeStruct(out_shape, BF16),
        compiler_params=_cparams("parallel", "parallel"),
        name="fft_stage2",
    )(y, g, cs, w_out)
    return yb.reshape(bsz * s, D_MODEL)


def _mix_kernel(x_ref, ya_ref, yb_ref, cin_ref, ln0g_ref, ln0b_ref, wg_ref, bg_ref, wo_ref, bo_ref,
                ln1g_ref, ln1b_ref, wr_ref, br_ref,
                h1_ref, idx_ref, gate_ref, rank_ref, cnt_ref, carry_ref):
    tm = x_ref.shape[0]

    @pl.when(pl.program_id(0) == 0)
    def _():
        carry_ref[...] = cin_ref[...]

    h = _layer_norm(x_ref[...], ln0g_ref[...], ln0b_ref[...])
    zg = jnp.dot(h.astype(BF16), wg_ref[...], preferred_element_type=F32) + bg_ref[...]
    m = (jax.nn.sigmoid(zg[:, :D_MODEL]) * ya_ref[...].astype(F32)
         + jax.nn.sigmoid(zg[:, D_MODEL:]) * yb_ref[...].astype(F32))
    mix = jnp.dot(m.astype(BF16), wo_ref[...], preferred_element_type=F32) + bo_ref[...]
    h1 = _layer_norm(DEEPNORM_ALPHA * h + mix, ln1g_ref[...], ln1b_ref[...])
    h1_ref[...] = h1

    logits = lax.dot_general(wr_ref[...], h1, (((1,), (1,)), ((), ())),
                             precision=lax.Precision.HIGHEST, preferred_element_type=F32) + br_ref[...]
    eio = lax.broadcasted_iota(I32, (N_EXPERTS, tm), 0)
    vals, idxs = [], []
    cur = logits
    for _ in range(TOP_K):
        mx = jnp.max(cur, axis=0, keepdims=True)
        ik = jnp.min(jnp.where(cur == mx, eio, N_EXPERTS), axis=0, keepdims=True)
        vals.append(mx)
        idxs.append(ik)
        cur = jnp.where(eio == ik, -jnp.inf, cur)
    exps = [jnp.exp(v - vals[0]) for v in vals]
    den = exps[0] + exps[1] + exps[2] + exps[3]
    gate_ref[...] = jnp.concatenate([e / den for e in exps], axis=0)
    idx_ref[...] = jnp.concatenate(idxs, axis=0)

    hot = jnp.zeros((N_EXPERTS, tm), F32)
    for ik in idxs:
        hot = hot + (eio == ik).astype(F32)
    before = (lax.broadcasted_iota(I32, (tm, tm), 0) < lax.broadcasted_iota(I32, (tm, tm), 1)).astype(BF16)
    prior = jnp.dot(hot.astype(BF16), before, preferred_element_type=F32) + carry_ref[:, 0:1]
    ranks = [jnp.sum(jnp.where(eio == ik, prior, 0.0), axis=0, keepdims=True) for ik in idxs]
    rank_ref[...] = jnp.concatenate(ranks, axis=0).astype(I32)
    carry_ref[...] = carry_ref[...] + jnp.sum(hot, axis=1, keepdims=True)
    cnt_ref[...] = carry_ref[...]


def _mix_route(x, ya, yb, carry_in, ln0_g, ln0_b, w_gate, b_gate, w_o, b_o, ln1_g, ln1_b, w_rt, b_rt):
    t = x.shape[0]
    tm = TM_MIX
    row = lambda d: pl.BlockSpec((tm, d), lambda i: (i, 0))
    col = lambda r: pl.BlockSpec((r, tm), lambda i: (0, i))
    return pl.pallas_call(
        _mix_kernel,
        grid=(t // tm,),
        in_specs=[
            row(D_MODEL), row(D_MODEL), row(D_MODEL), _full((N_EXPERTS, LANES)),
            _full((1, D_MODEL)), _full((1, D_MODEL)),
            _full((D_MODEL, 2 * D_MODEL)), _full((1, 2 * D_MODEL)),
            _full((D_MODEL, D_MODEL)), _full((1, D_MODEL)),
            _full((1, D_MODEL)), _full((1, D_MODEL)),
            _full((N_EXPERTS, D_MODEL)), _full((N_EXPERTS, 1)),
        ],
        out_specs=[row(D_MODEL), col(TOP_K), col(TOP_K), col(TOP_K), _full((N_EXPERTS, LANES))],
        out_shape=[
            jax.ShapeDtypeStruct((t, D_MODEL), F32),
            jax.ShapeDtypeStruct((TOP_K, t), I32),
            jax.ShapeDtypeStruct((TOP_K, t), F32),
            jax.ShapeDtypeStruct((TOP_K, t), I32),
            jax.ShapeDtypeStruct((N_EXPERTS, LANES), F32),
        ],
        scratch_shapes=[pltpu.VMEM((N_EXPERTS, LANES), F32)],
        compiler_params=_cparams("arbitrary"),
        name="mix_route",
    )(x, ya, yb, carry_in, ln0_g, ln0_b, w_gate, b_gate, w_o, b_o, ln1_g, ln1_b, w_rt, b_rt)


def _row_copy(src_ref, src_row, dst_ref, dst_row, sem):
    return pltpu.make_async_copy(src_ref.at[pl.ds(src_row, 1)], dst_ref.at[pl.ds(dst_row, 1)], sem)


def _fetch_slots(slots_hbm, slots_smem, sem, step, slot):
    return pltpu.make_async_copy(slots_hbm.at[step], slots_smem.at[slot], sem.at[slot])


def _pad_fill(h_ref, xs_out, pad_smem, pad_sem, n_pad_runs, wait):
    def go(cp):
        if wait:
            cp.wait()
        else:
            cp.start()

    def run(j, carry):
        start = pad_smem[2 * j]
        length = pad_smem[2 * j + 1]
        head = jnp.minimum((-start) & (SUBLANES - 1), length)
        for r in range(SUBLANES - 1):
            @pl.when(r < head)
            def _():
                go(_row_copy(h_ref, 0, xs_out, start + r, pad_sem))

        body = start + head
        rest = length - head
        for bit in range(SUBLANES.bit_length() - 1, BM_MOE.bit_length() - 1):
            size = 1 << bit

            @pl.when((rest & size) != 0)
            def _():
                off = pl.multiple_of(body + (rest & ~(2 * size - 1)), SUBLANES)
                go(pltpu.make_async_copy(h_ref.at[pl.ds(0, size)], xs_out.at[pl.ds(off, size)], pad_sem))
        return carry

    lax.fori_loop(0, n_pad_runs, run, 0)

    tm = h_ref.shape[0]

    def dead_block(j, carry):
        for part in range(BM_MOE // tm):
            off = pl.multiple_of(j * BM_MOE + part * tm, tm)
            go(pltpu.make_async_copy(h_ref, xs_out.at[pl.ds(off, tm)], pad_sem))
        return carry

    lax.fori_loop(pad_smem[2 * n_pad_runs], xs_out.shape[0] // BM_MOE, dead_block, 0)


def _dispatch_kernel(h_ref, slots_hbm, pad_hbm, *rest, n_pad_runs, aliased):
    xs_out, slots_smem, pad_smem, slot_sem, pad_sem, row_sem = rest[1:] if aliased else rest
    i = pl.program_id(0)
    n = pl.num_programs(0)
    tm = h_ref.shape[0]
    cur = lax.rem(i, 2)

    @pl.when(i == 0)
    def _():
        _fetch_slots(slots_hbm, slots_smem, slot_sem, 0, 0).start()

    @pl.when(i + 1 < n)
    def _():
        _fetch_slots(slots_hbm, slots_smem, slot_sem, i + 1, 1 - cur).start()

    if n_pad_runs:
        @pl.when(i == 0)
        def _():
            fetch = pltpu.make_async_copy(pad_hbm, pad_smem, pad_sem)
            fetch.start()
            fetch.wait()
            _pad_fill(h_ref, xs_out, pad_smem, pad_sem, n_pad_runs, wait=False)
            _pad_fill(h_ref, xs_out, pad_smem, pad_sem, n_pad_runs, wait=True)

    _fetch_slots(slots_hbm, slots_smem, slot_sem, i, cur).wait()

    def issue(r, carry):
        for k in range(TOP_K):
            _row_copy(h_ref, r, xs_out, slots_smem[cur, k * tm + r], row_sem).start()
        return carry

    lax.fori_loop(0, tm, issue, 0)

    def drain(r, carry):
        for k in range(TOP_K):
            _row_copy(h_ref, 0, xs_out, 0, row_sem).wait()
        return carry

    lax.fori_loop(0, tm, drain, 0)


def _dispatch(h1, slots, pad_runs, xs, n_rows):
    t = h1.shape[0]
    tm = TM_ROWS
    aliased = xs is not None
    n_pad_runs = 0 if aliased else N_EXPERTS
    operands = (h1, slots, pad_runs) + ((xs,) if aliased else ())
    return pl.pallas_call(
        functools.partial(_dispatch_kernel, n_pad_runs=n_pad_runs, aliased=aliased),
        grid=(t // tm,),
        in_specs=[pl.BlockSpec((tm, D_MODEL), lambda i: (i, 0))]
        + [pl.BlockSpec(memory_space=pl.ANY)] * (len(operands) - 1),
        out_specs=pl.BlockSpec(memory_space=pl.ANY),
        out_shape=jax.ShapeDtypeStruct((n_rows, D_MODEL), F32),
        input_output_aliases={3: 0} if aliased else {},
        scratch_shapes=[
            pltpu.SMEM((2, TOP_K * tm), I32),
            pltpu.SMEM((pad_runs.shape[0],), I32),
            pltpu.SemaphoreType.DMA((2,)),
            pltpu.SemaphoreType.DMA,
            pltpu.SemaphoreType.DMA,
        ],
        compiler_params=_cparams("arbitrary"),
        name="dispatch",
    )(*operands)


def _moe_kernel(blk_e_ref, nb_ref, x_ref, wgu_ref, bgu_ref, wd_ref, bd_ref, o_ref):
    del blk_e_ref
    is_live = pl.program_id(0) < nb_ref[0]

    @pl.when(jnp.logical_not(is_live))
    def _():
        o_ref[...] = jnp.zeros(o_ref.shape, F32)

    @pl.when(is_live)
    def _():
        x = x_ref[...].astype(BF16)
        acc = jnp.zeros(o_ref.shape, F32) + bd_ref[0]
        for c in range(D_FF // FF_CHUNK):
            lo = c * FF_CHUNK
            g = jnp.dot(x, wgu_ref[0, :, lo:lo + FF_CHUNK], preferred_element_type=F32) + bgu_ref[0, :, lo:lo + FF_CHUNK]
            u = (jnp.dot(x, wgu_ref[0, :, D_FF + lo:D_FF + lo + FF_CHUNK], preferred_element_type=F32)
                 + bgu_ref[0, :, D_FF + lo:D_FF + lo + FF_CHUNK])
            g = jnp.minimum(g, SWIGLU_LIMIT)
            u = jnp.clip(u, -SWIGLU_LIMIT, SWIGLU_LIMIT)
            act = (u + 1.0) * (g * jax.nn.sigmoid(g * SWIGLU_ALPHA))
            acc = acc + jnp.dot(act.astype(BF16), wd_ref[0, lo:lo + FF_CHUNK, :], preferred_element_type=F32)
        o_ref[...] = acc


def _moe_blocks(xs, blk_e, nb_used, w_gu, b_gu, w_down, b_down):
    p = xs.shape[0]
    n_blocks = p // BM_MOE
    live = lambda i, nb: jnp.minimum(i, nb[0] - 1)
    grid_spec = pltpu.PrefetchScalarGridSpec(
        num_scalar_prefetch=2,
        grid=(n_blocks,),
        in_specs=[
            pl.BlockSpec((BM_MOE, D_MODEL), lambda i, be, nb: (live(i, nb), 0)),
            pl.BlockSpec((1, D_MODEL, 2 * D_FF), lambda i, be, nb: (be[i], 0, 0)),
            pl.BlockSpec((1, 1, 2 * D_FF), lambda i, be, nb: (be[i], 0, 0)),
            pl.BlockSpec((1, D_FF, D_MODEL), lambda i, be, nb: (be[i], 0, 0)),
            pl.BlockSpec((1, 1, D_MODEL), lambda i, be, nb: (be[i], 0, 0)),
        ],
        out_specs=pl.BlockSpec((BM_MOE, D_MODEL), lambda i, be, nb: (i, 0)),
    )
    return pl.pallas_call(
        _moe_kernel,
        grid_spec=grid_spec,
        out_shape=jax.ShapeDtypeStruct((p, D_MODEL), F32),
        compiler_params=_cparams("arbitrary"),
        name="moe_experts",
    )(blk_e, nb_used, xs, w_gu, b_gu, w_down, b_down)


def _final_kernel(h_ref, p_ref, gate_ref, slots_hbm, y_hbm, wpg_ref, bpg_ref, wpp_ref, g_ref, b_ref, o_ref,
                  slots_smem, rows_ref, slot_sem, row_sem):
    i = pl.program_id(0)
    n = pl.num_programs(0)
    tm = h_ref.shape[0]
    cur = lax.rem(i, 2)

    @pl.when(i == 0)
    def _():
        _fetch_slots(slots_hbm, slots_smem, slot_sem, 0, 0).start()

    @pl.when(i + 1 < n)
    def _():
        _fetch_slots(slots_hbm, slots_smem, slot_sem, i + 1, 1 - cur).start()

    _fetch_slots(slots_hbm, slots_smem, slot_sem, i, cur).wait()

    def issue(r, carry):
        for k in range(TOP_K):
            _row_copy(y_hbm, slots_smem[cur, k * tm + r], rows_ref.at[k], r, row_sem).start()
        return carry

    lax.fori_loop(0, tm, issue, 0)

    h = h_ref[...]
    ple = (jax.nn.sigmoid(jnp.dot(h.astype(BF16), wpg_ref[...], preferred_element_type=F32) + bpg_ref[...])
           * jnp.dot(p_ref[...].astype(BF16), wpp_ref[...], preferred_element_type=F32))

    def drain(r, carry):
        for k in range(TOP_K):
            _row_copy(y_hbm, 0, rows_ref.at[k], 0, row_sem).wait()
        return carry

    lax.fori_loop(0, tm, drain, 0)

    gates = gate_ref[...]
    ffn = gates[:, 0:1] * rows_ref[0]
    for k in range(1, TOP_K):
        ffn = ffn + gates[:, k:k + 1] * rows_ref[k]
    o_ref[...] = _layer_norm(DEEPNORM_ALPHA * h + (ffn + ple), g_ref[...], b_ref[...])


def _combine_final(h1, p, gates, slots, y, w_pg, b_pg, w_pp, ln_g, ln_b):
    t = h1.shape[0]
    tm = TM_ROWS
    return pl.pallas_call(
        _final_kernel,
        grid=(t // tm,),
        in_specs=[
            pl.BlockSpec((tm, D_MODEL), lambda i: (i, 0)),
            pl.BlockSpec((tm, D_PLE), lambda i: (i, 0)),
            pl.BlockSpec((tm, TOP_K), lambda i: (i, 0)),
            pl.BlockSpec(memory_space=pl.ANY),
            pl.BlockSpec(memory_space=pl.ANY),
            _full((D_MODEL, D_MODEL)), _full((1, D_MODEL)), _full((D_PLE, D_MODEL)),
            _full((1, D_MODEL)), _full((1, D_MODEL)),
        ],
        out_specs=pl.BlockSpec((tm, D_MODEL), lambda i: (i, 0)),
        out_shape=jax.ShapeDtypeStruct((t, D_MODEL), F32),
        scratch_shapes=[
            pltpu.SMEM((2, TOP_K * tm), I32),
            pltpu.VMEM((TOP_K, tm, D_MODEL), F32),
            pltpu.SemaphoreType.DMA((2,)),
            pltpu.SemaphoreType.DMA,
        ],
        compiler_params=_cparams("arbitrary"),
        name="combine_final",
    )(h1, p, gates, slots, y, w_pg, b_pg, w_pp, ln_g, ln_b)


def _tile_slots(dest):
    t = dest.shape[1]
    return dest.reshape(TOP_K, t // TM_ROWS, TM_ROWS).transpose(1, 0, 2).reshape(t // TM_ROWS, TOP_K * TM_ROWS)


def kernel(x_prompt, x_sample, p_prompt, p_sample, ln0_g, ln0_b, w_in, b_in, conv_w, conv_b, lnc_g, lnc_b, w_conv_out, w_fnet_out, w_o, b_o, ln1_g, ln1_b, w_router, b_router, w_gu, b_gu, w_down, b_down, w_pg, b_pg, w_pp, ln2_g, ln2_b):
    row = lambda v: v.reshape(1, -1).astype(F32)
    n_branch = 2 * D_CONV + D_FNET
    w_in_b = w_in[0].astype(BF16)
    w_branch, w_gate = w_in_b[:, :n_branch], w_in_b[:, n_branch:]
    b_branch, b_gate = row(b_in[0, :n_branch]), row(b_in[0, n_branch:])
    w_conv_out_b = w_conv_out[0].astype(BF16)
    w_fnet_out_b = w_fnet_out[0].astype(BF16)
    w_o_b = w_o[0].astype(BF16)
    w_rt = w_router[0].T.astype(F32)
    b_rt = b_router[0].reshape(N_EXPERTS, 1).astype(F32)
    w_gu_b = w_gu[0].astype(BF16)
    w_down_b = w_down[0].astype(BF16)
    b_gu_r = b_gu[0].reshape(N_EXPERTS, 1, 2 * D_FF).astype(F32)
    b_down_r = b_down[0].reshape(N_EXPERTS, 1, D_MODEL).astype(F32)
    w_pg_b = w_pg[0].astype(BF16)
    w_pp_b = w_pp[0].astype(BF16)

    carry = jnp.zeros((N_EXPERTS, LANES), F32)
    routed = []
    for x, p in ((x_prompt, p_prompt[0]), (x_sample, p_sample[0])):
        bsz, s, _ = x.shape
        xt = x.reshape(bsz * s, D_MODEL)
        u, f_in = _inproj(xt, row(ln0_g), row(ln0_b), w_branch, b_branch)
        ya = _conv_branch(u.reshape(bsz, s, D_CONV), conv_w[0].astype(F32), row(conv_b[0]), row(lnc_g[0]),
                          row(lnc_b[0]), w_conv_out_b).reshape(bsz * s, D_MODEL)
        yb = _fourier_branch(f_in.reshape(bsz, s, D_FNET), w_fnet_out_b)
        h1, idx, gates, rank, carry = _mix_route(
            xt, ya, yb, carry, row(ln0_g), row(ln0_b), w_gate, b_gate, w_o_b, row(b_o[0]),
            row(ln1_g[0]), row(ln1_b[0]), w_rt, b_rt)
        routed.append((h1, idx, gates, rank, p.reshape(bsz * s, D_PLE), (bsz, s)))

    n_assign = sum(r[0].shape[0] for r in routed) * TOP_K
    n_blocks = (n_assign + N_EXPERTS * (BM_MOE - 1)) // BM_MOE
    counts = carry[:, 0].astype(I32)
    padded = ((counts + BM_MOE - 1) // BM_MOE) * BM_MOE
    ends = jnp.cumsum(padded)
    start = ends - padded
    nb_used = (ends[-1] // BM_MOE).reshape(1).astype(I32)
    blk = jnp.minimum(jnp.arange(n_blocks, dtype=I32), nb_used[0] - 1)
    blk_e = jnp.clip(jnp.searchsorted(ends, blk * BM_MOE, side='right'), 0, N_EXPERTS - 1).astype(I32)
    pad_runs = jnp.concatenate([jnp.stack([start + counts, padded - counts], axis=1).reshape(-1), nb_used,
                                jnp.zeros((LANES - 2 * N_EXPERTS - 1,), I32)]).astype(I32)

    xs = None
    slot_tables = []
    for h1, idx, _, rank, _, _ in routed:
        slots = _tile_slots(start[idx] + rank)
        slot_tables.append(slots)
        xs = _dispatch(h1, slots, pad_runs, xs, n_blocks * BM_MOE)

    y = _moe_blocks(xs, blk_e, nb_used, w_gu_b, b_gu_r, w_down_b, b_down_r)

    outs = []
    for (h1, _, gates, _, p, (bsz, s)), slots in zip(routed, slot_tables):
        o = _combine_final(h1, p, gates.T, slots, y, w_pg_b, row(b_pg[0]), w_pp_b, row(ln2_g[0]), row(ln2_b[0]))
        outs.append(o.reshape(bsz, s, D_MODEL))
    return tuple(outs)
```

```python
import functools

import jax
import jax.numpy as jnp
from jax import lax
from jax.experimental import pallas as pl
from jax.experimental.pallas import tpu as pltpu

F32 = jnp.float32
BF16 = jnp.bfloat16
I32 = jnp.int32

D_MODEL = 1024
D_CONV = D_MODEL // 2
CONV_WIDTH = 31
CONV_PAD = CONV_WIDTH // 2
D_FNET = D_MODEL // 2
FNET_GROUPS = 4
FNET_GROUP_DIM = D_FNET // FNET_GROUPS
D_PLE = 256
N_EXPERTS = 32
TOP_K = 4
D_FF = D_MODEL
SWIGLU_ALPHA = 1.702
SWIGLU_LIMIT = 7.0
LN_EPS = 1e-5
DEPTH = 1
DEEPNORM_ALPHA = (2 * DEPTH) ** 0.25

LANES = 128
SUBLANES = 8
DFT_N1 = 128
HALO = 16

TM_INPROJ = 512
TS_CONV = 512
RC_CONV = 32
CB_FFT1 = 4096
TM_MIX = 256
TM_ROWS = 256
BM_MOE = 512
FF_CHUNK = 512
ISSUE_UNROLL = 8
CAST_ROWS = 128
VMEM_LIMIT = 48 * 1024 * 1024
VMEM_LIMIT_MOE = 56 * 1024 * 1024


def _cparams(*sem):
    return pltpu.CompilerParams(dimension_semantics=sem, vmem_limit_bytes=VMEM_LIMIT)


def _layer_norm(x, g, b):
    mu = jnp.mean(x, axis=-1, keepdims=True)
    xc = x - mu
    var = jnp.mean(xc * xc, axis=-1, keepdims=True)
    return xc * lax.rsqrt(var + LN_EPS) * g + b


def _full(shape):
    return pl.BlockSpec(shape, lambda *_: (0,) * len(shape))


def _inproj_kernel(x_ref, g_ref, b_ref, w_ref, bias_ref, u_ref, f_ref):
    h = _layer_norm(x_ref[...], g_ref[...], b_ref[...])
    z = jnp.dot(h.astype(BF16), w_ref[...], preferred_element_type=F32) + bias_ref[...]
    u_ref[...] = (z[:, :D_CONV] * jax.nn.sigmoid(z[:, D_CONV:2 * D_CONV])).astype(BF16)
    f_ref[...] = z[:, 2 * D_CONV:].astype(BF16)


def _inproj(x, ln_g, ln_b, w, bias):
    t = x.shape[0]
    n_out = w.shape[1]
    return pl.pallas_call(
        _inproj_kernel,
        grid=(t // TM_INPROJ,),
        in_specs=[
            pl.BlockSpec((TM_INPROJ, D_MODEL), lambda i: (i, 0)),
            _full((1, D_MODEL)), _full((1, D_MODEL)),
            _full((D_MODEL, n_out)), _full((1, n_out)),
        ],
        out_specs=[
            pl.BlockSpec((TM_INPROJ, D_CONV), lambda i: (i, 0)),
            pl.BlockSpec((TM_INPROJ, D_FNET), lambda i: (i, 0)),
        ],
        out_shape=[jax.ShapeDtypeStruct((t, D_CONV), BF16), jax.ShapeDtypeStruct((t, D_FNET), BF16)],
        compiler_params=_cparams("parallel"),
        name="inproj",
    )(x, ln_g, ln_b, w, bias)


def _conv_kernel(prev_ref, cur_ref, next_ref, cw_ref, cb_ref, g_ref, b_ref, wout_ref, ya_ref,
                 ext_ref, act_ref):
    i = pl.program_id(1)
    last = pl.num_programs(1) - 1
    ts = cur_ref.shape[1]
    ext_rows = ts + 2 * HALO
    zero = jnp.zeros((HALO, D_CONV), F32)
    ext_ref[0, 0:HALO, :] = jnp.where(i > 0, prev_ref[0].astype(F32), zero)
    ext_ref[0, HALO:HALO + ts, :] = cur_ref[0].astype(F32)
    ext_ref[0, HALO + ts:ext_rows, :] = jnp.where(i < last, next_ref[0].astype(F32), zero)
    for r in range(1, SUBLANES):
        ext_ref[r, 0:ext_rows - SUBLANES, :] = ext_ref[0, r:r + ext_rows - SUBLANES, :]

    cb = cb_ref[...]
    g = g_ref[...]
    b = b_ref[...]

    def chunk(c, carry):
        r0 = pl.multiple_of(c * RC_CONV, RC_CONV)
        acc = jnp.zeros((RC_CONV, D_CONV), F32) + cb
        for j in range(CONV_WIDTH):
            off = j + HALO - CONV_PAD
            q, r = divmod(off, SUBLANES)
            acc = acc + cw_ref[j:j + 1, :] * ext_ref[r, pl.ds(r0 + q * SUBLANES, RC_CONV), :]
        y = _layer_norm(acc, g, b)
        act_ref[pl.ds(r0, RC_CONV), :] = (y * jax.nn.sigmoid(y)).astype(BF16)
        return carry

    lax.fori_loop(0, ts // RC_CONV, chunk, 0)
    ya_ref[0] = jnp.dot(act_ref[...], wout_ref[...], preferred_element_type=F32).astype(BF16)


def _conv_branch(u, conv_w, conv_b, ln_g, ln_b, w_out):
    bsz, s, _ = u.shape
    ts = TS_CONV
    nb = s // ts
    hb = ts // HALO
    return pl.pallas_call(
        _conv_kernel,
        grid=(bsz, nb),
        in_specs=[
            pl.BlockSpec((1, HALO, D_CONV), lambda b, i: (b, jnp.maximum(i * hb - 1, 0), 0)),
            pl.BlockSpec((1, ts, D_CONV), lambda b, i: (b, i, 0)),
            pl.BlockSpec((1, HALO, D_CONV), lambda b, i: (b, jnp.minimum((i + 1) * hb, s // HALO - 1), 0)),
            _full((CONV_WIDTH, D_CONV)), _full((1, D_CONV)), _full((1, D_CONV)), _full((1, D_CONV)),
            _full((D_CONV, D_MODEL)),
        ],
        out_specs=pl.BlockSpec((1, ts, D_MODEL), lambda b, i: (b, i, 0)),
        out_shape=jax.ShapeDtypeStruct((bsz, s, D_MODEL), BF16),
        scratch_shapes=[
            pltpu.VMEM((SUBLANES, ts + 2 * HALO, D_CONV), F32),
            pltpu.VMEM((ts, D_CONV), BF16),
        ],
        compiler_params=_cparams("parallel", "parallel"),
        name="conv_branch",
    )(u, u, u, conv_w, conv_b, ln_g, ln_b, w_out)


def _dft_tables(s):
    n2_len = s // DFT_N1
    k1_per_tile = DFT_N1 // n2_len
    two_pi = 2.0 * jnp.pi
    a = jnp.arange(DFT_N1, dtype=I32)
    ang1 = two_pi * ((a[:, None] * a[None, :]) % DFT_N1).astype(F32) / DFT_N1
    f1 = jnp.concatenate([jnp.cos(ang1), -jnp.sin(ang1)], axis=0).astype(BF16)
    cs = jnp.stack([jnp.cos(ang1), jnp.sin(ang1)]).astype(BF16)
    n_tiles = s // DFT_N1
    t = jnp.arange(n_tiles, dtype=I32)[:, None, None]
    row = jnp.arange(DFT_N1, dtype=I32)[None, :, None]
    col = jnp.arange(DFT_N1, dtype=I32)[None, None, :]
    k2, k1l_out = row // k1_per_tile, row % k1_per_tile
    k1l_in, n2 = col // n2_len, col % n2_len
    k = t * k1_per_tile + k1l_out + DFT_N1 * k2
    ang2 = two_pi * ((n2 * k) % s).astype(F32) / s
    hit = k1l_in == k1l_out
    gr = jnp.where(hit, jnp.cos(ang2), 0.0)
    gi = jnp.where(hit, -jnp.sin(ang2), 0.0)
    g = jnp.concatenate([jnp.concatenate([gr, -gi], axis=2),
                         jnp.concatenate([gi, gr], axis=2)], axis=1).astype(BF16)
    return f1, g, cs


def _fft1_kernel(x_ref, f_ref, y_ref):
    r = jnp.dot(f_ref[...], x_ref[0], preferred_element_type=F32)
    y_ref[0, 0] = r[:DFT_N1].astype(BF16)
    y_ref[0, 1] = r[DFT_N1:].astype(BF16)


def _fft2_kernel(y_ref, g_ref, cs_ref, w_ref, o_ref, *, scale):
    yb = y_ref[0].reshape(2 * DFT_N1, D_FNET)
    z = jnp.dot(g_ref[0], yb, preferred_element_type=F32)
    zr = z[:DFT_N1].astype(BF16)
    zi = z[DFT_N1:].astype(BF16)
    parts = []
    for grp in range(FNET_GROUPS):
        sl = slice(grp * FNET_GROUP_DIM, (grp + 1) * FNET_GROUP_DIM)
        parts.append(jnp.dot(zr[:, sl], cs_ref[0], preferred_element_type=F32)
                     + jnp.dot(zi[:, sl], cs_ref[1], preferred_element_type=F32))
    fm = (jnp.concatenate(parts, axis=1) * scale).astype(BF16)
    out = jnp.dot(fm, w_ref[...], preferred_element_type=F32)
    o_ref[...] = out.reshape(o_ref.shape).astype(o_ref.dtype)


def _fourier_branch(f_in, w_out):
    bsz, s, _ = f_in.shape
    n2_len = s // DFT_N1
    k1_per_tile = DFT_N1 // n2_len
    cols = n2_len * D_FNET
    cb = min(CB_FFT1, cols)
    f1, g, cs = _dft_tables(s)
    y = pl.pallas_call(
        _fft1_kernel,
        grid=(bsz, cols // cb),
        in_specs=[pl.BlockSpec((1, DFT_N1, cb), lambda b, j: (b, 0, j)), _full((2 * DFT_N1, DFT_N1))],
        out_specs=pl.BlockSpec((1, 2, DFT_N1, cb), lambda b, j: (b, 0, 0, j)),
        out_shape=jax.ShapeDtypeStruct((bsz, 2, DFT_N1, cols), BF16),
        compiler_params=_cparams("parallel", "parallel"),
        name="fft_stage1",
    )(f_in.reshape(bsz, DFT_N1, cols), f1)
    y = y.reshape(bsz, 2, s, D_FNET)
    scale = float((s * FNET_GROUP_DIM) ** -0.5)
    if k1_per_tile == 1:
        out_shape = (bsz, n2_len, DFT_N1 * D_MODEL)
        out_spec = pl.BlockSpec((1, n2_len, D_MODEL), lambda b, t: (b, 0, t))
    else:
        out_shape = (bsz, n2_len, DFT_N1 // k1_per_tile, k1_per_tile, D_MODEL)
        out_spec = pl.BlockSpec((1, n2_len, 1, k1_per_tile, D_MODEL), lambda b, t: (b, 0, t, 0, 0))
    yb = pl.pallas_call(
        functools.partial(_fft2_kernel, scale=scale),
        grid=(bsz, s // DFT_N1),
        in_specs=[
            pl.BlockSpec((1, 2, DFT_N1, D_FNET), lambda b, t: (b, 0, t, 0)),
            pl.BlockSpec((1, 2 * DFT_N1, 2 * DFT_N1), lambda b, t: (t, 0, 0)),
            _full((2, DFT_N1, DFT_N1)),
            _full((D_FNET, D_MODEL)),
        ],
        out_specs=out_spec,
        out_shape=jax.ShapeDtypeStruct(out_shape, BF16),
        compiler_params=_cparams("parallel", "parallel"),
        name="fft_stage2",
    )(y, g, cs, w_out)
    return yb.reshape(bsz * s, D_MODEL)


def _mix_kernel(x_ref, ya_ref, yb_ref, cin_ref, ln0g_ref, ln0b_ref, wg_ref, bg_ref, wo_ref, bo_ref,
                ln1g_ref, ln1b_ref, wr_ref, br_ref,
                h1_ref, idx_ref, gate_ref, rank_ref, cnt_ref, carry_ref):
    tm = x_ref.shape[0]

    @pl.when(pl.program_id(0) == 0)
    def _():
        carry_ref[...] = cin_ref[...]

    h = _layer_norm(x_ref[...], ln0g_ref[...], ln0b_ref[...])
    zg = jnp.dot(h.astype(BF16), wg_ref[...], preferred_element_type=F32) + bg_ref[...]
    m = (jax.nn.sigmoid(zg[:, :D_MODEL]) * ya_ref[...].astype(F32)
         + jax.nn.sigmoid(zg[:, D_MODEL:]) * yb_ref[...].astype(F32))
    mix = jnp.dot(m.astype(BF16), wo_ref[...], preferred_element_type=F32) + bo_ref[...]
    h1 = _layer_norm(DEEPNORM_ALPHA * h + mix, ln1g_ref[...], ln1b_ref[...])
    h1_ref[...] = h1

    logits = lax.dot_general(wr_ref[...], h1, (((1,), (1,)), ((), ())),
                             precision=lax.Precision.HIGHEST, preferred_element_type=F32) + br_ref[...]
    eio = lax.broadcasted_iota(I32, (N_EXPERTS, tm), 0)
    vals, idxs = [], []
    cur = logits
    for _ in range(TOP_K):
        mx = jnp.max(cur, axis=0, keepdims=True)
        ik = jnp.min(jnp.where(cur == mx, eio, N_EXPERTS), axis=0, keepdims=True)
        vals.append(mx)
        idxs.append(ik)
        cur = jnp.where(eio == ik, -jnp.inf, cur)
    exps = [jnp.exp(v - vals[0]) for v in vals]
    den = exps[0] + exps[1] + exps[2] + exps[3]
    gate_ref[...] = jnp.concatenate([e / den for e in exps], axis=0)
    idx_ref[...] = jnp.concatenate(idxs, axis=0)

    hot = jnp.zeros((N_EXPERTS, tm), F32)
    for ik in idxs:
        hot = hot + (eio == ik).astype(F32)
    before = (lax.broadcasted_iota(I32, (tm, tm), 0) < lax.broadcasted_iota(I32, (tm, tm), 1)).astype(BF16)
    prior = jnp.dot(hot.astype(BF16), before, preferred_element_type=F32) + carry_ref[:, 0:1]
    ranks = [jnp.sum(jnp.where(eio == ik, prior, 0.0), axis=0, keepdims=True) for ik in idxs]
    rank_ref[...] = jnp.concatenate(ranks, axis=0).astype(I32)
    carry_ref[...] = carry_ref[...] + jnp.sum(hot, axis=1, keepdims=True)
    cnt_ref[...] = carry_ref[...]


def _mix_route(x, ya, yb, carry_in, ln0_g, ln0_b, w_gate, b_gate, w_o, b_o, ln1_g, ln1_b, w_rt, b_rt):
    t = x.shape[0]
    tm = TM_MIX
    row = lambda d: pl.BlockSpec((tm, d), lambda i: (i, 0))
    col = lambda r: pl.BlockSpec((r, tm), lambda i: (0, i))
    return pl.pallas_call(
        _mix_kernel,
        grid=(t // tm,),
        in_specs=[
            row(D_MODEL), row(D_MODEL), row(D_MODEL), _full((N_EXPERTS, LANES)),
            _full((1, D_MODEL)), _full((1, D_MODEL)),
            _full((D_MODEL, 2 * D_MODEL)), _full((1, 2 * D_MODEL)),
            _full((D_MODEL, D_MODEL)), _full((1, D_MODEL)),
            _full((1, D_MODEL)), _full((1, D_MODEL)),
            _full((N_EXPERTS, D_MODEL)), _full((N_EXPERTS, 1)),
        ],
        out_specs=[row(D_MODEL), col(TOP_K), col(TOP_K), col(TOP_K), _full((N_EXPERTS, LANES))],
        out_shape=[
            jax.ShapeDtypeStruct((t, D_MODEL), F32),
            jax.ShapeDtypeStruct((TOP_K, t), I32),
            jax.ShapeDtypeStruct((TOP_K, t), F32),
            jax.ShapeDtypeStruct((TOP_K, t), I32),
            jax.ShapeDtypeStruct((N_EXPERTS, LANES), F32),
        ],
        scratch_shapes=[pltpu.VMEM((N_EXPERTS, LANES), F32)],
        compiler_params=_cparams("arbitrary"),
        name="mix_route",
    )(x, ya, yb, carry_in, ln0_g, ln0_b, w_gate, b_gate, w_o, b_o, ln1_g, ln1_b, w_rt, b_rt)


def _row_copy(src_ref, src_row, dst_ref, dst_row, sem):
    return pltpu.make_async_copy(src_ref.at[pl.ds(src_row, 1)], dst_ref.at[pl.ds(dst_row, 1)], sem)


def _fetch_slots(slots_hbm, slots_smem, sem, step, slot):
    return pltpu.make_async_copy(slots_hbm.at[step], slots_smem.at[slot], sem.at[slot])


def _pad_fill(h_ref, xs_out, pad_smem, pad_sem, n_pad_runs, wait):
    def go(cp):
        if wait:
            cp.wait()
        else:
            cp.start()

    def run(j, carry):
        start = pad_smem[2 * j]
        length = pad_smem[2 * j + 1]
        head = jnp.minimum((-start) & (SUBLANES - 1), length)
        for r in range(SUBLANES - 1):
            @pl.when(r < head)
            def _():
                go(_row_copy(h_ref, 0, xs_out, start + r, pad_sem))

        body = start + head
        rest = length - head
        for bit in range(SUBLANES.bit_length() - 1, BM_MOE.bit_length() - 1):
            size = 1 << bit

            @pl.when((rest & size) != 0)
            def _():
                off = pl.multiple_of(body + (rest & ~(2 * size - 1)), SUBLANES)
                go(pltpu.make_async_copy(h_ref.at[pl.ds(0, size)], xs_out.at[pl.ds(off, size)], pad_sem))
        return carry

    lax.fori_loop(0, n_pad_runs, run, 0)

    tm = h_ref.shape[0]

    def dead_block(j, carry):
        for part in range(BM_MOE // tm):
            off = pl.multiple_of(j * BM_MOE + part * tm, tm)
            go(pltpu.make_async_copy(h_ref, xs_out.at[pl.ds(off, tm)], pad_sem))
        return carry

    lax.fori_loop(pad_smem[2 * n_pad_runs], xs_out.shape[0] // BM_MOE, dead_block, 0)


def _dispatch_kernel(h_ref, slots_hbm, pad_hbm, *rest, n_pad_runs, aliased):
    xs_out, slots_smem, pad_smem, slot_sem, pad_sem, row_sem = rest[1:] if aliased else rest
    i = pl.program_id(0)
    n = pl.num_programs(0)
    tm = h_ref.shape[0]
    cur = lax.rem(i, 2)

    @pl.when(i == 0)
    def _():
        _fetch_slots(slots_hbm, slots_smem, slot_sem, 0, 0).start()

    @pl.when(i + 1 < n)
    def _():
        _fetch_slots(slots_hbm, slots_smem, slot_sem, i + 1, 1 - cur).start()

    if n_pad_runs:
        @pl.when(i == 0)
        def _():
            fetch = pltpu.make_async_copy(pad_hbm, pad_smem, pad_sem)
            fetch.start()
            fetch.wait()
            _pad_fill(h_ref, xs_out, pad_smem, pad_sem, n_pad_runs, wait=False)
            _pad_fill(h_ref, xs_out, pad_smem, pad_sem, n_pad_runs, wait=True)

    _fetch_slots(slots_hbm, slots_smem, slot_sem, i, cur).wait()

    def issue(r, carry):
        for k in range(TOP_K):
            _row_copy(h_ref, r, xs_out, slots_smem[cur, k * tm + r], row_sem).start()
        return carry

    lax.fori_loop(0, tm, issue, 0, unroll=ISSUE_UNROLL)
    for _ in range(TOP_K):
        pltpu.make_async_copy(h_ref, xs_out.at[pl.ds(0, tm)], row_sem).wait()


def _dispatch(h1, slots, pad_runs, xs, n_rows):
    t = h1.shape[0]
    tm = TM_ROWS
    aliased = xs is not None
    n_pad_runs = 0 if aliased else N_EXPERTS
    operands = (h1, slots, pad_runs) + ((xs,) if aliased else ())
    return pl.pallas_call(
        functools.partial(_dispatch_kernel, n_pad_runs=n_pad_runs, aliased=aliased),
        grid=(t // tm,),
        in_specs=[pl.BlockSpec((tm, D_MODEL), lambda i: (i, 0))]
        + [pl.BlockSpec(memory_space=pl.ANY)] * (len(operands) - 1),
        out_specs=pl.BlockSpec(memory_space=pl.ANY),
        out_shape=jax.ShapeDtypeStruct((n_rows, D_MODEL), F32),
        input_output_aliases={3: 0} if aliased else {},
        scratch_shapes=[
            pltpu.SMEM((2, TOP_K * tm), I32),
            pltpu.SMEM((pad_runs.shape[0],), I32),
            pltpu.SemaphoreType.DMA((2,)),
            pltpu.SemaphoreType.DMA,
            pltpu.SemaphoreType.DMA,
        ],
        compiler_params=_cparams("arbitrary"),
        name="dispatch",
    )(*operands)


def _moe_kernel(blk_e_ref, nb_ref, x_ref, wgu_ref, bgu_ref, wd_ref, bd_ref, o_ref, wgu_b, wd_b):
    i = pl.program_id(0)
    is_live = i < nb_ref[0]

    @pl.when(jnp.logical_not(is_live))
    def _():
        o_ref[...] = jnp.zeros(o_ref.shape, F32)

    @pl.when(jnp.logical_or(i == 0, blk_e_ref[i] != blk_e_ref[jnp.maximum(i - 1, 0)]))
    def _():
        for r in range(0, D_MODEL, CAST_ROWS):
            wgu_b[r:r + CAST_ROWS, :] = wgu_ref[0, r:r + CAST_ROWS, :].astype(BF16)
        for r in range(0, D_FF, CAST_ROWS):
            wd_b[r:r + CAST_ROWS, :] = wd_ref[0, r:r + CAST_ROWS, :].astype(BF16)

    @pl.when(is_live)
    def _():
        x = x_ref[...].astype(BF16)
        acc = jnp.zeros(o_ref.shape, F32) + bd_ref[0]
        for c in range(D_FF // FF_CHUNK):
            lo = c * FF_CHUNK
            g = jnp.dot(x, wgu_b[:, lo:lo + FF_CHUNK], preferred_element_type=F32) + bgu_ref[0, :, lo:lo + FF_CHUNK]
            u = (jnp.dot(x, wgu_b[:, D_FF + lo:D_FF + lo + FF_CHUNK], preferred_element_type=F32)
                 + bgu_ref[0, :, D_FF + lo:D_FF + lo + FF_CHUNK])
            g = jnp.minimum(g, SWIGLU_LIMIT)
            u = jnp.clip(u, -SWIGLU_LIMIT, SWIGLU_LIMIT)
            act = (u + 1.0) * (g * jax.nn.sigmoid(g * SWIGLU_ALPHA))
            acc = acc + jnp.dot(act.astype(BF16), wd_b[lo:lo + FF_CHUNK, :], preferred_element_type=F32)
        o_ref[...] = acc


def _moe_blocks(xs, blk_e, nb_used, w_gu, b_gu, w_down, b_down):
    p = xs.shape[0]
    n_blocks = p // BM_MOE
    live = lambda i, nb: jnp.minimum(i, nb[0] - 1)
    grid_spec = pltpu.PrefetchScalarGridSpec(
        num_scalar_prefetch=2,
        grid=(n_blocks,),
        in_specs=[
            pl.BlockSpec((BM_MOE, D_MODEL), lambda i, be, nb: (live(i, nb), 0)),
            pl.BlockSpec((1, D_MODEL, 2 * D_FF), lambda i, be, nb: (be[i], 0, 0)),
            pl.BlockSpec((1, 1, 2 * D_FF), lambda i, be, nb: (be[i], 0, 0)),
            pl.BlockSpec((1, D_FF, D_MODEL), lambda i, be, nb: (be[i], 0, 0)),
            pl.BlockSpec((1, 1, D_MODEL), lambda i, be, nb: (be[i], 0, 0)),
        ],
        out_specs=pl.BlockSpec((BM_MOE, D_MODEL), lambda i, be, nb: (i, 0)),
        scratch_shapes=[pltpu.VMEM((D_MODEL, 2 * D_FF), BF16), pltpu.VMEM((D_FF, D_MODEL), BF16)],
    )
    return pl.pallas_call(
        _moe_kernel,
        grid_spec=grid_spec,
        out_shape=jax.ShapeDtypeStruct((p, D_MODEL), F32),
        compiler_params=pltpu.CompilerParams(dimension_semantics=("arbitrary",), vmem_limit_bytes=VMEM_LIMIT_MOE),
        name="moe_experts",
    )(blk_e, nb_used, xs, w_gu, b_gu, w_down, b_down)


def _final_kernel(h_ref, p_ref, gate_ref, slots_hbm, y_hbm, wpg_ref, bpg_ref, wpp_ref, g_ref, b_ref, o_ref,
                  slots_smem, rows_ref, slot_sem, row_sem):
    i = pl.program_id(0)
    n = pl.num_programs(0)
    tm = h_ref.shape[0]
    cur = lax.rem(i, 2)

    @pl.when(i == 0)
    def _():
        _fetch_slots(slots_hbm, slots_smem, slot_sem, 0, 0).start()

    @pl.when(i + 1 < n)
    def _():
        _fetch_slots(slots_hbm, slots_smem, slot_sem, i + 1, 1 - cur).start()

    _fetch_slots(slots_hbm, slots_smem, slot_sem, i, cur).wait()

    def issue(r, carry):
        for k in range(TOP_K):
            _row_copy(y_hbm, slots_smem[cur, k * tm + r], rows_ref.at[k], r, row_sem).start()
        return carry

    lax.fori_loop(0, tm, issue, 0, unroll=ISSUE_UNROLL)

    h = h_ref[...]
    ple = (jax.nn.sigmoid(jnp.dot(h.astype(BF16), wpg_ref[...], preferred_element_type=F32) + bpg_ref[...])
           * jnp.dot(p_ref[...].astype(BF16), wpp_ref[...], preferred_element_type=F32))

    for k in range(TOP_K):
        pltpu.make_async_copy(y_hbm.at[pl.ds(0, tm)], rows_ref.at[k], row_sem).wait()

    gates = gate_ref[...]
    ffn = gates[:, 0:1] * rows_ref[0]
    for k in range(1, TOP_K):
        ffn = ffn + gates[:, k:k + 1] * rows_ref[k]
    o_ref[...] = _layer_norm(DEEPNORM_ALPHA * h + (ffn + ple), g_ref[...], b_ref[...])


def _combine_final(h1, p, gates, slots, y, w_pg, b_pg, w_pp, ln_g, ln_b):
    t = h1.shape[0]
    tm = TM_ROWS
    return pl.pallas_call(
        _final_kernel,
        grid=(t // tm,),
        in_specs=[
            pl.BlockSpec((tm, D_MODEL), lambda i: (i, 0)),
            pl.BlockSpec((tm, D_PLE), lambda i: (i, 0)),
            pl.BlockSpec((tm, TOP_K), lambda i: (i, 0)),
            pl.BlockSpec(memory_space=pl.ANY),
            pl.BlockSpec(memory_space=pl.ANY),
            _full((D_MODEL, D_MODEL)), _full((1, D_MODEL)), _full((D_PLE, D_MODEL)),
            _full((1, D_MODEL)), _full((1, D_MODEL)),
        ],
        out_specs=pl.BlockSpec((tm, D_MODEL), lambda i: (i, 0)),
        out_shape=jax.ShapeDtypeStruct((t, D_MODEL), F32),
        scratch_shapes=[
            pltpu.SMEM((2, TOP_K * tm), I32),
            pltpu.VMEM((TOP_K, tm, D_MODEL), F32),
            pltpu.SemaphoreType.DMA((2,)),
            pltpu.SemaphoreType.DMA,
        ],
        compiler_params=_cparams("arbitrary"),
        name="combine_final",
    )(h1, p, gates, slots, y, w_pg, b_pg, w_pp, ln_g, ln_b)


def _tile_slots(dest):
    t = dest.shape[1]
    return dest.reshape(TOP_K, t // TM_ROWS, TM_ROWS).transpose(1, 0, 2).reshape(t // TM_ROWS, TOP_K * TM_ROWS)


def kernel(x_prompt, x_sample, p_prompt, p_sample, ln0_g, ln0_b, w_in, b_in, conv_w, conv_b, lnc_g, lnc_b, w_conv_out, w_fnet_out, w_o, b_o, ln1_g, ln1_b, w_router, b_router, w_gu, b_gu, w_down, b_down, w_pg, b_pg, w_pp, ln2_g, ln2_b):
    row = lambda v: v.reshape(1, -1).astype(F32)
    n_branch = 2 * D_CONV + D_FNET
    w_in_b = w_in[0].astype(BF16)
    w_branch, w_gate = w_in_b[:, :n_branch], w_in_b[:, n_branch:]
    b_branch, b_gate = row(b_in[0, :n_branch]), row(b_in[0, n_branch:])
    w_conv_out_b = w_conv_out[0].astype(BF16)
    w_fnet_out_b = w_fnet_out[0].astype(BF16)
    w_o_b = w_o[0].astype(BF16)
    w_rt = w_router[0].T.astype(F32)
    b_rt = b_router[0].reshape(N_EXPERTS, 1).astype(F32)
    b_gu_r = b_gu[0].reshape(N_EXPERTS, 1, 2 * D_FF).astype(F32)
    b_down_r = b_down[0].reshape(N_EXPERTS, 1, D_MODEL).astype(F32)
    w_pg_b = w_pg[0].astype(BF16)
    w_pp_b = w_pp[0].astype(BF16)

    carry = jnp.zeros((N_EXPERTS, LANES), F32)
    routed = []
    for x, p in ((x_prompt, p_prompt[0]), (x_sample, p_sample[0])):
        bsz, s, _ = x.shape
        xt = x.reshape(bsz * s, D_MODEL)
        u, f_in = _inproj(xt, row(ln0_g), row(ln0_b), w_branch, b_branch)
        ya = _conv_branch(u.reshape(bsz, s, D_CONV), conv_w[0].astype(F32), row(conv_b[0]), row(lnc_g[0]),
                          row(lnc_b[0]), w_conv_out_b).reshape(bsz * s, D_MODEL)
        yb = _fourier_branch(f_in.reshape(bsz, s, D_FNET), w_fnet_out_b)
        h1, idx, gates, rank, carry = _mix_route(
            xt, ya, yb, carry, row(ln0_g), row(ln0_b), w_gate, b_gate, w_o_b, row(b_o[0]),
            row(ln1_g[0]), row(ln1_b[0]), w_rt, b_rt)
        routed.append((h1, idx, gates, rank, p.reshape(bsz * s, D_PLE), (bsz, s)))

    n_assign = sum(r[0].shape[0] for r in routed) * TOP_K
    n_blocks = (n_assign + N_EXPERTS * (BM_MOE - 1)) // BM_MOE
    counts = carry[:, 0].astype(I32)
    padded = ((counts + BM_MOE - 1) // BM_MOE) * BM_MOE
    ends = jnp.cumsum(padded)
    start = ends - padded
    nb_used = (ends[-1] // BM_MOE).reshape(1).astype(I32)
    blk = jnp.minimum(jnp.arange(n_blocks, dtype=I32), nb_used[0] - 1)
    blk_e = jnp.minimum(jnp.sum((ends[None, :] <= (blk * BM_MOE)[:, None]).astype(I32), axis=1), N_EXPERTS - 1)
    pad_runs = jnp.concatenate([jnp.stack([start + counts, padded - counts], axis=1).reshape(-1), nb_used,
                                jnp.zeros((LANES - 2 * N_EXPERTS - 1,), I32)]).astype(I32)

    xs = None
    slot_tables = []
    for h1, idx, _, rank, _, _ in routed:
        first = jnp.sum(jnp.where(idx[None] == jnp.arange(N_EXPERTS, dtype=I32)[:, None, None],
                                  start[:, None, None], 0), axis=0)
        slots = _tile_slots(first + rank)
        slot_tables.append(slots)
        xs = _dispatch(h1, slots, pad_runs, xs, n_blocks * BM_MOE)

    y = _moe_blocks(xs, blk_e, nb_used, w_gu.reshape(N_EXPERTS, D_MODEL, 2 * D_FF), b_gu_r,
                    w_down.reshape(N_EXPERTS, D_FF, D_MODEL), b_down_r)

    outs = []
    for (h1, _, gates, _, p, (bsz, s)), slots in zip(routed, slot_tables):
        o = _combine_final(h1, p, gates.T, slots, y, w_pg_b, row(b_pg[0]), w_pp_b, row(ln2_g[0]), row(ln2_b[0]))
        outs.append(o.reshape(bsz, s, D_MODEL))
    return tuple(outs)
```

```python
import functools

import jax
import jax.numpy as jnp
from jax import lax
from jax.experimental import pallas as pl
from jax.experimental.pallas import tpu as pltpu

F32 = jnp.float32
BF16 = jnp.bfloat16
I32 = jnp.int32

D_MODEL = 1024
D_CONV = D_MODEL // 2
CONV_WIDTH = 31
CONV_PAD = CONV_WIDTH // 2
D_FNET = D_MODEL // 2
FNET_GROUPS = 4
FNET_GROUP_DIM = D_FNET // FNET_GROUPS
D_PLE = 256
N_EXPERTS = 32
TOP_K = 4
D_FF = D_MODEL
SWIGLU_ALPHA = 1.702
SWIGLU_LIMIT = 7.0
LN_EPS = 1e-5
DEPTH = 1
DEEPNORM_ALPHA = (2 * DEPTH) ** 0.25

LANES = 128
SUBLANES = 8
DFT_N1 = 128
HALO = 16

TM_INPROJ = 512
TS_CONV = 512
RC_CONV = 32
CB_FFT1 = 4096
TM_MIX = 512
SUB_MIX = 256
TM_ROWS = 256
BM_MOE = 512
FF_CHUNK = 512
ISSUE_UNROLL = 8
CAST_ROWS = 128
VMEM_LIMIT = 48 * 1024 * 1024
VMEM_LIMIT_MOE = 56 * 1024 * 1024


def _cparams(*sem):
    return pltpu.CompilerParams(dimension_semantics=sem, vmem_limit_bytes=VMEM_LIMIT)


def _layer_norm(x, g, b):
    mu = jnp.mean(x, axis=-1, keepdims=True)
    xc = x - mu
    var = jnp.mean(xc * xc, axis=-1, keepdims=True)
    return xc * lax.rsqrt(var + LN_EPS) * g + b


def _full(shape):
    return pl.BlockSpec(shape, lambda *_: (0,) * len(shape))


def _to_row_tiles(ref, row0, x):
    rows = x.shape[0]
    for j in range(D_MODEL // LANES):
        ref[pl.ds(row0 * SUBLANES + j, rows, stride=SUBLANES), :] = x[:, j * LANES:(j + 1) * LANES]


def _from_row_tiles(ref, row0, rows):
    return jnp.concatenate([ref[pl.ds(row0 * SUBLANES + j, rows, stride=SUBLANES), :]
                            for j in range(D_MODEL // LANES)], axis=1)


def _inproj_kernel(x_ref, g_ref, b_ref, w_ref, bias_ref, u_ref, f_ref):
    h = _layer_norm(x_ref[...], g_ref[...], b_ref[...])
    z = jnp.dot(h.astype(BF16), w_ref[...], preferred_element_type=F32) + bias_ref[...]
    u_ref[...] = (z[:, :D_CONV] * jax.nn.sigmoid(z[:, D_CONV:2 * D_CONV])).astype(BF16)
    f_ref[...] = z[:, 2 * D_CONV:].astype(BF16)


def _inproj(x, ln_g, ln_b, w, bias):
    t = x.shape[0]
    n_out = w.shape[1]
    return pl.pallas_call(
        _inproj_kernel,
        grid=(t // TM_INPROJ,),
        in_specs=[
            pl.BlockSpec((TM_INPROJ, D_MODEL), lambda i: (i, 0)),
            _full((1, D_MODEL)), _full((1, D_MODEL)),
            _full((D_MODEL, n_out)), _full((1, n_out)),
        ],
        out_specs=[
            pl.BlockSpec((TM_INPROJ, D_CONV), lambda i: (i, 0)),
            pl.BlockSpec((TM_INPROJ, D_FNET), lambda i: (i, 0)),
        ],
        out_shape=[jax.ShapeDtypeStruct((t, D_CONV), BF16), jax.ShapeDtypeStruct((t, D_FNET), BF16)],
        compiler_params=_cparams("parallel"),
        name="inproj",
    )(x, ln_g, ln_b, w, bias)


def _conv_kernel(prev_ref, cur_ref, next_ref, cw_ref, cb_ref, g_ref, b_ref, wout_ref, ya_ref,
                 ext_ref, act_ref):
    i = pl.program_id(1)
    last = pl.num_programs(1) - 1
    ts = cur_ref.shape[1]
    ext_rows = ts + 2 * HALO
    zero = jnp.zeros((HALO, D_CONV), F32)
    ext_ref[0, 0:HALO, :] = jnp.where(i > 0, prev_ref[0].astype(F32), zero)
    ext_ref[0, HALO:HALO + ts, :] = cur_ref[0].astype(F32)
    ext_ref[0, HALO + ts:ext_rows, :] = jnp.where(i < last, next_ref[0].astype(F32), zero)
    for r in range(1, SUBLANES):
        ext_ref[r, 0:ext_rows - SUBLANES, :] = ext_ref[0, r:r + ext_rows - SUBLANES, :]

    cb = cb_ref[...]
    g = g_ref[...]
    b = b_ref[...]

    def chunk(c, carry):
        r0 = pl.multiple_of(c * RC_CONV, RC_CONV)
        acc = jnp.zeros((RC_CONV, D_CONV), F32) + cb
        for j in range(CONV_WIDTH):
            off = j + HALO - CONV_PAD
            q, r = divmod(off, SUBLANES)
            acc = acc + cw_ref[j:j + 1, :] * ext_ref[r, pl.ds(r0 + q * SUBLANES, RC_CONV), :]
        y = _layer_norm(acc, g, b)
        act_ref[pl.ds(r0, RC_CONV), :] = (y * jax.nn.sigmoid(y)).astype(BF16)
        return carry

    lax.fori_loop(0, ts // RC_CONV, chunk, 0)
    ya_ref[0] = jnp.dot(act_ref[...], wout_ref[...], preferred_element_type=F32).astype(BF16)


def _conv_branch(u, conv_w, conv_b, ln_g, ln_b, w_out):
    bsz, s, _ = u.shape
    ts = TS_CONV
    nb = s // ts
    hb = ts // HALO
    return pl.pallas_call(
        _conv_kernel,
        grid=(bsz, nb),
        in_specs=[
            pl.BlockSpec((1, HALO, D_CONV), lambda b, i: (b, jnp.maximum(i * hb - 1, 0), 0)),
            pl.BlockSpec((1, ts, D_CONV), lambda b, i: (b, i, 0)),
            pl.BlockSpec((1, HALO, D_CONV), lambda b, i: (b, jnp.minimum((i + 1) * hb, s // HALO - 1), 0)),
            _full((CONV_WIDTH, D_CONV)), _full((1, D_CONV)), _full((1, D_CONV)), _full((1, D_CONV)),
            _full((D_CONV, D_MODEL)),
        ],
        out_specs=pl.BlockSpec((1, ts, D_MODEL), lambda b, i: (b, i, 0)),
        out_shape=jax.ShapeDtypeStruct((bsz, s, D_MODEL), BF16),
        scratch_shapes=[
            pltpu.VMEM((SUBLANES, ts + 2 * HALO, D_CONV), F32),
            pltpu.VMEM((ts, D_CONV), BF16),
        ],
        compiler_params=_cparams("parallel", "parallel"),
        name="conv_branch",
    )(u, u, u, conv_w, conv_b, ln_g, ln_b, w_out)


def _dft_tables(s):
    n2_len = s // DFT_N1
    k1_per_tile = DFT_N1 // n2_len
    two_pi = 2.0 * jnp.pi
    a = jnp.arange(DFT_N1, dtype=I32)
    ang1 = two_pi * ((a[:, None] * a[None, :]) % DFT_N1).astype(F32) / DFT_N1
    f1 = jnp.concatenate([jnp.cos(ang1), -jnp.sin(ang1)], axis=0).astype(BF16)
    cs = jnp.stack([jnp.cos(ang1), jnp.sin(ang1)]).astype(BF16)
    n_tiles = s // DFT_N1
    t = jnp.arange(n_tiles, dtype=I32)[:, None, None]
    row = jnp.arange(DFT_N1, dtype=I32)[None, :, None]
    col = jnp.arange(DFT_N1, dtype=I32)[None, None, :]
    k2, k1l_out = row // k1_per_tile, row % k1_per_tile
    k1l_in, n2 = col // n2_len, col % n2_len
    k = t * k1_per_tile + k1l_out + DFT_N1 * k2
    ang2 = two_pi * ((n2 * k) % s).astype(F32) / s
    hit = k1l_in == k1l_out
    gr = jnp.where(hit, jnp.cos(ang2), 0.0)
    gi = jnp.where(hit, -jnp.sin(ang2), 0.0)
    g = jnp.concatenate([jnp.concatenate([gr, -gi], axis=2),
                         jnp.concatenate([gi, gr], axis=2)], axis=1).astype(BF16)
    return f1, g, cs


def _fft1_kernel(x_ref, f_ref, y_ref):
    r = jnp.dot(f_ref[...], x_ref[0], preferred_element_type=F32)
    y_ref[0, 0] = r[:DFT_N1].astype(BF16)
    y_ref[0, 1] = r[DFT_N1:].astype(BF16)


def _fft2_kernel(y_ref, g_ref, cs_ref, w_ref, o_ref, *, scale):
    yb = y_ref[0].reshape(2 * DFT_N1, D_FNET)
    z = jnp.dot(g_ref[0], yb, preferred_element_type=F32)
    zr = z[:DFT_N1].astype(BF16)
    zi = z[DFT_N1:].astype(BF16)
    parts = []
    for grp in range(FNET_GROUPS):
        sl = slice(grp * FNET_GROUP_DIM, (grp + 1) * FNET_GROUP_DIM)
        parts.append(jnp.dot(zr[:, sl], cs_ref[0], preferred_element_type=F32)
                     + jnp.dot(zi[:, sl], cs_ref[1], preferred_element_type=F32))
    fm = (jnp.concatenate(parts, axis=1) * scale).astype(BF16)
    out = jnp.dot(fm, w_ref[...], preferred_element_type=F32)
    o_ref[...] = out.reshape(o_ref.shape).astype(o_ref.dtype)


def _fourier_branch(f_in, w_out):
    bsz, s, _ = f_in.shape
    n2_len = s // DFT_N1
    k1_per_tile = DFT_N1 // n2_len
    cols = n2_len * D_FNET
    cb = min(CB_FFT1, cols)
    f1, g, cs = _dft_tables(s)
    y = pl.pallas_call(
        _fft1_kernel,
        grid=(bsz, cols // cb),
        in_specs=[pl.BlockSpec((1, DFT_N1, cb), lambda b, j: (b, 0, j)), _full((2 * DFT_N1, DFT_N1))],
        out_specs=pl.BlockSpec((1, 2, DFT_N1, cb), lambda b, j: (b, 0, 0, j)),
        out_shape=jax.ShapeDtypeStruct((bsz, 2, DFT_N1, cols), BF16),
        compiler_params=_cparams("parallel", "parallel"),
        name="fft_stage1",
    )(f_in.reshape(bsz, DFT_N1, cols), f1)
    y = y.reshape(bsz, 2, s, D_FNET)
    scale = float((s * FNET_GROUP_DIM) ** -0.5)
    if k1_per_tile == 1:
        out_shape = (bsz, n2_len, DFT_N1 * D_MODEL)
        out_spec = pl.BlockSpec((1, n2_len, D_MODEL), lambda b, t: (b, 0, t))
    else:
        out_shape = (bsz, n2_len, DFT_N1 // k1_per_tile, k1_per_tile, D_MODEL)
        out_spec = pl.BlockSpec((1, n2_len, 1, k1_per_tile, D_MODEL), lambda b, t: (b, 0, t, 0, 0))
    yb = pl.pallas_call(
        functools.partial(_fft2_kernel, scale=scale),
        grid=(bsz, s // DFT_N1),
        in_specs=[
            pl.BlockSpec((1, 2, DFT_N1, D_FNET), lambda b, t: (b, 0, t, 0)),
            pl.BlockSpec((1, 2 * DFT_N1, 2 * DFT_N1), lambda b, t: (t, 0, 0)),
            _full((2, DFT_N1, DFT_N1)),
            _full((D_FNET, D_MODEL)),
        ],
        out_specs=out_spec,
        out_shape=jax.ShapeDtypeStruct(out_shape, BF16),
        compiler_params=_cparams("parallel", "parallel"),
        name="fft_stage2",
    )(y, g, cs, w_out)
    return yb.reshape(bsz * s, D_MODEL)


def _mix_kernel(x_ref, ya_ref, yb_ref, cin_ref, ln0g_ref, ln0b_ref, wg_ref, bg_ref, wo_ref, bo_ref,
                ln1g_ref, ln1b_ref, wr_ref, br_ref,
                h1_ref, idx_ref, gate_ref, rank_ref, cnt_ref, carry_ref):
    tm = x_ref.shape[0]
    sub = min(SUB_MIX, tm)

    @pl.when(pl.program_id(0) == 0)
    def _():
        carry_ref[...] = cin_ref[...]

    carry = carry_ref[...]
    eio = lax.broadcasted_iota(I32, (N_EXPERTS, sub), 0)
    before = (lax.broadcasted_iota(I32, (sub, sub), 0) < lax.broadcasted_iota(I32, (sub, sub), 1)).astype(BF16)
    for r0 in range(0, tm, sub):
        rows = slice(r0, r0 + sub)
        h = _layer_norm(x_ref[rows, :], ln0g_ref[...], ln0b_ref[...])
        zg = jnp.dot(h.astype(BF16), wg_ref[...], preferred_element_type=F32) + bg_ref[...]
        m = (jax.nn.sigmoid(zg[:, :D_MODEL]) * ya_ref[rows, :].astype(F32)
             + jax.nn.sigmoid(zg[:, D_MODEL:]) * yb_ref[rows, :].astype(F32))
        mix = jnp.dot(m.astype(BF16), wo_ref[...], preferred_element_type=F32) + bo_ref[...]
        h1 = _layer_norm(DEEPNORM_ALPHA * h + mix, ln1g_ref[...], ln1b_ref[...])
        _to_row_tiles(h1_ref, r0, h1)

        logits = lax.dot_general(wr_ref[...], h1, (((1,), (1,)), ((), ())),
                                 precision=lax.Precision.HIGHEST, preferred_element_type=F32) + br_ref[...]
        vals, idxs = [], []
        cur = logits
        for _ in range(TOP_K):
            mx = jnp.max(cur, axis=0, keepdims=True)
            ik = jnp.min(jnp.where(cur == mx, eio, N_EXPERTS), axis=0, keepdims=True)
            vals.append(mx)
            idxs.append(ik)
            cur = jnp.where(eio == ik, -jnp.inf, cur)
        exps = [jnp.exp(v - vals[0]) for v in vals]
        den = exps[0] + exps[1] + exps[2] + exps[3]
        gate_ref[:, rows] = jnp.concatenate([e / den for e in exps], axis=0)
        idx_ref[:, rows] = jnp.concatenate(idxs, axis=0)

        hot = jnp.zeros((N_EXPERTS, sub), F32)
        for ik in idxs:
            hot = hot + (eio == ik).astype(F32)
        prior = jnp.dot(hot.astype(BF16), before, preferred_element_type=F32) + carry[:, 0:1]
        ranks = [jnp.sum(jnp.where(eio == ik, prior, 0.0), axis=0, keepdims=True) for ik in idxs]
        rank_ref[:, rows] = jnp.concatenate(ranks, axis=0).astype(I32)
        carry = carry + jnp.sum(hot, axis=1, keepdims=True)
    carry_ref[...] = carry
    cnt_ref[...] = carry


def _mix_route(x, ya, yb, carry_in, ln0_g, ln0_b, w_gate, b_gate, w_o, b_o, ln1_g, ln1_b, w_rt, b_rt):
    t = x.shape[0]
    tm = TM_MIX
    row = lambda d: pl.BlockSpec((tm, d), lambda i: (i, 0))
    col = lambda r: pl.BlockSpec((r, tm), lambda i: (0, i))
    return pl.pallas_call(
        _mix_kernel,
        grid=(t // tm,),
        in_specs=[
            row(D_MODEL), row(D_MODEL), row(D_MODEL), _full((N_EXPERTS, LANES)),
            _full((1, D_MODEL)), _full((1, D_MODEL)),
            _full((D_MODEL, 2 * D_MODEL)), _full((1, 2 * D_MODEL)),
            _full((D_MODEL, D_MODEL)), _full((1, D_MODEL)),
            _full((1, D_MODEL)), _full((1, D_MODEL)),
            _full((N_EXPERTS, D_MODEL)), _full((N_EXPERTS, 1)),
        ],
        out_specs=[pl.BlockSpec((tm * SUBLANES, LANES), lambda i: (i, 0)), col(TOP_K), col(TOP_K), col(TOP_K),
                   _full((N_EXPERTS, LANES))],
        out_shape=[
            jax.ShapeDtypeStruct((t * SUBLANES, LANES), F32),
            jax.ShapeDtypeStruct((TOP_K, t), I32),
            jax.ShapeDtypeStruct((TOP_K, t), F32),
            jax.ShapeDtypeStruct((TOP_K, t), I32),
            jax.ShapeDtypeStruct((N_EXPERTS, LANES), F32),
        ],
        scratch_shapes=[pltpu.VMEM((N_EXPERTS, LANES), F32)],
        compiler_params=_cparams("arbitrary"),
        name="mix_route",
    )(x, ya, yb, carry_in, ln0_g, ln0_b, w_gate, b_gate, w_o, b_o, ln1_g, ln1_b, w_rt, b_rt)


def _row_copy(src_ref, src_row, dst_ref, dst_row, sem):
    return pltpu.make_async_copy(src_ref.at[src_row], dst_ref.at[dst_row], sem)


def _fetch_slots(slots_hbm, slots_smem, sem, step, slot):
    return pltpu.make_async_copy(slots_hbm.at[step], slots_smem.at[slot], sem.at[slot])


def _pad_fill(h_ref, xs_out, pad_smem, pad_sem, n_pad_runs, wait):
    def go(cp):
        if wait:
            cp.wait()
        else:
            cp.start()

    def run(j, carry):
        start = pad_smem[2 * j]
        length = pad_smem[2 * j + 1]
        for bit in range(BM_MOE.bit_length() - 1):
            size = 1 << bit

            @pl.when((length & size) != 0)
            def _():
                off = start + (length & ~(2 * size - 1))
                go(pltpu.make_async_copy(h_ref.at[pl.ds(0, size)], xs_out.at[pl.ds(off, size)], pad_sem))
        return carry

    lax.fori_loop(0, n_pad_runs, run, 0)

    tm = h_ref.shape[0]

    def dead_block(j, carry):
        for part in range(BM_MOE // tm):
            off = j * BM_MOE + part * tm
            go(pltpu.make_async_copy(h_ref, xs_out.at[pl.ds(off, tm)], pad_sem))
        return carry

    lax.fori_loop(pad_smem[2 * n_pad_runs], xs_out.shape[0] // BM_MOE, dead_block, 0)


def _dispatch_kernel(h_ref, slots_hbm, pad_hbm, *rest, n_pad_runs, aliased):
    xs_out, slots_smem, pad_smem, slot_sem, pad_sem, row_sem = rest[1:] if aliased else rest
    i = pl.program_id(0)
    n = pl.num_programs(0)
    tm = h_ref.shape[0]
    cur = lax.rem(i, 2)

    @pl.when(i == 0)
    def _():
        _fetch_slots(slots_hbm, slots_smem, slot_sem, 0, 0).start()

    @pl.when(i + 1 < n)
    def _():
        _fetch_slots(slots_hbm, slots_smem, slot_sem, i + 1, 1 - cur).start()

    if n_pad_runs:
        @pl.when(i == 0)
        def _():
            fetch = pltpu.make_async_copy(pad_hbm, pad_smem, pad_sem)
            fetch.start()
            fetch.wait()
            _pad_fill(h_ref, xs_out, pad_smem, pad_sem, n_pad_runs, wait=False)
            _pad_fill(h_ref, xs_out, pad_smem, pad_sem, n_pad_runs, wait=True)

    _fetch_slots(slots_hbm, slots_smem, slot_sem, i, cur).wait()

    def issue(r, carry):
        for k in range(TOP_K):
            _row_copy(h_ref, r, xs_out, slots_smem[cur, k * tm + r], row_sem).start()
        return carry

    lax.fori_loop(0, tm, issue, 0, unroll=ISSUE_UNROLL)
    for _ in range(TOP_K):
        pltpu.make_async_copy(h_ref, xs_out.at[pl.ds(0, tm)], row_sem).wait()


def _dispatch(h1, slots, pad_runs, xs, n_rows):
    h1 = h1.reshape(-1, SUBLANES, LANES)
    t = h1.shape[0]
    tm = TM_ROWS
    aliased = xs is not None
    n_pad_runs = 0 if aliased else N_EXPERTS
    operands = (h1, slots, pad_runs) + ((xs,) if aliased else ())
    return pl.pallas_call(
        functools.partial(_dispatch_kernel, n_pad_runs=n_pad_runs, aliased=aliased),
        grid=(t // tm,),
        in_specs=[pl.BlockSpec((tm, SUBLANES, LANES), lambda i: (i, 0, 0))]
        + [pl.BlockSpec(memory_space=pl.ANY)] * (len(operands) - 1),
        out_specs=pl.BlockSpec(memory_space=pl.ANY),
        out_shape=jax.ShapeDtypeStruct((n_rows, SUBLANES, LANES), F32),
        input_output_aliases={3: 0} if aliased else {},
        scratch_shapes=[
            pltpu.SMEM((2, TOP_K * tm), I32),
            pltpu.SMEM((pad_runs.shape[0],), I32),
            pltpu.SemaphoreType.DMA((2,)),
            pltpu.SemaphoreType.DMA,
            pltpu.SemaphoreType.DMA,
        ],
        compiler_params=_cparams("arbitrary"),
        name="dispatch",
    )(*operands)


def _moe_kernel(blk_e_ref, nb_ref, x_ref, wgu_ref, bgu_ref, wd_ref, bd_ref, o_ref, wgu_b, wd_b):
    i = pl.program_id(0)
    is_live = i < nb_ref[0]

    @pl.when(jnp.logical_not(is_live))
    def _():
        o_ref[...] = jnp.zeros(o_ref.shape, F32)

    @pl.when(jnp.logical_or(i == 0, blk_e_ref[i] != blk_e_ref[jnp.maximum(i - 1, 0)]))
    def _():
        for r in range(0, D_MODEL, CAST_ROWS):
            wgu_b[r:r + CAST_ROWS, :] = wgu_ref[0, r:r + CAST_ROWS, :].astype(BF16)
        for r in range(0, D_FF, CAST_ROWS):
            wd_b[r:r + CAST_ROWS, :] = wd_ref[0, r:r + CAST_ROWS, :].astype(BF16)

    @pl.when(is_live)
    def _():
        x = _from_row_tiles(x_ref, 0, BM_MOE).astype(BF16)
        acc = jnp.zeros((BM_MOE, D_MODEL), F32) + bd_ref[0]
        for c in range(D_FF // FF_CHUNK):
            lo = c * FF_CHUNK
            g = jnp.dot(x, wgu_b[:, lo:lo + FF_CHUNK], preferred_element_type=F32) + bgu_ref[0, :, lo:lo + FF_CHUNK]
            u = (jnp.dot(x, wgu_b[:, D_FF + lo:D_FF + lo + FF_CHUNK], preferred_element_type=F32)
                 + bgu_ref[0, :, D_FF + lo:D_FF + lo + FF_CHUNK])
            g = jnp.minimum(g, SWIGLU_LIMIT)
            u = jnp.clip(u, -SWIGLU_LIMIT, SWIGLU_LIMIT)
            act = (u + 1.0) * (g * jax.nn.sigmoid(g * SWIGLU_ALPHA))
            acc = acc + jnp.dot(act.astype(BF16), wd_b[lo:lo + FF_CHUNK, :], preferred_element_type=F32)
        _to_row_tiles(o_ref, 0, acc)


def _moe_blocks(xs, blk_e, nb_used, w_gu, b_gu, w_down, b_down):
    p = xs.shape[0] // SUBLANES
    n_blocks = p // BM_MOE
    live = lambda i, nb: jnp.minimum(i, nb[0] - 1)
    grid_spec = pltpu.PrefetchScalarGridSpec(
        num_scalar_prefetch=2,
        grid=(n_blocks,),
        in_specs=[
            pl.BlockSpec((BM_MOE * SUBLANES, LANES), lambda i, be, nb: (live(i, nb), 0)),
            pl.BlockSpec((1, D_MODEL, 2 * D_FF), lambda i, be, nb: (be[i], 0, 0)),
            pl.BlockSpec((1, 1, 2 * D_FF), lambda i, be, nb: (be[i], 0, 0)),
            pl.BlockSpec((1, D_FF, D_MODEL), lambda i, be, nb: (be[i], 0, 0)),
            pl.BlockSpec((1, 1, D_MODEL), lambda i, be, nb: (be[i], 0, 0)),
        ],
        out_specs=pl.BlockSpec((BM_MOE * SUBLANES, LANES), lambda i, be, nb: (i, 0)),
        scratch_shapes=[pltpu.VMEM((D_MODEL, 2 * D_FF), BF16), pltpu.VMEM((D_FF, D_MODEL), BF16)],
    )
    return pl.pallas_call(
        _moe_kernel,
        grid_spec=grid_spec,
        out_shape=jax.ShapeDtypeStruct((p * SUBLANES, LANES), F32),
        compiler_params=pltpu.CompilerParams(dimension_semantics=("arbitrary",), vmem_limit_bytes=VMEM_LIMIT_MOE),
        name="moe_experts",
    )(blk_e, nb_used, xs, w_gu, b_gu, w_down, b_down)


def _final_kernel(h_ref, p_ref, gate_ref, slots_hbm, y_hbm, wpg_ref, bpg_ref, wpp_ref, g_ref, b_ref, o_ref,
                  slots_smem, rows_ref, slot_sem, row_sem):
    i = pl.program_id(0)
    n = pl.num_programs(0)
    tm = h_ref.shape[0] // SUBLANES
    cur = lax.rem(i, 2)

    @pl.when(i == 0)
    def _():
        _fetch_slots(slots_hbm, slots_smem, slot_sem, 0, 0).start()

    @pl.when(i + 1 < n)
    def _():
        _fetch_slots(slots_hbm, slots_smem, slot_sem, i + 1, 1 - cur).start()

    _fetch_slots(slots_hbm, slots_smem, slot_sem, i, cur).wait()

    def issue(r, carry):
        for k in range(TOP_K):
            src = y_hbm.at[pl.ds(pl.multiple_of(slots_smem[cur, k * tm + r] * SUBLANES, SUBLANES), SUBLANES)]
            dst = rows_ref.at[k, pl.ds(pl.multiple_of(r * SUBLANES, SUBLANES), SUBLANES)]
            pltpu.make_async_copy(src, dst, row_sem).start()
        return carry

    lax.fori_loop(0, tm, issue, 0, unroll=ISSUE_UNROLL)

    h = _from_row_tiles(h_ref, 0, tm)
    ple = (jax.nn.sigmoid(jnp.dot(h.astype(BF16), wpg_ref[...], preferred_element_type=F32) + bpg_ref[...])
           * jnp.dot(p_ref[...].astype(BF16), wpp_ref[...], preferred_element_type=F32))

    for k in range(TOP_K):
        pltpu.make_async_copy(y_hbm.at[pl.ds(0, tm * SUBLANES)], rows_ref.at[k], row_sem).wait()

    gates = gate_ref[...]
    ffn = gates[:, 0:1] * _from_row_tiles(rows_ref.at[0], 0, tm)
    for k in range(1, TOP_K):
        ffn = ffn + gates[:, k:k + 1] * _from_row_tiles(rows_ref.at[k], 0, tm)
    o_ref[...] = _layer_norm(DEEPNORM_ALPHA * h + (ffn + ple), g_ref[...], b_ref[...])


def _combine_final(h1, p, gates, slots, y, w_pg, b_pg, w_pp, ln_g, ln_b):
    t = h1.shape[0] // SUBLANES
    tm = TM_ROWS
    return pl.pallas_call(
        _final_kernel,
        grid=(t // tm,),
        in_specs=[
            pl.BlockSpec((tm * SUBLANES, LANES), lambda i: (i, 0)),
            pl.BlockSpec((tm, D_PLE), lambda i: (i, 0)),
            pl.BlockSpec((tm, TOP_K), lambda i: (i, 0)),
            pl.BlockSpec(memory_space=pl.ANY),
            pl.BlockSpec(memory_space=pl.ANY),
            _full((D_MODEL, D_MODEL)), _full((1, D_MODEL)), _full((D_PLE, D_MODEL)),
            _full((1, D_MODEL)), _full((1, D_MODEL)),
        ],
        out_specs=pl.BlockSpec((tm, D_MODEL), lambda i: (i, 0)),
        out_shape=jax.ShapeDtypeStruct((t, D_MODEL), F32),
        scratch_shapes=[
            pltpu.SMEM((2, TOP_K * tm), I32),
            pltpu.VMEM((TOP_K, tm * SUBLANES, LANES), F32),
            pltpu.SemaphoreType.DMA((2,)),
            pltpu.SemaphoreType.DMA,
        ],
        compiler_params=_cparams("arbitrary"),
        name="combine_final",
    )(h1, p, gates, slots, y, w_pg, b_pg, w_pp, ln_g, ln_b)


def _tile_slots(dest):
    t = dest.shape[1]
    return dest.reshape(TOP_K, t // TM_ROWS, TM_ROWS).transpose(1, 0, 2).reshape(t // TM_ROWS, TOP_K * TM_ROWS)


def kernel(x_prompt, x_sample, p_prompt, p_sample, ln0_g, ln0_b, w_in, b_in, conv_w, conv_b, lnc_g, lnc_b, w_conv_out, w_fnet_out, w_o, b_o, ln1_g, ln1_b, w_router, b_router, w_gu, b_gu, w_down, b_down, w_pg, b_pg, w_pp, ln2_g, ln2_b):
    row = lambda v: v.reshape(1, -1).astype(F32)
    n_branch = 2 * D_CONV + D_FNET
    w_in_b = w_in[0].astype(BF16)
    w_branch, w_gate = w_in_b[:, :n_branch], w_in_b[:, n_branch:]
    b_branch, b_gate = row(b_in[0, :n_branch]), row(b_in[0, n_branch:])
    w_conv_out_b = w_conv_out[0].astype(BF16)
    w_fnet_out_b = w_fnet_out[0].astype(BF16)
    w_o_b = w_o[0].astype(BF16)
    w_rt = w_router[0].T.astype(F32)
    b_rt = b_router[0].reshape(N_EXPERTS, 1).astype(F32)
    b_gu_r = b_gu[0].reshape(N_EXPERTS, 1, 2 * D_FF).astype(F32)
    b_down_r = b_down[0].reshape(N_EXPERTS, 1, D_MODEL).astype(F32)
    w_pg_b = w_pg[0].astype(BF16)
    w_pp_b = w_pp[0].astype(BF16)

    carry = jnp.zeros((N_EXPERTS, LANES), F32)
    routed = []
    for x, p in ((x_prompt, p_prompt[0]), (x_sample, p_sample[0])):
        bsz, s, _ = x.shape
        xt = x.reshape(bsz * s, D_MODEL)
        u, f_in = _inproj(xt, row(ln0_g), row(ln0_b), w_branch, b_branch)
        ya = _conv_branch(u.reshape(bsz, s, D_CONV), conv_w[0].astype(F32), row(conv_b[0]), row(lnc_g[0]),
                          row(lnc_b[0]), w_conv_out_b).reshape(bsz * s, D_MODEL)
        yb = _fourier_branch(f_in.reshape(bsz, s, D_FNET), w_fnet_out_b)
        h1, idx, gates, rank, carry = _mix_route(
            xt, ya, yb, carry, row(ln0_g), row(ln0_b), w_gate, b_gate, w_o_b, row(b_o[0]),
            row(ln1_g[0]), row(ln1_b[0]), w_rt, b_rt)
        routed.append((h1, idx, gates, rank, p.reshape(bsz * s, D_PLE), (bsz, s)))

    n_assign = sum(r[1].shape[1] for r in routed) * TOP_K
    n_blocks = (n_assign + N_EXPERTS * (BM_MOE - 1)) // BM_MOE
    counts = carry[:, 0].astype(I32)
    padded = ((counts + BM_MOE - 1) // BM_MOE) * BM_MOE
    ends = jnp.cumsum(padded)
    start = ends - padded
    nb_used = (ends[-1] // BM_MOE).reshape(1).astype(I32)
    blk = jnp.minimum(jnp.arange(n_blocks, dtype=I32), nb_used[0] - 1)
    blk_e = jnp.minimum(jnp.sum((ends[None, :] <= (blk * BM_MOE)[:, None]).astype(I32), axis=1), N_EXPERTS - 1)
    pad_runs = jnp.concatenate([jnp.stack([start + counts, padded - counts], axis=1).reshape(-1), nb_used,
                                jnp.zeros((LANES - 2 * N_EXPERTS - 1,), I32)]).astype(I32)

    xs = None
    slot_tables = []
    for h1, idx, _, rank, _, _ in routed:
        first = jnp.sum(jnp.where(idx[None] == jnp.arange(N_EXPERTS, dtype=I32)[:, None, None],
                                  start[:, None, None], 0), axis=0)
        slots = _tile_slots(first + rank)
        slot_tables.append(slots)
        xs = _dispatch(h1, slots, pad_runs, xs, n_blocks * BM_MOE)

    y = _moe_blocks(xs.reshape(-1, LANES), blk_e, nb_used, w_gu.reshape(N_EXPERTS, D_MODEL, 2 * D_FF), b_gu_r,
                    w_down.reshape(N_EXPERTS, D_FF, D_MODEL), b_down_r)

    outs = []
    for (h1, _, gates, _, p, (bsz, s)), slots in zip(routed, slot_tables):
        o = _combine_final(h1, p, gates.T, slots, y, w_pg_b, row(b_pg[0]), w_pp_b, row(ln2_g[0]), row(ln2_b[0]))
        outs.append(o.reshape(bsz, s, D_MODEL))
    return tuple(outs)
```

```python
import functools

import jax
import jax.numpy as jnp
from jax import lax
from jax.experimental import pallas as pl
from jax.experimental.pallas import tpu as pltpu

F32 = jnp.float32
BF16 = jnp.bfloat16
I32 = jnp.int32

D_MODEL = 1024
D_CONV = D_MODEL // 2
CONV_WIDTH = 31
CONV_PAD = CONV_WIDTH // 2
D_FNET = D_MODEL // 2
FNET_GROUPS = 4
FNET_GROUP_DIM = D_FNET // FNET_GROUPS
D_PLE = 256
N_EXPERTS = 32
TOP_K = 4
D_FF = D_MODEL
SWIGLU_ALPHA = 1.702
SWIGLU_LIMIT = 7.0
LN_EPS = 1e-5
DEPTH = 1
DEEPNORM_ALPHA = (2 * DEPTH) ** 0.25

LANES = 128
SUBLANES = 8
DFT_N1 = 128
HALO = 16

TM_INPROJ = 512
TS_CONV = 512
RC_CONV = 32
CB_FFT1 = 4096
TM_MIX = 512
SUB_MIX = 256
TM_ROWS = 256
BM_MOE = 512
FF_CHUNK = 512
ISSUE_UNROLL = 8
CAST_ROWS = 128
VMEM_LIMIT = 48 * 1024 * 1024
VMEM_LIMIT_MOE = 56 * 1024 * 1024


def _cparams(*sem):
    return pltpu.CompilerParams(dimension_semantics=sem, vmem_limit_bytes=VMEM_LIMIT)


def _layer_norm(x, g, b):
    mu = jnp.mean(x, axis=-1, keepdims=True)
    xc = x - mu
    var = jnp.mean(xc * xc, axis=-1, keepdims=True)
    return xc * lax.rsqrt(var + LN_EPS) * g + b


def _full(shape):
    return pl.BlockSpec(shape, lambda *_: (0,) * len(shape))


def _to_row_tiles(ref, row0, x):
    rows = x.shape[0]
    for j in range(D_MODEL // LANES):
        ref[pl.ds(row0 * SUBLANES + j, rows, stride=SUBLANES), :] = x[:, j * LANES:(j + 1) * LANES]


def _from_row_tiles(ref, row0, rows):
    return jnp.concatenate([ref[pl.ds(row0 * SUBLANES + j, rows, stride=SUBLANES), :]
                            for j in range(D_MODEL // LANES)], axis=1)


def _inproj_kernel(x_ref, g_ref, b_ref, w_ref, bias_ref, u_ref, f_ref):
    h = _layer_norm(x_ref[...], g_ref[...], b_ref[...])
    z = jnp.dot(h.astype(BF16), w_ref[...], preferred_element_type=F32) + bias_ref[...]
    u_ref[...] = (z[:, :D_CONV] * jax.nn.sigmoid(z[:, D_CONV:2 * D_CONV])).astype(BF16)
    f_ref[...] = z[:, 2 * D_CONV:].astype(BF16)


def _inproj(x, ln_g, ln_b, w, bias):
    t = x.shape[0]
    n_out = w.shape[1]
    return pl.pallas_call(
        _inproj_kernel,
        grid=(t // TM_INPROJ,),
        in_specs=[
            pl.BlockSpec((TM_INPROJ, D_MODEL), lambda i: (i, 0)),
            _full((1, D_MODEL)), _full((1, D_MODEL)),
            _full((D_MODEL, n_out)), _full((1, n_out)),
        ],
        out_specs=[
            pl.BlockSpec((TM_INPROJ, D_CONV), lambda i: (i, 0)),
            pl.BlockSpec((TM_INPROJ, D_FNET), lambda i: (i, 0)),
        ],
        out_shape=[jax.ShapeDtypeStruct((t, D_CONV), BF16), jax.ShapeDtypeStruct((t, D_FNET), BF16)],
        compiler_params=_cparams("parallel"),
        name="inproj",
    )(x, ln_g, ln_b, w, bias)


def _conv_kernel(prev_ref, cur_ref, next_ref, cw_ref, cb_ref, g_ref, b_ref, wout_ref, ya_ref,
                 ext_ref, act_ref):
    i = pl.program_id(1)
    last = pl.num_programs(1) - 1
    ts = cur_ref.shape[1]
    ext_rows = ts + 2 * HALO
    zero = jnp.zeros((HALO, D_CONV), F32)
    ext_ref[0, 0:HALO, :] = jnp.where(i > 0, prev_ref[0].astype(F32), zero)
    ext_ref[0, HALO:HALO + ts, :] = cur_ref[0].astype(F32)
    ext_ref[0, HALO + ts:ext_rows, :] = jnp.where(i < last, next_ref[0].astype(F32), zero)
    for r in range(1, SUBLANES):
        ext_ref[r, 0:ext_rows - SUBLANES, :] = ext_ref[0, r:r + ext_rows - SUBLANES, :]

    cb = cb_ref[...]
    g = g_ref[...]
    b = b_ref[...]

    def chunk(c, carry):
        r0 = pl.multiple_of(c * RC_CONV, RC_CONV)
        acc = jnp.zeros((RC_CONV, D_CONV), F32) + cb
        for j in range(CONV_WIDTH):
            off = j + HALO - CONV_PAD
            q, r = divmod(off, SUBLANES)
            acc = acc + cw_ref[j:j + 1, :] * ext_ref[r, pl.ds(r0 + q * SUBLANES, RC_CONV), :]
        y = _layer_norm(acc, g, b)
        act_ref[pl.ds(r0, RC_CONV), :] = (y * jax.nn.sigmoid(y)).astype(BF16)
        return carry

    lax.fori_loop(0, ts // RC_CONV, chunk, 0)
    ya_ref[0] = jnp.dot(act_ref[...], wout_ref[...], preferred_element_type=F32).astype(BF16)


def _conv_branch(u, conv_w, conv_b, ln_g, ln_b, w_out):
    bsz, s, _ = u.shape
    ts = TS_CONV
    nb = s // ts
    hb = ts // HALO
    return pl.pallas_call(
        _conv_kernel,
        grid=(bsz, nb),
        in_specs=[
            pl.BlockSpec((1, HALO, D_CONV), lambda b, i: (b, jnp.maximum(i * hb - 1, 0), 0)),
            pl.BlockSpec((1, ts, D_CONV), lambda b, i: (b, i, 0)),
            pl.BlockSpec((1, HALO, D_CONV), lambda b, i: (b, jnp.minimum((i + 1) * hb, s // HALO - 1), 0)),
            _full((CONV_WIDTH, D_CONV)), _full((1, D_CONV)), _full((1, D_CONV)), _full((1, D_CONV)),
            _full((D_CONV, D_MODEL)),
        ],
        out_specs=pl.BlockSpec((1, ts, D_MODEL), lambda b, i: (b, i, 0)),
        out_shape=jax.ShapeDtypeStruct((bsz, s, D_MODEL), BF16),
        scratch_shapes=[
            pltpu.VMEM((SUBLANES, ts + 2 * HALO, D_CONV), F32),
            pltpu.VMEM((ts, D_CONV), BF16),
        ],
        compiler_params=_cparams("parallel", "parallel"),
        name="conv_branch",
    )(u, u, u, conv_w, conv_b, ln_g, ln_b, w_out)


def _dft_tables(s):
    n2_len = s // DFT_N1
    k1_per_tile = DFT_N1 // n2_len
    two_pi = 2.0 * jnp.pi
    a = jnp.arange(DFT_N1, dtype=I32)
    ang1 = two_pi * ((a[:, None] * a[None, :]) % DFT_N1).astype(F32) / DFT_N1
    f1 = jnp.concatenate([jnp.cos(ang1), -jnp.sin(ang1)], axis=0).astype(BF16)
    cs = jnp.stack([jnp.cos(ang1), jnp.sin(ang1)]).astype(BF16)
    n_tiles = s // DFT_N1
    t = jnp.arange(n_tiles, dtype=I32)[:, None, None]
    row = jnp.arange(DFT_N1, dtype=I32)[None, :, None]
    col = jnp.arange(DFT_N1, dtype=I32)[None, None, :]
    k2, k1l_out = row // k1_per_tile, row % k1_per_tile
    k1l_in, n2 = col // n2_len, col % n2_len
    k = t * k1_per_tile + k1l_out + DFT_N1 * k2
    ang2 = two_pi * ((n2 * k) % s).astype(F32) / s
    hit = k1l_in == k1l_out
    gr = jnp.where(hit, jnp.cos(ang2), 0.0)
    gi = jnp.where(hit, -jnp.sin(ang2), 0.0)
    g = jnp.concatenate([jnp.concatenate([gr, -gi], axis=2),
                         jnp.concatenate([gi, gr], axis=2)], axis=1).astype(BF16)
    return f1, g, cs


def _fft1_kernel(x_ref, f_ref, y_ref):
    r = jnp.dot(f_ref[...], x_ref[0], preferred_element_type=F32)
    y_ref[0, 0] = r[:DFT_N1].astype(BF16)
    y_ref[0, 1] = r[DFT_N1:].astype(BF16)


def _fft2_kernel(y_ref, g_ref, cs_ref, w_ref, o_ref, *, scale):
    yb = y_ref[0].reshape(2 * DFT_N1, D_FNET)
    z = jnp.dot(g_ref[0], yb, preferred_element_type=F32)
    zr = z[:DFT_N1].astype(BF16)
    zi = z[DFT_N1:].astype(BF16)
    parts = []
    for grp in range(FNET_GROUPS):
        sl = slice(grp * FNET_GROUP_DIM, (grp + 1) * FNET_GROUP_DIM)
        parts.append(jnp.dot(zr[:, sl], cs_ref[0], preferred_element_type=F32)
                     + jnp.dot(zi[:, sl], cs_ref[1], preferred_element_type=F32))
    fm = (jnp.concatenate(parts, axis=1) * scale).astype(BF16)
    out = jnp.dot(fm, w_ref[...], preferred_element_type=F32)
    o_ref[...] = out.reshape(o_ref.shape).astype(o_ref.dtype)


def _fourier_branch(f_in, w_out):
    bsz, s, _ = f_in.shape
    n2_len = s // DFT_N1
    k1_per_tile = DFT_N1 // n2_len
    cols = n2_len * D_FNET
    cb = min(CB_FFT1, cols)
    f1, g, cs = _dft_tables(s)
    y = pl.pallas_call(
        _fft1_kernel,
        grid=(bsz, cols // cb),
        in_specs=[pl.BlockSpec((1, DFT_N1, cb), lambda b, j: (b, 0, j)), _full((2 * DFT_N1, DFT_N1))],
        out_specs=pl.BlockSpec((1, 2, DFT_N1, cb), lambda b, j: (b, 0, 0, j)),
        out_shape=jax.ShapeDtypeStruct((bsz, 2, DFT_N1, cols), BF16),
        compiler_params=_cparams("parallel", "parallel"),
        name="fft_stage1",
    )(f_in.reshape(bsz, DFT_N1, cols), f1)
    y = y.reshape(bsz, 2, s, D_FNET)
    scale = float((s * FNET_GROUP_DIM) ** -0.5)
    if k1_per_tile == 1:
        out_shape = (bsz, n2_len, DFT_N1 * D_MODEL)
        out_spec = pl.BlockSpec((1, n2_len, D_MODEL), lambda b, t: (b, 0, t))
    else:
        out_shape = (bsz, n2_len, DFT_N1 // k1_per_tile, k1_per_tile, D_MODEL)
        out_spec = pl.BlockSpec((1, n2_len, 1, k1_per_tile, D_MODEL), lambda b, t: (b, 0, t, 0, 0))
    yb = pl.pallas_call(
        functools.partial(_fft2_kernel, scale=scale),
        grid=(bsz, s // DFT_N1),
        in_specs=[
            pl.BlockSpec((1, 2, DFT_N1, D_FNET), lambda b, t: (b, 0, t, 0)),
            pl.BlockSpec((1, 2 * DFT_N1, 2 * DFT_N1), lambda b, t: (t, 0, 0)),
            _full((2, DFT_N1, DFT_N1)),
            _full((D_FNET, D_MODEL)),
        ],
        out_specs=out_spec,
        out_shape=jax.ShapeDtypeStruct(out_shape, BF16),
        compiler_params=_cparams("parallel", "parallel"),
        name="fft_stage2",
    )(y, g, cs, w_out)
    return yb.reshape(bsz * s, D_MODEL)


def _mix_kernel(x_ref, ya_ref, yb_ref, cin_ref, ln0g_ref, ln0b_ref, wg_ref, bg_ref, wo_ref, bo_ref,
                ln1g_ref, ln1b_ref, wr_ref, br_ref,
                h1_ref, idx_ref, gate_ref, rank_ref, cnt_ref, carry_ref):
    tm = x_ref.shape[0]
    sub = min(SUB_MIX, tm)

    @pl.when(pl.program_id(0) == 0)
    def _():
        carry_ref[...] = cin_ref[...]

    carry = carry_ref[...]
    eio = lax.broadcasted_iota(I32, (N_EXPERTS, sub), 0)
    before = (lax.broadcasted_iota(I32, (sub, sub), 0) < lax.broadcasted_iota(I32, (sub, sub), 1)).astype(BF16)
    for r0 in range(0, tm, sub):
        rows = slice(r0, r0 + sub)
        h = _layer_norm(x_ref[rows, :], ln0g_ref[...], ln0b_ref[...])
        zg = jnp.dot(h.astype(BF16), wg_ref[...], preferred_element_type=F32) + bg_ref[...]
        m = (jax.nn.sigmoid(zg[:, :D_MODEL]) * ya_ref[rows, :].astype(F32)
             + jax.nn.sigmoid(zg[:, D_MODEL:]) * yb_ref[rows, :].astype(F32))
        mix = jnp.dot(m.astype(BF16), wo_ref[...], preferred_element_type=F32) + bo_ref[...]
        h1 = _layer_norm(DEEPNORM_ALPHA * h + mix, ln1g_ref[...], ln1b_ref[...])
        _to_row_tiles(h1_ref, r0, h1)

        logits = lax.dot_general(wr_ref[...], h1, (((1,), (1,)), ((), ())),
                                 precision=lax.Precision.HIGHEST, preferred_element_type=F32) + br_ref[...]
        vals, idxs = [], []
        cur = logits
        for _ in range(TOP_K):
            mx = jnp.max(cur, axis=0, keepdims=True)
            ik = jnp.min(jnp.where(cur == mx, eio, N_EXPERTS), axis=0, keepdims=True)
            vals.append(mx)
            idxs.append(ik)
            cur = jnp.where(eio == ik, -jnp.inf, cur)
        exps = [jnp.exp(v - vals[0]) for v in vals]
        den = exps[0] + exps[1] + exps[2] + exps[3]
        gate_ref[:, rows] = jnp.concatenate([e / den for e in exps], axis=0)
        idx_ref[:, rows] = jnp.concatenate(idxs, axis=0)

        hot = jnp.zeros((N_EXPERTS, sub), F32)
        for ik in idxs:
            hot = hot + (eio == ik).astype(F32)
        prior = jnp.dot(hot.astype(BF16), before, preferred_element_type=F32) + carry[:, 0:1]
        ranks = [jnp.sum(jnp.where(eio == ik, prior, 0.0), axis=0, keepdims=True) for ik in idxs]
        rank_ref[:, rows] = jnp.concatenate(ranks, axis=0).astype(I32)
        carry = carry + jnp.sum(hot, axis=1, keepdims=True)
    carry_ref[...] = carry
    cnt_ref[...] = carry


def _mix_route(x, ya, yb, carry_in, ln0_g, ln0_b, w_gate, b_gate, w_o, b_o, ln1_g, ln1_b, w_rt, b_rt):
    t = x.shape[0]
    tm = TM_MIX
    row = lambda d: pl.BlockSpec((tm, d), lambda i: (i, 0))
    col = lambda r: pl.BlockSpec((r, tm), lambda i: (0, i))
    return pl.pallas_call(
        _mix_kernel,
        grid=(t // tm,),
        in_specs=[
            row(D_MODEL), row(D_MODEL), row(D_MODEL), _full((N_EXPERTS, LANES)),
            _full((1, D_MODEL)), _full((1, D_MODEL)),
            _full((D_MODEL, 2 * D_MODEL)), _full((1, 2 * D_MODEL)),
            _full((D_MODEL, D_MODEL)), _full((1, D_MODEL)),
            _full((1, D_MODEL)), _full((1, D_MODEL)),
            _full((N_EXPERTS, D_MODEL)), _full((N_EXPERTS, 1)),
        ],
        out_specs=[pl.BlockSpec((tm * SUBLANES, LANES), lambda i: (i, 0)), col(TOP_K), col(TOP_K), col(TOP_K),
                   _full((N_EXPERTS, LANES))],
        out_shape=[
            jax.ShapeDtypeStruct((t * SUBLANES, LANES), F32),
            jax.ShapeDtypeStruct((TOP_K, t), I32),
            jax.ShapeDtypeStruct((TOP_K, t), F32),
            jax.ShapeDtypeStruct((TOP_K, t), I32),
            jax.ShapeDtypeStruct((N_EXPERTS, LANES), F32),
        ],
        scratch_shapes=[pltpu.VMEM((N_EXPERTS, LANES), F32)],
        compiler_params=_cparams("arbitrary"),
        name="mix_route",
    )(x, ya, yb, carry_in, ln0_g, ln0_b, w_gate, b_gate, w_o, b_o, ln1_g, ln1_b, w_rt, b_rt)


def _row_copy(src_ref, src_row, dst_ref, dst_row, sem):
    return pltpu.make_async_copy(src_ref.at[src_row], dst_ref.at[dst_row], sem)


def _fetch_slots(slots_hbm, slots_smem, sem, step, slot):
    n = slots_hbm.shape[1]
    return pltpu.make_async_copy(slots_hbm.at[step], slots_smem.at[pl.ds(pl.multiple_of(slot * n, n), n)],
                                 sem.at[slot])


def _pad_fill(h_ref, xs_out, pad_smem, pad_sem, n_pad_runs, wait):
    def go(cp):
        if wait:
            cp.wait()
        else:
            cp.start()

    def run(j, carry):
        start = pad_smem[2 * j]
        length = pad_smem[2 * j + 1]
        for bit in range(BM_MOE.bit_length() - 1):
            size = 1 << bit

            @pl.when((length & size) != 0)
            def _():
                off = start + (length & ~(2 * size - 1))
                go(pltpu.make_async_copy(h_ref.at[pl.ds(0, size)], xs_out.at[pl.ds(off, size)], pad_sem))
        return carry

    lax.fori_loop(0, n_pad_runs, run, 0)

    tm = h_ref.shape[0]

    def dead_block(j, carry):
        for part in range(BM_MOE // tm):
            off = j * BM_MOE + part * tm
            go(pltpu.make_async_copy(h_ref, xs_out.at[pl.ds(off, tm)], pad_sem))
        return carry

    lax.fori_loop(pad_smem[2 * n_pad_runs], xs_out.shape[0] // BM_MOE, dead_block, 0)


def _dispatch_kernel(h_ref, slots_hbm, pad_hbm, *rest, n_pad_runs, aliased):
    xs_out, slots_smem, pad_smem, slot_sem, pad_sem, row_sem = rest[1:] if aliased else rest
    i = pl.program_id(0)
    n = pl.num_programs(0)
    tm = h_ref.shape[0]
    cur = lax.rem(i, 2)

    @pl.when(i == 0)
    def _():
        _fetch_slots(slots_hbm, slots_smem, slot_sem, 0, 0).start()

    @pl.when(i + 1 < n)
    def _():
        _fetch_slots(slots_hbm, slots_smem, slot_sem, i + 1, 1 - cur).start()

    if n_pad_runs:
        @pl.when(i == 0)
        def _():
            fetch = pltpu.make_async_copy(pad_hbm, pad_smem, pad_sem)
            fetch.start()
            fetch.wait()
            _pad_fill(h_ref, xs_out, pad_smem, pad_sem, n_pad_runs, wait=False)
            _pad_fill(h_ref, xs_out, pad_smem, pad_sem, n_pad_runs, wait=True)

    _fetch_slots(slots_hbm, slots_smem, slot_sem, i, cur).wait()

    base = cur * (TOP_K * tm)

    def issue(r, carry):
        for k in range(TOP_K):
            _row_copy(h_ref, r, xs_out, slots_smem[base + TOP_K * r + k], row_sem).start(priority=k % 2)
        return carry

    lax.fori_loop(0, tm, issue, 0, unroll=ISSUE_UNROLL)
    for _ in range(TOP_K):
        pltpu.make_async_copy(h_ref, xs_out.at[pl.ds(0, tm)], row_sem).wait()


def _dispatch(h1, slots, pad_runs, xs, n_rows):
    h1 = h1.reshape(-1, SUBLANES, LANES)
    t = h1.shape[0]
    tm = TM_ROWS
    aliased = xs is not None
    n_pad_runs = 0 if aliased else N_EXPERTS
    operands = (h1, slots, pad_runs) + ((xs,) if aliased else ())
    return pl.pallas_call(
        functools.partial(_dispatch_kernel, n_pad_runs=n_pad_runs, aliased=aliased),
        grid=(t // tm,),
        in_specs=[pl.BlockSpec((tm, SUBLANES, LANES), lambda i: (i, 0, 0))]
        + [pl.BlockSpec(memory_space=pl.ANY)] * (len(operands) - 1),
        out_specs=pl.BlockSpec(memory_space=pl.ANY),
        out_shape=jax.ShapeDtypeStruct((n_rows, SUBLANES, LANES), F32),
        input_output_aliases={3: 0} if aliased else {},
        scratch_shapes=[
            pltpu.SMEM((2 * TOP_K * tm,), I32),
            pltpu.SMEM((pad_runs.shape[0],), I32),
            pltpu.SemaphoreType.DMA((2,)),
            pltpu.SemaphoreType.DMA,
            pltpu.SemaphoreType.DMA,
        ],
        compiler_params=_cparams("arbitrary"),
        name="dispatch",
    )(*operands)


def _moe_kernel(blk_e_ref, nb_ref, x_ref, wgu_ref, bgu_ref, wd_ref, bd_ref, o_ref, wgu_b, wd_b):
    i = pl.program_id(0)
    is_live = i < nb_ref[0]

    @pl.when(jnp.logical_not(is_live))
    def _():
        o_ref[...] = jnp.zeros(o_ref.shape, F32)

    @pl.when(jnp.logical_or(i == 0, blk_e_ref[i] != blk_e_ref[jnp.maximum(i - 1, 0)]))
    def _():
        for r in range(0, D_MODEL, CAST_ROWS):
            wgu_b[r:r + CAST_ROWS, :] = wgu_ref[0, r:r + CAST_ROWS, :].astype(BF16)
        for r in range(0, D_FF, CAST_ROWS):
            wd_b[r:r + CAST_ROWS, :] = wd_ref[0, r:r + CAST_ROWS, :].astype(BF16)

    @pl.when(is_live)
    def _():
        x = _from_row_tiles(x_ref, 0, BM_MOE).astype(BF16)
        acc = jnp.zeros((BM_MOE, D_MODEL), F32) + bd_ref[0]
        for c in range(D_FF // FF_CHUNK):
            lo = c * FF_CHUNK
            g = jnp.dot(x, wgu_b[:, lo:lo + FF_CHUNK], preferred_element_type=F32) + bgu_ref[0, :, lo:lo + FF_CHUNK]
            u = (jnp.dot(x, wgu_b[:, D_FF + lo:D_FF + lo + FF_CHUNK], preferred_element_type=F32)
                 + bgu_ref[0, :, D_FF + lo:D_FF + lo + FF_CHUNK])
            g = jnp.minimum(g, SWIGLU_LIMIT)
            u = jnp.clip(u, -SWIGLU_LIMIT, SWIGLU_LIMIT)
            act = (u + 1.0) * (g * jax.nn.sigmoid(g * SWIGLU_ALPHA))
            acc = acc + jnp.dot(act.astype(BF16), wd_b[lo:lo + FF_CHUNK, :], preferred_element_type=F32)
        _to_row_tiles(o_ref, 0, acc)


def _moe_blocks(xs, blk_e, nb_used, w_gu, b_gu, w_down, b_down):
    p = xs.shape[0] // SUBLANES
    n_blocks = p // BM_MOE
    live = lambda i, nb: jnp.minimum(i, nb[0] - 1)
    grid_spec = pltpu.PrefetchScalarGridSpec(
        num_scalar_prefetch=2,
        grid=(n_blocks,),
        in_specs=[
            pl.BlockSpec((BM_MOE * SUBLANES, LANES), lambda i, be, nb: (live(i, nb), 0)),
            pl.BlockSpec((1, D_MODEL, 2 * D_FF), lambda i, be, nb: (be[i], 0, 0)),
            pl.BlockSpec((1, 1, 2 * D_FF), lambda i, be, nb: (be[i], 0, 0)),
            pl.BlockSpec((1, D_FF, D_MODEL), lambda i, be, nb: (be[i], 0, 0)),
            pl.BlockSpec((1, 1, D_MODEL), lambda i, be, nb: (be[i], 0, 0)),
        ],
        out_specs=pl.BlockSpec((BM_MOE * SUBLANES, LANES), lambda i, be, nb: (i, 0)),
        scratch_shapes=[pltpu.VMEM((D_MODEL, 2 * D_FF), BF16), pltpu.VMEM((D_FF, D_MODEL), BF16)],
    )
    return pl.pallas_call(
        _moe_kernel,
        grid_spec=grid_spec,
        out_shape=jax.ShapeDtypeStruct((p * SUBLANES, LANES), F32),
        compiler_params=pltpu.CompilerParams(dimension_semantics=("arbitrary",), vmem_limit_bytes=VMEM_LIMIT_MOE),
        name="moe_experts",
    )(blk_e, nb_used, xs, w_gu, b_gu, w_down, b_down)


def _final_kernel(h_ref, p_ref, gate_ref, slots_hbm, y_hbm, wpg_ref, bpg_ref, wpp_ref, g_ref, b_ref, o_ref,
                  slots_smem, rows_ref, slot_sem, row_sem):
    i = pl.program_id(0)
    n = pl.num_programs(0)
    tm = h_ref.shape[0] // SUBLANES
    cur = lax.rem(i, 2)

    @pl.when(i == 0)
    def _():
        _fetch_slots(slots_hbm, slots_smem, slot_sem, 0, 0).start()

    @pl.when(i + 1 < n)
    def _():
        _fetch_slots(slots_hbm, slots_smem, slot_sem, i + 1, 1 - cur).start()

    _fetch_slots(slots_hbm, slots_smem, slot_sem, i, cur).wait()

    base = cur * (TOP_K * tm)

    def issue(r, carry):
        for k in range(TOP_K):
            slot = slots_smem[base + TOP_K * r + k]
            src = y_hbm.at[pl.ds(pl.multiple_of(slot * SUBLANES, SUBLANES), SUBLANES)]
            dst = rows_ref.at[k, pl.ds(pl.multiple_of(r * SUBLANES, SUBLANES), SUBLANES)]
            pltpu.make_async_copy(src, dst, row_sem).start(priority=k % 2)
        return carry

    lax.fori_loop(0, tm, issue, 0, unroll=ISSUE_UNROLL)

    h = _from_row_tiles(h_ref, 0, tm)
    ple = (jax.nn.sigmoid(jnp.dot(h.astype(BF16), wpg_ref[...], preferred_element_type=F32) + bpg_ref[...])
           * jnp.dot(p_ref[...].astype(BF16), wpp_ref[...], preferred_element_type=F32))

    for k in range(TOP_K):
        pltpu.make_async_copy(y_hbm.at[pl.ds(0, tm * SUBLANES)], rows_ref.at[k], row_sem).wait()

    gates = gate_ref[...]
    ffn = gates[:, 0:1] * _from_row_tiles(rows_ref.at[0], 0, tm)
    for k in range(1, TOP_K):
        ffn = ffn + gates[:, k:k + 1] * _from_row_tiles(rows_ref.at[k], 0, tm)
    o_ref[...] = _layer_norm(DEEPNORM_ALPHA * h + (ffn + ple), g_ref[...], b_ref[...])


def _combine_final(h1, p, gates, slots, y, w_pg, b_pg, w_pp, ln_g, ln_b):
    t = h1.shape[0] // SUBLANES
    tm = TM_ROWS
    return pl.pallas_call(
        _final_kernel,
        grid=(t // tm,),
        in_specs=[
            pl.BlockSpec((tm * SUBLANES, LANES), lambda i: (i, 0)),
            pl.BlockSpec((tm, D_PLE), lambda i: (i, 0)),
            pl.BlockSpec((tm, TOP_K), lambda i: (i, 0)),
            pl.BlockSpec(memory_space=pl.ANY),
            pl.BlockSpec(memory_space=pl.ANY),
            _full((D_MODEL, D_MODEL)), _full((1, D_MODEL)), _full((D_PLE, D_MODEL)),
            _full((1, D_MODEL)), _full((1, D_MODEL)),
        ],
        out_specs=pl.BlockSpec((tm, D_MODEL), lambda i: (i, 0)),
        out_shape=jax.ShapeDtypeStruct((t, D_MODEL), F32),
        scratch_shapes=[
            pltpu.SMEM((2 * TOP_K * tm,), I32),
            pltpu.VMEM((TOP_K, tm * SUBLANES, LANES), F32),
            pltpu.SemaphoreType.DMA((2,)),
            pltpu.SemaphoreType.DMA,
        ],
        compiler_params=_cparams("arbitrary"),
        name="combine_final",
    )(h1, p, gates, slots, y, w_pg, b_pg, w_pp, ln_g, ln_b)


def _tile_slots(dest):
    t = dest.shape[1]
    return dest.T.reshape(t // TM_ROWS, TM_ROWS * TOP_K)


def kernel(x_prompt, x_sample, p_prompt, p_sample, ln0_g, ln0_b, w_in, b_in, conv_w, conv_b, lnc_g, lnc_b, w_conv_out, w_fnet_out, w_o, b_o, ln1_g, ln1_b, w_router, b_router, w_gu, b_gu, w_down, b_down, w_pg, b_pg, w_pp, ln2_g, ln2_b):
    row = lambda v: v.reshape(1, -1).astype(F32)
    n_branch = 2 * D_CONV + D_FNET
    w_in_b = w_in[0].astype(BF16)
    w_branch, w_gate = w_in_b[:, :n_branch], w_in_b[:, n_branch:]
    b_branch, b_gate = row(b_in[0, :n_branch]), row(b_in[0, n_branch:])
    w_conv_out_b = w_conv_out[0].astype(BF16)
    w_fnet_out_b = w_fnet_out[0].astype(BF16)
    w_o_b = w_o[0].astype(BF16)
    w_rt = w_router[0].T.astype(F32)
    b_rt = b_router[0].reshape(N_EXPERTS, 1).astype(F32)
    b_gu_r = b_gu[0].reshape(N_EXPERTS, 1, 2 * D_FF).astype(F32)
    b_down_r = b_down[0].reshape(N_EXPERTS, 1, D_MODEL).astype(F32)
    w_pg_b = w_pg[0].astype(BF16)
    w_pp_b = w_pp[0].astype(BF16)

    carry = jnp.zeros((N_EXPERTS, LANES), F32)
    routed = []
    for x, p in ((x_prompt, p_prompt[0]), (x_sample, p_sample[0])):
        bsz, s, _ = x.shape
        xt = x.reshape(bsz * s, D_MODEL)
        u, f_in = _inproj(xt, row(ln0_g), row(ln0_b), w_branch, b_branch)
        ya = _conv_branch(u.reshape(bsz, s, D_CONV), conv_w[0].astype(F32), row(conv_b[0]), row(lnc_g[0]),
                          row(lnc_b[0]), w_conv_out_b).reshape(bsz * s, D_MODEL)
        yb = _fourier_branch(f_in.reshape(bsz, s, D_FNET), w_fnet_out_b)
        h1, idx, gates, rank, carry = _mix_route(
            xt, ya, yb, carry, row(ln0_g), row(ln0_b), w_gate, b_gate, w_o_b, row(b_o[0]),
            row(ln1_g[0]), row(ln1_b[0]), w_rt, b_rt)
        routed.append((h1, idx, gates, rank, p.reshape(bsz * s, D_PLE), (bsz, s)))

    n_assign = sum(r[1].shape[1] for r in routed) * TOP_K
    n_blocks = (n_assign + N_EXPERTS * (BM_MOE - 1)) // BM_MOE
    counts = carry[:, 0].astype(I32)
    padded = ((counts + BM_MOE - 1) // BM_MOE) * BM_MOE
    ends = jnp.cumsum(padded)
    start = ends - padded
    nb_used = (ends[-1] // BM_MOE).reshape(1).astype(I32)
    blk = jnp.minimum(jnp.arange(n_blocks, dtype=I32), nb_used[0] - 1)
    blk_e = jnp.minimum(jnp.sum((ends[None, :] <= (blk * BM_MOE)[:, None]).astype(I32), axis=1), N_EXPERTS - 1)
    pad_runs = jnp.concatenate([jnp.stack([start + counts, padded - counts], axis=1).reshape(-1), nb_used,
                                jnp.zeros((LANES - 2 * N_EXPERTS - 1,), I32)]).astype(I32)

    xs = None
    slot_tables = []
    for h1, idx, _, rank, _, _ in routed:
        first = jnp.sum(jnp.where(idx[None] == jnp.arange(N_EXPERTS, dtype=I32)[:, None, None],
                                  start[:, None, None], 0), axis=0)
        slots = _tile_slots(first + rank)
        slot_tables.append(slots)
        xs = _dispatch(h1, slots, pad_runs, xs, n_blocks * BM_MOE)

    y = _moe_blocks(xs.reshape(-1, LANES), blk_e, nb_used, w_gu.reshape(N_EXPERTS, D_MODEL, 2 * D_FF), b_gu_r,
                    w_down.reshape(N_EXPERTS, D_FF, D_MODEL), b_down_r)

    outs = []
    for (h1, _, gates, _, p, (bsz, s)), slots in zip(routed, slot_tables):
        o = _combine_final(h1, p, gates.T, slots, y, w_pg_b, row(b_pg[0]), w_pp_b, row(ln2_g[0]), row(ln2_b[0]))
        outs.append(o.reshape(bsz, s, D_MODEL))
    return tuple(outs)
```

```python
import functools

import jax
import jax.numpy as jnp
from jax import lax
from jax.experimental import pallas as pl
from jax.experimental.pallas import tpu as pltpu

F32 = jnp.float32
BF16 = jnp.bfloat16
I32 = jnp.int32

D_MODEL = 1024
D_CONV = D_MODEL // 2
CONV_WIDTH = 31
CONV_PAD = CONV_WIDTH // 2
D_FNET = D_MODEL // 2
FNET_GROUPS = 4
FNET_GROUP_DIM = D_FNET // FNET_GROUPS
D_PLE = 256
N_EXPERTS = 32
TOP_K = 4
D_FF = D_MODEL
SWIGLU_ALPHA = 1.702
SWIGLU_LIMIT = 7.0
LN_EPS = 1e-5
DEPTH = 1
DEEPNORM_ALPHA = (2 * DEPTH) ** 0.25

LANES = 128
SUBLANES = 8
DFT_N1 = 128
HALO = 16

TM_INPROJ = 512
TS_CONV = 512
RC_CONV = 128
CB_FFT1 = 4096
TM_MIX = 512
SUB_MIX = 256
TM_ROWS = 256
BM_MOE = 512
FF_CHUNK = 512
ISSUE_UNROLL = 8
CAST_ROWS = 128
VMEM_LIMIT = 48 * 1024 * 1024
VMEM_LIMIT_MOE = 56 * 1024 * 1024


def _cparams(*sem):
    return pltpu.CompilerParams(dimension_semantics=sem, vmem_limit_bytes=VMEM_LIMIT)


def _layer_norm(x, g, b):
    mu = jnp.mean(x, axis=-1, keepdims=True)
    xc = x - mu
    var = jnp.mean(xc * xc, axis=-1, keepdims=True)
    return xc * lax.rsqrt(var + LN_EPS) * g + b


def _full(shape):
    return pl.BlockSpec(shape, lambda *_: (0,) * len(shape))


def _to_row_tiles(ref, row0, x):
    rows = x.shape[0]
    for j in range(D_MODEL // LANES):
        ref[pl.ds(row0 * SUBLANES + j, rows, stride=SUBLANES), :] = x[:, j * LANES:(j + 1) * LANES]


def _from_row_tiles(ref, row0, rows):
    return jnp.concatenate([ref[pl.ds(row0 * SUBLANES + j, rows, stride=SUBLANES), :]
                            for j in range(D_MODEL // LANES)], axis=1)


def _inproj_kernel(x_ref, g_ref, b_ref, w_ref, bias_ref, u_ref, f_ref):
    h = _layer_norm(x_ref[...], g_ref[...], b_ref[...])
    z = jnp.dot(h.astype(BF16), w_ref[...], preferred_element_type=F32) + bias_ref[...]
    u_ref[...] = (z[:, :D_CONV] * jax.nn.sigmoid(z[:, D_CONV:2 * D_CONV])).astype(BF16)
    f_ref[...] = z[:, 2 * D_CONV:].astype(BF16)


def _inproj(x, ln_g, ln_b, w, bias):
    t = x.shape[0]
    n_out = w.shape[1]
    return pl.pallas_call(
        _inproj_kernel,
        grid=(t // TM_INPROJ,),
        in_specs=[
            pl.BlockSpec((TM_INPROJ, D_MODEL), lambda i: (i, 0)),
            _full((1, D_MODEL)), _full((1, D_MODEL)),
            _full((D_MODEL, n_out)), _full((1, n_out)),
        ],
        out_specs=[
            pl.BlockSpec((TM_INPROJ, D_CONV), lambda i: (i, 0)),
            pl.BlockSpec((TM_INPROJ, D_FNET), lambda i: (i, 0)),
        ],
        out_shape=[jax.ShapeDtypeStruct((t, D_CONV), BF16), jax.ShapeDtypeStruct((t, D_FNET), BF16)],
        compiler_params=_cparams("parallel"),
        name="inproj",
    )(x, ln_g, ln_b, w, bias)


def _conv_kernel(prev_ref, cur_ref, next_ref, cw_ref, cb_ref, g_ref, b_ref, wout_ref, ya_ref,
                 ext_ref, act_ref):
    i = pl.program_id(1)
    last = pl.num_programs(1) - 1
    ts = cur_ref.shape[1]
    ext_rows = ts + 2 * HALO
    zero = jnp.zeros((HALO, D_CONV), F32)
    ext_ref[0, 0:HALO, :] = jnp.where(i > 0, prev_ref[0].astype(F32), zero)
    ext_ref[0, HALO:HALO + ts, :] = cur_ref[0].astype(F32)
    ext_ref[0, HALO + ts:ext_rows, :] = jnp.where(i < last, next_ref[0].astype(F32), zero)
    for r in range(1, SUBLANES):
        ext_ref[r, 0:ext_rows - SUBLANES, :] = ext_ref[0, r:r + ext_rows - SUBLANES, :]

    cb = cb_ref[...]
    g = g_ref[...]
    b = b_ref[...]

    def chunk(c, carry):
        r0 = pl.multiple_of(c * RC_CONV, RC_CONV)
        acc = jnp.zeros((RC_CONV, D_CONV), F32) + cb
        for j in range(CONV_WIDTH):
            off = j + HALO - CONV_PAD
            q, r = divmod(off, SUBLANES)
            acc = acc + cw_ref[j:j + 1, :] * ext_ref[r, pl.ds(r0 + q * SUBLANES, RC_CONV), :]
        y = _layer_norm(acc, g, b)
        act_ref[pl.ds(r0, RC_CONV), :] = (y * jax.nn.sigmoid(y)).astype(BF16)
        return carry

    lax.fori_loop(0, ts // RC_CONV, chunk, 0)
    ya_ref[0] = jnp.dot(act_ref[...], wout_ref[...], preferred_element_type=F32).astype(BF16)


def _conv_branch(u, conv_w, conv_b, ln_g, ln_b, w_out):
    bsz, s, _ = u.shape
    ts = TS_CONV
    nb = s // ts
    hb = ts // HALO
    return pl.pallas_call(
        _conv_kernel,
        grid=(bsz, nb),
        in_specs=[
            pl.BlockSpec((1, HALO, D_CONV), lambda b, i: (b, jnp.maximum(i * hb - 1, 0), 0)),
            pl.BlockSpec((1, ts, D_CONV), lambda b, i: (b, i, 0)),
            pl.BlockSpec((1, HALO, D_CONV), lambda b, i: (b, jnp.minimum((i + 1) * hb, s // HALO - 1), 0)),
            _full((CONV_WIDTH, D_CONV)), _full((1, D_CONV)), _full((1, D_CONV)), _full((1, D_CONV)),
            _full((D_CONV, D_MODEL)),
        ],
        out_specs=pl.BlockSpec((1, ts, D_MODEL), lambda b, i: (b, i, 0)),
        out_shape=jax.ShapeDtypeStruct((bsz, s, D_MODEL), BF16),
        scratch_shapes=[
            pltpu.VMEM((SUBLANES, ts + 2 * HALO, D_CONV), F32),
            pltpu.VMEM((ts, D_CONV), BF16),
        ],
        compiler_params=_cparams("parallel", "parallel"),
        name="conv_branch",
    )(u, u, u, conv_w, conv_b, ln_g, ln_b, w_out)


def _dft_tables(s):
    n2_len = s // DFT_N1
    k1_per_tile = DFT_N1 // n2_len
    two_pi = 2.0 * jnp.pi
    a = jnp.arange(DFT_N1, dtype=I32)
    ang1 = two_pi * ((a[:, None] * a[None, :]) % DFT_N1).astype(F32) / DFT_N1
    f1 = jnp.concatenate([jnp.cos(ang1), -jnp.sin(ang1)], axis=0).astype(BF16)
    cs = jnp.stack([jnp.cos(ang1), jnp.sin(ang1)]).astype(BF16)
    n_tiles = s // DFT_N1
    t = jnp.arange(n_tiles, dtype=I32)[:, None, None]
    row = jnp.arange(DFT_N1, dtype=I32)[None, :, None]
    col = jnp.arange(DFT_N1, dtype=I32)[None, None, :]
    k2, k1l_out = row // k1_per_tile, row % k1_per_tile
    k1l_in, n2 = col // n2_len, col % n2_len
    k = t * k1_per_tile + k1l_out + DFT_N1 * k2
    ang2 = two_pi * ((n2 * k) % s).astype(F32) / s
    hit = k1l_in == k1l_out
    gr = jnp.where(hit, jnp.cos(ang2), 0.0)
    gi = jnp.where(hit, -jnp.sin(ang2), 0.0)
    g = jnp.concatenate([jnp.concatenate([gr, -gi], axis=2),
                         jnp.concatenate([gi, gr], axis=2)], axis=1).astype(BF16)
    return f1, g, cs


def _fft1_kernel(x_ref, f_ref, y_ref):
    r = jnp.dot(f_ref[...], x_ref[0], preferred_element_type=F32)
    y_ref[0, 0] = r[:DFT_N1].astype(BF16)
    y_ref[0, 1] = r[DFT_N1:].astype(BF16)


def _fft2_kernel(y_ref, g_ref, cs_ref, w_ref, o_ref, *, scale):
    yb = y_ref[0].reshape(2 * DFT_N1, D_FNET)
    z = jnp.dot(g_ref[0], yb, preferred_element_type=F32)
    zr = z[:DFT_N1].astype(BF16)
    zi = z[DFT_N1:].astype(BF16)
    parts = []
    for grp in range(FNET_GROUPS):
        sl = slice(grp * FNET_GROUP_DIM, (grp + 1) * FNET_GROUP_DIM)
        parts.append(jnp.dot(zr[:, sl], cs_ref[0], preferred_element_type=F32)
                     + jnp.dot(zi[:, sl], cs_ref[1], preferred_element_type=F32))
    fm = (jnp.concatenate(parts, axis=1) * scale).astype(BF16)
    out = jnp.dot(fm, w_ref[...], preferred_element_type=F32)
    o_ref[...] = out.reshape(o_ref.shape).astype(o_ref.dtype)


def _fourier_branch(f_in, w_out):
    bsz, s, _ = f_in.shape
    n2_len = s // DFT_N1
    k1_per_tile = DFT_N1 // n2_len
    cols = n2_len * D_FNET
    cb = min(CB_FFT1, cols)
    f1, g, cs = _dft_tables(s)
    y = pl.pallas_call(
        _fft1_kernel,
        grid=(bsz, cols // cb),
        in_specs=[pl.BlockSpec((1, DFT_N1, cb), lambda b, j: (b, 0, j)), _full((2 * DFT_N1, DFT_N1))],
        out_specs=pl.BlockSpec((1, 2, DFT_N1, cb), lambda b, j: (b, 0, 0, j)),
        out_shape=jax.ShapeDtypeStruct((bsz, 2, DFT_N1, cols), BF16),
        compiler_params=_cparams("parallel", "parallel"),
        name="fft_stage1",
    )(f_in.reshape(bsz, DFT_N1, cols), f1)
    y = y.reshape(bsz, 2, s, D_FNET)
    scale = float((s * FNET_GROUP_DIM) ** -0.5)
    if k1_per_tile == 1:
        out_shape = (bsz, n2_len, DFT_N1 * D_MODEL)
        out_spec = pl.BlockSpec((1, n2_len, D_MODEL), lambda b, t: (b, 0, t))
    else:
        out_shape = (bsz, n2_len, DFT_N1 // k1_per_tile, k1_per_tile, D_MODEL)
        out_spec = pl.BlockSpec((1, n2_len, 1, k1_per_tile, D_MODEL), lambda b, t: (b, 0, t, 0, 0))
    yb = pl.pallas_call(
        functools.partial(_fft2_kernel, scale=scale),
        grid=(bsz, s // DFT_N1),
        in_specs=[
            pl.BlockSpec((1, 2, DFT_N1, D_FNET), lambda b, t: (b, 0, t, 0)),
            pl.BlockSpec((1, 2 * DFT_N1, 2 * DFT_N1), lambda b, t: (t, 0, 0)),
            _full((2, DFT_N1, DFT_N1)),
            _full((D_FNET, D_MODEL)),
        ],
        out_specs=out_spec,
        out_shape=jax.ShapeDtypeStruct(out_shape, BF16),
        compiler_params=_cparams("parallel", "parallel"),
        name="fft_stage2",
    )(y, g, cs, w_out)
    return yb.reshape(bsz * s, D_MODEL)


def _mix_kernel(x_ref, ya_ref, yb_ref, cin_ref, ln0g_ref, ln0b_ref, wg_ref, bg_ref, wo_ref, bo_ref,
                ln1g_ref, ln1b_ref, wr_ref, br_ref,
                h1_ref, idx_ref, gate_ref, rank_ref, cnt_ref, carry_ref):
    tm = x_ref.shape[0]
    sub = min(SUB_MIX, tm)

    @pl.when(pl.program_id(0) == 0)
    def _():
        carry_ref[...] = cin_ref[...]

    carry = carry_ref[...]
    eio = lax.broadcasted_iota(I32, (N_EXPERTS, sub), 0)
    before = (lax.broadcasted_iota(I32, (sub, sub), 0) < lax.broadcasted_iota(I32, (sub, sub), 1)).astype(BF16)
    for r0 in range(0, tm, sub):
        rows = slice(r0, r0 + sub)
        h = _layer_norm(x_ref[rows, :], ln0g_ref[...], ln0b_ref[...])
        zg = jnp.dot(h.astype(BF16), wg_ref[...], preferred_element_type=F32) + bg_ref[...]
        m = (jax.nn.sigmoid(zg[:, :D_MODEL]) * ya_ref[rows, :].astype(F32)
             + jax.nn.sigmoid(zg[:, D_MODEL:]) * yb_ref[rows, :].astype(F32))
        mix = jnp.dot(m.astype(BF16), wo_ref[...], preferred_element_type=F32) + bo_ref[...]
        h1 = _layer_norm(DEEPNORM_ALPHA * h + mix, ln1g_ref[...], ln1b_ref[...])
        _to_row_tiles(h1_ref, r0, h1)

        h_hi = h1.astype(BF16)
        h_lo = (h1 - h_hi.astype(F32)).astype(BF16)
        hw = jnp.dot(h_hi, wr_ref[...], preferred_element_type=F32)
        lw = jnp.dot(h_lo, wr_ref[:, :LANES], preferred_element_type=F32)
        logits = (hw[:, :LANES] + hw[:, LANES:] + lw).T[:N_EXPERTS] + br_ref[...]
        vals, idxs = [], []
        cur = logits
        for _ in range(TOP_K):
            mx = jnp.max(cur, axis=0, keepdims=True)
            ik = jnp.min(jnp.where(cur == mx, eio, N_EXPERTS), axis=0, keepdims=True)
            vals.append(mx)
            idxs.append(ik)
            cur = jnp.where(eio == ik, -jnp.inf, cur)
        exps = [jnp.exp(v - vals[0]) for v in vals]
        den = exps[0] + exps[1] + exps[2] + exps[3]
        gate_ref[:, rows] = jnp.concatenate([e / den for e in exps], axis=0)
        idx_ref[:, rows] = jnp.concatenate(idxs, axis=0)

        hot = jnp.zeros((N_EXPERTS, sub), F32)
        for ik in idxs:
            hot = hot + (eio == ik).astype(F32)
        prior = jnp.dot(hot.astype(BF16), before, preferred_element_type=F32) + carry[:, 0:1]
        ranks = [jnp.sum(jnp.where(eio == ik, prior, 0.0), axis=0, keepdims=True) for ik in idxs]
        rank_ref[:, rows] = jnp.concatenate(ranks, axis=0).astype(I32)
        carry = carry + jnp.sum(hot, axis=1, keepdims=True)
    carry_ref[...] = carry
    cnt_ref[...] = carry


def _mix_route(x, ya, yb, carry_in, ln0_g, ln0_b, w_gate, b_gate, w_o, b_o, ln1_g, ln1_b, w_rt, b_rt):
    t = x.shape[0]
    tm = TM_MIX
    row = lambda d: pl.BlockSpec((tm, d), lambda i: (i, 0))
    col = lambda r: pl.BlockSpec((r, tm), lambda i: (0, i))
    return pl.pallas_call(
        _mix_kernel,
        grid=(t // tm,),
        in_specs=[
            row(D_MODEL), row(D_MODEL), row(D_MODEL), _full((N_EXPERTS, LANES)),
            _full((1, D_MODEL)), _full((1, D_MODEL)),
            _full((D_MODEL, 2 * D_MODEL)), _full((1, 2 * D_MODEL)),
            _full((D_MODEL, D_MODEL)), _full((1, D_MODEL)),
            _full((1, D_MODEL)), _full((1, D_MODEL)),
            _full((D_MODEL, 2 * LANES)), _full((N_EXPERTS, 1)),
        ],
        out_specs=[pl.BlockSpec((tm * SUBLANES, LANES), lambda i: (i, 0)), col(TOP_K), col(TOP_K), col(TOP_K),
                   _full((N_EXPERTS, LANES))],
        out_shape=[
            jax.ShapeDtypeStruct((t * SUBLANES, LANES), F32),
            jax.ShapeDtypeStruct((TOP_K, t), I32),
            jax.ShapeDtypeStruct((TOP_K, t), F32),
            jax.ShapeDtypeStruct((TOP_K, t), I32),
            jax.ShapeDtypeStruct((N_EXPERTS, LANES), F32),
        ],
        scratch_shapes=[pltpu.VMEM((N_EXPERTS, LANES), F32)],
        compiler_params=_cparams("arbitrary"),
        name="mix_route",
    )(x, ya, yb, carry_in, ln0_g, ln0_b, w_gate, b_gate, w_o, b_o, ln1_g, ln1_b, w_rt, b_rt)


def _row_copy(src_ref, src_row, dst_ref, dst_row, sem):
    return pltpu.make_async_copy(src_ref.at[src_row], dst_ref.at[dst_row], sem)


def _fetch_slots(slots_hbm, slots_smem, sem, step, slot):
    n = slots_hbm.shape[1]
    return pltpu.make_async_copy(slots_hbm.at[step], slots_smem.at[pl.ds(pl.multiple_of(slot * n, n), n)],
                                 sem.at[slot])


def _pad_fill(h_ref, xs_out, pad_smem, pad_sem, n_pad_runs, wait):
    def go(cp):
        if wait:
            cp.wait()
        else:
            cp.start()

    def run(j, carry):
        start = pad_smem[2 * j]
        length = pad_smem[2 * j + 1]
        for bit in range(BM_MOE.bit_length() - 1):
            size = 1 << bit

            @pl.when((length & size) != 0)
            def _():
                off = start + (length & ~(2 * size - 1))
                go(pltpu.make_async_copy(h_ref.at[pl.ds(0, size)], xs_out.at[pl.ds(off, size)], pad_sem))
        return carry

    lax.fori_loop(0, n_pad_runs, run, 0)

    tm = h_ref.shape[0]

    def dead_block(j, carry):
        for part in range(BM_MOE // tm):
            off = j * BM_MOE + part * tm
            go(pltpu.make_async_copy(h_ref, xs_out.at[pl.ds(off, tm)], pad_sem))
        return carry

    lax.fori_loop(pad_smem[2 * n_pad_runs], xs_out.shape[0] // BM_MOE, dead_block, 0)


def _dispatch_kernel(h_ref, slots_hbm, pad_hbm, *rest, n_pad_runs, aliased):
    xs_out, slots_smem, pad_smem, slot_sem, pad_sem, row_sem = rest[1:] if aliased else rest
    i = pl.program_id(0)
    n = pl.num_programs(0)
    tm = h_ref.shape[0]
    cur = lax.rem(i, 2)

    @pl.when(i == 0)
    def _():
        _fetch_slots(slots_hbm, slots_smem, slot_sem, 0, 0).start()

    @pl.when(i + 1 < n)
    def _():
        _fetch_slots(slots_hbm, slots_smem, slot_sem, i + 1, 1 - cur).start()

    if n_pad_runs:
        @pl.when(i == 0)
        def _():
            fetch = pltpu.make_async_copy(pad_hbm, pad_smem, pad_sem)
            fetch.start()
            fetch.wait()
            _pad_fill(h_ref, xs_out, pad_smem, pad_sem, n_pad_runs, wait=False)
            _pad_fill(h_ref, xs_out, pad_smem, pad_sem, n_pad_runs, wait=True)

    _fetch_slots(slots_hbm, slots_smem, slot_sem, i, cur).wait()

    base = cur * (TOP_K * tm)

    def issue(r, carry):
        for k in range(TOP_K):
            _row_copy(h_ref, r, xs_out, slots_smem[base + TOP_K * r + k], row_sem).start(priority=k % 2)
        return carry

    lax.fori_loop(0, tm, issue, 0, unroll=ISSUE_UNROLL)
    for _ in range(TOP_K):
        pltpu.make_async_copy(h_ref, xs_out.at[pl.ds(0, tm)], row_sem).wait()


def _dispatch(h1, slots, pad_runs, xs, n_rows):
    h1 = h1.reshape(-1, SUBLANES, LANES)
    t = h1.shape[0]
    tm = TM_ROWS
    aliased = xs is not None
    n_pad_runs = 0 if aliased else N_EXPERTS
    operands = (h1, slots, pad_runs) + ((xs,) if aliased else ())
    return pl.pallas_call(
        functools.partial(_dispatch_kernel, n_pad_runs=n_pad_runs, aliased=aliased),
        grid=(t // tm,),
        in_specs=[pl.BlockSpec((tm, SUBLANES, LANES), lambda i: (i, 0, 0))]
        + [pl.BlockSpec(memory_space=pl.ANY)] * (len(operands) - 1),
        out_specs=pl.BlockSpec(memory_space=pl.ANY),
        out_shape=jax.ShapeDtypeStruct((n_rows, SUBLANES, LANES), F32),
        input_output_aliases={3: 0} if aliased else {},
        scratch_shapes=[
            pltpu.SMEM((2 * TOP_K * tm,), I32),
            pltpu.SMEM((pad_runs.shape[0],), I32),
            pltpu.SemaphoreType.DMA((2,)),
            pltpu.SemaphoreType.DMA,
            pltpu.SemaphoreType.DMA,
        ],
        compiler_params=_cparams("arbitrary"),
        name="dispatch",
    )(*operands)


def _moe_kernel(blk_e_ref, nb_ref, x_ref, wgu_ref, bgu_ref, wd_ref, bd_ref, o_ref, wgu_b, wd_b):
    i = pl.program_id(0)
    is_live = i < nb_ref[0]

    @pl.when(jnp.logical_not(is_live))
    def _():
        o_ref[...] = jnp.zeros(o_ref.shape, F32)

    @pl.when(jnp.logical_or(i == 0, blk_e_ref[i] != blk_e_ref[jnp.maximum(i - 1, 0)]))
    def _():
        for r in range(0, D_MODEL, CAST_ROWS):
            wgu_b[r:r + CAST_ROWS, :] = wgu_ref[0, r:r + CAST_ROWS, :].astype(BF16)
        for r in range(0, D_FF, CAST_ROWS):
            wd_b[r:r + CAST_ROWS, :] = wd_ref[0, r:r + CAST_ROWS, :].astype(BF16)

    @pl.when(is_live)
    def _():
        x = _from_row_tiles(x_ref, 0, BM_MOE).astype(BF16)
        acc = jnp.zeros((BM_MOE, D_MODEL), F32) + bd_ref[0]
        for c in range(D_FF // FF_CHUNK):
            lo = c * FF_CHUNK
            g = jnp.dot(x, wgu_b[:, lo:lo + FF_CHUNK], preferred_element_type=F32) + bgu_ref[0, :, lo:lo + FF_CHUNK]
            u = (jnp.dot(x, wgu_b[:, D_FF + lo:D_FF + lo + FF_CHUNK], preferred_element_type=F32)
                 + bgu_ref[0, :, D_FF + lo:D_FF + lo + FF_CHUNK])
            g = jnp.minimum(g, SWIGLU_LIMIT)
            u = jnp.clip(u, -SWIGLU_LIMIT, SWIGLU_LIMIT)
            act = (u + 1.0) * (g * jax.nn.sigmoid(g * SWIGLU_ALPHA))
            acc = acc + jnp.dot(act.astype(BF16), wd_b[lo:lo + FF_CHUNK, :], preferred_element_type=F32)
        _to_row_tiles(o_ref, 0, acc)


def _moe_blocks(xs, blk_e, nb_used, w_gu, b_gu, w_down, b_down):
    p = xs.shape[0] // SUBLANES
    n_blocks = p // BM_MOE
    live = lambda i, nb: jnp.minimum(i, nb[0] - 1)
    grid_spec = pltpu.PrefetchScalarGridSpec(
        num_scalar_prefetch=2,
        grid=(n_blocks,),
        in_specs=[
            pl.BlockSpec((BM_MOE * SUBLANES, LANES), lambda i, be, nb: (live(i, nb), 0)),
            pl.BlockSpec((1, D_MODEL, 2 * D_FF), lambda i, be, nb: (be[i], 0, 0)),
            pl.BlockSpec((1, 1, 2 * D_FF), lambda i, be, nb: (be[i], 0, 0)),
            pl.BlockSpec((1, D_FF, D_MODEL), lambda i, be, nb: (be[i], 0, 0)),
            pl.BlockSpec((1, 1, D_MODEL), lambda i, be, nb: (be[i], 0, 0)),
        ],
        out_specs=pl.BlockSpec((BM_MOE * SUBLANES, LANES), lambda i, be, nb: (i, 0)),
        scratch_shapes=[pltpu.VMEM((D_MODEL, 2 * D_FF), BF16), pltpu.VMEM((D_FF, D_MODEL), BF16)],
    )
    return pl.pallas_call(
        _moe_kernel,
        grid_spec=grid_spec,
        out_shape=jax.ShapeDtypeStruct((p * SUBLANES, LANES), F32),
        compiler_params=pltpu.CompilerParams(dimension_semantics=("arbitrary",), vmem_limit_bytes=VMEM_LIMIT_MOE),
        name="moe_experts",
    )(blk_e, nb_used, xs, w_gu, b_gu, w_down, b_down)


def _final_kernel(h_ref, p_ref, gate_ref, slots_hbm, y_hbm, wpg_ref, bpg_ref, wpp_ref, g_ref, b_ref, o_ref,
                  slots_smem, rows_ref, slot_sem, row_sem):
    i = pl.program_id(0)
    n = pl.num_programs(0)
    tm = h_ref.shape[0] // SUBLANES
    cur = lax.rem(i, 2)

    @pl.when(i == 0)
    def _():
        _fetch_slots(slots_hbm, slots_smem, slot_sem, 0, 0).start()

    @pl.when(i + 1 < n)
    def _():
        _fetch_slots(slots_hbm, slots_smem, slot_sem, i + 1, 1 - cur).start()

    _fetch_slots(slots_hbm, slots_smem, slot_sem, i, cur).wait()

    base = cur * (TOP_K * tm)

    def issue(r, carry):
        for k in range(TOP_K):
            slot = slots_smem[base + TOP_K * r + k]
            src = y_hbm.at[pl.ds(pl.multiple_of(slot * SUBLANES, SUBLANES), SUBLANES)]
            dst = rows_ref.at[k, pl.ds(pl.multiple_of(r * SUBLANES, SUBLANES), SUBLANES)]
            pltpu.make_async_copy(src, dst, row_sem).start(priority=k % 2)
        return carry

    lax.fori_loop(0, tm, issue, 0, unroll=ISSUE_UNROLL)

    h = _from_row_tiles(h_ref, 0, tm)
    ple = (jax.nn.sigmoid(jnp.dot(h.astype(BF16), wpg_ref[...], preferred_element_type=F32) + bpg_ref[...])
           * jnp.dot(p_ref[...].astype(BF16), wpp_ref[...], preferred_element_type=F32))

    for k in range(TOP_K):
        pltpu.make_async_copy(y_hbm.at[pl.ds(0, tm * SUBLANES)], rows_ref.at[k], row_sem).wait()

    gates = gate_ref[...]
    ffn = gates[:, 0:1] * _from_row_tiles(rows_ref.at[0], 0, tm)
    for k in range(1, TOP_K):
        ffn = ffn + gates[:, k:k + 1] * _from_row_tiles(rows_ref.at[k], 0, tm)
    o_ref[...] = _layer_norm(DEEPNORM_ALPHA * h + (ffn + ple), g_ref[...], b_ref[...])


def _combine_final(h1, p, gates, slots, y, w_pg, b_pg, w_pp, ln_g, ln_b):
    t = h1.shape[0] // SUBLANES
    tm = TM_ROWS
    return pl.pallas_call(
        _final_kernel,
        grid=(t // tm,),
        in_specs=[
            pl.BlockSpec((tm * SUBLANES, LANES), lambda i: (i, 0)),
            pl.BlockSpec((tm, D_PLE), lambda i: (i, 0)),
            pl.BlockSpec((tm, TOP_K), lambda i: (i, 0)),
            pl.BlockSpec(memory_space=pl.ANY),
            pl.BlockSpec(memory_space=pl.ANY),
            _full((D_MODEL, D_MODEL)), _full((1, D_MODEL)), _full((D_PLE, D_MODEL)),
            _full((1, D_MODEL)), _full((1, D_MODEL)),
        ],
        out_specs=pl.BlockSpec((tm, D_MODEL), lambda i: (i, 0)),
        out_shape=jax.ShapeDtypeStruct((t, D_MODEL), F32),
        scratch_shapes=[
            pltpu.SMEM((2 * TOP_K * tm,), I32),
            pltpu.VMEM((TOP_K, tm * SUBLANES, LANES), F32),
            pltpu.SemaphoreType.DMA((2,)),
            pltpu.SemaphoreType.DMA,
        ],
        compiler_params=_cparams("arbitrary"),
        name="combine_final",
    )(h1, p, gates, slots, y, w_pg, b_pg, w_pp, ln_g, ln_b)


def _tile_slots(dest):
    t = dest.shape[1]
    return dest.T.reshape(t // TM_ROWS, TM_ROWS * TOP_K)


def kernel(x_prompt, x_sample, p_prompt, p_sample, ln0_g, ln0_b, w_in, b_in, conv_w, conv_b, lnc_g, lnc_b, w_conv_out, w_fnet_out, w_o, b_o, ln1_g, ln1_b, w_router, b_router, w_gu, b_gu, w_down, b_down, w_pg, b_pg, w_pp, ln2_g, ln2_b):
    row = lambda v: v.reshape(1, -1).astype(F32)
    n_branch = 2 * D_CONV + D_FNET
    w_in_b = w_in[0].astype(BF16)
    w_branch, w_gate = w_in_b[:, :n_branch], w_in_b[:, n_branch:]
    b_branch, b_gate = row(b_in[0, :n_branch]), row(b_in[0, n_branch:])
    w_conv_out_b = w_conv_out[0].astype(BF16)
    w_fnet_out_b = w_fnet_out[0].astype(BF16)
    w_o_b = w_o[0].astype(BF16)
    w_rt_f = jnp.pad(w_router[0].astype(F32), ((0, 0), (0, LANES - N_EXPERTS)))
    w_rt_hi = w_rt_f.astype(BF16)
    w_rt = jnp.concatenate([w_rt_hi, (w_rt_f - w_rt_hi.astype(F32)).astype(BF16)], axis=1)
    b_rt = b_router[0].reshape(N_EXPERTS, 1).astype(F32)
    b_gu_r = b_gu[0].reshape(N_EXPERTS, 1, 2 * D_FF).astype(F32)
    b_down_r = b_down[0].reshape(N_EXPERTS, 1, D_MODEL).astype(F32)
    w_pg_b = w_pg[0].astype(BF16)
    w_pp_b = w_pp[0].astype(BF16)

    carry = jnp.zeros((N_EXPERTS, LANES), F32)
    routed = []
    for x, p in ((x_prompt, p_prompt[0]), (x_sample, p_sample[0])):
        bsz, s, _ = x.shape
        xt = x.reshape(bsz * s, D_MODEL)
        u, f_in = _inproj(xt, row(ln0_g), row(ln0_b), w_branch, b_branch)
        ya = _conv_branch(u.reshape(bsz, s, D_CONV), conv_w[0].astype(F32), row(conv_b[0]), row(lnc_g[0]),
                          row(lnc_b[0]), w_conv_out_b).reshape(bsz * s, D_MODEL)
        yb = _fourier_branch(f_in.reshape(bsz, s, D_FNET), w_fnet_out_b)
        h1, idx, gates, rank, carry = _mix_route(
            xt, ya, yb, carry, row(ln0_g), row(ln0_b), w_gate, b_gate, w_o_b, row(b_o[0]),
            row(ln1_g[0]), row(ln1_b[0]), w_rt, b_rt)
        routed.append((h1, idx, gates, rank, p.reshape(bsz * s, D_PLE), (bsz, s)))

    n_assign = sum(r[1].shape[1] for r in routed) * TOP_K
    n_blocks = (n_assign + N_EXPERTS * (BM_MOE - 1)) // BM_MOE
    counts = carry[:, 0].astype(I32)
    padded = ((counts + BM_MOE - 1) // BM_MOE) * BM_MOE
    ends = jnp.cumsum(padded)
    start = ends - padded
    nb_used = (ends[-1] // BM_MOE).reshape(1).astype(I32)
    blk = jnp.minimum(jnp.arange(n_blocks, dtype=I32), nb_used[0] - 1)
    blk_e = jnp.minimum(jnp.sum((ends[None, :] <= (blk * BM_MOE)[:, None]).astype(I32), axis=1), N_EXPERTS - 1)
    pad_runs = jnp.concatenate([jnp.stack([start + counts, padded - counts], axis=1).reshape(-1), nb_used,
                                jnp.zeros((LANES - 2 * N_EXPERTS - 1,), I32)]).astype(I32)

    xs = None
    slot_tables = []
    for h1, idx, _, rank, _, _ in routed:
        first = jnp.sum(jnp.where(idx[None] == jnp.arange(N_EXPERTS, dtype=I32)[:, None, None],
                                  start[:, None, None], 0), axis=0)
        slots = _tile_slots(first + rank)
        slot_tables.append(slots)
        xs = _dispatch(h1, slots, pad_runs, xs, n_blocks * BM_MOE)

    y = _moe_blocks(xs.reshape(-1, LANES), blk_e, nb_used, w_gu.reshape(N_EXPERTS, D_MODEL, 2 * D_FF), b_gu_r,
                    w_down.reshape(N_EXPERTS, D_FF, D_MODEL), b_down_r)

    outs = []
    for (h1, _, gates, _, p, (bsz, s)), slots in zip(routed, slot_tables):
        o = _combine_final(h1, p, gates.T, slots, y, w_pg_b, row(b_pg[0]), w_pp_b, row(ln2_g[0]), row(ln2_b[0]))
        outs.append(o.reshape(bsz, s, D_MODEL))
    return tuple(outs)
```

```python
import functools

import jax
import jax.numpy as jnp
from jax import lax
from jax.experimental import pallas as pl
from jax.experimental.pallas import tpu as pltpu

F32 = jnp.float32
BF16 = jnp.bfloat16
I32 = jnp.int32

D_MODEL = 1024
D_CONV = D_MODEL // 2
CONV_WIDTH = 31
CONV_PAD = CONV_WIDTH // 2
D_FNET = D_MODEL // 2
FNET_GROUPS = 4
FNET_GROUP_DIM = D_FNET // FNET_GROUPS
D_PLE = 256
N_EXPERTS = 32
TOP_K = 4
D_FF = D_MODEL
SWIGLU_ALPHA = 1.702
SWIGLU_LIMIT = 7.0
LN_EPS = 1e-5
DEPTH = 1
DEEPNORM_ALPHA = (2 * DEPTH) ** 0.25

LANES = 128
SUBLANES = 8
DFT_N1 = 128
HALO = 16

TM_INPROJ = 512
TS_CONV = 512
RC_CONV = 128
TM_MIX = 512
SUB_MIX = 256
TM_ROWS = 512
BM_MOE = 512
FF_CHUNK = 512
ISSUE_UNROLL = 8
CAST_ROWS = 128
VMEM_LIMIT = 48 * 1024 * 1024
VMEM_LIMIT_MOE = 56 * 1024 * 1024


def _cparams(*sem):
    return pltpu.CompilerParams(dimension_semantics=sem, vmem_limit_bytes=VMEM_LIMIT)


def _layer_norm(x, g, b):
    mu = jnp.mean(x, axis=-1, keepdims=True)
    xc = x - mu
    var = jnp.mean(xc * xc, axis=-1, keepdims=True)
    return xc * lax.rsqrt(var + LN_EPS) * g + b


def _full(shape):
    return pl.BlockSpec(shape, lambda *_: (0,) * len(shape))


def _to_row_tiles(ref, row0, x):
    rows = x.shape[0]
    for j in range(D_MODEL // LANES):
        ref[pl.ds(row0 * SUBLANES + j, rows, stride=SUBLANES), :] = x[:, j * LANES:(j + 1) * LANES]


def _from_row_tiles(ref, row0, rows):
    return jnp.concatenate([ref[pl.ds(row0 * SUBLANES + j, rows, stride=SUBLANES), :]
                            for j in range(D_MODEL // LANES)], axis=1)


def _inproj_kernel(x_ref, g_ref, b_ref, w_ref, bias_ref, u_ref, f_ref):
    h = _layer_norm(x_ref[...], g_ref[...], b_ref[...])
    z = jnp.dot(h.astype(BF16), w_ref[...], preferred_element_type=F32) + bias_ref[...]
    u_ref[...] = (z[:, :D_CONV] * jax.nn.sigmoid(z[:, D_CONV:2 * D_CONV])).astype(BF16)
    f_ref[...] = z[:, 2 * D_CONV:]


def _inproj(x, ln_g, ln_b, w, bias):
    t = x.shape[0]
    n_out = w.shape[1]
    return pl.pallas_call(
        _inproj_kernel,
        grid=(t // TM_INPROJ,),
        in_specs=[
            pl.BlockSpec((TM_INPROJ, D_MODEL), lambda i: (i, 0)),
            _full((1, D_MODEL)), _full((1, D_MODEL)),
            _full((D_MODEL, n_out)), _full((1, n_out)),
        ],
        out_specs=[
            pl.BlockSpec((TM_INPROJ, D_CONV), lambda i: (i, 0)),
            pl.BlockSpec((TM_INPROJ, D_FNET), lambda i: (i, 0)),
        ],
        out_shape=[jax.ShapeDtypeStruct((t, D_CONV), BF16), jax.ShapeDtypeStruct((t, D_FNET), F32)],
        compiler_params=_cparams("parallel"),
        name="inproj",
    )(x, ln_g, ln_b, w, bias)


def _conv_kernel(prev_ref, cur_ref, next_ref, cw_ref, cb_ref, g_ref, b_ref, wout_ref, ya_ref,
                 ext_ref, act_ref):
    i = pl.program_id(1)
    last = pl.num_programs(1) - 1
    ts = cur_ref.shape[1]
    ext_rows = ts + 2 * HALO
    zero = jnp.zeros((HALO, D_CONV), F32)
    ext_ref[0, 0:HALO, :] = jnp.where(i > 0, prev_ref[0].astype(F32), zero)
    ext_ref[0, HALO:HALO + ts, :] = cur_ref[0].astype(F32)
    ext_ref[0, HALO + ts:ext_rows, :] = jnp.where(i < last, next_ref[0].astype(F32), zero)
    for r in range(1, SUBLANES):
        ext_ref[r, 0:ext_rows - SUBLANES, :] = ext_ref[0, r:r + ext_rows - SUBLANES, :]

    cb = cb_ref[...]
    g = g_ref[...]
    b = b_ref[...]

    def chunk(c, carry):
        r0 = pl.multiple_of(c * RC_CONV, RC_CONV)
        acc = jnp.zeros((RC_CONV, D_CONV), F32) + cb
        for j in range(CONV_WIDTH):
            off = j + HALO - CONV_PAD
            q, r = divmod(off, SUBLANES)
            acc = acc + cw_ref[j:j + 1, :] * ext_ref[r, pl.ds(r0 + q * SUBLANES, RC_CONV), :]
        y = _layer_norm(acc, g, b)
        act_ref[pl.ds(r0, RC_CONV), :] = (y * jax.nn.sigmoid(y)).astype(BF16)
        return carry

    lax.fori_loop(0, ts // RC_CONV, chunk, 0)
    ya_ref[0] = jnp.dot(act_ref[...], wout_ref[...], preferred_element_type=F32).astype(BF16)


def _conv_branch(u, conv_w, conv_b, ln_g, ln_b, w_out):
    bsz, s, _ = u.shape
    ts = TS_CONV
    nb = s // ts
    hb = ts // HALO
    return pl.pallas_call(
        _conv_kernel,
        grid=(bsz, nb),
        in_specs=[
            pl.BlockSpec((1, HALO, D_CONV), lambda b, i: (b, jnp.maximum(i * hb - 1, 0), 0)),
            pl.BlockSpec((1, ts, D_CONV), lambda b, i: (b, i, 0)),
            pl.BlockSpec((1, HALO, D_CONV), lambda b, i: (b, jnp.minimum((i + 1) * hb, s // HALO - 1), 0)),
            _full((CONV_WIDTH, D_CONV)), _full((1, D_CONV)), _full((1, D_CONV)), _full((1, D_CONV)),
            _full((D_CONV, D_MODEL)),
        ],
        out_specs=pl.BlockSpec((1, ts, D_MODEL), lambda b, i: (b, i, 0)),
        out_shape=jax.ShapeDtypeStruct((bsz, s, D_MODEL), BF16),
        scratch_shapes=[
            pltpu.VMEM((SUBLANES, ts + 2 * HALO, D_CONV), F32),
            pltpu.VMEM((ts, D_CONV), BF16),
        ],
        compiler_params=_cparams("parallel", "parallel"),
        name="conv_branch",
    )(u, u, u, conv_w, conv_b, ln_g, ln_b, w_out)


def _dft_tables(s):
    n2_len = s // DFT_N1
    k1_per_tile = DFT_N1 // n2_len
    two_pi = 2.0 * jnp.pi
    a = jnp.arange(DFT_N1, dtype=I32)
    ang1 = two_pi * ((a[:, None] * a[None, :]) % DFT_N1).astype(F32) / DFT_N1
    f1 = jnp.concatenate([jnp.cos(ang1), -jnp.sin(ang1)], axis=0).astype(BF16)
    cs = jnp.stack([jnp.cos(ang1), jnp.sin(ang1)]).astype(BF16)
    n_tiles = s // DFT_N1
    t = jnp.arange(n_tiles, dtype=I32)[:, None, None]
    row = jnp.arange(DFT_N1, dtype=I32)[None, :, None]
    col = jnp.arange(DFT_N1, dtype=I32)[None, None, :]
    k2, k1l_out = row // k1_per_tile, row % k1_per_tile
    k1l_in, n2 = col // n2_len, col % n2_len
    k = t * k1_per_tile + k1l_out + DFT_N1 * k2
    ang2 = two_pi * ((n2 * k) % s).astype(F32) / s
    hit = k1l_in == k1l_out
    gr = jnp.where(hit, jnp.cos(ang2), 0.0)
    gi = jnp.where(hit, -jnp.sin(ang2), 0.0)
    g = jnp.concatenate([jnp.concatenate([gr, -gi], axis=2),
                         jnp.concatenate([gi, gr], axis=2)], axis=1).astype(BF16)
    return f1, g, cs


def _fft1_kernel(x_ref, f_ref, y_ref):
    for j in range(x_ref.shape[2]):
        r = jnp.dot(f_ref[...], x_ref[0, :, j, :].astype(BF16), preferred_element_type=F32)
        y_ref[0, 0, :, j, :] = r[:DFT_N1]
        y_ref[0, 1, :, j, :] = r[DFT_N1:]


def _fft2_kernel(y_ref, g_ref, cs_ref, w_ref, o_ref, *, scale, tiles):
    for t in range(tiles):
        rows = slice(t * DFT_N1, (t + 1) * DFT_N1)
        yb = jnp.concatenate([y_ref[0, 0, rows, :], y_ref[0, 1, rows, :]], axis=0).astype(BF16)
        z = jnp.dot(g_ref[t], yb, preferred_element_type=F32)
        zr = z[:DFT_N1].astype(BF16)
        zi = z[DFT_N1:].astype(BF16)
        parts = []
        for grp in range(FNET_GROUPS):
            sl = slice(grp * FNET_GROUP_DIM, (grp + 1) * FNET_GROUP_DIM)
            parts.append(jnp.dot(zr[:, sl], cs_ref[0], preferred_element_type=F32)
                         + jnp.dot(zi[:, sl], cs_ref[1], preferred_element_type=F32))
        fm = (jnp.concatenate(parts, axis=1) * scale).astype(BF16)
        out = jnp.dot(fm, w_ref[...], preferred_element_type=F32)
        if tiles == 1:
            o_ref[0] = out.reshape(o_ref.shape[1:])
        else:
            o_ref[0, :, t, :] = out


def _fourier_branch(f_in, w_out):
    bsz, s, _ = f_in.shape
    n2_len = s // DFT_N1
    k1_per_tile = DFT_N1 // n2_len
    f1, g, cs = _dft_tables(s)
    y = pl.pallas_call(
        _fft1_kernel,
        grid=(bsz, n2_len // SUBLANES),
        in_specs=[pl.BlockSpec((1, DFT_N1, SUBLANES, D_FNET), lambda b, j: (b, 0, j, 0)),
                  _full((2 * DFT_N1, DFT_N1))],
        out_specs=pl.BlockSpec((1, 2, DFT_N1, SUBLANES, D_FNET), lambda b, j: (b, 0, 0, j, 0)),
        out_shape=jax.ShapeDtypeStruct((bsz, 2, DFT_N1, n2_len, D_FNET), F32),
        compiler_params=_cparams("parallel", "parallel"),
        name="fft_stage1",
    )(f_in.reshape(bsz, DFT_N1, n2_len, D_FNET), f1)
    y = y.reshape(bsz, 2, s, D_FNET)
    scale = float((s * FNET_GROUP_DIM) ** -0.5)
    assert n2_len % SUBLANES == 0 and (k1_per_tile == 1 or k1_per_tile % SUBLANES == 0)
    tiles = SUBLANES if k1_per_tile == 1 else 1
    k1_per_step = tiles * k1_per_tile
    yb = pl.pallas_call(
        functools.partial(_fft2_kernel, scale=scale, tiles=tiles),
        grid=(bsz, s // (DFT_N1 * tiles)),
        in_specs=[
            pl.BlockSpec((1, 2, DFT_N1 * tiles, D_FNET), lambda b, t: (b, 0, t, 0)),
            pl.BlockSpec((tiles, 2 * DFT_N1, 2 * DFT_N1), lambda b, t: (t, 0, 0)),
            _full((2, DFT_N1, DFT_N1)),
            _full((D_FNET, D_MODEL)),
        ],
        out_specs=pl.BlockSpec((1, n2_len, k1_per_step, D_MODEL), lambda b, t: (b, 0, t, 0)),
        out_shape=jax.ShapeDtypeStruct((bsz, n2_len, DFT_N1, D_MODEL), F32),
        compiler_params=_cparams("parallel", "parallel"),
        name="fft_stage2",
    )(y, g, cs, w_out)
    return yb.reshape(bsz * s, D_MODEL)


def _mix_kernel(x_ref, ya_ref, yb_ref, cin_ref, ln0g_ref, ln0b_ref, wg_ref, bg_ref, wo_ref, bo_ref,
                ln1g_ref, ln1b_ref, wr_ref, br_ref,
                h1_ref, idx_ref, gate_ref, rank_ref, cnt_ref, carry_ref):
    tm = x_ref.shape[0]
    sub = min(SUB_MIX, tm)

    @pl.when(pl.program_id(0) == 0)
    def _():
        carry_ref[...] = cin_ref[...]

    carry = carry_ref[...]
    eio = lax.broadcasted_iota(I32, (N_EXPERTS, sub), 0)
    before = (lax.broadcasted_iota(I32, (sub, sub), 0) < lax.broadcasted_iota(I32, (sub, sub), 1)).astype(BF16)
    for r0 in range(0, tm, sub):
        rows = slice(r0, r0 + sub)
        h = _layer_norm(x_ref[rows, :], ln0g_ref[...], ln0b_ref[...])
        zg = jnp.dot(h.astype(BF16), wg_ref[...], preferred_element_type=F32) + bg_ref[...]
        m = (jax.nn.sigmoid(zg[:, :D_MODEL]) * ya_ref[rows, :].astype(F32)
             + jax.nn.sigmoid(zg[:, D_MODEL:]) * yb_ref[rows, :].astype(F32))
        mix = jnp.dot(m.astype(BF16), wo_ref[...], preferred_element_type=F32) + bo_ref[...]
        h1 = _layer_norm(DEEPNORM_ALPHA * h + mix, ln1g_ref[...], ln1b_ref[...])
        _to_row_tiles(h1_ref, r0, h1)

        h_hi = h1.astype(BF16)
        h_lo = (h1 - h_hi.astype(F32)).astype(BF16)
        hw = jnp.dot(h_hi, wr_ref[...], preferred_element_type=F32)
        lw = jnp.dot(h_lo, wr_ref[:, :LANES], preferred_element_type=F32)
        logits = (hw[:, :LANES] + hw[:, LANES:] + lw).T[:N_EXPERTS] + br_ref[...]
        vals, idxs = [], []
        cur = logits
        for _ in range(TOP_K):
            mx = jnp.max(cur, axis=0, keepdims=True)
            ik = jnp.min(jnp.where(cur == mx, eio, N_EXPERTS), axis=0, keepdims=True)
            vals.append(mx)
            idxs.append(ik)
            cur = jnp.where(eio == ik, -jnp.inf, cur)
        exps = [jnp.exp(v - vals[0]) for v in vals]
        den = exps[0] + exps[1] + exps[2] + exps[3]
        gate_ref[:, rows] = jnp.concatenate([e / den for e in exps], axis=0)
        idx_ref[:, rows] = jnp.concatenate(idxs, axis=0)

        hot = jnp.zeros((N_EXPERTS, sub), F32)
        for ik in idxs:
            hot = hot + (eio == ik).astype(F32)
        prior = jnp.dot(hot.astype(BF16), before, preferred_element_type=F32) + carry[:, 0:1]
        ranks = [jnp.sum(jnp.where(eio == ik, prior, 0.0), axis=0, keepdims=True) for ik in idxs]
        rank_ref[:, rows] = jnp.concatenate(ranks, axis=0).astype(I32)
        carry = carry + jnp.sum(hot, axis=1, keepdims=True)
    carry_ref[...] = carry
    cnt_ref[...] = carry


def _mix_route(x, ya, yb, carry_in, ln0_g, ln0_b, w_gate, b_gate, w_o, b_o, ln1_g, ln1_b, w_rt, b_rt):
    t = x.shape[0]
    tm = TM_MIX
    row = lambda d: pl.BlockSpec((tm, d), lambda i: (i, 0))
    col = lambda r: pl.BlockSpec((r, tm), lambda i: (0, i))
    return pl.pallas_call(
        _mix_kernel,
        grid=(t // tm,),
        in_specs=[
            row(D_MODEL), row(D_MODEL), row(D_MODEL), _full((N_EXPERTS, LANES)),
            _full((1, D_MODEL)), _full((1, D_MODEL)),
            _full((D_MODEL, 2 * D_MODEL)), _full((1, 2 * D_MODEL)),
            _full((D_MODEL, D_MODEL)), _full((1, D_MODEL)),
            _full((1, D_MODEL)), _full((1, D_MODEL)),
            _full((D_MODEL, 2 * LANES)), _full((N_EXPERTS, 1)),
        ],
        out_specs=[pl.BlockSpec((tm * SUBLANES, LANES), lambda i: (i, 0)), col(TOP_K), col(TOP_K), col(TOP_K),
                   _full((N_EXPERTS, LANES))],
        out_shape=[
            jax.ShapeDtypeStruct((t * SUBLANES, LANES), F32),
            jax.ShapeDtypeStruct((TOP_K, t), I32),
            jax.ShapeDtypeStruct((TOP_K, t), F32),
            jax.ShapeDtypeStruct((TOP_K, t), I32),
            jax.ShapeDtypeStruct((N_EXPERTS, LANES), F32),
        ],
        scratch_shapes=[pltpu.VMEM((N_EXPERTS, LANES), F32)],
        compiler_params=_cparams("arbitrary"),
        name="mix_route",
    )(x, ya, yb, carry_in, ln0_g, ln0_b, w_gate, b_gate, w_o, b_o, ln1_g, ln1_b, w_rt, b_rt)


def _row_copy(src_ref, src_row, dst_ref, dst_row, sem):
    return pltpu.make_async_copy(src_ref.at[src_row], dst_ref.at[dst_row], sem)


def _fetch_slots(slots_hbm, slots_smem, sem, step, slot):
    n = slots_hbm.shape[1]
    return pltpu.make_async_copy(slots_hbm.at[step], slots_smem.at[pl.ds(pl.multiple_of(slot * n, n), n)],
                                 sem.at[slot])


def _pad_fill(h_ref, xs_out, pad_smem, pad_sem, n_pad_runs, wait):
    def go(cp):
        if wait:
            cp.wait()
        else:
            cp.start()

    def run(j, carry):
        start = pad_smem[2 * j]
        length = pad_smem[2 * j + 1]
        for bit in range(BM_MOE.bit_length() - 1):
            size = 1 << bit

            @pl.when((length & size) != 0)
            def _():
                off = start + (length & ~(2 * size - 1))
                go(pltpu.make_async_copy(h_ref.at[pl.ds(0, size)], xs_out.at[pl.ds(off, size)], pad_sem))
        return carry

    lax.fori_loop(0, n_pad_runs, run, 0)

    tm = h_ref.shape[0]

    def dead_block(j, carry):
        for part in range(BM_MOE // tm):
            off = j * BM_MOE + part * tm
            go(pltpu.make_async_copy(h_ref, xs_out.at[pl.ds(off, tm)], pad_sem))
        return carry

    lax.fori_loop(pad_smem[2 * n_pad_runs], xs_out.shape[0] // BM_MOE, dead_block, 0)


def _dispatch_kernel(h_ref, slots_hbm, pad_hbm, *rest, n_pad_runs, aliased):
    xs_out, slots_smem, pad_smem, slot_sem, pad_sem, row_sem = rest[1:] if aliased else rest
    i = pl.program_id(0)
    n = pl.num_programs(0)
    tm = h_ref.shape[0]
    cur = lax.rem(i, 2)

    @pl.when(i == 0)
    def _():
        _fetch_slots(slots_hbm, slots_smem, slot_sem, 0, 0).start()

    @pl.when(i + 1 < n)
    def _():
        _fetch_slots(slots_hbm, slots_smem, slot_sem, i + 1, 1 - cur).start()

    if n_pad_runs:
        @pl.when(i == 0)
        def _():
            fetch = pltpu.make_async_copy(pad_hbm, pad_smem, pad_sem)
            fetch.start()
            fetch.wait()
            _pad_fill(h_ref, xs_out, pad_smem, pad_sem, n_pad_runs, wait=False)
            _pad_fill(h_ref, xs_out, pad_smem, pad_sem, n_pad_runs, wait=True)

    _fetch_slots(slots_hbm, slots_smem, slot_sem, i, cur).wait()

    base = cur * (TOP_K * tm)

    def issue(r, carry):
        for k in range(TOP_K):
            _row_copy(h_ref, r, xs_out, slots_smem[base + TOP_K * r + k], row_sem).start(priority=k % 2)
        return carry

    lax.fori_loop(0, tm, issue, 0, unroll=ISSUE_UNROLL)
    for _ in range(TOP_K):
        pltpu.make_async_copy(h_ref, xs_out.at[pl.ds(0, tm)], row_sem).wait()


def _dispatch(h1, slots, pad_runs, xs, n_rows):
    h1 = h1.reshape(-1, SUBLANES, LANES)
    t = h1.shape[0]
    tm = TM_ROWS
    aliased = xs is not None
    n_pad_runs = 0 if aliased else N_EXPERTS
    operands = (h1, slots, pad_runs) + ((xs,) if aliased else ())
    return pl.pallas_call(
        functools.partial(_dispatch_kernel, n_pad_runs=n_pad_runs, aliased=aliased),
        grid=(t // tm,),
        in_specs=[pl.BlockSpec((tm, SUBLANES, LANES), lambda i: (i, 0, 0))]
        + [pl.BlockSpec(memory_space=pl.ANY)] * (len(operands) - 1),
        out_specs=pl.BlockSpec(memory_space=pl.ANY),
        out_shape=jax.ShapeDtypeStruct((n_rows, SUBLANES, LANES), F32),
        input_output_aliases={3: 0} if aliased else {},
        scratch_shapes=[
            pltpu.SMEM((2 * TOP_K * tm,), I32),
            pltpu.SMEM((pad_runs.shape[0],), I32),
            pltpu.SemaphoreType.DMA((2,)),
            pltpu.SemaphoreType.DMA,
            pltpu.SemaphoreType.DMA,
        ],
        compiler_params=_cparams("arbitrary"),
        name="dispatch",
    )(*operands)


def _moe_kernel(blk_e_ref, nb_ref, x_ref, wgu_ref, bgu_ref, wd_ref, bd_ref, o_ref, wgu_b, wd_b):
    i = pl.program_id(0)
    is_live = i < nb_ref[0]

    @pl.when(jnp.logical_not(is_live))
    def _():
        o_ref[...] = jnp.zeros(o_ref.shape, F32)

    @pl.when(jnp.logical_or(i == 0, blk_e_ref[i] != blk_e_ref[jnp.maximum(i - 1, 0)]))
    def _():
        for r in range(0, D_MODEL, CAST_ROWS):
            wgu_b[r:r + CAST_ROWS, :] = wgu_ref[0, r:r + CAST_ROWS, :].astype(BF16)
        for r in range(0, D_FF, CAST_ROWS):
            wd_b[r:r + CAST_ROWS, :] = wd_ref[0, r:r + CAST_ROWS, :].astype(BF16)

    @pl.when(is_live)
    def _():
        x = _from_row_tiles(x_ref, 0, BM_MOE).astype(BF16)
        acc = jnp.zeros((BM_MOE, D_MODEL), F32) + bd_ref[0]
        for c in range(D_FF // FF_CHUNK):
            lo = c * FF_CHUNK
            g = jnp.dot(x, wgu_b[:, lo:lo + FF_CHUNK], preferred_element_type=F32) + bgu_ref[0, :, lo:lo + FF_CHUNK]
            u = (jnp.dot(x, wgu_b[:, D_FF + lo:D_FF + lo + FF_CHUNK], preferred_element_type=F32)
                 + bgu_ref[0, :, D_FF + lo:D_FF + lo + FF_CHUNK])
            g = jnp.minimum(g, SWIGLU_LIMIT)
            u = jnp.clip(u, -SWIGLU_LIMIT, SWIGLU_LIMIT)
            act = (u + 1.0) * (g * jax.nn.sigmoid(g * SWIGLU_ALPHA))
            acc = acc + jnp.dot(act.astype(BF16), wd_b[lo:lo + FF_CHUNK, :], preferred_element_type=F32)
        _to_row_tiles(o_ref, 0, acc)


def _moe_blocks(xs, blk_e, nb_used, w_gu, b_gu, w_down, b_down):
    p = xs.shape[0] // SUBLANES
    n_blocks = p // BM_MOE
    live = lambda i, nb: jnp.minimum(i, nb[0] - 1)
    grid_spec = pltpu.PrefetchScalarGridSpec(
        num_scalar_prefetch=2,
        grid=(n_blocks,),
        in_specs=[
            pl.BlockSpec((BM_MOE * SUBLANES, LANES), lambda i, be, nb: (live(i, nb), 0)),
            pl.BlockSpec((1, D_MODEL, 2 * D_FF), lambda i, be, nb: (be[i], 0, 0)),
            pl.BlockSpec((1, 1, 2 * D_FF), lambda i, be, nb: (be[i], 0, 0)),
            pl.BlockSpec((1, D_FF, D_MODEL), lambda i, be, nb: (be[i], 0, 0)),
            pl.BlockSpec((1, 1, D_MODEL), lambda i, be, nb: (be[i], 0, 0)),
        ],
        out_specs=pl.BlockSpec((BM_MOE * SUBLANES, LANES), lambda i, be, nb: (i, 0)),
        scratch_shapes=[pltpu.VMEM((D_MODEL, 2 * D_FF), BF16), pltpu.VMEM((D_FF, D_MODEL), BF16)],
    )
    return pl.pallas_call(
        _moe_kernel,
        grid_spec=grid_spec,
        out_shape=jax.ShapeDtypeStruct((p * SUBLANES, LANES), F32),
        compiler_params=pltpu.CompilerParams(dimension_semantics=("arbitrary",), vmem_limit_bytes=VMEM_LIMIT_MOE),
        name="moe_experts",
    )(blk_e, nb_used, xs, w_gu, b_gu, w_down, b_down)


def _final_kernel(h_ref, p_ref, gate_ref, slots_hbm, y_hbm, wpg_ref, bpg_ref, wpp_ref, g_ref, b_ref, o_ref,
                  slots_smem, rows_ref, slot_sem, row_sem):
    i = pl.program_id(0)
    n = pl.num_programs(0)
    tm = h_ref.shape[0] // SUBLANES
    cur = lax.rem(i, 2)

    @pl.when(i == 0)
    def _():
        _fetch_slots(slots_hbm, slots_smem, slot_sem, 0, 0).start()

    @pl.when(i + 1 < n)
    def _():
        _fetch_slots(slots_hbm, slots_smem, slot_sem, i + 1, 1 - cur).start()

    _fetch_slots(slots_hbm, slots_smem, slot_sem, i, cur).wait()

    base = cur * (TOP_K * tm)

    def issue(r, carry):
        for k in range(TOP_K):
            slot = slots_smem[base + TOP_K * r + k]
            src = y_hbm.at[pl.ds(pl.multiple_of(slot * SUBLANES, SUBLANES), SUBLANES)]
            dst = rows_ref.at[k, pl.ds(pl.multiple_of(r * SUBLANES, SUBLANES), SUBLANES)]
            pltpu.make_async_copy(src, dst, row_sem).start(priority=k % 2)
        return carry

    lax.fori_loop(0, tm, issue, 0, unroll=ISSUE_UNROLL)

    h = _from_row_tiles(h_ref, 0, tm)
    ple = (jax.nn.sigmoid(jnp.dot(h.astype(BF16), wpg_ref[...], preferred_element_type=F32) + bpg_ref[...])
           * jnp.dot(p_ref[...].astype(BF16), wpp_ref[...], preferred_element_type=F32))

    for k in range(TOP_K):
        pltpu.make_async_copy(y_hbm.at[pl.ds(0, tm * SUBLANES)], rows_ref.at[k], row_sem).wait()

    gates = gate_ref[...]
    ffn = gates[:, 0:1] * _from_row_tiles(rows_ref.at[0], 0, tm)
    for k in range(1, TOP_K):
        ffn = ffn + gates[:, k:k + 1] * _from_row_tiles(rows_ref.at[k], 0, tm)
    o_ref[...] = _layer_norm(DEEPNORM_ALPHA * h + (ffn + ple), g_ref[...], b_ref[...])


def _combine_final(h1, p, gates, slots, y, w_pg, b_pg, w_pp, ln_g, ln_b):
    t = h1.shape[0] // SUBLANES
    tm = TM_ROWS
    return pl.pallas_call(
        _final_kernel,
        grid=(t // tm,),
        in_specs=[
            pl.BlockSpec((tm * SUBLANES, LANES), lambda i: (i, 0)),
            pl.BlockSpec((tm, D_PLE), lambda i: (i, 0)),
            pl.BlockSpec((tm, TOP_K), lambda i: (i, 0)),
            pl.BlockSpec(memory_space=pl.ANY),
            pl.BlockSpec(memory_space=pl.ANY),
            _full((D_MODEL, D_MODEL)), _full((1, D_MODEL)), _full((D_PLE, D_MODEL)),
            _full((1, D_MODEL)), _full((1, D_MODEL)),
        ],
        out_specs=pl.BlockSpec((tm, D_MODEL), lambda i: (i, 0)),
        out_shape=jax.ShapeDtypeStruct((t, D_MODEL), F32),
        scratch_shapes=[
            pltpu.SMEM((2 * TOP_K * tm,), I32),
            pltpu.VMEM((TOP_K, tm * SUBLANES, LANES), F32),
            pltpu.SemaphoreType.DMA((2,)),
            pltpu.SemaphoreType.DMA,
        ],
        compiler_params=_cparams("arbitrary"),
        name="combine_final",
    )(h1, p, gates, slots, y, w_pg, b_pg, w_pp, ln_g, ln_b)


def _tile_slots(dest):
    t = dest.shape[1]
    return dest.T.reshape(t // TM_ROWS, TM_ROWS * TOP_K)


def kernel(x_prompt, x_sample, p_prompt, p_sample, ln0_g, ln0_b, w_in, b_in, conv_w, conv_b, lnc_g, lnc_b, w_conv_out, w_fnet_out, w_o, b_o, ln1_g, ln1_b, w_router, b_router, w_gu, b_gu, w_down, b_down, w_pg, b_pg, w_pp, ln2_g, ln2_b):
    row = lambda v: v.reshape(1, -1).astype(F32)
    n_branch = 2 * D_CONV + D_FNET
    w_in_b = w_in[0].astype(BF16)
    w_branch, w_gate = w_in_b[:, :n_branch], w_in_b[:, n_branch:]
    b_branch, b_gate = row(b_in[0, :n_branch]), row(b_in[0, n_branch:])
    w_conv_out_b = w_conv_out[0].astype(BF16)
    w_fnet_out_b = w_fnet_out[0].astype(BF16)
    w_o_b = w_o[0].astype(BF16)
    w_rt_f = jnp.pad(w_router[0].astype(F32), ((0, 0), (0, LANES - N_EXPERTS)))
    w_rt_hi = w_rt_f.astype(BF16)
    w_rt = jnp.concatenate([w_rt_hi, (w_rt_f - w_rt_hi.astype(F32)).astype(BF16)], axis=1)
    b_rt = b_router[0].reshape(N_EXPERTS, 1).astype(F32)
    b_gu_r = b_gu[0].reshape(N_EXPERTS, 1, 2 * D_FF).astype(F32)
    b_down_r = b_down[0].reshape(N_EXPERTS, 1, D_MODEL).astype(F32)
    w_pg_b = w_pg[0].astype(BF16)
    w_pp_b = w_pp[0].astype(BF16)

    carry = jnp.zeros((N_EXPERTS, LANES), F32)
    routed = []
    for x, p in ((x_prompt, p_prompt[0]), (x_sample, p_sample[0])):
        bsz, s, _ = x.shape
        xt = x.reshape(bsz * s, D_MODEL)
        u, f_in = _inproj(xt, row(ln0_g), row(ln0_b), w_branch, b_branch)
        ya = _conv_branch(u.reshape(bsz, s, D_CONV), conv_w[0].astype(F32), row(conv_b[0]), row(lnc_g[0]),
                          row(lnc_b[0]), w_conv_out_b).reshape(bsz * s, D_MODEL)
        yb = _fourier_branch(f_in.reshape(bsz, s, D_FNET), w_fnet_out_b)
        h1, idx, gates, rank, carry = _mix_route(
            xt, ya, yb, carry, row(ln0_g), row(ln0_b), w_gate, b_gate, w_o_b, row(b_o[0]),
            row(ln1_g[0]), row(ln1_b[0]), w_rt, b_rt)
        routed.append((h1, idx, gates, rank, p.reshape(bsz * s, D_PLE), (bsz, s)))

    n_assign = sum(r[1].shape[1] for r in routed) * TOP_K
    n_blocks = (n_assign + N_EXPERTS * (BM_MOE - 1)) // BM_MOE
    counts = carry[:, 0].astype(I32)
    padded = ((counts + BM_MOE - 1) // BM_MOE) * BM_MOE
    ends = jnp.cumsum(padded)
    start = ends - padded
    nb_used = (ends[-1] // BM_MOE).reshape(1).astype(I32)
    blk = jnp.minimum(jnp.arange(n_blocks, dtype=I32), nb_used[0] - 1)
    blk_e = jnp.minimum(jnp.sum((ends[None, :] <= (blk * BM_MOE)[:, None]).astype(I32), axis=1), N_EXPERTS - 1)
    pad_runs = jnp.concatenate([jnp.stack([start + counts, padded - counts], axis=1).reshape(-1), nb_used,
                                jnp.zeros((LANES - 2 * N_EXPERTS - 1,), I32)]).astype(I32)

    xs = None
    slot_tables = []
    for h1, idx, _, rank, _, _ in routed:
        first = jnp.sum(jnp.where(idx[None] == jnp.arange(N_EXPERTS, dtype=I32)[:, None, None],
                                  start[:, None, None], 0), axis=0)
        slots = _tile_slots(first + rank)
        slot_tables.append(slots)
        xs = _dispatch(h1, slots, pad_runs, xs, n_blocks * BM_MOE)

    y = _moe_blocks(xs.reshape(-1, LANES), blk_e, nb_used, w_gu.reshape(N_EXPERTS, D_MODEL, 2 * D_FF), b_gu_r,
                    w_down.reshape(N_EXPERTS, D_FF, D_MODEL), b_down_r)

    outs = []
    for (h1, _, gates, _, p, (bsz, s)), slots in zip(routed, slot_tables):
        o = _combine_final(h1, p, gates.T, slots, y, w_pg_b, row(b_pg[0]), w_pp_b, row(ln2_g[0]), row(ln2_b[0]))
        outs.append(o.reshape(bsz, s, D_MODEL))
    return tuple(outs)
```

```python
import functools

import jax
import jax.numpy as jnp
from jax import lax
from jax.experimental import pallas as pl
from jax.experimental.pallas import tpu as pltpu

F32 = jnp.float32
BF16 = jnp.bfloat16
I32 = jnp.int32

D_MODEL = 1024
D_CONV = D_MODEL // 2
CONV_WIDTH = 31
CONV_PAD = CONV_WIDTH // 2
D_FNET = D_MODEL // 2
FNET_GROUPS = 4
FNET_GROUP_DIM = D_FNET // FNET_GROUPS
D_PLE = 256
N_EXPERTS = 32
TOP_K = 4
D_FF = D_MODEL
SWIGLU_ALPHA = 1.702
SWIGLU_LIMIT = 7.0
LN_EPS = 1e-5
DEPTH = 1
DEEPNORM_ALPHA = (2 * DEPTH) ** 0.25

LANES = 128
SUBLANES = 8
DFT_N1 = 128
HALO = 16

TM_INPROJ = 512
TS_CONV = 512
RC_CONV = 128
TM_MIX = 512
SUB_MIX = 256
TM_ROWS = 512
TM_COMBINE = 256
BM_MOE = 512
FF_CHUNK = 512
ISSUE_UNROLL = 8
CAST_ROWS = 128
VMEM_LIMIT = 48 * 1024 * 1024
VMEM_LIMIT_MOE = 56 * 1024 * 1024


def _cparams(*sem):
    return pltpu.CompilerParams(dimension_semantics=sem, vmem_limit_bytes=VMEM_LIMIT)


def _layer_norm(x, g, b):
    mu = jnp.mean(x, axis=-1, keepdims=True)
    xc = x - mu
    var = jnp.mean(xc * xc, axis=-1, keepdims=True)
    return xc * lax.rsqrt(var + LN_EPS) * g + b


def _full(shape):
    return pl.BlockSpec(shape, lambda *_: (0,) * len(shape))


def _to_row_tiles(ref, row0, x):
    rows = x.shape[0]
    for j in range(D_MODEL // LANES):
        ref[pl.ds(row0 * SUBLANES + j, rows, stride=SUBLANES), :] = x[:, j * LANES:(j + 1) * LANES]


def _from_row_tiles(ref, row0, rows):
    return jnp.concatenate([ref[pl.ds(row0 * SUBLANES + j, rows, stride=SUBLANES), :]
                            for j in range(D_MODEL // LANES)], axis=1)


def _inproj_kernel(x_ref, g_ref, b_ref, w_ref, bias_ref, u_ref, f_ref):
    h = _layer_norm(x_ref[...], g_ref[...], b_ref[...])
    z = jnp.dot(h.astype(BF16), w_ref[...], preferred_element_type=F32) + bias_ref[...]
    u_ref[...] = (z[:, :D_CONV] * jax.nn.sigmoid(z[:, D_CONV:2 * D_CONV])).astype(BF16)
    f_ref[...] = z[:, 2 * D_CONV:]


def _inproj(x, ln_g, ln_b, w, bias):
    t = x.shape[0]
    n_out = w.shape[1]
    return pl.pallas_call(
        _inproj_kernel,
        grid=(t // TM_INPROJ,),
        in_specs=[
            pl.BlockSpec((TM_INPROJ, D_MODEL), lambda i: (i, 0)),
            _full((1, D_MODEL)), _full((1, D_MODEL)),
            _full((D_MODEL, n_out)), _full((1, n_out)),
        ],
        out_specs=[
            pl.BlockSpec((TM_INPROJ, D_CONV), lambda i: (i, 0)),
            pl.BlockSpec((TM_INPROJ, D_FNET), lambda i: (i, 0)),
        ],
        out_shape=[jax.ShapeDtypeStruct((t, D_CONV), BF16), jax.ShapeDtypeStruct((t, D_FNET), F32)],
        compiler_params=_cparams("parallel"),
        name="inproj",
    )(x, ln_g, ln_b, w, bias)


def _conv_kernel(prev_ref, cur_ref, next_ref, cw_ref, cb_ref, g_ref, b_ref, wout_ref, ya_ref,
                 ext_ref, act_ref):
    i = pl.program_id(1)
    last = pl.num_programs(1) - 1
    ts = cur_ref.shape[1]
    ext_rows = ts + 2 * HALO
    zero = jnp.zeros((HALO, D_CONV), F32)
    ext_ref[0, 0:HALO, :] = jnp.where(i > 0, prev_ref[0].astype(F32), zero)
    ext_ref[0, HALO:HALO + ts, :] = cur_ref[0].astype(F32)
    ext_ref[0, HALO + ts:ext_rows, :] = jnp.where(i < last, next_ref[0].astype(F32), zero)
    for r in range(1, SUBLANES):
        ext_ref[r, 0:ext_rows - SUBLANES, :] = ext_ref[0, r:r + ext_rows - SUBLANES, :]

    cb = cb_ref[...]
    g = g_ref[...]
    b = b_ref[...]

    def chunk(c, carry):
        r0 = pl.multiple_of(c * RC_CONV, RC_CONV)
        acc = jnp.zeros((RC_CONV, D_CONV), F32) + cb
        for j in range(CONV_WIDTH):
            off = j + HALO - CONV_PAD
            q, r = divmod(off, SUBLANES)
            acc = acc + cw_ref[j:j + 1, :] * ext_ref[r, pl.ds(r0 + q * SUBLANES, RC_CONV), :]
        y = _layer_norm(acc, g, b)
        act_ref[pl.ds(r0, RC_CONV), :] = (y * jax.nn.sigmoid(y)).astype(BF16)
        return carry

    lax.fori_loop(0, ts // RC_CONV, chunk, 0)
    ya_ref[0] = jnp.dot(act_ref[...], wout_ref[...], preferred_element_type=F32).astype(BF16)


def _conv_branch(u, conv_w, conv_b, ln_g, ln_b, w_out):
    bsz, s, _ = u.shape
    ts = TS_CONV
    nb = s // ts
    hb = ts // HALO
    return pl.pallas_call(
        _conv_kernel,
        grid=(bsz, nb),
        in_specs=[
            pl.BlockSpec((1, HALO, D_CONV), lambda b, i: (b, jnp.maximum(i * hb - 1, 0), 0)),
            pl.BlockSpec((1, ts, D_CONV), lambda b, i: (b, i, 0)),
            pl.BlockSpec((1, HALO, D_CONV), lambda b, i: (b, jnp.minimum((i + 1) * hb, s // HALO - 1), 0)),
            _full((CONV_WIDTH, D_CONV)), _full((1, D_CONV)), _full((1, D_CONV)), _full((1, D_CONV)),
            _full((D_CONV, D_MODEL)),
        ],
        out_specs=pl.BlockSpec((1, ts, D_MODEL), lambda b, i: (b, i, 0)),
        out_shape=jax.ShapeDtypeStruct((bsz, s, D_MODEL), BF16),
        scratch_shapes=[
            pltpu.VMEM((SUBLANES, ts + 2 * HALO, D_CONV), F32),
            pltpu.VMEM((ts, D_CONV), BF16),
        ],
        compiler_params=_cparams("parallel", "parallel"),
        name="conv_branch",
    )(u, u, u, conv_w, conv_b, ln_g, ln_b, w_out)


def _dft_tables(s):
    n2_len = s // DFT_N1
    k1_per_tile = DFT_N1 // n2_len
    two_pi = 2.0 * jnp.pi
    a = jnp.arange(DFT_N1, dtype=I32)
    ang1 = two_pi * ((a[:, None] * a[None, :]) % DFT_N1).astype(F32) / DFT_N1
    f1 = jnp.concatenate([jnp.cos(ang1), -jnp.sin(ang1)], axis=0).astype(BF16)
    cs = jnp.stack([jnp.cos(ang1), jnp.sin(ang1)]).astype(BF16)
    n_tiles = s // DFT_N1
    t = jnp.arange(n_tiles, dtype=I32)[:, None, None]
    row = jnp.arange(DFT_N1, dtype=I32)[None, :, None]
    col = jnp.arange(DFT_N1, dtype=I32)[None, None, :]
    k2, k1l_out = row // k1_per_tile, row % k1_per_tile
    k1l_in, n2 = col // n2_len, col % n2_len
    k = t * k1_per_tile + k1l_out + DFT_N1 * k2
    ang2 = two_pi * ((n2 * k) % s).astype(F32) / s
    hit = k1l_in == k1l_out
    gr = jnp.where(hit, jnp.cos(ang2), 0.0)
    gi = jnp.where(hit, -jnp.sin(ang2), 0.0)
    g = jnp.concatenate([jnp.concatenate([gr, -gi], axis=2),
                         jnp.concatenate([gi, gr], axis=2)], axis=1).astype(BF16)
    return f1, g, cs


def _fft1_kernel(x_ref, f_ref, y_ref):
    nb = x_ref.shape[2]
    x = x_ref[0].reshape(DFT_N1 * nb, D_FNET).astype(BF16)
    r = jnp.dot(f_ref[...], x, preferred_element_type=F32)
    y_ref[0] = r.reshape(2, DFT_N1, nb, D_FNET)


def _fft2_kernel(y_ref, g_ref, cs_ref, w_ref, o_ref, *, scale, tiles):
    for t in range(tiles):
        rows = slice(t * DFT_N1, (t + 1) * DFT_N1)
        yb = jnp.concatenate([y_ref[0, 0, rows, :], y_ref[0, 1, rows, :]], axis=0).astype(BF16)
        z = jnp.dot(g_ref[t], yb, preferred_element_type=F32)
        zr = z[:DFT_N1].astype(BF16)
        zi = z[DFT_N1:].astype(BF16)
        parts = []
        for grp in range(FNET_GROUPS):
            sl = slice(grp * FNET_GROUP_DIM, (grp + 1) * FNET_GROUP_DIM)
            parts.append(jnp.dot(zr[:, sl], cs_ref[0], preferred_element_type=F32)
                         + jnp.dot(zi[:, sl], cs_ref[1], preferred_element_type=F32))
        fm = (jnp.concatenate(parts, axis=1) * scale).astype(BF16)
        out = jnp.dot(fm, w_ref[...], preferred_element_type=F32)
        if tiles == 1:
            o_ref[0] = out.reshape(o_ref.shape[1:])
        else:
            o_ref[0, :, t, :] = out


def _fourier_branch(f_in, w_out):
    bsz, s, _ = f_in.shape
    n2_len = s // DFT_N1
    k1_per_tile = DFT_N1 // n2_len
    f1, g, cs = _dft_tables(s)
    y = pl.pallas_call(
        _fft1_kernel,
        grid=(bsz, n2_len // SUBLANES),
        in_specs=[pl.BlockSpec((1, DFT_N1, SUBLANES, D_FNET), lambda b, j: (b, 0, j, 0)),
                  _full((2 * DFT_N1 * SUBLANES, DFT_N1 * SUBLANES))],
        out_specs=pl.BlockSpec((1, 2, DFT_N1, SUBLANES, D_FNET), lambda b, j: (b, 0, 0, j, 0)),
        out_shape=jax.ShapeDtypeStruct((bsz, 2, DFT_N1, n2_len, D_FNET), F32),
        compiler_params=_cparams("parallel", "parallel"),
        name="fft_stage1",
    )(f_in.reshape(bsz, DFT_N1, n2_len, D_FNET), jnp.kron(f1, jnp.eye(SUBLANES, dtype=BF16)))
    y = y.reshape(bsz, 2, s, D_FNET)
    scale = float((s * FNET_GROUP_DIM) ** -0.5)
    assert n2_len % SUBLANES == 0 and (k1_per_tile == 1 or k1_per_tile % SUBLANES == 0)
    tiles = SUBLANES if k1_per_tile == 1 else 1
    k1_per_step = tiles * k1_per_tile
    yb = pl.pallas_call(
        functools.partial(_fft2_kernel, scale=scale, tiles=tiles),
        grid=(bsz, s // (DFT_N1 * tiles)),
        in_specs=[
            pl.BlockSpec((1, 2, DFT_N1 * tiles, D_FNET), lambda b, t: (b, 0, t, 0)),
            pl.BlockSpec((tiles, 2 * DFT_N1, 2 * DFT_N1), lambda b, t: (t, 0, 0)),
            _full((2, DFT_N1, DFT_N1)),
            _full((D_FNET, D_MODEL)),
        ],
        out_specs=pl.BlockSpec((1, n2_len, k1_per_step, D_MODEL), lambda b, t: (b, 0, t, 0)),
        out_shape=jax.ShapeDtypeStruct((bsz, n2_len, DFT_N1, D_MODEL), F32),
        compiler_params=_cparams("parallel", "parallel"),
        name="fft_stage2",
    )(y, g, cs, w_out)
    return yb.reshape(bsz * s, D_MODEL)


def _mix_kernel(x_ref, ya_ref, yb_ref, cin_ref, ln0g_ref, ln0b_ref, wg_ref, bg_ref, wo_ref, bo_ref,
                ln1g_ref, ln1b_ref, wr_ref, br_ref,
                h1_ref, idx_ref, gate_ref, rank_ref, cnt_ref, carry_ref):
    tm = x_ref.shape[0]
    sub = min(SUB_MIX, tm)

    @pl.when(pl.program_id(0) == 0)
    def _():
        carry_ref[...] = cin_ref[...]

    carry = carry_ref[...]
    eio = lax.broadcasted_iota(I32, (N_EXPERTS, sub), 0)
    before = (lax.broadcasted_iota(I32, (sub, sub), 0) < lax.broadcasted_iota(I32, (sub, sub), 1)).astype(BF16)
    for r0 in range(0, tm, sub):
        rows = slice(r0, r0 + sub)
        h = _layer_norm(x_ref[rows, :], ln0g_ref[...], ln0b_ref[...])
        zg = jnp.dot(h.astype(BF16), wg_ref[...], preferred_element_type=F32) + bg_ref[...]
        m = (jax.nn.sigmoid(zg[:, :D_MODEL]) * ya_ref[rows, :].astype(F32)
             + jax.nn.sigmoid(zg[:, D_MODEL:]) * yb_ref[rows, :].astype(F32))
        mix = jnp.dot(m.astype(BF16), wo_ref[...], preferred_element_type=F32) + bo_ref[...]
        h1 = _layer_norm(DEEPNORM_ALPHA * h + mix, ln1g_ref[...], ln1b_ref[...])
        _to_row_tiles(h1_ref, r0, h1)

        h_hi = h1.astype(BF16)
        h_lo = (h1 - h_hi.astype(F32)).astype(BF16)
        hw = jnp.dot(h_hi, wr_ref[...], preferred_element_type=F32)
        lw = jnp.dot(h_lo, wr_ref[:, :LANES], preferred_element_type=F32)
        logits = (hw[:, :LANES] + hw[:, LANES:] + lw).T[:N_EXPERTS] + br_ref[...]
        vals, idxs = [], []
        cur = logits
        for _ in range(TOP_K):
            mx = jnp.max(cur, axis=0, keepdims=True)
            ik = jnp.min(jnp.where(cur == mx, eio, N_EXPERTS), axis=0, keepdims=True)
            vals.append(mx)
            idxs.append(ik)
            cur = jnp.where(eio == ik, -jnp.inf, cur)
        exps = [jnp.exp(v - vals[0]) for v in vals]
        den = exps[0] + exps[1] + exps[2] + exps[3]
        gate_ref[:, rows] = jnp.concatenate([e / den for e in exps], axis=0)
        idx_ref[:, rows] = jnp.concatenate(idxs, axis=0)

        hot = jnp.zeros((N_EXPERTS, sub), F32)
        for ik in idxs:
            hot = hot + (eio == ik).astype(F32)
        prior = jnp.dot(hot.astype(BF16), before, preferred_element_type=F32) + carry[:, 0:1]
        ranks = [jnp.sum(jnp.where(eio == ik, prior, 0.0), axis=0, keepdims=True) for ik in idxs]
        rank_ref[:, rows] = jnp.concatenate(ranks, axis=0).astype(I32)
        carry = carry + jnp.sum(hot, axis=1, keepdims=True)
    carry_ref[...] = carry
    cnt_ref[...] = carry


def _mix_route(x, ya, yb, carry_in, ln0_g, ln0_b, w_gate, b_gate, w_o, b_o, ln1_g, ln1_b, w_rt, b_rt):
    t = x.shape[0]
    tm = TM_MIX
    row = lambda d: pl.BlockSpec((tm, d), lambda i: (i, 0))
    col = lambda r: pl.BlockSpec((r, tm), lambda i: (0, i))
    return pl.pallas_call(
        _mix_kernel,
        grid=(t // tm,),
        in_specs=[
            row(D_MODEL), row(D_MODEL), row(D_MODEL), _full((N_EXPERTS, LANES)),
            _full((1, D_MODEL)), _full((1, D_MODEL)),
            _full((D_MODEL, 2 * D_MODEL)), _full((1, 2 * D_MODEL)),
            _full((D_MODEL, D_MODEL)), _full((1, D_MODEL)),
            _full((1, D_MODEL)), _full((1, D_MODEL)),
            _full((D_MODEL, 2 * LANES)), _full((N_EXPERTS, 1)),
        ],
        out_specs=[pl.BlockSpec((tm * SUBLANES, LANES), lambda i: (i, 0)), col(TOP_K), col(TOP_K), col(TOP_K),
                   _full((N_EXPERTS, LANES))],
        out_shape=[
            jax.ShapeDtypeStruct((t * SUBLANES, LANES), F32),
            jax.ShapeDtypeStruct((TOP_K, t), I32),
            jax.ShapeDtypeStruct((TOP_K, t), F32),
            jax.ShapeDtypeStruct((TOP_K, t), I32),
            jax.ShapeDtypeStruct((N_EXPERTS, LANES), F32),
        ],
        scratch_shapes=[pltpu.VMEM((N_EXPERTS, LANES), F32)],
        compiler_params=_cparams("arbitrary"),
        name="mix_route",
    )(x, ya, yb, carry_in, ln0_g, ln0_b, w_gate, b_gate, w_o, b_o, ln1_g, ln1_b, w_rt, b_rt)


def _row_copy(src_ref, src_row, dst_ref, dst_row, sem):
    return pltpu.make_async_copy(src_ref.at[src_row], dst_ref.at[dst_row], sem)


def _fetch_slots(slots_hbm, slots_smem, sem, step, slot):
    n = slots_hbm.shape[1]
    return pltpu.make_async_copy(slots_hbm.at[step], slots_smem.at[pl.ds(pl.multiple_of(slot * n, n), n)],
                                 sem.at[slot])


def _pad_fill(h_ref, xs_out, pad_smem, pad_sem, n_pad_runs, wait):
    def go(cp):
        if wait:
            cp.wait()
        else:
            cp.start()

    def run(j, carry):
        start = pad_smem[2 * j]
        length = pad_smem[2 * j + 1]
        for bit in range(BM_MOE.bit_length() - 1):
            size = 1 << bit

            @pl.when((length & size) != 0)
            def _():
                off = start + (length & ~(2 * size - 1))
                go(pltpu.make_async_copy(h_ref.at[pl.ds(0, size)], xs_out.at[pl.ds(off, size)], pad_sem))
        return carry

    lax.fori_loop(0, n_pad_runs, run, 0)

    tm = h_ref.shape[0]

    def dead_block(j, carry):
        for part in range(BM_MOE // tm):
            off = j * BM_MOE + part * tm
            go(pltpu.make_async_copy(h_ref, xs_out.at[pl.ds(off, tm)], pad_sem))
        return carry

    lax.fori_loop(pad_smem[2 * n_pad_runs], xs_out.shape[0] // BM_MOE, dead_block, 0)


def _dispatch_kernel(h_ref, slots_hbm, pad_hbm, *rest, n_pad_runs, aliased):
    xs_out, slots_smem, pad_smem, slot_sem, pad_sem, row_sem = rest[1:] if aliased else rest
    i = pl.program_id(0)
    n = pl.num_programs(0)
    tm = h_ref.shape[0]
    cur = lax.rem(i, 2)

    @pl.when(i == 0)
    def _():
        _fetch_slots(slots_hbm, slots_smem, slot_sem, 0, 0).start()

    @pl.when(i + 1 < n)
    def _():
        _fetch_slots(slots_hbm, slots_smem, slot_sem, i + 1, 1 - cur).start()

    if n_pad_runs:
        @pl.when(i == 0)
        def _():
            fetch = pltpu.make_async_copy(pad_hbm, pad_smem, pad_sem)
            fetch.start()
            fetch.wait()
            _pad_fill(h_ref, xs_out, pad_smem, pad_sem, n_pad_runs, wait=False)
            _pad_fill(h_ref, xs_out, pad_smem, pad_sem, n_pad_runs, wait=True)

    _fetch_slots(slots_hbm, slots_smem, slot_sem, i, cur).wait()

    base = cur * (TOP_K * tm)

    def issue(r, carry):
        for k in range(TOP_K):
            _row_copy(h_ref, r, xs_out, slots_smem[base + TOP_K * r + k], row_sem).start(priority=k % 2)
        return carry

    lax.fori_loop(0, tm, issue, 0, unroll=ISSUE_UNROLL)
    for _ in range(TOP_K):
        pltpu.make_async_copy(h_ref, xs_out.at[pl.ds(0, tm)], row_sem).wait()


def _dispatch(h1, slots, pad_runs, xs, n_rows):
    h1 = h1.reshape(-1, SUBLANES, LANES)
    t = h1.shape[0]
    tm = TM_ROWS
    aliased = xs is not None
    n_pad_runs = 0 if aliased else N_EXPERTS
    operands = (h1, slots, pad_runs) + ((xs,) if aliased else ())
    return pl.pallas_call(
        functools.partial(_dispatch_kernel, n_pad_runs=n_pad_runs, aliased=aliased),
        grid=(t // tm,),
        in_specs=[pl.BlockSpec((tm, SUBLANES, LANES), lambda i: (i, 0, 0))]
        + [pl.BlockSpec(memory_space=pl.ANY)] * (len(operands) - 1),
        out_specs=pl.BlockSpec(memory_space=pl.ANY),
        out_shape=jax.ShapeDtypeStruct((n_rows, SUBLANES, LANES), F32),
        input_output_aliases={3: 0} if aliased else {},
        scratch_shapes=[
            pltpu.SMEM((2 * TOP_K * tm,), I32),
            pltpu.SMEM((pad_runs.shape[0],), I32),
            pltpu.SemaphoreType.DMA((2,)),
            pltpu.SemaphoreType.DMA,
            pltpu.SemaphoreType.DMA,
        ],
        compiler_params=_cparams("arbitrary"),
        name="dispatch",
    )(*operands)


def _moe_kernel(blk_e_ref, nb_ref, x_ref, wgu_ref, bgu_ref, wd_ref, bd_ref, o_ref, wgu_b, wd_b):
    i = pl.program_id(0)
    is_live = i < nb_ref[0]

    @pl.when(jnp.logical_not(is_live))
    def _():
        o_ref[...] = jnp.zeros(o_ref.shape, F32)

    @pl.when(jnp.logical_or(i == 0, blk_e_ref[i] != blk_e_ref[jnp.maximum(i - 1, 0)]))
    def _():
        for r in range(0, D_MODEL, CAST_ROWS):
            wgu_b[r:r + CAST_ROWS, :] = wgu_ref[0, r:r + CAST_ROWS, :].astype(BF16)
        for r in range(0, D_FF, CAST_ROWS):
            wd_b[r:r + CAST_ROWS, :] = wd_ref[0, r:r + CAST_ROWS, :].astype(BF16)

    @pl.when(is_live)
    def _():
        x = _from_row_tiles(x_ref, 0, BM_MOE).astype(BF16)
        acc = jnp.zeros((BM_MOE, D_MODEL), F32) + bd_ref[0]
        for c in range(D_FF // FF_CHUNK):
            lo = c * FF_CHUNK
            g = jnp.dot(x, wgu_b[:, lo:lo + FF_CHUNK], preferred_element_type=F32) + bgu_ref[0, :, lo:lo + FF_CHUNK]
            u = (jnp.dot(x, wgu_b[:, D_FF + lo:D_FF + lo + FF_CHUNK], preferred_element_type=F32)
                 + bgu_ref[0, :, D_FF + lo:D_FF + lo + FF_CHUNK])
            g = jnp.minimum(g, SWIGLU_LIMIT)
            u = jnp.clip(u, -SWIGLU_LIMIT, SWIGLU_LIMIT)
            act = (u + 1.0) * (g * jax.nn.sigmoid(g * SWIGLU_ALPHA))
            acc = acc + jnp.dot(act.astype(BF16), wd_b[lo:lo + FF_CHUNK, :], preferred_element_type=F32)
        _to_row_tiles(o_ref, 0, acc)


def _moe_blocks(xs, blk_e, nb_used, w_gu, b_gu, w_down, b_down):
    p = xs.shape[0] // SUBLANES
    n_blocks = p // BM_MOE
    live = lambda i, nb: jnp.minimum(i, nb[0] - 1)
    grid_spec = pltpu.PrefetchScalarGridSpec(
        num_scalar_prefetch=2,
        grid=(n_blocks,),
        in_specs=[
            pl.BlockSpec((BM_MOE * SUBLANES, LANES), lambda i, be, nb: (live(i, nb), 0)),
            pl.BlockSpec((1, D_MODEL, 2 * D_FF), lambda i, be, nb: (be[i], 0, 0)),
            pl.BlockSpec((1, 1, 2 * D_FF), lambda i, be, nb: (be[i], 0, 0)),
            pl.BlockSpec((1, D_FF, D_MODEL), lambda i, be, nb: (be[i], 0, 0)),
            pl.BlockSpec((1, 1, D_MODEL), lambda i, be, nb: (be[i], 0, 0)),
        ],
        out_specs=pl.BlockSpec((BM_MOE * SUBLANES, LANES), lambda i, be, nb: (i, 0)),
        scratch_shapes=[pltpu.VMEM((D_MODEL, 2 * D_FF), BF16), pltpu.VMEM((D_FF, D_MODEL), BF16)],
    )
    return pl.pallas_call(
        _moe_kernel,
        grid_spec=grid_spec,
        out_shape=jax.ShapeDtypeStruct((p * SUBLANES, LANES), F32),
        compiler_params=pltpu.CompilerParams(dimension_semantics=("arbitrary",), vmem_limit_bytes=VMEM_LIMIT_MOE),
        name="moe_experts",
    )(blk_e, nb_used, xs, w_gu, b_gu, w_down, b_down)


def _final_kernel(h_ref, p_ref, gate_ref, slots_hbm, y_hbm, wpg_ref, bpg_ref, wpp_ref, g_ref, b_ref, o_ref,
                  slots_smem, rows_ref, slot_sem, row_sem):
    i = pl.program_id(0)
    n = pl.num_programs(0)
    tm = h_ref.shape[0] // SUBLANES
    cur = lax.rem(i, 2)
    nxt = 1 - cur

    def gather(r, buf, base):
        for k in range(TOP_K):
            slot = slots_smem[base + TOP_K * r + k]
            src = y_hbm.at[pl.ds(pl.multiple_of(slot * SUBLANES, SUBLANES), SUBLANES)]
            dst = rows_ref.at[buf, k, pl.ds(pl.multiple_of(r * SUBLANES, SUBLANES), SUBLANES)]
            pltpu.make_async_copy(src, dst, row_sem.at[buf]).start(priority=k % 2)

    def wait_rows(buf):
        for k in range(TOP_K):
            pltpu.make_async_copy(y_hbm.at[pl.ds(0, tm * SUBLANES)], rows_ref.at[buf, k], row_sem.at[buf]).wait()

    @pl.when(i == 0)
    def _():
        first = _fetch_slots(slots_hbm, slots_smem, slot_sem, 0, 0)
        first.start()
        first.wait()

        def issue(r, carry):
            gather(r, 0, 0)
            return carry

        lax.fori_loop(0, tm, issue, 0, unroll=ISSUE_UNROLL)
        _fetch_slots(slots_hbm, slots_smem, slot_sem, 1, 1).start()

    _fetch_slots(slots_hbm, slots_smem, slot_sem, i + 1, nxt).wait()

    @pl.when(i + 2 <= n)
    def _():
        _fetch_slots(slots_hbm, slots_smem, slot_sem, i + 2, cur).start()

    base = nxt * (TOP_K * tm)
    for r in range(tm):
        gather(r, nxt, base)

    h = _from_row_tiles(h_ref, 0, tm)
    ple = (jax.nn.sigmoid(jnp.dot(h.astype(BF16), wpg_ref[...], preferred_element_type=F32) + bpg_ref[...])
           * jnp.dot(p_ref[...].astype(BF16), wpp_ref[...], preferred_element_type=F32))

    wait_rows(cur)
    gates = gate_ref[...]
    ffn = gates[:, 0:1] * _from_row_tiles(rows_ref.at[cur, 0], 0, tm)
    for k in range(1, TOP_K):
        ffn = ffn + gates[:, k:k + 1] * _from_row_tiles(rows_ref.at[cur, k], 0, tm)
    o_ref[...] = _layer_norm(DEEPNORM_ALPHA * h + (ffn + ple), g_ref[...], b_ref[...])

    @pl.when(i == n - 1)
    def _():
        wait_rows(nxt)


def _combine_final(h1, p, gates, slots, y, w_pg, b_pg, w_pp, ln_g, ln_b):
    t = h1.shape[0] // SUBLANES
    tm = TM_COMBINE
    slots = jnp.concatenate([slots, slots[-1:]], axis=0)
    return pl.pallas_call(
        _final_kernel,
        grid=(t // tm,),
        in_specs=[
            pl.BlockSpec((tm * SUBLANES, LANES), lambda i: (i, 0)),
            pl.BlockSpec((tm, D_PLE), lambda i: (i, 0)),
            pl.BlockSpec((tm, TOP_K), lambda i: (i, 0)),
            pl.BlockSpec(memory_space=pl.ANY),
            pl.BlockSpec(memory_space=pl.ANY),
            _full((D_MODEL, D_MODEL)), _full((1, D_MODEL)), _full((D_PLE, D_MODEL)),
            _full((1, D_MODEL)), _full((1, D_MODEL)),
        ],
        out_specs=pl.BlockSpec((tm, D_MODEL), lambda i: (i, 0)),
        out_shape=jax.ShapeDtypeStruct((t, D_MODEL), F32),
        scratch_shapes=[
            pltpu.SMEM((2 * TOP_K * tm,), I32),
            pltpu.VMEM((2, TOP_K, tm * SUBLANES, LANES), F32),
            pltpu.SemaphoreType.DMA((2,)),
            pltpu.SemaphoreType.DMA((2,)),
        ],
        compiler_params=_cparams("arbitrary"),
        name="combine_final",
    )(h1, p, gates, slots, y, w_pg, b_pg, w_pp, ln_g, ln_b)


def _tile_slots(dest, tm):
    t = dest.shape[1]
    return dest.T.reshape(t // tm, tm * TOP_K)


def kernel(x_prompt, x_sample, p_prompt, p_sample, ln0_g, ln0_b, w_in, b_in, conv_w, conv_b, lnc_g, lnc_b, w_conv_out, w_fnet_out, w_o, b_o, ln1_g, ln1_b, w_router, b_router, w_gu, b_gu, w_down, b_down, w_pg, b_pg, w_pp, ln2_g, ln2_b):
    row = lambda v: v.reshape(1, -1).astype(F32)
    n_branch = 2 * D_CONV + D_FNET
    w_in_b = w_in[0].astype(BF16)
    w_branch, w_gate = w_in_b[:, :n_branch], w_in_b[:, n_branch:]
    b_branch, b_gate = row(b_in[0, :n_branch]), row(b_in[0, n_branch:])
    w_conv_out_b = w_conv_out[0].astype(BF16)
    w_fnet_out_b = w_fnet_out[0].astype(BF16)
    w_o_b = w_o[0].astype(BF16)
    w_rt_f = jnp.pad(w_router[0].astype(F32), ((0, 0), (0, LANES - N_EXPERTS)))
    w_rt_hi = w_rt_f.astype(BF16)
    w_rt = jnp.concatenate([w_rt_hi, (w_rt_f - w_rt_hi.astype(F32)).astype(BF16)], axis=1)
    b_rt = b_router[0].reshape(N_EXPERTS, 1).astype(F32)
    b_gu_r = b_gu[0].reshape(N_EXPERTS, 1, 2 * D_FF).astype(F32)
    b_down_r = b_down[0].reshape(N_EXPERTS, 1, D_MODEL).astype(F32)
    w_pg_b = w_pg[0].astype(BF16)
    w_pp_b = w_pp[0].astype(BF16)

    carry = jnp.zeros((N_EXPERTS, LANES), F32)
    routed = []
    for x, p in ((x_prompt, p_prompt[0]), (x_sample, p_sample[0])):
        bsz, s, _ = x.shape
        xt = x.reshape(bsz * s, D_MODEL)
        u, f_in = _inproj(xt, row(ln0_g), row(ln0_b), w_branch, b_branch)
        ya = _conv_branch(u.reshape(bsz, s, D_CONV), conv_w[0].astype(F32), row(conv_b[0]), row(lnc_g[0]),
                          row(lnc_b[0]), w_conv_out_b).reshape(bsz * s, D_MODEL)
        yb = _fourier_branch(f_in.reshape(bsz, s, D_FNET), w_fnet_out_b)
        h1, idx, gates, rank, carry = _mix_route(
            xt, ya, yb, carry, row(ln0_g), row(ln0_b), w_gate, b_gate, w_o_b, row(b_o[0]),
            row(ln1_g[0]), row(ln1_b[0]), w_rt, b_rt)
        routed.append((h1, idx, gates, rank, p.reshape(bsz * s, D_PLE), (bsz, s)))

    n_assign = sum(r[1].shape[1] for r in routed) * TOP_K
    n_blocks = (n_assign + N_EXPERTS * (BM_MOE - 1)) // BM_MOE
    counts = carry[:, 0].astype(I32)
    padded = ((counts + BM_MOE - 1) // BM_MOE) * BM_MOE
    ends = jnp.cumsum(padded)
    start = ends - padded
    nb_used = (ends[-1] // BM_MOE).reshape(1).astype(I32)
    blk = jnp.minimum(jnp.arange(n_blocks, dtype=I32), nb_used[0] - 1)
    blk_e = jnp.minimum(jnp.sum((ends[None, :] <= (blk * BM_MOE)[:, None]).astype(I32), axis=1), N_EXPERTS - 1)
    pad_runs = jnp.concatenate([jnp.stack([start + counts, padded - counts], axis=1).reshape(-1), nb_used,
                                jnp.zeros((LANES - 2 * N_EXPERTS - 1,), I32)]).astype(I32)

    xs = None
    slot_tables = []
    for h1, idx, _, rank, _, _ in routed:
        first = jnp.sum(jnp.where(idx[None] == jnp.arange(N_EXPERTS, dtype=I32)[:, None, None],
                                  start[:, None, None], 0), axis=0)
        slot_tables.append(_tile_slots(first + rank, TM_COMBINE))
        xs = _dispatch(h1, _tile_slots(first + rank, TM_ROWS), pad_runs, xs, n_blocks * BM_MOE)

    y = _moe_blocks(xs.reshape(-1, LANES), blk_e, nb_used, w_gu.reshape(N_EXPERTS, D_MODEL, 2 * D_FF), b_gu_r,
                    w_down.reshape(N_EXPERTS, D_FF, D_MODEL), b_down_r)

    outs = []
    for (h1, _, gates, _, p, (bsz, s)), slots in zip(routed, slot_tables):
        o = _combine_final(h1, p, gates.T, slots, y, w_pg_b, row(b_pg[0]), w_pp_b, row(ln2_g[0]), row(ln2_b[0]))
        outs.append(o.reshape(bsz, s, D_MODEL))
    return tuple(outs)
```

```python
import functools

import jax
import jax.numpy as jnp
from jax import lax
from jax.experimental import pallas as pl
from jax.experimental.pallas import tpu as pltpu

F32 = jnp.float32
BF16 = jnp.bfloat16
I32 = jnp.int32

D_MODEL = 1024
D_CONV = D_MODEL // 2
CONV_WIDTH = 31
CONV_PAD = CONV_WIDTH // 2
D_FNET = D_MODEL // 2
FNET_GROUPS = 4
FNET_GROUP_DIM = D_FNET // FNET_GROUPS
D_PLE = 256
N_EXPERTS = 32
TOP_K = 4
D_FF = D_MODEL
SWIGLU_ALPHA = 1.702
SWIGLU_LIMIT = 7.0
LN_EPS = 1e-5
DEPTH = 1
DEEPNORM_ALPHA = (2 * DEPTH) ** 0.25

LANES = 128
SUBLANES = 8
DFT_N1 = 128
HALO = 16

TM_INPROJ = 512
TS_CONV = 512
RC_CONV = 128
TM_MIX = 512
SUB_MIX = 512
FFT2_TILES = 4
TM_ROWS = 512
TM_COMBINE = 256
BM_MOE = 512
FF_CHUNK = 512
ISSUE_UNROLL = 8
CAST_ROWS = 128
VMEM_LIMIT = 48 * 1024 * 1024
VMEM_LIMIT_MOE = 56 * 1024 * 1024


def _cparams(*sem):
    return pltpu.CompilerParams(dimension_semantics=sem, vmem_limit_bytes=VMEM_LIMIT)


def _layer_norm(x, g, b):
    mu = jnp.mean(x, axis=-1, keepdims=True)
    xc = x - mu
    var = jnp.mean(xc * xc, axis=-1, keepdims=True)
    return xc * lax.rsqrt(var + LN_EPS) * g + b


def _full(shape):
    return pl.BlockSpec(shape, lambda *_: (0,) * len(shape))


def _to_row_tiles(ref, row0, x):
    rows = x.shape[0]
    for j in range(D_MODEL // LANES):
        ref[pl.ds(row0 * SUBLANES + j, rows, stride=SUBLANES), :] = x[:, j * LANES:(j + 1) * LANES]


def _from_row_tiles(ref, row0, rows):
    return jnp.concatenate([ref[pl.ds(row0 * SUBLANES + j, rows, stride=SUBLANES), :]
                            for j in range(D_MODEL // LANES)], axis=1)


def _inproj_kernel(x_ref, g_ref, b_ref, w_ref, bias_ref, u_ref, f_ref):
    h = _layer_norm(x_ref[...], g_ref[...], b_ref[...])
    z = jnp.dot(h.astype(BF16), w_ref[...], preferred_element_type=F32) + bias_ref[...]
    u_ref[...] = (z[:, :D_CONV] * jax.nn.sigmoid(z[:, D_CONV:2 * D_CONV])).astype(BF16)
    f_ref[...] = z[:, 2 * D_CONV:]


def _inproj(x, ln_g, ln_b, w, bias):
    t = x.shape[0]
    n_out = w.shape[1]
    return pl.pallas_call(
        _inproj_kernel,
        grid=(t // TM_INPROJ,),
        in_specs=[
            pl.BlockSpec((TM_INPROJ, D_MODEL), lambda i: (i, 0)),
            _full((1, D_MODEL)), _full((1, D_MODEL)),
            _full((D_MODEL, n_out)), _full((1, n_out)),
        ],
        out_specs=[
            pl.BlockSpec((TM_INPROJ, D_CONV), lambda i: (i, 0)),
            pl.BlockSpec((TM_INPROJ, D_FNET), lambda i: (i, 0)),
        ],
        out_shape=[jax.ShapeDtypeStruct((t, D_CONV), BF16), jax.ShapeDtypeStruct((t, D_FNET), F32)],
        compiler_params=_cparams("parallel"),
        name="inproj",
    )(x, ln_g, ln_b, w, bias)


def _conv_kernel(prev_ref, cur_ref, next_ref, cw_ref, cb_ref, g_ref, b_ref, wout_ref, ya_ref,
                 ext_ref, act_ref):
    i = pl.program_id(1)
    last = pl.num_programs(1) - 1
    ts = cur_ref.shape[1]
    ext_rows = ts + 2 * HALO
    zero = jnp.zeros((HALO, D_CONV), F32)
    ext_ref[0, 0:HALO, :] = jnp.where(i > 0, prev_ref[0].astype(F32), zero)
    ext_ref[0, HALO:HALO + ts, :] = cur_ref[0].astype(F32)
    ext_ref[0, HALO + ts:ext_rows, :] = jnp.where(i < last, next_ref[0].astype(F32), zero)
    for r in range(1, SUBLANES):
        ext_ref[r, 0:ext_rows - SUBLANES, :] = ext_ref[0, r:r + ext_rows - SUBLANES, :]

    cb = cb_ref[...]
    g = g_ref[...]
    b = b_ref[...]

    def chunk(c, carry):
        r0 = pl.multiple_of(c * RC_CONV, RC_CONV)
        acc = jnp.zeros((RC_CONV, D_CONV), F32) + cb
        for j in range(CONV_WIDTH):
            off = j + HALO - CONV_PAD
            q, r = divmod(off, SUBLANES)
            acc = acc + cw_ref[j:j + 1, :] * ext_ref[r, pl.ds(r0 + q * SUBLANES, RC_CONV), :]
        y = _layer_norm(acc, g, b)
        act_ref[pl.ds(r0, RC_CONV), :] = (y * jax.nn.sigmoid(y)).astype(BF16)
        return carry

    lax.fori_loop(0, ts // RC_CONV, chunk, 0)
    ya_ref[0] = jnp.dot(act_ref[...], wout_ref[...], preferred_element_type=F32).astype(BF16)


def _conv_branch(u, conv_w, conv_b, ln_g, ln_b, w_out):
    bsz, s, _ = u.shape
    ts = TS_CONV
    nb = s // ts
    hb = ts // HALO
    return pl.pallas_call(
        _conv_kernel,
        grid=(bsz, nb),
        in_specs=[
            pl.BlockSpec((1, HALO, D_CONV), lambda b, i: (b, jnp.maximum(i * hb - 1, 0), 0)),
            pl.BlockSpec((1, ts, D_CONV), lambda b, i: (b, i, 0)),
            pl.BlockSpec((1, HALO, D_CONV), lambda b, i: (b, jnp.minimum((i + 1) * hb, s // HALO - 1), 0)),
            _full((CONV_WIDTH, D_CONV)), _full((1, D_CONV)), _full((1, D_CONV)), _full((1, D_CONV)),
            _full((D_CONV, D_MODEL)),
        ],
        out_specs=pl.BlockSpec((1, ts, D_MODEL), lambda b, i: (b, i, 0)),
        out_shape=jax.ShapeDtypeStruct((bsz, s, D_MODEL), BF16),
        scratch_shapes=[
            pltpu.VMEM((SUBLANES, ts + 2 * HALO, D_CONV), F32),
            pltpu.VMEM((ts, D_CONV), BF16),
        ],
        compiler_params=_cparams("parallel", "parallel"),
        name="conv_branch",
    )(u, u, u, conv_w, conv_b, ln_g, ln_b, w_out)


def _dft_tables(s):
    n2_len = s // DFT_N1
    k1_per_tile = DFT_N1 // n2_len
    two_pi = 2.0 * jnp.pi
    a = jnp.arange(DFT_N1, dtype=I32)
    ang1 = two_pi * ((a[:, None] * a[None, :]) % DFT_N1).astype(F32) / DFT_N1
    r = jnp.arange(2 * DFT_N1 * SUBLANES, dtype=I32)[:, None]
    c = jnp.arange(DFT_N1 * SUBLANES, dtype=I32)[None, :]
    k1r = (r % (DFT_N1 * SUBLANES)) // SUBLANES
    ang = two_pi * ((k1r * (c // SUBLANES)) % DFT_N1).astype(F32) / DFT_N1
    f1 = jnp.where(r % SUBLANES == c % SUBLANES,
                   jnp.where(r < DFT_N1 * SUBLANES, jnp.cos(ang), -jnp.sin(ang)), 0.0).astype(BF16)
    cs = jnp.stack([jnp.cos(ang1), jnp.sin(ang1)]).astype(BF16)
    n_tiles = s // DFT_N1
    t = jnp.arange(n_tiles, dtype=I32)[:, None, None]
    row = jnp.arange(DFT_N1, dtype=I32)[None, :, None]
    col = jnp.arange(DFT_N1, dtype=I32)[None, None, :]
    k2, k1l_out = row // k1_per_tile, row % k1_per_tile
    k1l_in, n2 = col // n2_len, col % n2_len
    k = t * k1_per_tile + k1l_out + DFT_N1 * k2
    ang2 = two_pi * ((n2 * k) % s).astype(F32) / s
    hit = k1l_in == k1l_out
    gr = jnp.where(hit, jnp.cos(ang2), 0.0)
    gi = jnp.where(hit, -jnp.sin(ang2), 0.0)
    g = jnp.concatenate([jnp.concatenate([gr, -gi], axis=2),
                         jnp.concatenate([gi, gr], axis=2)], axis=1).astype(BF16)
    return f1, g, cs


def _fft1_kernel(x_ref, f_ref, y_ref):
    nb = x_ref.shape[2]
    x = x_ref[0].reshape(DFT_N1 * nb, D_FNET).astype(BF16)
    r = jnp.dot(f_ref[...], x, preferred_element_type=F32)
    y_ref[0] = r.reshape(2, DFT_N1, nb, D_FNET)


def _fft2_kernel(y_ref, g_ref, cs_ref, w_ref, o_ref, *, scale, tiles):
    for t in range(tiles):
        rows = slice(t * DFT_N1, (t + 1) * DFT_N1)
        yb = jnp.concatenate([y_ref[0, 0, rows, :], y_ref[0, 1, rows, :]], axis=0).astype(BF16)
        z = jnp.dot(g_ref[t], yb, preferred_element_type=F32)
        zr = z[:DFT_N1].astype(BF16)
        zi = z[DFT_N1:].astype(BF16)
        parts = []
        for grp in range(FNET_GROUPS):
            sl = slice(grp * FNET_GROUP_DIM, (grp + 1) * FNET_GROUP_DIM)
            parts.append(jnp.dot(zr[:, sl], cs_ref[0], preferred_element_type=F32)
                         + jnp.dot(zi[:, sl], cs_ref[1], preferred_element_type=F32))
        fm = (jnp.concatenate(parts, axis=1) * scale).astype(BF16)
        out = jnp.dot(fm, w_ref[...], preferred_element_type=F32)
        k1_per_tile = o_ref.shape[2] // tiles
        if k1_per_tile == 1:
            o_ref[0, :, t, :] = out
        else:
            o_ref[0, :, t * k1_per_tile:(t + 1) * k1_per_tile, :] = out.reshape(o_ref.shape[1], k1_per_tile, D_MODEL)


def _fourier_branch(f_in, w_out):
    bsz, s, _ = f_in.shape
    n2_len = s // DFT_N1
    k1_per_tile = DFT_N1 // n2_len
    f1, g, cs = _dft_tables(s)
    y = pl.pallas_call(
        _fft1_kernel,
        grid=(bsz, n2_len // SUBLANES),
        in_specs=[pl.BlockSpec((1, DFT_N1, SUBLANES, D_FNET), lambda b, j: (b, 0, j, 0)),
                  _full((2 * DFT_N1 * SUBLANES, DFT_N1 * SUBLANES))],
        out_specs=pl.BlockSpec((1, 2, DFT_N1, SUBLANES, D_FNET), lambda b, j: (b, 0, 0, j, 0)),
        out_shape=jax.ShapeDtypeStruct((bsz, 2, DFT_N1, n2_len, D_FNET), F32),
        compiler_params=_cparams("parallel", "parallel"),
        name="fft_stage1",
    )(f_in.reshape(bsz, DFT_N1, n2_len, D_FNET), f1)
    y = y.reshape(bsz, 2, s, D_FNET)
    scale = float((s * FNET_GROUP_DIM) ** -0.5)
    assert n2_len % SUBLANES == 0 and (k1_per_tile == 1 or k1_per_tile % SUBLANES == 0)
    tiles = SUBLANES if k1_per_tile == 1 else min(FFT2_TILES, n2_len)
    k1_per_step = tiles * k1_per_tile
    yb = pl.pallas_call(
        functools.partial(_fft2_kernel, scale=scale, tiles=tiles),
        grid=(bsz, s // (DFT_N1 * tiles)),
        in_specs=[
            pl.BlockSpec((1, 2, DFT_N1 * tiles, D_FNET), lambda b, t: (b, 0, t, 0)),
            pl.BlockSpec((tiles, 2 * DFT_N1, 2 * DFT_N1), lambda b, t: (t, 0, 0)),
            _full((2, DFT_N1, DFT_N1)),
            _full((D_FNET, D_MODEL)),
        ],
        out_specs=pl.BlockSpec((1, n2_len, k1_per_step, D_MODEL), lambda b, t: (b, 0, t, 0)),
        out_shape=jax.ShapeDtypeStruct((bsz, n2_len, DFT_N1, D_MODEL), F32),
        compiler_params=_cparams("parallel", "parallel"),
        name="fft_stage2",
    )(y, g, cs, w_out)
    return yb.reshape(bsz * s, D_MODEL)


def _mix_kernel(x_ref, ya_ref, yb_ref, cin_ref, ln0g_ref, ln0b_ref, wg_ref, bg_ref, wo_ref, bo_ref,
                ln1g_ref, ln1b_ref, wr_ref, br_ref,
                h1_ref, idx_ref, gate_ref, rank_ref, cnt_ref, carry_ref):
    tm = x_ref.shape[0]
    sub = min(SUB_MIX, tm)

    @pl.when(pl.program_id(0) == 0)
    def _():
        carry_ref[...] = cin_ref[...]

    carry = carry_ref[...]
    eio = lax.broadcasted_iota(I32, (N_EXPERTS, sub), 0)
    before = (lax.broadcasted_iota(I32, (sub, sub), 0) < lax.broadcasted_iota(I32, (sub, sub), 1)).astype(BF16)
    for r0 in range(0, tm, sub):
        rows = slice(r0, r0 + sub)
        h = _layer_norm(x_ref[rows, :], ln0g_ref[...], ln0b_ref[...])
        zg = jnp.dot(h.astype(BF16), wg_ref[...], preferred_element_type=F32) + bg_ref[...]
        m = (jax.nn.sigmoid(zg[:, :D_MODEL]) * ya_ref[rows, :].astype(F32)
             + jax.nn.sigmoid(zg[:, D_MODEL:]) * yb_ref[rows, :].astype(F32))
        mix = jnp.dot(m.astype(BF16), wo_ref[...], preferred_element_type=F32) + bo_ref[...]
        h1 = _layer_norm(DEEPNORM_ALPHA * h + mix, ln1g_ref[...], ln1b_ref[...])
        _to_row_tiles(h1_ref, r0, h1)

        h_hi = h1.astype(BF16)
        h_lo = (h1 - h_hi.astype(F32)).astype(BF16)
        hw = jnp.dot(h_hi, wr_ref[...], preferred_element_type=F32)
        lw = jnp.dot(h_lo, wr_ref[:, :LANES], preferred_element_type=F32)
        logits = (hw[:, :LANES] + hw[:, LANES:] + lw).T[:N_EXPERTS] + br_ref[...]
        vals, idxs = [], []
        cur = logits
        for _ in range(TOP_K):
            mx = jnp.max(cur, axis=0, keepdims=True)
            ik = jnp.min(jnp.where(cur == mx, eio, N_EXPERTS), axis=0, keepdims=True)
            vals.append(mx)
            idxs.append(ik)
            cur = jnp.where(eio == ik, -jnp.inf, cur)
        exps = [jnp.exp(v - vals[0]) for v in vals]
        den = exps[0] + exps[1] + exps[2] + exps[3]
        gate_ref[:, rows] = jnp.concatenate([e / den for e in exps], axis=0)
        idx_ref[:, rows] = jnp.concatenate(idxs, axis=0)

        hot = jnp.zeros((N_EXPERTS, sub), F32)
        for ik in idxs:
            hot = hot + (eio == ik).astype(F32)
        prior = jnp.dot(hot.astype(BF16), before, preferred_element_type=F32) + carry[:, 0:1]
        ranks = [jnp.sum(jnp.where(eio == ik, prior, 0.0), axis=0, keepdims=True) for ik in idxs]
        rank_ref[:, rows] = jnp.concatenate(ranks, axis=0).astype(I32)
        carry = carry + jnp.sum(hot, axis=1, keepdims=True)
    carry_ref[...] = carry
    cnt_ref[...] = carry


def _mix_route(x, ya, yb, carry_in, ln0_g, ln0_b, w_gate, b_gate, w_o, b_o, ln1_g, ln1_b, w_rt, b_rt):
    t = x.shape[0]
    tm = TM_MIX
    row = lambda d: pl.BlockSpec((tm, d), lambda i: (i, 0))
    col = lambda r: pl.BlockSpec((r, tm), lambda i: (0, i))
    return pl.pallas_call(
        _mix_kernel,
        grid=(t // tm,),
        in_specs=[
            row(D_MODEL), row(D_MODEL), row(D_MODEL), _full((N_EXPERTS, LANES)),
            _full((1, D_MODEL)), _full((1, D_MODEL)),
            _full((D_MODEL, 2 * D_MODEL)), _full((1, 2 * D_MODEL)),
            _full((D_MODEL, D_MODEL)), _full((1, D_MODEL)),
            _full((1, D_MODEL)), _full((1, D_MODEL)),
            _full((D_MODEL, 2 * LANES)), _full((N_EXPERTS, 1)),
        ],
        out_specs=[pl.BlockSpec((tm * SUBLANES, LANES), lambda i: (i, 0)), col(TOP_K), col(TOP_K), col(TOP_K),
                   _full((N_EXPERTS, LANES))],
        out_shape=[
            jax.ShapeDtypeStruct((t * SUBLANES, LANES), F32),
            jax.ShapeDtypeStruct((TOP_K, t), I32),
            jax.ShapeDtypeStruct((TOP_K, t), F32),
            jax.ShapeDtypeStruct((TOP_K, t), I32),
            jax.ShapeDtypeStruct((N_EXPERTS, LANES), F32),
        ],
        scratch_shapes=[pltpu.VMEM((N_EXPERTS, LANES), F32)],
        compiler_params=_cparams("arbitrary"),
        name="mix_route",
    )(x, ya, yb, carry_in, ln0_g, ln0_b, w_gate, b_gate, w_o, b_o, ln1_g, ln1_b, w_rt, b_rt)


def _row_copy(src_ref, src_row, dst_ref, dst_row, sem):
    return pltpu.make_async_copy(src_ref.at[src_row], dst_ref.at[dst_row], sem)


def _fetch_slots(slots_hbm, slots_smem, sem, step, slot):
    n = slots_hbm.shape[1]
    return pltpu.make_async_copy(slots_hbm.at[step], slots_smem.at[pl.ds(pl.multiple_of(slot * n, n), n)],
                                 sem.at[slot])


def _pad_fill(h_ref, xs_out, pad_smem, pad_sem, n_pad_runs, wait):
    def go(cp):
        if wait:
            cp.wait()
        else:
            cp.start()

    def run(j, carry):
        start = pad_smem[2 * j]
        length = pad_smem[2 * j + 1]
        for bit in range(BM_MOE.bit_length() - 1):
            size = 1 << bit

            @pl.when((length & size) != 0)
            def _():
                off = start + (length & ~(2 * size - 1))
                go(pltpu.make_async_copy(h_ref.at[pl.ds(0, size)], xs_out.at[pl.ds(off, size)], pad_sem))
        return carry

    lax.fori_loop(0, n_pad_runs, run, 0)

    tm = h_ref.shape[0]

    def dead_block(j, carry):
        for part in range(BM_MOE // tm):
            off = j * BM_MOE + part * tm
            go(pltpu.make_async_copy(h_ref, xs_out.at[pl.ds(off, tm)], pad_sem))
        return carry

    lax.fori_loop(pad_smem[2 * n_pad_runs], xs_out.shape[0] // BM_MOE, dead_block, 0)


def _dispatch_kernel(h_ref, slots_hbm, pad_hbm, *rest, n_pad_runs, aliased):
    xs_out, slots_smem, pad_smem, slot_sem, pad_sem, row_sem = rest[1:] if aliased else rest
    i = pl.program_id(0)
    n = pl.num_programs(0)
    tm = h_ref.shape[0]
    cur = lax.rem(i, 2)

    @pl.when(i == 0)
    def _():
        _fetch_slots(slots_hbm, slots_smem, slot_sem, 0, 0).start()

    @pl.when(i + 1 < n)
    def _():
        _fetch_slots(slots_hbm, slots_smem, slot_sem, i + 1, 1 - cur).start()

    if n_pad_runs:
        @pl.when(i == 0)
        def _():
            fetch = pltpu.make_async_copy(pad_hbm, pad_smem, pad_sem)
            fetch.start()
            fetch.wait()
            _pad_fill(h_ref, xs_out, pad_smem, pad_sem, n_pad_runs, wait=False)
            _pad_fill(h_ref, xs_out, pad_smem, pad_sem, n_pad_runs, wait=True)

    _fetch_slots(slots_hbm, slots_smem, slot_sem, i, cur).wait()

    base = cur * (TOP_K * tm)

    def issue(r, carry):
        for k in range(TOP_K):
            _row_copy(h_ref, r, xs_out, slots_smem[base + TOP_K * r + k], row_sem).start(priority=k % 2)
        return carry

    lax.fori_loop(0, tm, issue, 0, unroll=ISSUE_UNROLL)
    for _ in range(TOP_K):
        pltpu.make_async_copy(h_ref, xs_out.at[pl.ds(0, tm)], row_sem).wait()


def _dispatch(h1, slots, pad_runs, xs, n_rows):
    h1 = h1.reshape(-1, SUBLANES, LANES)
    t = h1.shape[0]
    tm = TM_ROWS
    aliased = xs is not None
    n_pad_runs = 0 if aliased else N_EXPERTS
    operands = (h1, slots, pad_runs) + ((xs,) if aliased else ())
    return pl.pallas_call(
        functools.partial(_dispatch_kernel, n_pad_runs=n_pad_runs, aliased=aliased),
        grid=(t // tm,),
        in_specs=[pl.BlockSpec((tm, SUBLANES, LANES), lambda i: (i, 0, 0))]
        + [pl.BlockSpec(memory_space=pl.ANY)] * (len(operands) - 1),
        out_specs=pl.BlockSpec(memory_space=pl.ANY),
        out_shape=jax.ShapeDtypeStruct((n_rows, SUBLANES, LANES), F32),
        input_output_aliases={3: 0} if aliased else {},
        scratch_shapes=[
            pltpu.SMEM((2 * TOP_K * tm,), I32),
            pltpu.SMEM((pad_runs.shape[0],), I32),
            pltpu.SemaphoreType.DMA((2,)),
            pltpu.SemaphoreType.DMA,
            pltpu.SemaphoreType.DMA,
        ],
        compiler_params=_cparams("arbitrary"),
        name="dispatch",
    )(*operands)


def _moe_kernel(blk_e_ref, nb_ref, x_ref, wgu_ref, bgu_ref, wd_ref, bd_ref, o_ref, wgu_b, wd_b):
    i = pl.program_id(0)
    is_live = i < nb_ref[0]

    @pl.when(jnp.logical_not(is_live))
    def _():
        o_ref[...] = jnp.zeros(o_ref.shape, F32)

    @pl.when(jnp.logical_or(i == 0, blk_e_ref[i] != blk_e_ref[jnp.maximum(i - 1, 0)]))
    def _():
        for r in range(0, D_MODEL, CAST_ROWS):
            wgu_b[r:r + CAST_ROWS, :] = wgu_ref[0, r:r + CAST_ROWS, :].astype(BF16)
        for r in range(0, D_FF, CAST_ROWS):
            wd_b[r:r + CAST_ROWS, :] = wd_ref[0, r:r + CAST_ROWS, :].astype(BF16)

    @pl.when(is_live)
    def _():
        x = _from_row_tiles(x_ref, 0, BM_MOE).astype(BF16)
        acc = jnp.zeros((BM_MOE, D_MODEL), F32) + bd_ref[0]
        for c in range(D_FF // FF_CHUNK):
            lo = c * FF_CHUNK
            g = jnp.dot(x, wgu_b[:, lo:lo + FF_CHUNK], preferred_element_type=F32) + bgu_ref[0, :, lo:lo + FF_CHUNK]
            u = (jnp.dot(x, wgu_b[:, D_FF + lo:D_FF + lo + FF_CHUNK], preferred_element_type=F32)
                 + bgu_ref[0, :, D_FF + lo:D_FF + lo + FF_CHUNK])
            g = jnp.minimum(g, SWIGLU_LIMIT)
            u = jnp.clip(u, -SWIGLU_LIMIT, SWIGLU_LIMIT)
            act = (u + 1.0) * (g * jax.nn.sigmoid(g * SWIGLU_ALPHA))
            acc = acc + jnp.dot(act.astype(BF16), wd_b[lo:lo + FF_CHUNK, :], preferred_element_type=F32)
        _to_row_tiles(o_ref, 0, acc)


def _moe_blocks(xs, blk_e, nb_used, w_gu, b_gu, w_down, b_down):
    p = xs.shape[0] // SUBLANES
    n_blocks = p // BM_MOE
    live = lambda i, nb: jnp.minimum(i, nb[0] - 1)
    grid_spec = pltpu.PrefetchScalarGridSpec(
        num_scalar_prefetch=2,
        grid=(n_blocks,),
        in_specs=[
            pl.BlockSpec((BM_MOE * SUBLANES, LANES), lambda i, be, nb: (live(i, nb), 0)),
            pl.BlockSpec((1, D_MODEL, 2 * D_FF), lambda i, be, nb: (be[i], 0, 0)),
            pl.BlockSpec((1, 1, 2 * D_FF), lambda i, be, nb: (be[i], 0, 0)),
            pl.BlockSpec((1, D_FF, D_MODEL), lambda i, be, nb: (be[i], 0, 0)),
            pl.BlockSpec((1, 1, D_MODEL), lambda i, be, nb: (be[i], 0, 0)),
        ],
        out_specs=pl.BlockSpec((BM_MOE * SUBLANES, LANES), lambda i, be, nb: (i, 0)),
        scratch_shapes=[pltpu.VMEM((D_MODEL, 2 * D_FF), BF16), pltpu.VMEM((D_FF, D_MODEL), BF16)],
    )
    return pl.pallas_call(
        _moe_kernel,
        grid_spec=grid_spec,
        out_shape=jax.ShapeDtypeStruct((p * SUBLANES, LANES), F32),
        compiler_params=pltpu.CompilerParams(dimension_semantics=("arbitrary",), vmem_limit_bytes=VMEM_LIMIT_MOE),
        name="moe_experts",
    )(blk_e, nb_used, xs, w_gu, b_gu, w_down, b_down)


def _final_kernel(h_ref, p_ref, gate_ref, slots_hbm, y_hbm, wpg_ref, bpg_ref, wpp_ref, g_ref, b_ref, o_ref,
                  slots_smem, rows_ref, slot_sem, row_sem):
    i = pl.program_id(0)
    n = pl.num_programs(0)
    tm = h_ref.shape[0] // SUBLANES
    cur = lax.rem(i, 2)
    nxt = 1 - cur

    def gather(r, buf, base):
        for k in range(TOP_K):
            slot = slots_smem[base + TOP_K * r + k]
            src = y_hbm.at[pl.ds(pl.multiple_of(slot * SUBLANES, SUBLANES), SUBLANES)]
            dst = rows_ref.at[buf, k, pl.ds(pl.multiple_of(r * SUBLANES, SUBLANES), SUBLANES)]
            pltpu.make_async_copy(src, dst, row_sem.at[buf]).start(priority=k % 2)

    def wait_rows(buf):
        for k in range(TOP_K):
            pltpu.make_async_copy(y_hbm.at[pl.ds(0, tm * SUBLANES)], rows_ref.at[buf, k], row_sem.at[buf]).wait()

    @pl.when(i == 0)
    def _():
        first = _fetch_slots(slots_hbm, slots_smem, slot_sem, 0, 0)
        first.start()
        first.wait()

        def issue(r, carry):
            gather(r, 0, 0)
            return carry

        lax.fori_loop(0, tm, issue, 0, unroll=ISSUE_UNROLL)
        _fetch_slots(slots_hbm, slots_smem, slot_sem, 1, 1).start()

    _fetch_slots(slots_hbm, slots_smem, slot_sem, i + 1, nxt).wait()

    @pl.when(i + 2 <= n)
    def _():
        _fetch_slots(slots_hbm, slots_smem, slot_sem, i + 2, cur).start()

    base = nxt * (TOP_K * tm)
    for r in range(tm):
        gather(r, nxt, base)

    h = _from_row_tiles(h_ref, 0, tm)
    ple = (jax.nn.sigmoid(jnp.dot(h.astype(BF16), wpg_ref[...], preferred_element_type=F32) + bpg_ref[...])
           * jnp.dot(p_ref[...].astype(BF16), wpp_ref[...], preferred_element_type=F32))

    wait_rows(cur)
    gates = gate_ref[...]
    ffn = gates[:, 0:1] * _from_row_tiles(rows_ref.at[cur, 0], 0, tm)
    for k in range(1, TOP_K):
        ffn = ffn + gates[:, k:k + 1] * _from_row_tiles(rows_ref.at[cur, k], 0, tm)
    o_ref[...] = _layer_norm(DEEPNORM_ALPHA * h + (ffn + ple), g_ref[...], b_ref[...])

    @pl.when(i == n - 1)
    def _():
        wait_rows(nxt)


def _combine_final(h1, p, gates, slots, y, w_pg, b_pg, w_pp, ln_g, ln_b):
    t = h1.shape[0] // SUBLANES
    tm = TM_COMBINE
    slots = jnp.concatenate([slots, slots[-1:]], axis=0)
    return pl.pallas_call(
        _final_kernel,
        grid=(t // tm,),
        in_specs=[
            pl.BlockSpec((tm * SUBLANES, LANES), lambda i: (i, 0)),
            pl.BlockSpec((tm, D_PLE), lambda i: (i, 0)),
            pl.BlockSpec((tm, TOP_K), lambda i: (i, 0)),
            pl.BlockSpec(memory_space=pl.ANY),
            pl.BlockSpec(memory_space=pl.ANY),
            _full((D_MODEL, D_MODEL)), _full((1, D_MODEL)), _full((D_PLE, D_MODEL)),
            _full((1, D_MODEL)), _full((1, D_MODEL)),
        ],
        out_specs=pl.BlockSpec((tm, D_MODEL), lambda i: (i, 0)),
        out_shape=jax.ShapeDtypeStruct((t, D_MODEL), F32),
        scratch_shapes=[
            pltpu.SMEM((2 * TOP_K * tm,), I32),
            pltpu.VMEM((2, TOP_K, tm * SUBLANES, LANES), F32),
            pltpu.SemaphoreType.DMA((2,)),
            pltpu.SemaphoreType.DMA((2,)),
        ],
        compiler_params=_cparams("arbitrary"),
        name="combine_final",
    )(h1, p, gates, slots, y, w_pg, b_pg, w_pp, ln_g, ln_b)


def _tile_slots(dest, tm):
    t = dest.shape[1]
    return dest.T.reshape(t // tm, tm * TOP_K)


def kernel(x_prompt, x_sample, p_prompt, p_sample, ln0_g, ln0_b, w_in, b_in, conv_w, conv_b, lnc_g, lnc_b, w_conv_out, w_fnet_out, w_o, b_o, ln1_g, ln1_b, w_router, b_router, w_gu, b_gu, w_down, b_down, w_pg, b_pg, w_pp, ln2_g, ln2_b):
    row = lambda v: v.reshape(1, -1).astype(F32)
    n_branch = 2 * D_CONV + D_FNET
    w_in_b = w_in[0].astype(BF16)
    w_branch, w_gate = w_in_b[:, :n_branch], w_in_b[:, n_branch:]
    b_branch, b_gate = row(b_in[0, :n_branch]), row(b_in[0, n_branch:])
    w_conv_out_b = w_conv_out[0].astype(BF16)
    w_fnet_out_b = w_fnet_out[0].astype(BF16)
    w_o_b = w_o[0].astype(BF16)
    w_rt_f = jnp.pad(w_router[0].astype(F32), ((0, 0), (0, LANES - N_EXPERTS)))
    w_rt_hi = w_rt_f.astype(BF16)
    w_rt = jnp.concatenate([w_rt_hi, (w_rt_f - w_rt_hi.astype(F32)).astype(BF16)], axis=1)
    b_rt = b_router[0].reshape(N_EXPERTS, 1).astype(F32)
    b_gu_r = b_gu[0].reshape(N_EXPERTS, 1, 2 * D_FF).astype(F32)
    b_down_r = b_down[0].reshape(N_EXPERTS, 1, D_MODEL).astype(F32)
    w_pg_b = w_pg[0].astype(BF16)
    w_pp_b = w_pp[0].astype(BF16)

    carry = jnp.zeros((N_EXPERTS, LANES), F32)
    routed = []
    for x, p in ((x_prompt, p_prompt[0]), (x_sample, p_sample[0])):
        bsz, s, _ = x.shape
        xt = x.reshape(bsz * s, D_MODEL)
        u, f_in = _inproj(xt, row(ln0_g), row(ln0_b), w_branch, b_branch)
        ya = _conv_branch(u.reshape(bsz, s, D_CONV), conv_w[0].astype(F32), row(conv_b[0]), row(lnc_g[0]),
                          row(lnc_b[0]), w_conv_out_b).reshape(bsz * s, D_MODEL)
        yb = _fourier_branch(f_in.reshape(bsz, s, D_FNET), w_fnet_out_b)
        h1, idx, gates, rank, carry = _mix_route(
            xt, ya, yb, carry, row(ln0_g), row(ln0_b), w_gate, b_gate, w_o_b, row(b_o[0]),
            row(ln1_g[0]), row(ln1_b[0]), w_rt, b_rt)
        routed.append((h1, idx, gates, rank, p.reshape(bsz * s, D_PLE), (bsz, s)))

    n_assign = sum(r[1].shape[1] for r in routed) * TOP_K
    n_blocks = (n_assign + N_EXPERTS * (BM_MOE - 1)) // BM_MOE
    counts = carry[:, 0].astype(I32)
    padded = ((counts + BM_MOE - 1) // BM_MOE) * BM_MOE
    ends = jnp.cumsum(padded)
    start = ends - padded
    nb_used = (ends[-1] // BM_MOE).reshape(1).astype(I32)
    blk = jnp.minimum(jnp.arange(n_blocks, dtype=I32), nb_used[0] - 1)
    blk_e = jnp.minimum(jnp.sum((ends[None, :] <= (blk * BM_MOE)[:, None]).astype(I32), axis=1), N_EXPERTS - 1)
    pad_runs = jnp.concatenate([jnp.stack([start + counts, padded - counts], axis=1).reshape(-1), nb_used,
                                jnp.zeros((LANES - 2 * N_EXPERTS - 1,), I32)]).astype(I32)

    xs = None
    slot_tables = []
    for h1, idx, _, rank, _, _ in routed:
        first = jnp.sum(jnp.where(idx[None] == jnp.arange(N_EXPERTS, dtype=I32)[:, None, None],
                                  start[:, None, None], 0), axis=0)
        slot_tables.append(_tile_slots(first + rank, TM_COMBINE))
        xs = _dispatch(h1, _tile_slots(first + rank, TM_ROWS), pad_runs, xs, n_blocks * BM_MOE)

    y = _moe_blocks(xs.reshape(-1, LANES), blk_e, nb_used, w_gu.reshape(N_EXPERTS, D_MODEL, 2 * D_FF), b_gu_r,
                    w_down.reshape(N_EXPERTS, D_FF, D_MODEL), b_down_r)

    outs = []
    for (h1, _, gates, _, p, (bsz, s)), slots in zip(routed, slot_tables):
        o = _combine_final(h1, p, gates.T, slots, y, w_pg_b, row(b_pg[0]), w_pp_b, row(ln2_g[0]), row(ln2_b[0]))
        outs.append(o.reshape(bsz, s, D_MODEL))
    return tuple(outs)
```

```python
import functools

import jax
import jax.numpy as jnp
from jax import lax
from jax.experimental import pallas as pl
from jax.experimental.pallas import tpu as pltpu

F32 = jnp.float32
BF16 = jnp.bfloat16
I32 = jnp.int32

D_MODEL = 1024
D_CONV = D_MODEL // 2
CONV_WIDTH = 31
CONV_PAD = CONV_WIDTH // 2
D_FNET = D_MODEL // 2
FNET_GROUPS = 4
FNET_GROUP_DIM = D_FNET // FNET_GROUPS
D_PLE = 256
N_EXPERTS = 32
TOP_K = 4
D_FF = D_MODEL
SWIGLU_ALPHA = 1.702
SWIGLU_LIMIT = 7.0
LN_EPS = 1e-5
DEPTH = 1
DEEPNORM_ALPHA = (2 * DEPTH) ** 0.25

LANES = 128
SUBLANES = 8
DFT_N1 = 128
HALO = 16

TM_INPROJ = 512
TS_CONV = 512
RC_CONV = 128
TM_MIX = 512
SUB_MIX = 512
FFT2_TILES = 4
TM_ROWS = 1024
TM_COMBINE = 256
BM_MOE = 512
FF_CHUNK = 512
ISSUE_UNROLL = 8
CAST_ROWS = 128
VMEM_LIMIT = 48 * 1024 * 1024
VMEM_LIMIT_MOE = 56 * 1024 * 1024


def _cparams(*sem):
    return pltpu.CompilerParams(dimension_semantics=sem, vmem_limit_bytes=VMEM_LIMIT)


def _layer_norm(x, g, b):
    mu = jnp.mean(x, axis=-1, keepdims=True)
    xc = x - mu
    var = jnp.mean(xc * xc, axis=-1, keepdims=True)
    return xc * lax.rsqrt(var + LN_EPS) * g + b


def _full(shape):
    return pl.BlockSpec(shape, lambda *_: (0,) * len(shape))


def _to_row_tiles(ref, row0, x):
    rows = x.shape[0]
    for j in range(D_MODEL // LANES):
        ref[pl.ds(row0 * SUBLANES + j, rows, stride=SUBLANES), :] = x[:, j * LANES:(j + 1) * LANES]


def _from_row_tiles(ref, row0, rows):
    return jnp.concatenate([ref[pl.ds(row0 * SUBLANES + j, rows, stride=SUBLANES), :]
                            for j in range(D_MODEL // LANES)], axis=1)


def _inproj_kernel(x_ref, g_ref, b_ref, w_ref, bias_ref, u_ref, f_ref):
    h = _layer_norm(x_ref[...], g_ref[...], b_ref[...])
    z = jnp.dot(h.astype(BF16), w_ref[...], preferred_element_type=F32) + bias_ref[...]
    u_ref[...] = (z[:, :D_CONV] * jax.nn.sigmoid(z[:, D_CONV:2 * D_CONV])).astype(BF16)
    f_ref[...] = z[:, 2 * D_CONV:]


def _inproj(x, ln_g, ln_b, w, bias):
    t = x.shape[0]
    n_out = w.shape[1]
    return pl.pallas_call(
        _inproj_kernel,
        grid=(t // TM_INPROJ,),
        in_specs=[
            pl.BlockSpec((TM_INPROJ, D_MODEL), lambda i: (i, 0)),
            _full((1, D_MODEL)), _full((1, D_MODEL)),
            _full((D_MODEL, n_out)), _full((1, n_out)),
        ],
        out_specs=[
            pl.BlockSpec((TM_INPROJ, D_CONV), lambda i: (i, 0)),
            pl.BlockSpec((TM_INPROJ, D_FNET), lambda i: (i, 0)),
        ],
        out_shape=[jax.ShapeDtypeStruct((t, D_CONV), BF16), jax.ShapeDtypeStruct((t, D_FNET), F32)],
        compiler_params=_cparams("parallel"),
        name="inproj",
    )(x, ln_g, ln_b, w, bias)


def _conv_kernel(prev_ref, cur_ref, next_ref, cw_ref, cb_ref, g_ref, b_ref, wout_ref, ya_ref,
                 ext_ref, act_ref):
    i = pl.program_id(1)
    last = pl.num_programs(1) - 1
    ts = cur_ref.shape[1]
    ext_rows = ts + 2 * HALO
    zero = jnp.zeros((HALO, D_CONV), F32)
    ext_ref[0, 0:HALO, :] = jnp.where(i > 0, prev_ref[0].astype(F32), zero)
    ext_ref[0, HALO:HALO + ts, :] = cur_ref[0].astype(F32)
    ext_ref[0, HALO + ts:ext_rows, :] = jnp.where(i < last, next_ref[0].astype(F32), zero)
    for r in range(1, SUBLANES):
        ext_ref[r, 0:ext_rows - SUBLANES, :] = ext_ref[0, r:r + ext_rows - SUBLANES, :]

    cb = cb_ref[...]
    g = g_ref[...]
    b = b_ref[...]

    def chunk(c, carry):
        r0 = pl.multiple_of(c * RC_CONV, RC_CONV)
        acc = jnp.zeros((RC_CONV, D_CONV), F32) + cb
        for j in range(CONV_WIDTH):
            off = j + HALO - CONV_PAD
            q, r = divmod(off, SUBLANES)
            acc = acc + cw_ref[j:j + 1, :] * ext_ref[r, pl.ds(r0 + q * SUBLANES, RC_CONV), :]
        y = _layer_norm(acc, g, b)
        act_ref[pl.ds(r0, RC_CONV), :] = (y * jax.nn.sigmoid(y)).astype(BF16)
        return carry

    lax.fori_loop(0, ts // RC_CONV, chunk, 0)
    ya_ref[0] = jnp.dot(act_ref[...], wout_ref[...], preferred_element_type=F32).astype(BF16)


def _conv_branch(u, conv_w, conv_b, ln_g, ln_b, w_out):
    bsz, s, _ = u.shape
    ts = TS_CONV
    nb = s // ts
    hb = ts // HALO
    return pl.pallas_call(
        _conv_kernel,
        grid=(bsz, nb),
        in_specs=[
            pl.BlockSpec((1, HALO, D_CONV), lambda b, i: (b, jnp.maximum(i * hb - 1, 0), 0)),
            pl.BlockSpec((1, ts, D_CONV), lambda b, i: (b, i, 0)),
            pl.BlockSpec((1, HALO, D_CONV), lambda b, i: (b, jnp.minimum((i + 1) * hb, s // HALO - 1), 0)),
            _full((CONV_WIDTH, D_CONV)), _full((1, D_CONV)), _full((1, D_CONV)), _full((1, D_CONV)),
            _full((D_CONV, D_MODEL)),
        ],
        out_specs=pl.BlockSpec((1, ts, D_MODEL), lambda b, i: (b, i, 0)),
        out_shape=jax.ShapeDtypeStruct((bsz, s, D_MODEL), BF16),
        scratch_shapes=[
            pltpu.VMEM((SUBLANES, ts + 2 * HALO, D_CONV), F32),
            pltpu.VMEM((ts, D_CONV), BF16),
        ],
        compiler_params=_cparams("parallel", "parallel"),
        name="conv_branch",
    )(u, u, u, conv_w, conv_b, ln_g, ln_b, w_out)


def _dft_tables(s):
    n2_len = s // DFT_N1
    k1_per_tile = DFT_N1 // n2_len
    two_pi = 2.0 * jnp.pi
    a = jnp.arange(DFT_N1, dtype=I32)
    ang1 = two_pi * ((a[:, None] * a[None, :]) % DFT_N1).astype(F32) / DFT_N1
    small = jnp.concatenate([jnp.cos(ang1), -jnp.sin(ang1)], axis=0)
    r = jnp.arange(2 * DFT_N1 * SUBLANES, dtype=I32)[:, None]
    c = jnp.arange(DFT_N1 * SUBLANES, dtype=I32)[None, :]
    pick_r = (r // SUBLANES == jnp.arange(2 * DFT_N1, dtype=I32)[None, :]).astype(F32)
    pick_c = (jnp.arange(DFT_N1, dtype=I32)[:, None] == c // SUBLANES).astype(F32)
    rep = jnp.dot(jnp.dot(pick_r, small, precision=lax.Precision.HIGHEST), pick_c, precision=lax.Precision.HIGHEST)
    f1 = jnp.where(r % SUBLANES == c % SUBLANES, rep, 0.0).astype(BF16)
    cs = jnp.stack([jnp.cos(ang1), jnp.sin(ang1)]).astype(BF16)
    n_tiles = s // DFT_N1
    t = jnp.arange(n_tiles, dtype=I32)[:, None, None]
    row = jnp.arange(DFT_N1, dtype=I32)[None, :, None]
    col = jnp.arange(DFT_N1, dtype=I32)[None, None, :]
    k2, k1l_out = row // k1_per_tile, row % k1_per_tile
    k1l_in, n2 = col // n2_len, col % n2_len
    k = t * k1_per_tile + k1l_out + DFT_N1 * k2
    ang2 = two_pi * ((n2 * k) % s).astype(F32) / s
    hit = k1l_in == k1l_out
    gr = jnp.where(hit, jnp.cos(ang2), 0.0)
    gi = jnp.where(hit, -jnp.sin(ang2), 0.0)
    g = jnp.concatenate([jnp.concatenate([gr, -gi], axis=2),
                         jnp.concatenate([gi, gr], axis=2)], axis=1).astype(BF16)
    return f1, g, cs


def _fft1_kernel(x_ref, f_ref, y_ref):
    nb = x_ref.shape[2]
    x = x_ref[0].reshape(DFT_N1 * nb, D_FNET).astype(BF16)
    r = jnp.dot(f_ref[...], x, preferred_element_type=F32)
    y_ref[0] = r.reshape(2, DFT_N1, nb, D_FNET)


def _fft2_kernel(y_ref, g_ref, cs_ref, w_ref, o_ref, *, scale, tiles):
    for t in range(tiles):
        rows = slice(t * DFT_N1, (t + 1) * DFT_N1)
        yb = jnp.concatenate([y_ref[0, 0, rows, :], y_ref[0, 1, rows, :]], axis=0).astype(BF16)
        z = jnp.dot(g_ref[t], yb, preferred_element_type=F32)
        zr = z[:DFT_N1].astype(BF16)
        zi = z[DFT_N1:].astype(BF16)
        parts = []
        for grp in range(FNET_GROUPS):
            sl = slice(grp * FNET_GROUP_DIM, (grp + 1) * FNET_GROUP_DIM)
            parts.append(jnp.dot(zr[:, sl], cs_ref[0], preferred_element_type=F32)
                         + jnp.dot(zi[:, sl], cs_ref[1], preferred_element_type=F32))
        fm = (jnp.concatenate(parts, axis=1) * scale).astype(BF16)
        out = jnp.dot(fm, w_ref[...], preferred_element_type=F32)
        k1_per_tile = o_ref.shape[2] // tiles
        if k1_per_tile == 1:
            o_ref[0, :, t, :] = out
        else:
            o_ref[0, :, t * k1_per_tile:(t + 1) * k1_per_tile, :] = out.reshape(o_ref.shape[1], k1_per_tile, D_MODEL)


def _fourier_branch(f_in, w_out):
    bsz, s, _ = f_in.shape
    n2_len = s // DFT_N1
    k1_per_tile = DFT_N1 // n2_len
    f1, g, cs = _dft_tables(s)
    y = pl.pallas_call(
        _fft1_kernel,
        grid=(bsz, n2_len // SUBLANES),
        in_specs=[pl.BlockSpec((1, DFT_N1, SUBLANES, D_FNET), lambda b, j: (b, 0, j, 0)),
                  _full((2 * DFT_N1 * SUBLANES, DFT_N1 * SUBLANES))],
        out_specs=pl.BlockSpec((1, 2, DFT_N1, SUBLANES, D_FNET), lambda b, j: (b, 0, 0, j, 0)),
        out_shape=jax.ShapeDtypeStruct((bsz, 2, DFT_N1, n2_len, D_FNET), F32),
        compiler_params=_cparams("parallel", "parallel"),
        name="fft_stage1",
    )(f_in.reshape(bsz, DFT_N1, n2_len, D_FNET), f1)
    y = y.reshape(bsz, 2, s, D_FNET)
    scale = float((s * FNET_GROUP_DIM) ** -0.5)
    assert n2_len % SUBLANES == 0 and (k1_per_tile == 1 or k1_per_tile % SUBLANES == 0)
    tiles = SUBLANES if k1_per_tile == 1 else min(FFT2_TILES, n2_len)
    k1_per_step = tiles * k1_per_tile
    yb = pl.pallas_call(
        functools.partial(_fft2_kernel, scale=scale, tiles=tiles),
        grid=(bsz, s // (DFT_N1 * tiles)),
        in_specs=[
            pl.BlockSpec((1, 2, DFT_N1 * tiles, D_FNET), lambda b, t: (b, 0, t, 0)),
            pl.BlockSpec((tiles, 2 * DFT_N1, 2 * DFT_N1), lambda b, t: (t, 0, 0)),
            _full((2, DFT_N1, DFT_N1)),
            _full((D_FNET, D_MODEL)),
        ],
        out_specs=pl.BlockSpec((1, n2_len, k1_per_step, D_MODEL), lambda b, t: (b, 0, t, 0)),
        out_shape=jax.ShapeDtypeStruct((bsz, n2_len, DFT_N1, D_MODEL), F32),
        compiler_params=_cparams("parallel", "parallel"),
        name="fft_stage2",
    )(y, g, cs, w_out)
    return yb.reshape(bsz * s, D_MODEL)


def _mix_kernel(x_ref, ya_ref, yb_ref, cin_ref, ln0g_ref, ln0b_ref, wg_ref, bg_ref, wo_ref, bo_ref,
                ln1g_ref, ln1b_ref, wr_ref, br_ref,
                h1_ref, idx_ref, gate_ref, rank_ref, cnt_ref, carry_ref):
    tm = x_ref.shape[0]
    sub = min(SUB_MIX, tm)

    @pl.when(pl.program_id(0) == 0)
    def _():
        carry_ref[...] = cin_ref[...]

    carry = carry_ref[...]
    eio = lax.broadcasted_iota(I32, (N_EXPERTS, sub), 0)
    before = (lax.broadcasted_iota(I32, (sub, sub), 0) < lax.broadcasted_iota(I32, (sub, sub), 1)).astype(BF16)
    for r0 in range(0, tm, sub):
        rows = slice(r0, r0 + sub)
        h = _layer_norm(x_ref[rows, :], ln0g_ref[...], ln0b_ref[...])
        zg = jnp.dot(h.astype(BF16), wg_ref[...], preferred_element_type=F32) + bg_ref[...]
        m = (jax.nn.sigmoid(zg[:, :D_MODEL]) * ya_ref[rows, :].astype(F32)
             + jax.nn.sigmoid(zg[:, D_MODEL:]) * yb_ref[rows, :].astype(F32))
        mix = jnp.dot(m.astype(BF16), wo_ref[...], preferred_element_type=F32) + bo_ref[...]
        h1 = _layer_norm(DEEPNORM_ALPHA * h + mix, ln1g_ref[...], ln1b_ref[...])
        _to_row_tiles(h1_ref, r0, h1)

        h_hi = h1.astype(BF16)
        h_lo = (h1 - h_hi.astype(F32)).astype(BF16)
        hw = jnp.dot(h_hi, wr_ref[...], preferred_element_type=F32)
        lw = jnp.dot(h_lo, wr_ref[:, :LANES], preferred_element_type=F32)
        logits = (hw[:, :LANES] + hw[:, LANES:] + lw).T[:N_EXPERTS] + br_ref[...]
        vals, idxs = [], []
        cur = logits
        for _ in range(TOP_K):
            mx = jnp.max(cur, axis=0, keepdims=True)
            ik = jnp.min(jnp.where(cur == mx, eio, N_EXPERTS), axis=0, keepdims=True)
            vals.append(mx)
            idxs.append(ik)
            cur = jnp.where(eio == ik, -jnp.inf, cur)
        exps = [jnp.exp(v - vals[0]) for v in vals]
        den = exps[0] + exps[1] + exps[2] + exps[3]
        gate_ref[:, rows] = jnp.concatenate([e / den for e in exps], axis=0)
        idx_ref[:, rows] = jnp.concatenate(idxs, axis=0)

        hot = jnp.zeros((N_EXPERTS, sub), F32)
        for ik in idxs:
            hot = hot + (eio == ik).astype(F32)
        prior = jnp.dot(hot.astype(BF16), before, preferred_element_type=F32) + carry[:, 0:1]
        ranks = [jnp.sum(jnp.where(eio == ik, prior, 0.0), axis=0, keepdims=True) for ik in idxs]
        rank_ref[:, rows] = jnp.concatenate(ranks, axis=0).astype(I32)
        carry = carry + jnp.sum(hot, axis=1, keepdims=True)
    carry_ref[...] = carry
    cnt_ref[...] = carry


def _mix_route(x, ya, yb, carry_in, ln0_g, ln0_b, w_gate, b_gate, w_o, b_o, ln1_g, ln1_b, w_rt, b_rt):
    t = x.shape[0]
    tm = TM_MIX
    row = lambda d: pl.BlockSpec((tm, d), lambda i: (i, 0))
    col = lambda r: pl.BlockSpec((r, tm), lambda i: (0, i))
    return pl.pallas_call(
        _mix_kernel,
        grid=(t // tm,),
        in_specs=[
            row(D_MODEL), row(D_MODEL), row(D_MODEL), _full((N_EXPERTS, LANES)),
            _full((1, D_MODEL)), _full((1, D_MODEL)),
            _full((D_MODEL, 2 * D_MODEL)), _full((1, 2 * D_MODEL)),
            _full((D_MODEL, D_MODEL)), _full((1, D_MODEL)),
            _full((1, D_MODEL)), _full((1, D_MODEL)),
            _full((D_MODEL, 2 * LANES)), _full((N_EXPERTS, 1)),
        ],
        out_specs=[pl.BlockSpec((tm * SUBLANES, LANES), lambda i: (i, 0)), col(TOP_K), col(TOP_K), col(TOP_K),
                   _full((N_EXPERTS, LANES))],
        out_shape=[
            jax.ShapeDtypeStruct((t * SUBLANES, LANES), F32),
            jax.ShapeDtypeStruct((TOP_K, t), I32),
            jax.ShapeDtypeStruct((TOP_K, t), F32),
            jax.ShapeDtypeStruct((TOP_K, t), I32),
            jax.ShapeDtypeStruct((N_EXPERTS, LANES), F32),
        ],
        scratch_shapes=[pltpu.VMEM((N_EXPERTS, LANES), F32)],
        compiler_params=_cparams("arbitrary"),
        name="mix_route",
    )(x, ya, yb, carry_in, ln0_g, ln0_b, w_gate, b_gate, w_o, b_o, ln1_g, ln1_b, w_rt, b_rt)


def _row_copy(src_ref, src_row, dst_ref, dst_row, sem):
    return pltpu.make_async_copy(src_ref.at[src_row], dst_ref.at[dst_row], sem)


def _fetch_slots(slots_hbm, slots_smem, sem, step, slot):
    n = slots_hbm.shape[1]
    return pltpu.make_async_copy(slots_hbm.at[step], slots_smem.at[pl.ds(pl.multiple_of(slot * n, n), n)],
                                 sem.at[slot])


def _pad_fill(h_ref, xs_out, pad_smem, pad_sem, n_pad_runs, wait):
    def go(cp):
        if wait:
            cp.wait()
        else:
            cp.start()

    def run(j, carry):
        start = pad_smem[2 * j]
        length = pad_smem[2 * j + 1]
        for bit in range(BM_MOE.bit_length() - 1):
            size = 1 << bit

            @pl.when((length & size) != 0)
            def _():
                off = start + (length & ~(2 * size - 1))
                go(pltpu.make_async_copy(h_ref.at[pl.ds(0, size)], xs_out.at[pl.ds(off, size)], pad_sem))
        return carry

    lax.fori_loop(0, n_pad_runs, run, 0)

    piece = min(h_ref.shape[0], BM_MOE)

    def dead_block(j, carry):
        for part in range(BM_MOE // piece):
            off = j * BM_MOE + part * piece
            go(pltpu.make_async_copy(h_ref.at[pl.ds(0, piece)], xs_out.at[pl.ds(off, piece)], pad_sem))
        return carry

    lax.fori_loop(pad_smem[2 * n_pad_runs], xs_out.shape[0] // BM_MOE, dead_block, 0)


def _dispatch_kernel(h_ref, slots_hbm, pad_hbm, *rest, n_pad_runs, aliased):
    xs_out, slots_smem, pad_smem, slot_sem, pad_sem, row_sem = rest[1:] if aliased else rest
    i = pl.program_id(0)
    n = pl.num_programs(0)
    tm = h_ref.shape[0]
    cur = lax.rem(i, 2)

    @pl.when(i == 0)
    def _():
        _fetch_slots(slots_hbm, slots_smem, slot_sem, 0, 0).start()

    @pl.when(i + 1 < n)
    def _():
        _fetch_slots(slots_hbm, slots_smem, slot_sem, i + 1, 1 - cur).start()

    if n_pad_runs:
        @pl.when(i == 0)
        def _():
            fetch = pltpu.make_async_copy(pad_hbm, pad_smem, pad_sem)
            fetch.start()
            fetch.wait()
            _pad_fill(h_ref, xs_out, pad_smem, pad_sem, n_pad_runs, wait=False)
            _pad_fill(h_ref, xs_out, pad_smem, pad_sem, n_pad_runs, wait=True)

    _fetch_slots(slots_hbm, slots_smem, slot_sem, i, cur).wait()

    base = cur * (TOP_K * tm)

    def issue(r, carry):
        for k in range(TOP_K):
            _row_copy(h_ref, r, xs_out, slots_smem[base + TOP_K * r + k], row_sem).start(priority=k % 2)
        return carry

    lax.fori_loop(0, tm, issue, 0, unroll=ISSUE_UNROLL)
    for _ in range(TOP_K):
        pltpu.make_async_copy(h_ref, xs_out.at[pl.ds(0, tm)], row_sem).wait()


def _dispatch(h1, slots, pad_runs, xs, n_rows):
    h1 = h1.reshape(-1, SUBLANES, LANES)
    t = h1.shape[0]
    tm = TM_ROWS
    aliased = xs is not None
    n_pad_runs = 0 if aliased else N_EXPERTS
    operands = (h1, slots, pad_runs) + ((xs,) if aliased else ())
    return pl.pallas_call(
        functools.partial(_dispatch_kernel, n_pad_runs=n_pad_runs, aliased=aliased),
        grid=(t // tm,),
        in_specs=[pl.BlockSpec((tm, SUBLANES, LANES), lambda i: (i, 0, 0))]
        + [pl.BlockSpec(memory_space=pl.ANY)] * (len(operands) - 1),
        out_specs=pl.BlockSpec(memory_space=pl.ANY),
        out_shape=jax.ShapeDtypeStruct((n_rows, SUBLANES, LANES), F32),
        input_output_aliases={3: 0} if aliased else {},
        scratch_shapes=[
            pltpu.SMEM((2 * TOP_K * tm,), I32),
            pltpu.SMEM((pad_runs.shape[0],), I32),
            pltpu.SemaphoreType.DMA((2,)),
            pltpu.SemaphoreType.DMA,
            pltpu.SemaphoreType.DMA,
        ],
        compiler_params=_cparams("arbitrary"),
        name="dispatch",
    )(*operands)


def _moe_kernel(blk_e_ref, nb_ref, x_ref, wgu_ref, bgu_ref, wd_ref, bd_ref, o_ref, wgu_b, wd_b):
    i = pl.program_id(0)
    is_live = i < nb_ref[0]

    @pl.when(jnp.logical_not(is_live))
    def _():
        o_ref[...] = jnp.zeros(o_ref.shape, F32)

    @pl.when(jnp.logical_or(i == 0, blk_e_ref[i] != blk_e_ref[jnp.maximum(i - 1, 0)]))
    def _():
        for r in range(0, D_MODEL, CAST_ROWS):
            wgu_b[r:r + CAST_ROWS, :] = wgu_ref[0, r:r + CAST_ROWS, :].astype(BF16)
        for r in range(0, D_FF, CAST_ROWS):
            wd_b[r:r + CAST_ROWS, :] = wd_ref[0, r:r + CAST_ROWS, :].astype(BF16)

    @pl.when(is_live)
    def _():
        x = _from_row_tiles(x_ref, 0, BM_MOE).astype(BF16)
        acc = jnp.zeros((BM_MOE, D_MODEL), F32) + bd_ref[0]
        for c in range(D_FF // FF_CHUNK):
            lo = c * FF_CHUNK
            g = jnp.dot(x, wgu_b[:, lo:lo + FF_CHUNK], preferred_element_type=F32) + bgu_ref[0, :, lo:lo + FF_CHUNK]
            u = (jnp.dot(x, wgu_b[:, D_FF + lo:D_FF + lo + FF_CHUNK], preferred_element_type=F32)
                 + bgu_ref[0, :, D_FF + lo:D_FF + lo + FF_CHUNK])
            g = jnp.minimum(g, SWIGLU_LIMIT)
            u = jnp.clip(u, -SWIGLU_LIMIT, SWIGLU_LIMIT)
            act = (u + 1.0) * (g * jax.nn.sigmoid(g * SWIGLU_ALPHA))
            acc = acc + jnp.dot(act.astype(BF16), wd_b[lo:lo + FF_CHUNK, :], preferred_element_type=F32)
        _to_row_tiles(o_ref, 0, acc)


def _moe_blocks(xs, blk_e, nb_used, w_gu, b_gu, w_down, b_down):
    p = xs.shape[0] // SUBLANES
    n_blocks = p // BM_MOE
    live = lambda i, nb: jnp.minimum(i, nb[0] - 1)
    grid_spec = pltpu.PrefetchScalarGridSpec(
        num_scalar_prefetch=2,
        grid=(n_blocks,),
        in_specs=[
            pl.BlockSpec((BM_MOE * SUBLANES, LANES), lambda i, be, nb: (live(i, nb), 0)),
            pl.BlockSpec((1, D_MODEL, 2 * D_FF), lambda i, be, nb: (be[i], 0, 0)),
            pl.BlockSpec((1, 1, 2 * D_FF), lambda i, be, nb: (be[i], 0, 0)),
            pl.BlockSpec((1, D_FF, D_MODEL), lambda i, be, nb: (be[i], 0, 0)),
            pl.BlockSpec((1, 1, D_MODEL), lambda i, be, nb: (be[i], 0, 0)),
        ],
        out_specs=pl.BlockSpec((BM_MOE * SUBLANES, LANES), lambda i, be, nb: (i, 0)),
        scratch_shapes=[pltpu.VMEM((D_MODEL, 2 * D_FF), BF16), pltpu.VMEM((D_FF, D_MODEL), BF16)],
    )
    return pl.pallas_call(
        _moe_kernel,
        grid_spec=grid_spec,
        out_shape=jax.ShapeDtypeStruct((p * SUBLANES, LANES), F32),
        compiler_params=pltpu.CompilerParams(dimension_semantics=("arbitrary",), vmem_limit_bytes=VMEM_LIMIT_MOE),
        name="moe_experts",
    )(blk_e, nb_used, xs, w_gu, b_gu, w_down, b_down)


def _final_kernel(h_ref, p_ref, gate_ref, slots_hbm, y_hbm, wpg_ref, bpg_ref, wpp_ref, g_ref, b_ref, o_ref,
                  slots_smem, rows_ref, slot_sem, row_sem):
    i = pl.program_id(0)
    n = pl.num_programs(0)
    tm = h_ref.shape[0] // SUBLANES
    cur = lax.rem(i, 2)
    nxt = 1 - cur

    def gather(r, buf, base):
        for k in range(TOP_K):
            slot = slots_smem[base + TOP_K * r + k]
            src = y_hbm.at[pl.ds(pl.multiple_of(slot * SUBLANES, SUBLANES), SUBLANES)]
            dst = rows_ref.at[buf, k, pl.ds(pl.multiple_of(r * SUBLANES, SUBLANES), SUBLANES)]
            pltpu.make_async_copy(src, dst, row_sem.at[buf]).start(priority=k % 2)

    def wait_rows(buf):
        for k in range(TOP_K):
            pltpu.make_async_copy(y_hbm.at[pl.ds(0, tm * SUBLANES)], rows_ref.at[buf, k], row_sem.at[buf]).wait()

    @pl.when(i == 0)
    def _():
        first = _fetch_slots(slots_hbm, slots_smem, slot_sem, 0, 0)
        first.start()
        first.wait()

        def issue(r, carry):
            gather(r, 0, 0)
            return carry

        lax.fori_loop(0, tm, issue, 0, unroll=ISSUE_UNROLL)
        _fetch_slots(slots_hbm, slots_smem, slot_sem, 1, 1).start()

    _fetch_slots(slots_hbm, slots_smem, slot_sem, i + 1, nxt).wait()

    @pl.when(i + 2 <= n)
    def _():
        _fetch_slots(slots_hbm, slots_smem, slot_sem, i + 2, cur).start()

    base = nxt * (TOP_K * tm)
    for r in range(tm):
        gather(r, nxt, base)

    h = _from_row_tiles(h_ref, 0, tm)
    ple = (jax.nn.sigmoid(jnp.dot(h.astype(BF16), wpg_ref[...], preferred_element_type=F32) + bpg_ref[...])
           * jnp.dot(p_ref[...].astype(BF16), wpp_ref[...], preferred_element_type=F32))

    wait_rows(cur)
    gates = gate_ref[...]
    ffn = gates[:, 0:1] * _from_row_tiles(rows_ref.at[cur, 0], 0, tm)
    for k in range(1, TOP_K):
        ffn = ffn + gates[:, k:k + 1] * _from_row_tiles(rows_ref.at[cur, k], 0, tm)
    o_ref[...] = _layer_norm(DEEPNORM_ALPHA * h + (ffn + ple), g_ref[...], b_ref[...])

    @pl.when(i == n - 1)
    def _():
        wait_rows(nxt)


def _combine_final(h1, p, gates, slots, y, w_pg, b_pg, w_pp, ln_g, ln_b):
    t = h1.shape[0] // SUBLANES
    tm = TM_COMBINE
    slots = jnp.concatenate([slots, slots[-1:]], axis=0)
    return pl.pallas_call(
        _final_kernel,
        grid=(t // tm,),
        in_specs=[
            pl.BlockSpec((tm * SUBLANES, LANES), lambda i: (i, 0)),
            pl.BlockSpec((tm, D_PLE), lambda i: (i, 0)),
            pl.BlockSpec((tm, TOP_K), lambda i: (i, 0)),
            pl.BlockSpec(memory_space=pl.ANY),
            pl.BlockSpec(memory_space=pl.ANY),
            _full((D_MODEL, D_MODEL)), _full((1, D_MODEL)), _full((D_PLE, D_MODEL)),
            _full((1, D_MODEL)), _full((1, D_MODEL)),
        ],
        out_specs=pl.BlockSpec((tm, D_MODEL), lambda i: (i, 0)),
        out_shape=jax.ShapeDtypeStruct((t, D_MODEL), F32),
        scratch_shapes=[
            pltpu.SMEM((2 * TOP_K * tm,), I32),
            pltpu.VMEM((2, TOP_K, tm * SUBLANES, LANES), F32),
            pltpu.SemaphoreType.DMA((2,)),
            pltpu.SemaphoreType.DMA((2,)),
        ],
        compiler_params=_cparams("arbitrary"),
        name="combine_final",
    )(h1, p, gates, slots, y, w_pg, b_pg, w_pp, ln_g, ln_b)


def _tile_slots(dest, tm):
    t = dest.shape[1]
    return dest.T.reshape(t // tm, tm * TOP_K)


def kernel(x_prompt, x_sample, p_prompt, p_sample, ln0_g, ln0_b, w_in, b_in, conv_w, conv_b, lnc_g, lnc_b, w_conv_out, w_fnet_out, w_o, b_o, ln1_g, ln1_b, w_router, b_router, w_gu, b_gu, w_down, b_down, w_pg, b_pg, w_pp, ln2_g, ln2_b):
    row = lambda v: v.reshape(1, -1).astype(F32)
    n_branch = 2 * D_CONV + D_FNET
    w_in_b = w_in[0].astype(BF16)
    w_branch, w_gate = w_in_b[:, :n_branch], w_in_b[:, n_branch:]
    b_branch, b_gate = row(b_in[0, :n_branch]), row(b_in[0, n_branch:])
    w_conv_out_b = w_conv_out[0].astype(BF16)
    w_fnet_out_b = w_fnet_out[0].astype(BF16)
    w_o_b = w_o[0].astype(BF16)
    w_rt_f = jnp.pad(w_router[0].astype(F32), ((0, 0), (0, LANES - N_EXPERTS)))
    w_rt_hi = w_rt_f.astype(BF16)
    w_rt = jnp.concatenate([w_rt_hi, (w_rt_f - w_rt_hi.astype(F32)).astype(BF16)], axis=1)
    b_rt = b_router[0].reshape(N_EXPERTS, 1).astype(F32)
    b_gu_r = b_gu[0].reshape(N_EXPERTS, 1, 2 * D_FF).astype(F32)
    b_down_r = b_down[0].reshape(N_EXPERTS, 1, D_MODEL).astype(F32)
    w_pg_b = w_pg[0].astype(BF16)
    w_pp_b = w_pp[0].astype(BF16)

    carry = jnp.zeros((N_EXPERTS, LANES), F32)
    routed = []
    for x, p in ((x_prompt, p_prompt[0]), (x_sample, p_sample[0])):
        bsz, s, _ = x.shape
        xt = x.reshape(bsz * s, D_MODEL)
        u, f_in = _inproj(xt, row(ln0_g), row(ln0_b), w_branch, b_branch)
        ya = _conv_branch(u.reshape(bsz, s, D_CONV), conv_w[0].astype(F32), row(conv_b[0]), row(lnc_g[0]),
                          row(lnc_b[0]), w_conv_out_b).reshape(bsz * s, D_MODEL)
        yb = _fourier_branch(f_in.reshape(bsz, s, D_FNET), w_fnet_out_b)
        h1, idx, gates, rank, carry = _mix_route(
            xt, ya, yb, carry, row(ln0_g), row(ln0_b), w_gate, b_gate, w_o_b, row(b_o[0]),
            row(ln1_g[0]), row(ln1_b[0]), w_rt, b_rt)
        routed.append((h1, idx, gates, rank, p.reshape(bsz * s, D_PLE), (bsz, s)))

    n_assign = sum(r[1].shape[1] for r in routed) * TOP_K
    n_blocks = (n_assign + N_EXPERTS * (BM_MOE - 1)) // BM_MOE
    counts = carry[:, 0].astype(I32)
    padded = ((counts + BM_MOE - 1) // BM_MOE) * BM_MOE
    ends = jnp.cumsum(padded)
    start = ends - padded
    nb_used = (ends[-1] // BM_MOE).reshape(1).astype(I32)
    blk = jnp.minimum(jnp.arange(n_blocks, dtype=I32), nb_used[0] - 1)
    blk_e = jnp.minimum(jnp.sum((ends[None, :] <= (blk * BM_MOE)[:, None]).astype(I32), axis=1), N_EXPERTS - 1)
    pad_runs = jnp.concatenate([jnp.stack([start + counts, padded - counts], axis=1).reshape(-1), nb_used,
                                jnp.zeros((LANES - 2 * N_EXPERTS - 1,), I32)]).astype(I32)

    xs = None
    slot_tables = []
    for h1, idx, _, rank, _, _ in routed:
        first = jnp.sum(jnp.where(idx[None] == jnp.arange(N_EXPERTS, dtype=I32)[:, None, None],
                                  start[:, None, None], 0), axis=0)
        slot_tables.append(_tile_slots(first + rank, TM_COMBINE))
        xs = _dispatch(h1, _tile_slots(first + rank, TM_ROWS), pad_runs, xs, n_blocks * BM_MOE)

    y = _moe_blocks(xs.reshape(-1, LANES), blk_e, nb_used, w_gu.reshape(N_EXPERTS, D_MODEL, 2 * D_FF), b_gu_r,
                    w_down.reshape(N_EXPERTS, D_FF, D_MODEL), b_down_r)

    outs = []
    for (h1, _, gates, _, p, (bsz, s)), slots in zip(routed, slot_tables):
        o = _combine_final(h1, p, gates.T, slots, y, w_pg_b, row(b_pg[0]), w_pp_b, row(ln2_g[0]), row(ln2_b[0]))
        outs.append(o.reshape(bsz, s, D_MODEL))
    return tuple(outs)
```

```python
import functools

import jax
import jax.numpy as jnp
from jax import lax
from jax.experimental import pallas as pl
from jax.experimental.pallas import tpu as pltpu

F32 = jnp.float32
BF16 = jnp.bfloat16
I32 = jnp.int32

D_MODEL = 1024
D_CONV = D_MODEL // 2
CONV_WIDTH = 31
CONV_PAD = CONV_WIDTH // 2
D_FNET = D_MODEL // 2
FNET_GROUPS = 4
FNET_GROUP_DIM = D_FNET // FNET_GROUPS
D_PLE = 256
N_EXPERTS = 32
TOP_K = 4
D_FF = D_MODEL
SWIGLU_ALPHA = 1.702
SWIGLU_LIMIT = 7.0
LN_EPS = 1e-5
DEPTH = 1
DEEPNORM_ALPHA = (2 * DEPTH) ** 0.25

LANES = 128
SUBLANES = 8
DFT_N1 = 128
HALO = 16

TM_INPROJ = 512
TS_CONV = 512
RC_CONV = 128
TM_MIX = 512
SUB_MIX = 512
FFT2_TILES = 4
TM_ROWS = 1024
TM_COMBINE = 256
BM_MOE = 512
FF_CHUNK = 512
ISSUE_UNROLL = 8
CAST_ROWS = 128
VMEM_LIMIT = 48 * 1024 * 1024
VMEM_LIMIT_MOE = 56 * 1024 * 1024


def _cparams(*sem):
    return pltpu.CompilerParams(dimension_semantics=sem, vmem_limit_bytes=VMEM_LIMIT)


def _layer_norm(x, g, b):
    mu = jnp.mean(x, axis=-1, keepdims=True)
    xc = x - mu
    var = jnp.mean(xc * xc, axis=-1, keepdims=True)
    return xc * lax.rsqrt(var + LN_EPS) * g + b


def _full(shape):
    return pl.BlockSpec(shape, lambda *_: (0,) * len(shape))


def _to_row_tiles(ref, row0, x):
    rows = x.shape[0]
    for j in range(D_MODEL // LANES):
        ref[pl.ds(row0 * SUBLANES + j, rows, stride=SUBLANES), :] = x[:, j * LANES:(j + 1) * LANES]


def _from_row_tiles(ref, row0, rows):
    return jnp.concatenate([ref[pl.ds(row0 * SUBLANES + j, rows, stride=SUBLANES), :]
                            for j in range(D_MODEL // LANES)], axis=1)


def _inproj_kernel(x_ref, g_ref, b_ref, w_ref, bias_ref, u_ref, f_ref):
    h = _layer_norm(x_ref[...], g_ref[...], b_ref[...])
    z = jnp.dot(h.astype(BF16), w_ref[...], preferred_element_type=F32) + bias_ref[...]
    u_ref[...] = (z[:, :D_CONV] * jax.nn.sigmoid(z[:, D_CONV:2 * D_CONV])).astype(BF16)
    f_ref[...] = z[:, 2 * D_CONV:]


def _inproj(x, ln_g, ln_b, w, bias):
    t = x.shape[0]
    n_out = w.shape[1]
    return pl.pallas_call(
        _inproj_kernel,
        grid=(t // TM_INPROJ,),
        in_specs=[
            pl.BlockSpec((TM_INPROJ, D_MODEL), lambda i: (i, 0)),
            _full((1, D_MODEL)), _full((1, D_MODEL)),
            _full((D_MODEL, n_out)), _full((1, n_out)),
        ],
        out_specs=[
            pl.BlockSpec((TM_INPROJ, D_CONV), lambda i: (i, 0)),
            pl.BlockSpec((TM_INPROJ, D_FNET), lambda i: (i, 0)),
        ],
        out_shape=[jax.ShapeDtypeStruct((t, D_CONV), BF16), jax.ShapeDtypeStruct((t, D_FNET), F32)],
        compiler_params=_cparams("parallel"),
        name="inproj",
    )(x, ln_g, ln_b, w, bias)


def _conv_kernel(prev_ref, cur_ref, next_ref, cw_ref, cb_ref, g_ref, b_ref, wout_ref, ya_ref,
                 ext_ref, act_ref):
    i = pl.program_id(1)
    last = pl.num_programs(1) - 1
    ts = cur_ref.shape[1]
    ext_rows = ts + 2 * HALO
    zero = jnp.zeros((HALO, D_CONV), F32)
    ext_ref[0, 0:HALO, :] = jnp.where(i > 0, prev_ref[0].astype(F32), zero)
    ext_ref[0, HALO:HALO + ts, :] = cur_ref[0].astype(F32)
    ext_ref[0, HALO + ts:ext_rows, :] = jnp.where(i < last, next_ref[0].astype(F32), zero)
    for r in range(1, SUBLANES):
        ext_ref[r, 0:ext_rows - SUBLANES, :] = ext_ref[0, r:r + ext_rows - SUBLANES, :]

    cb = cb_ref[...]
    g = g_ref[...]
    b = b_ref[...]

    def chunk(c, carry):
        r0 = pl.multiple_of(c * RC_CONV, RC_CONV)
        acc = jnp.zeros((RC_CONV, D_CONV), F32) + cb
        for j in range(CONV_WIDTH):
            off = j + HALO - CONV_PAD
            q, r = divmod(off, SUBLANES)
            acc = acc + cw_ref[j:j + 1, :] * ext_ref[r, pl.ds(r0 + q * SUBLANES, RC_CONV), :]
        y = _layer_norm(acc, g, b)
        act_ref[pl.ds(r0, RC_CONV), :] = (y * jax.nn.sigmoid(y)).astype(BF16)
        return carry

    lax.fori_loop(0, ts // RC_CONV, chunk, 0)
    ya_ref[0] = jnp.dot(act_ref[...], wout_ref[...], preferred_element_type=F32).astype(BF16)


def _conv_branch(u, conv_w, conv_b, ln_g, ln_b, w_out):
    bsz, s, _ = u.shape
    ts = TS_CONV
    nb = s // ts
    hb = ts // HALO
    return pl.pallas_call(
        _conv_kernel,
        grid=(bsz, nb),
        in_specs=[
            pl.BlockSpec((1, HALO, D_CONV), lambda b, i: (b, jnp.maximum(i * hb - 1, 0), 0)),
            pl.BlockSpec((1, ts, D_CONV), lambda b, i: (b, i, 0)),
            pl.BlockSpec((1, HALO, D_CONV), lambda b, i: (b, jnp.minimum((i + 1) * hb, s // HALO - 1), 0)),
            _full((CONV_WIDTH, D_CONV)), _full((1, D_CONV)), _full((1, D_CONV)), _full((1, D_CONV)),
            _full((D_CONV, D_MODEL)),
        ],
        out_specs=pl.BlockSpec((1, ts, D_MODEL), lambda b, i: (b, i, 0)),
        out_shape=jax.ShapeDtypeStruct((bsz, s, D_MODEL), BF16),
        scratch_shapes=[
            pltpu.VMEM((SUBLANES, ts + 2 * HALO, D_CONV), F32),
            pltpu.VMEM((ts, D_CONV), BF16),
        ],
        compiler_params=_cparams("parallel", "parallel"),
        name="conv_branch",
    )(u, u, u, conv_w, conv_b, ln_g, ln_b, w_out)


def _dft_tables(s):
    n2_len = s // DFT_N1
    k1_per_tile = DFT_N1 // n2_len
    two_pi = 2.0 * jnp.pi
    a = jnp.arange(DFT_N1, dtype=I32)
    ang1 = two_pi * ((a[:, None] * a[None, :]) % DFT_N1).astype(F32) / DFT_N1
    small = jnp.concatenate([jnp.cos(ang1), -jnp.sin(ang1)], axis=0)
    r = jnp.arange(2 * DFT_N1 * SUBLANES, dtype=I32)[:, None]
    c = jnp.arange(DFT_N1 * SUBLANES, dtype=I32)[None, :]
    pick_r = (r // SUBLANES == jnp.arange(2 * DFT_N1, dtype=I32)[None, :]).astype(F32)
    pick_c = (jnp.arange(DFT_N1, dtype=I32)[:, None] == c // SUBLANES).astype(F32)
    rep = jnp.dot(jnp.dot(pick_r, small, precision=lax.Precision.HIGHEST), pick_c, precision=lax.Precision.HIGHEST)
    f1 = jnp.where(r % SUBLANES == c % SUBLANES, rep, 0.0).astype(BF16)
    cs = jnp.stack([jnp.cos(ang1), jnp.sin(ang1)]).astype(BF16)
    n_tiles = s // DFT_N1
    t = jnp.arange(n_tiles, dtype=I32)[:, None, None]
    row = jnp.arange(DFT_N1, dtype=I32)[None, :, None]
    col = jnp.arange(DFT_N1, dtype=I32)[None, None, :]
    k2, k1l_out = row // k1_per_tile, row % k1_per_tile
    k1l_in, n2 = col // n2_len, col % n2_len
    k = t * k1_per_tile + k1l_out + DFT_N1 * k2
    ang2 = two_pi * ((n2 * k) % s).astype(F32) / s
    hit = k1l_in == k1l_out
    gr = jnp.where(hit, jnp.cos(ang2), 0.0)
    gi = jnp.where(hit, -jnp.sin(ang2), 0.0)
    g = jnp.concatenate([jnp.concatenate([gr, -gi], axis=2),
                         jnp.concatenate([gi, gr], axis=2)], axis=1).astype(BF16)
    return f1, g, cs


def _fft1_kernel(x_ref, f_ref, y_ref):
    nb = x_ref.shape[2]
    x = x_ref[0].reshape(DFT_N1 * nb, D_FNET).astype(BF16)
    r = jnp.dot(f_ref[...], x, preferred_element_type=F32)
    y_ref[0] = r.reshape(2, DFT_N1, nb, D_FNET)


def _fft2_kernel(y_ref, g_ref, cs_ref, w_ref, o_ref, *, scale, tiles):
    for t in range(tiles):
        rows = slice(t * DFT_N1, (t + 1) * DFT_N1)
        yb = jnp.concatenate([y_ref[0, 0, rows, :], y_ref[0, 1, rows, :]], axis=0).astype(BF16)
        z = jnp.dot(g_ref[t], yb, preferred_element_type=F32)
        zr = z[:DFT_N1].astype(BF16)
        zi = z[DFT_N1:].astype(BF16)
        parts = []
        for grp in range(FNET_GROUPS):
            sl = slice(grp * FNET_GROUP_DIM, (grp + 1) * FNET_GROUP_DIM)
            parts.append(jnp.dot(zr[:, sl], cs_ref[0], preferred_element_type=F32)
                         + jnp.dot(zi[:, sl], cs_ref[1], preferred_element_type=F32))
        fm = (jnp.concatenate(parts, axis=1) * scale).astype(BF16)
        out = jnp.dot(fm, w_ref[...], preferred_element_type=F32)
        k1_per_tile = o_ref.shape[2] // tiles
        if k1_per_tile == 1:
            o_ref[0, :, t, :] = out
        else:
            o_ref[0, :, t * k1_per_tile:(t + 1) * k1_per_tile, :] = out.reshape(o_ref.shape[1], k1_per_tile, D_MODEL)


def _fourier_branch(f_in, w_out):
    bsz, s, _ = f_in.shape
    n2_len = s // DFT_N1
    k1_per_tile = DFT_N1 // n2_len
    f1, g, cs = _dft_tables(s)
    y = pl.pallas_call(
        _fft1_kernel,
        grid=(bsz, n2_len // SUBLANES),
        in_specs=[pl.BlockSpec((1, DFT_N1, SUBLANES, D_FNET), lambda b, j: (b, 0, j, 0)),
                  _full((2 * DFT_N1 * SUBLANES, DFT_N1 * SUBLANES))],
        out_specs=pl.BlockSpec((1, 2, DFT_N1, SUBLANES, D_FNET), lambda b, j: (b, 0, 0, j, 0)),
        out_shape=jax.ShapeDtypeStruct((bsz, 2, DFT_N1, n2_len, D_FNET), F32),
        compiler_params=_cparams("parallel", "parallel"),
        name="fft_stage1",
    )(f_in.reshape(bsz, DFT_N1, n2_len, D_FNET), f1)
    y = y.reshape(bsz, 2, s, D_FNET)
    scale = float((s * FNET_GROUP_DIM) ** -0.5)
    assert n2_len % SUBLANES == 0 and (k1_per_tile == 1 or k1_per_tile % SUBLANES == 0)
    tiles = SUBLANES if k1_per_tile == 1 else min(FFT2_TILES, n2_len)
    k1_per_step = tiles * k1_per_tile
    yb = pl.pallas_call(
        functools.partial(_fft2_kernel, scale=scale, tiles=tiles),
        grid=(bsz, s // (DFT_N1 * tiles)),
        in_specs=[
            pl.BlockSpec((1, 2, DFT_N1 * tiles, D_FNET), lambda b, t: (b, 0, t, 0)),
            pl.BlockSpec((tiles, 2 * DFT_N1, 2 * DFT_N1), lambda b, t: (t, 0, 0)),
            _full((2, DFT_N1, DFT_N1)),
            _full((D_FNET, D_MODEL)),
        ],
        out_specs=pl.BlockSpec((1, n2_len, k1_per_step, D_MODEL), lambda b, t: (b, 0, t, 0)),
        out_shape=jax.ShapeDtypeStruct((bsz, n2_len, DFT_N1, D_MODEL), F32),
        compiler_params=_cparams("parallel", "parallel"),
        name="fft_stage2",
    )(y, g, cs, w_out)
    return yb.reshape(bsz * s, D_MODEL)


def _mix_kernel(x_ref, ya_ref, yb_ref, cin_ref, ln0g_ref, ln0b_ref, wg_ref, bg_ref, wo_ref, bo_ref,
                ln1g_ref, ln1b_ref, wr_ref, br_ref,
                h1_ref, idx_ref, gate_ref, rank_ref, cnt_ref, carry_ref):
    tm = x_ref.shape[0]
    sub = min(SUB_MIX, tm)

    @pl.when(pl.program_id(0) == 0)
    def _():
        carry_ref[...] = cin_ref[...]

    carry = carry_ref[...]
    eio = lax.broadcasted_iota(I32, (N_EXPERTS, sub), 0)
    before = (lax.broadcasted_iota(I32, (sub, sub), 0) < lax.broadcasted_iota(I32, (sub, sub), 1)).astype(BF16)
    for r0 in range(0, tm, sub):
        rows = slice(r0, r0 + sub)
        h = _layer_norm(x_ref[rows, :], ln0g_ref[...], ln0b_ref[...])
        zg = jnp.dot(h.astype(BF16), wg_ref[...], preferred_element_type=F32) + bg_ref[...]
        m = (jax.nn.sigmoid(zg[:, :D_MODEL]) * ya_ref[rows, :].astype(F32)
             + jax.nn.sigmoid(zg[:, D_MODEL:]) * yb_ref[rows, :].astype(F32))
        mix = jnp.dot(m.astype(BF16), wo_ref[...], preferred_element_type=F32) + bo_ref[...]
        h1 = _layer_norm(DEEPNORM_ALPHA * h + mix, ln1g_ref[...], ln1b_ref[...])
        _to_row_tiles(h1_ref, r0, h1)

        h_hi = h1.astype(BF16)
        h_lo = (h1 - h_hi.astype(F32)).astype(BF16)
        hw = jnp.dot(h_hi, wr_ref[...], preferred_element_type=F32)
        lw = jnp.dot(h_lo, wr_ref[:, :LANES], preferred_element_type=F32)
        logits = (hw[:, :LANES] + hw[:, LANES:] + lw).T[:N_EXPERTS] + br_ref[...]
        vals, idxs = [], []
        cur = logits
        for _ in range(TOP_K):
            mx = jnp.max(cur, axis=0, keepdims=True)
            ik = jnp.min(jnp.where(cur == mx, eio, N_EXPERTS), axis=0, keepdims=True)
            vals.append(mx)
            idxs.append(ik)
            cur = jnp.where(eio == ik, -jnp.inf, cur)
        exps = [jnp.exp(v - vals[0]) for v in vals]
        den = exps[0] + exps[1] + exps[2] + exps[3]
        gate_ref[:, rows] = jnp.concatenate([e / den for e in exps], axis=0)
        idx_ref[:, rows] = jnp.concatenate(idxs, axis=0)

        hot = jnp.zeros((N_EXPERTS, sub), F32)
        for ik in idxs:
            hot = hot + (eio == ik).astype(F32)
        prior = jnp.dot(hot.astype(BF16), before, preferred_element_type=F32) + carry[:, 0:1]
        ranks = [jnp.sum(jnp.where(eio == ik, prior, 0.0), axis=0, keepdims=True) for ik in idxs]
        rank_ref[:, rows] = jnp.concatenate(ranks, axis=0).astype(I32)
        carry = carry + jnp.sum(hot, axis=1, keepdims=True)
    carry_ref[...] = carry
    cnt_ref[...] = carry


def _mix_route(x, ya, yb, carry_in, ln0_g, ln0_b, w_gate, b_gate, w_o, b_o, ln1_g, ln1_b, w_rt, b_rt):
    t = x.shape[0]
    tm = TM_MIX
    row = lambda d: pl.BlockSpec((tm, d), lambda i: (i, 0))
    col = lambda r: pl.BlockSpec((r, tm), lambda i: (0, i))
    return pl.pallas_call(
        _mix_kernel,
        grid=(t // tm,),
        in_specs=[
            row(D_MODEL), row(D_MODEL), row(D_MODEL), _full((N_EXPERTS, LANES)),
            _full((1, D_MODEL)), _full((1, D_MODEL)),
            _full((D_MODEL, 2 * D_MODEL)), _full((1, 2 * D_MODEL)),
            _full((D_MODEL, D_MODEL)), _full((1, D_MODEL)),
            _full((1, D_MODEL)), _full((1, D_MODEL)),
            _full((D_MODEL, 2 * LANES)), _full((N_EXPERTS, 1)),
        ],
        out_specs=[pl.BlockSpec((tm * SUBLANES, LANES), lambda i: (i, 0)), col(TOP_K), col(TOP_K), col(TOP_K),
                   _full((N_EXPERTS, LANES))],
        out_shape=[
            jax.ShapeDtypeStruct((t * SUBLANES, LANES), F32),
            jax.ShapeDtypeStruct((TOP_K, t), I32),
            jax.ShapeDtypeStruct((TOP_K, t), F32),
            jax.ShapeDtypeStruct((TOP_K, t), I32),
            jax.ShapeDtypeStruct((N_EXPERTS, LANES), F32),
        ],
        scratch_shapes=[pltpu.VMEM((N_EXPERTS, LANES), F32)],
        compiler_params=_cparams("arbitrary"),
        name="mix_route",
    )(x, ya, yb, carry_in, ln0_g, ln0_b, w_gate, b_gate, w_o, b_o, ln1_g, ln1_b, w_rt, b_rt)


def _row_copy(src_ref, src_row, dst_ref, dst_row, sem):
    return pltpu.make_async_copy(src_ref.at[src_row], dst_ref.at[dst_row], sem)


def _fetch_slots(slots_hbm, slots_smem, sem, step, slot):
    n = slots_hbm.shape[1]
    return pltpu.make_async_copy(slots_hbm.at[step], slots_smem.at[pl.ds(pl.multiple_of(slot * n, n), n)],
                                 sem.at[slot])


def _pad_fill(h_ref, xs_out, pad_smem, pad_sem, n_pad_runs, wait):
    def go(cp):
        if wait:
            cp.wait()
        else:
            cp.start()

    def run(j, carry):
        start = pad_smem[2 * j]
        length = pad_smem[2 * j + 1]
        for bit in range(BM_MOE.bit_length() - 1):
            size = 1 << bit

            @pl.when((length & size) != 0)
            def _():
                off = start + (length & ~(2 * size - 1))
                go(pltpu.make_async_copy(h_ref.at[pl.ds(0, size)], xs_out.at[pl.ds(off, size)], pad_sem))
        return carry

    lax.fori_loop(0, n_pad_runs, run, 0)

    piece = min(h_ref.shape[0], BM_MOE)

    def dead_block(j, carry):
        for part in range(BM_MOE // piece):
            off = j * BM_MOE + part * piece
            go(pltpu.make_async_copy(h_ref.at[pl.ds(0, piece)], xs_out.at[pl.ds(off, piece)], pad_sem))
        return carry

    lax.fori_loop(pad_smem[2 * n_pad_runs], xs_out.shape[0] // BM_MOE, dead_block, 0)


def _dispatch_kernel(*refs, tiles):
    n_sets = len(tiles)
    h_refs = refs[:n_sets]
    slots_hbm, pad_hbm, xs_out, slots_smem, pad_smem, slot_sem, pad_sem, row_sem = refs[n_sets:]
    i = pl.program_id(0)
    n = pl.num_programs(0)
    tm = h_refs[0].shape[0]
    cur = lax.rem(i, 2)

    @pl.when(i == 0)
    def _():
        _fetch_slots(slots_hbm, slots_smem, slot_sem, 0, 0).start()
        fetch = pltpu.make_async_copy(pad_hbm, pad_smem, pad_sem)
        fetch.start()
        fetch.wait()
        _pad_fill(h_refs[0], xs_out, pad_smem, pad_sem, N_EXPERTS, wait=False)
        _pad_fill(h_refs[0], xs_out, pad_smem, pad_sem, N_EXPERTS, wait=True)

    @pl.when(i + 1 < n)
    def _():
        _fetch_slots(slots_hbm, slots_smem, slot_sem, i + 1, 1 - cur).start()

    _fetch_slots(slots_hbm, slots_smem, slot_sem, i, cur).wait()
    base = cur * (TOP_K * tm)

    first = 0
    for h_ref, count in zip(h_refs, tiles):
        @pl.when(jnp.logical_and(i >= first, i < first + count))
        def _(h_ref=h_ref):
            def issue(r, carry):
                for k in range(TOP_K):
                    _row_copy(h_ref, r, xs_out, slots_smem[base + TOP_K * r + k], row_sem).start(priority=k % 2)
                return carry

            lax.fori_loop(0, tm, issue, 0, unroll=ISSUE_UNROLL)
            for _ in range(TOP_K):
                pltpu.make_async_copy(h_ref, xs_out.at[pl.ds(0, tm)], row_sem).wait()
        first += count


def _dispatch(h1s, slots, pad_runs, n_rows):
    tm = TM_ROWS
    h1s = [h.reshape(-1, SUBLANES, LANES) for h in h1s]
    tiles = [h.shape[0] // tm for h in h1s]
    firsts = [sum(tiles[:j]) for j in range(len(tiles))]

    def tile_of(first, count):
        return lambda i: (jnp.clip(i - first, 0, count - 1), 0, 0)

    return pl.pallas_call(
        functools.partial(_dispatch_kernel, tiles=tuple(tiles)),
        grid=(sum(tiles),),
        in_specs=[pl.BlockSpec((tm, SUBLANES, LANES), tile_of(f, c)) for f, c in zip(firsts, tiles)]
        + [pl.BlockSpec(memory_space=pl.ANY)] * 2,
        out_specs=pl.BlockSpec(memory_space=pl.ANY),
        out_shape=jax.ShapeDtypeStruct((n_rows, SUBLANES, LANES), F32),
        scratch_shapes=[
            pltpu.SMEM((2 * TOP_K * tm,), I32),
            pltpu.SMEM((pad_runs.shape[0],), I32),
            pltpu.SemaphoreType.DMA((2,)),
            pltpu.SemaphoreType.DMA,
            pltpu.SemaphoreType.DMA,
        ],
        compiler_params=_cparams("arbitrary"),
        name="dispatch",
    )(*h1s, slots, pad_runs)


def _moe_kernel(blk_e_ref, nb_ref, x_ref, wgu_ref, bgu_ref, wd_ref, bd_ref, o_ref, wgu_b, wd_b):
    i = pl.program_id(0)
    is_live = i < nb_ref[0]

    @pl.when(jnp.logical_not(is_live))
    def _():
        o_ref[...] = jnp.zeros(o_ref.shape, F32)

    @pl.when(jnp.logical_or(i == 0, blk_e_ref[i] != blk_e_ref[jnp.maximum(i - 1, 0)]))
    def _():
        for r in range(0, D_MODEL, CAST_ROWS):
            wgu_b[r:r + CAST_ROWS, :] = wgu_ref[0, r:r + CAST_ROWS, :].astype(BF16)
        for r in range(0, D_FF, CAST_ROWS):
            wd_b[r:r + CAST_ROWS, :] = wd_ref[0, r:r + CAST_ROWS, :].astype(BF16)

    @pl.when(is_live)
    def _():
        x = _from_row_tiles(x_ref, 0, BM_MOE).astype(BF16)
        acc = jnp.zeros((BM_MOE, D_MODEL), F32) + bd_ref[0]
        for c in range(D_FF // FF_CHUNK):
            lo = c * FF_CHUNK
            g = jnp.dot(x, wgu_b[:, lo:lo + FF_CHUNK], preferred_element_type=F32) + bgu_ref[0, :, lo:lo + FF_CHUNK]
            u = (jnp.dot(x, wgu_b[:, D_FF + lo:D_FF + lo + FF_CHUNK], preferred_element_type=F32)
                 + bgu_ref[0, :, D_FF + lo:D_FF + lo + FF_CHUNK])
            g = jnp.minimum(g, SWIGLU_LIMIT)
            u = jnp.clip(u, -SWIGLU_LIMIT, SWIGLU_LIMIT)
            act = (u + 1.0) * (g * jax.nn.sigmoid(g * SWIGLU_ALPHA))
            acc = acc + jnp.dot(act.astype(BF16), wd_b[lo:lo + FF_CHUNK, :], preferred_element_type=F32)
        _to_row_tiles(o_ref, 0, acc)


def _moe_blocks(xs, blk_e, nb_used, w_gu, b_gu, w_down, b_down):
    p = xs.shape[0] // SUBLANES
    n_blocks = p // BM_MOE
    live = lambda i, nb: jnp.minimum(i, nb[0] - 1)
    grid_spec = pltpu.PrefetchScalarGridSpec(
        num_scalar_prefetch=2,
        grid=(n_blocks,),
        in_specs=[
            pl.BlockSpec((BM_MOE * SUBLANES, LANES), lambda i, be, nb: (live(i, nb), 0)),
            pl.BlockSpec((1, D_MODEL, 2 * D_FF), lambda i, be, nb: (be[i], 0, 0)),
            pl.BlockSpec((1, 1, 2 * D_FF), lambda i, be, nb: (be[i], 0, 0)),
            pl.BlockSpec((1, D_FF, D_MODEL), lambda i, be, nb: (be[i], 0, 0)),
            pl.BlockSpec((1, 1, D_MODEL), lambda i, be, nb: (be[i], 0, 0)),
        ],
        out_specs=pl.BlockSpec((BM_MOE * SUBLANES, LANES), lambda i, be, nb: (i, 0)),
        scratch_shapes=[pltpu.VMEM((D_MODEL, 2 * D_FF), BF16), pltpu.VMEM((D_FF, D_MODEL), BF16)],
    )
    return pl.pallas_call(
        _moe_kernel,
        grid_spec=grid_spec,
        out_shape=jax.ShapeDtypeStruct((p * SUBLANES, LANES), F32),
        compiler_params=pltpu.CompilerParams(dimension_semantics=("arbitrary",), vmem_limit_bytes=VMEM_LIMIT_MOE),
        name="moe_experts",
    )(blk_e, nb_used, xs, w_gu, b_gu, w_down, b_down)


def _final_kernel(h_ref, p_ref, gate_ref, slots_hbm, y_hbm, wpg_ref, bpg_ref, wpp_ref, g_ref, b_ref, o_ref,
                  slots_smem, rows_ref, slot_sem, row_sem):
    i = pl.program_id(0)
    n = pl.num_programs(0)
    tm = h_ref.shape[0] // SUBLANES
    cur = lax.rem(i, 2)
    nxt = 1 - cur

    def gather(r, buf, base):
        for k in range(TOP_K):
            slot = slots_smem[base + TOP_K * r + k]
            src = y_hbm.at[pl.ds(pl.multiple_of(slot * SUBLANES, SUBLANES), SUBLANES)]
            dst = rows_ref.at[buf, k, pl.ds(pl.multiple_of(r * SUBLANES, SUBLANES), SUBLANES)]
            pltpu.make_async_copy(src, dst, row_sem.at[buf]).start(priority=k % 2)

    def wait_rows(buf):
        for k in range(TOP_K):
            pltpu.make_async_copy(y_hbm.at[pl.ds(0, tm * SUBLANES)], rows_ref.at[buf, k], row_sem.at[buf]).wait()

    @pl.when(i == 0)
    def _():
        first = _fetch_slots(slots_hbm, slots_smem, slot_sem, 0, 0)
        first.start()
        first.wait()

        def issue(r, carry):
            gather(r, 0, 0)
            return carry

        lax.fori_loop(0, tm, issue, 0, unroll=ISSUE_UNROLL)
        _fetch_slots(slots_hbm, slots_smem, slot_sem, 1, 1).start()

    _fetch_slots(slots_hbm, slots_smem, slot_sem, i + 1, nxt).wait()

    @pl.when(i + 2 <= n)
    def _():
        _fetch_slots(slots_hbm, slots_smem, slot_sem, i + 2, cur).start()

    base = nxt * (TOP_K * tm)
    for r in range(tm):
        gather(r, nxt, base)

    h = _from_row_tiles(h_ref, 0, tm)
    ple = (jax.nn.sigmoid(jnp.dot(h.astype(BF16), wpg_ref[...], preferred_element_type=F32) + bpg_ref[...])
           * jnp.dot(p_ref[...].astype(BF16), wpp_ref[...], preferred_element_type=F32))

    wait_rows(cur)
    gates = gate_ref[...]
    ffn = gates[:, 0:1] * _from_row_tiles(rows_ref.at[cur, 0], 0, tm)
    for k in range(1, TOP_K):
        ffn = ffn + gates[:, k:k + 1] * _from_row_tiles(rows_ref.at[cur, k], 0, tm)
    o_ref[...] = _layer_norm(DEEPNORM_ALPHA * h + (ffn + ple), g_ref[...], b_ref[...])

    @pl.when(i == n - 1)
    def _():
        wait_rows(nxt)


def _combine_final(h1, p, gates, slots, y, w_pg, b_pg, w_pp, ln_g, ln_b):
    t = h1.shape[0] // SUBLANES
    tm = TM_COMBINE
    slots = jnp.concatenate([slots, slots[-1:]], axis=0)
    return pl.pallas_call(
        _final_kernel,
        grid=(t // tm,),
        in_specs=[
            pl.BlockSpec((tm * SUBLANES, LANES), lambda i: (i, 0)),
            pl.BlockSpec((tm, D_PLE), lambda i: (i, 0)),
            pl.BlockSpec((tm, TOP_K), lambda i: (i, 0)),
            pl.BlockSpec(memory_space=pl.ANY),
            pl.BlockSpec(memory_space=pl.ANY),
            _full((D_MODEL, D_MODEL)), _full((1, D_MODEL)), _full((D_PLE, D_MODEL)),
            _full((1, D_MODEL)), _full((1, D_MODEL)),
        ],
        out_specs=pl.BlockSpec((tm, D_MODEL), lambda i: (i, 0)),
        out_shape=jax.ShapeDtypeStruct((t, D_MODEL), F32),
        scratch_shapes=[
            pltpu.SMEM((2 * TOP_K * tm,), I32),
            pltpu.VMEM((2, TOP_K, tm * SUBLANES, LANES), F32),
            pltpu.SemaphoreType.DMA((2,)),
            pltpu.SemaphoreType.DMA((2,)),
        ],
        compiler_params=_cparams("arbitrary"),
        name="combine_final",
    )(h1, p, gates, slots, y, w_pg, b_pg, w_pp, ln_g, ln_b)


def _tile_slots(dest, tm):
    t = dest.shape[1]
    return dest.T.reshape(t // tm, tm * TOP_K)


def kernel(x_prompt, x_sample, p_prompt, p_sample, ln0_g, ln0_b, w_in, b_in, conv_w, conv_b, lnc_g, lnc_b, w_conv_out, w_fnet_out, w_o, b_o, ln1_g, ln1_b, w_router, b_router, w_gu, b_gu, w_down, b_down, w_pg, b_pg, w_pp, ln2_g, ln2_b):
    row = lambda v: v.reshape(1, -1).astype(F32)
    n_branch = 2 * D_CONV + D_FNET
    w_in_b = w_in[0].astype(BF16)
    w_branch, w_gate = w_in_b[:, :n_branch], w_in_b[:, n_branch:]
    b_branch, b_gate = row(b_in[0, :n_branch]), row(b_in[0, n_branch:])
    w_conv_out_b = w_conv_out[0].astype(BF16)
    w_fnet_out_b = w_fnet_out[0].astype(BF16)
    w_o_b = w_o[0].astype(BF16)
    w_rt_f = jnp.pad(w_router[0].astype(F32), ((0, 0), (0, LANES - N_EXPERTS)))
    w_rt_hi = w_rt_f.astype(BF16)
    w_rt = jnp.concatenate([w_rt_hi, (w_rt_f - w_rt_hi.astype(F32)).astype(BF16)], axis=1)
    b_rt = b_router[0].reshape(N_EXPERTS, 1).astype(F32)
    b_gu_r = b_gu[0].reshape(N_EXPERTS, 1, 2 * D_FF).astype(F32)
    b_down_r = b_down[0].reshape(N_EXPERTS, 1, D_MODEL).astype(F32)
    w_pg_b = w_pg[0].astype(BF16)
    w_pp_b = w_pp[0].astype(BF16)

    carry = jnp.zeros((N_EXPERTS, LANES), F32)
    routed = []
    for x, p in ((x_prompt, p_prompt[0]), (x_sample, p_sample[0])):
        bsz, s, _ = x.shape
        xt = x.reshape(bsz * s, D_MODEL)
        u, f_in = _inproj(xt, row(ln0_g), row(ln0_b), w_branch, b_branch)
        ya = _conv_branch(u.reshape(bsz, s, D_CONV), conv_w[0].astype(F32), row(conv_b[0]), row(lnc_g[0]),
                          row(lnc_b[0]), w_conv_out_b).reshape(bsz * s, D_MODEL)
        yb = _fourier_branch(f_in.reshape(bsz, s, D_FNET), w_fnet_out_b)
        h1, idx, gates, rank, carry = _mix_route(
            xt, ya, yb, carry, row(ln0_g), row(ln0_b), w_gate, b_gate, w_o_b, row(b_o[0]),
            row(ln1_g[0]), row(ln1_b[0]), w_rt, b_rt)
        routed.append((h1, idx, gates, rank, p.reshape(bsz * s, D_PLE), (bsz, s)))

    n_assign = sum(r[1].shape[1] for r in routed) * TOP_K
    n_blocks = (n_assign + N_EXPERTS * (BM_MOE - 1)) // BM_MOE
    counts = carry[:, 0].astype(I32)
    padded = ((counts + BM_MOE - 1) // BM_MOE) * BM_MOE
    ends = jnp.cumsum(padded)
    start = ends - padded
    nb_used = (ends[-1] // BM_MOE).reshape(1).astype(I32)
    blk = jnp.minimum(jnp.arange(n_blocks, dtype=I32), nb_used[0] - 1)
    blk_e = jnp.minimum(jnp.sum((ends[None, :] <= (blk * BM_MOE)[:, None]).astype(I32), axis=1), N_EXPERTS - 1)
    pad_runs = jnp.concatenate([jnp.stack([start + counts, padded - counts], axis=1).reshape(-1), nb_used,
                                jnp.zeros((LANES - 2 * N_EXPERTS - 1,), I32)]).astype(I32)

    slot_tables = []
    dispatch_slots = []
    for _, idx, _, rank, _, _ in routed:
        first = jnp.sum(jnp.where(idx[None] == jnp.arange(N_EXPERTS, dtype=I32)[:, None, None],
                                  start[:, None, None], 0), axis=0)
        slot_tables.append(_tile_slots(first + rank, TM_COMBINE))
        dispatch_slots.append(_tile_slots(first + rank, TM_ROWS))
    xs = _dispatch([r[0] for r in routed], jnp.concatenate(dispatch_slots, axis=0), pad_runs, n_blocks * BM_MOE)

    y = _moe_blocks(xs.reshape(-1, LANES), blk_e, nb_used, w_gu.reshape(N_EXPERTS, D_MODEL, 2 * D_FF), b_gu_r,
                    w_down.reshape(N_EXPERTS, D_FF, D_MODEL), b_down_r)

    outs = []
    for (h1, _, gates, _, p, (bsz, s)), slots in zip(routed, slot_tables):
        o = _combine_final(h1, p, gates.T, slots, y, w_pg_b, row(b_pg[0]), w_pp_b, row(ln2_g[0]), row(ln2_b[0]))
        outs.append(o.reshape(bsz, s, D_MODEL))
    return tuple(outs)
```

```python
import functools

import jax
import jax.numpy as jnp
from jax import lax
from jax.experimental import pallas as pl
from jax.experimental.pallas import tpu as pltpu

F32 = jnp.float32
BF16 = jnp.bfloat16
I32 = jnp.int32

D_MODEL = 1024
D_CONV = D_MODEL // 2
CONV_WIDTH = 31
CONV_PAD = CONV_WIDTH // 2
D_FNET = D_MODEL // 2
FNET_GROUPS = 4
FNET_GROUP_DIM = D_FNET // FNET_GROUPS
D_PLE = 256
N_EXPERTS = 32
TOP_K = 4
D_FF = D_MODEL
SWIGLU_ALPHA = 1.702
SWIGLU_LIMIT = 7.0
LN_EPS = 1e-5
DEPTH = 1
DEEPNORM_ALPHA = (2 * DEPTH) ** 0.25

LANES = 128
SUBLANES = 8
DFT_N1 = 128
HALO = 16

TM_INPROJ = 512
TS_CONV = 512
RC_CONV = 128
TM_MIX = 512
SUB_MIX = 512
FFT2_TILES = 4
TM_ROWS = 1024
TM_COMBINE = 256
BM_MOE = 512
FF_CHUNK = 512
ISSUE_UNROLL = 8
CAST_ROWS = 128
VMEM_LIMIT = 48 * 1024 * 1024
VMEM_LIMIT_MOE = 56 * 1024 * 1024


def _cparams(*sem):
    return pltpu.CompilerParams(dimension_semantics=sem, vmem_limit_bytes=VMEM_LIMIT)


def _layer_norm(x, g, b):
    mu = jnp.mean(x, axis=-1, keepdims=True)
    xc = x - mu
    var = jnp.mean(xc * xc, axis=-1, keepdims=True)
    return xc * lax.rsqrt(var + LN_EPS) * g + b


def _sigmoid(x):
    return 0.5 * jnp.tanh(0.5 * x) + 0.5


def _full(shape):
    return pl.BlockSpec(shape, lambda *_: (0,) * len(shape))


def _to_row_tiles(ref, row0, x):
    rows = x.shape[0]
    for j in range(D_MODEL // LANES):
        ref[pl.ds(row0 * SUBLANES + j, rows, stride=SUBLANES), :] = x[:, j * LANES:(j + 1) * LANES]


def _from_row_tiles(ref, row0, rows):
    return jnp.concatenate([ref[pl.ds(row0 * SUBLANES + j, rows, stride=SUBLANES), :]
                            for j in range(D_MODEL // LANES)], axis=1)


def _inproj_kernel(x_ref, g_ref, b_ref, w_ref, bias_ref, u_ref, f_ref):
    h = _layer_norm(x_ref[...], g_ref[...], b_ref[...])
    z = jnp.dot(h.astype(BF16), w_ref[...], preferred_element_type=F32) + bias_ref[...]
    u_ref[...] = (z[:, :D_CONV] * _sigmoid(z[:, D_CONV:2 * D_CONV])).astype(BF16)
    f_ref[...] = z[:, 2 * D_CONV:]


def _inproj(x, ln_g, ln_b, w, bias):
    t = x.shape[0]
    n_out = w.shape[1]
    return pl.pallas_call(
        _inproj_kernel,
        grid=(t // TM_INPROJ,),
        in_specs=[
            pl.BlockSpec((TM_INPROJ, D_MODEL), lambda i: (i, 0)),
            _full((1, D_MODEL)), _full((1, D_MODEL)),
            _full((D_MODEL, n_out)), _full((1, n_out)),
        ],
        out_specs=[
            pl.BlockSpec((TM_INPROJ, D_CONV), lambda i: (i, 0)),
            pl.BlockSpec((TM_INPROJ, D_FNET), lambda i: (i, 0)),
        ],
        out_shape=[jax.ShapeDtypeStruct((t, D_CONV), BF16), jax.ShapeDtypeStruct((t, D_FNET), F32)],
        compiler_params=_cparams("parallel"),
        name="inproj",
    )(x, ln_g, ln_b, w, bias)


def _conv_kernel(prev_ref, cur_ref, next_ref, cw_ref, cb_ref, g_ref, b_ref, wout_ref, ya_ref,
                 ext_ref, act_ref):
    i = pl.program_id(1)
    last = pl.num_programs(1) - 1
    ts = cur_ref.shape[1]
    ext_rows = ts + 2 * HALO
    zero = jnp.zeros((HALO, D_CONV), F32)
    ext_ref[0, 0:HALO, :] = jnp.where(i > 0, prev_ref[0].astype(F32), zero)
    ext_ref[0, HALO:HALO + ts, :] = cur_ref[0].astype(F32)
    ext_ref[0, HALO + ts:ext_rows, :] = jnp.where(i < last, next_ref[0].astype(F32), zero)
    for r in range(1, SUBLANES):
        ext_ref[r, 0:ext_rows - SUBLANES, :] = ext_ref[0, r:r + ext_rows - SUBLANES, :]

    cb = cb_ref[...]
    g = g_ref[...]
    b = b_ref[...]

    def chunk(c, carry):
        r0 = pl.multiple_of(c * RC_CONV, RC_CONV)
        acc = jnp.zeros((RC_CONV, D_CONV), F32) + cb
        for j in range(CONV_WIDTH):
            off = j + HALO - CONV_PAD
            q, r = divmod(off, SUBLANES)
            acc = acc + cw_ref[j:j + 1, :] * ext_ref[r, pl.ds(r0 + q * SUBLANES, RC_CONV), :]
        y = _layer_norm(acc, g, b)
        act_ref[pl.ds(r0, RC_CONV), :] = (y * _sigmoid(y)).astype(BF16)
        return carry

    lax.fori_loop(0, ts // RC_CONV, chunk, 0)
    ya_ref[0] = jnp.dot(act_ref[...], wout_ref[...], preferred_element_type=F32).astype(BF16)


def _conv_branch(u, conv_w, conv_b, ln_g, ln_b, w_out):
    bsz, s, _ = u.shape
    ts = TS_CONV
    nb = s // ts
    hb = ts // HALO
    return pl.pallas_call(
        _conv_kernel,
        grid=(bsz, nb),
        in_specs=[
            pl.BlockSpec((1, HALO, D_CONV), lambda b, i: (b, jnp.maximum(i * hb - 1, 0), 0)),
            pl.BlockSpec((1, ts, D_CONV), lambda b, i: (b, i, 0)),
            pl.BlockSpec((1, HALO, D_CONV), lambda b, i: (b, jnp.minimum((i + 1) * hb, s // HALO - 1), 0)),
            _full((CONV_WIDTH, D_CONV)), _full((1, D_CONV)), _full((1, D_CONV)), _full((1, D_CONV)),
            _full((D_CONV, D_MODEL)),
        ],
        out_specs=pl.BlockSpec((1, ts, D_MODEL), lambda b, i: (b, i, 0)),
        out_shape=jax.ShapeDtypeStruct((bsz, s, D_MODEL), BF16),
        scratch_shapes=[
            pltpu.VMEM((SUBLANES, ts + 2 * HALO, D_CONV), F32),
            pltpu.VMEM((ts, D_CONV), BF16),
        ],
        compiler_params=_cparams("parallel", "parallel"),
        name="conv_branch",
    )(u, u, u, conv_w, conv_b, ln_g, ln_b, w_out)


def _dft_tables(s):
    n2_len = s // DFT_N1
    k1_per_tile = DFT_N1 // n2_len
    two_pi = 2.0 * jnp.pi
    a = jnp.arange(DFT_N1, dtype=I32)
    ang1 = two_pi * ((a[:, None] * a[None, :]) % DFT_N1).astype(F32) / DFT_N1
    small = jnp.concatenate([jnp.cos(ang1), -jnp.sin(ang1)], axis=0)
    r = jnp.arange(2 * DFT_N1 * SUBLANES, dtype=I32)[:, None]
    c = jnp.arange(DFT_N1 * SUBLANES, dtype=I32)[None, :]
    pick_r = (r // SUBLANES == jnp.arange(2 * DFT_N1, dtype=I32)[None, :]).astype(F32)
    pick_c = (jnp.arange(DFT_N1, dtype=I32)[:, None] == c // SUBLANES).astype(F32)
    rep = jnp.dot(jnp.dot(pick_r, small, precision=lax.Precision.HIGHEST), pick_c, precision=lax.Precision.HIGHEST)
    f1 = jnp.where(r % SUBLANES == c % SUBLANES, rep, 0.0).astype(BF16)
    cs = jnp.stack([jnp.cos(ang1), jnp.sin(ang1)]).astype(BF16)
    n_tiles = s // DFT_N1
    t = jnp.arange(n_tiles, dtype=I32)[:, None, None]
    row = jnp.arange(DFT_N1, dtype=I32)[None, :, None]
    col = jnp.arange(DFT_N1, dtype=I32)[None, None, :]
    k2, k1l_out = row // k1_per_tile, row % k1_per_tile
    k1l_in, n2 = col // n2_len, col % n2_len
    k = t * k1_per_tile + k1l_out + DFT_N1 * k2
    ang2 = two_pi * ((n2 * k) % s).astype(F32) / s
    hit = k1l_in == k1l_out
    gr = jnp.where(hit, jnp.cos(ang2), 0.0)
    gi = jnp.where(hit, -jnp.sin(ang2), 0.0)
    g = jnp.concatenate([jnp.concatenate([gr, -gi], axis=2),
                         jnp.concatenate([gi, gr], axis=2)], axis=1).astype(BF16)
    return f1, g, cs


def _fft1_kernel(x_ref, f_ref, y_ref):
    nb = x_ref.shape[2]
    x = x_ref[0].reshape(DFT_N1 * nb, D_FNET).astype(BF16)
    r = jnp.dot(f_ref[...], x, preferred_element_type=F32)
    y_ref[0] = r.reshape(2, DFT_N1, nb, D_FNET)


def _fft2_kernel(y_ref, g_ref, cs_ref, w_ref, o_ref, *, scale, tiles):
    for t in range(tiles):
        rows = slice(t * DFT_N1, (t + 1) * DFT_N1)
        yb = jnp.concatenate([y_ref[0, 0, rows, :], y_ref[0, 1, rows, :]], axis=0).astype(BF16)
        z = jnp.dot(g_ref[t], yb, preferred_element_type=F32)
        zr = z[:DFT_N1].astype(BF16)
        zi = z[DFT_N1:].astype(BF16)
        parts = []
        for grp in range(FNET_GROUPS):
            sl = slice(grp * FNET_GROUP_DIM, (grp + 1) * FNET_GROUP_DIM)
            parts.append(jnp.dot(zr[:, sl], cs_ref[0], preferred_element_type=F32)
                         + jnp.dot(zi[:, sl], cs_ref[1], preferred_element_type=F32))
        fm = (jnp.concatenate(parts, axis=1) * scale).astype(BF16)
        out = jnp.dot(fm, w_ref[...], preferred_element_type=F32)
        k1_per_tile = o_ref.shape[2] // tiles
        if k1_per_tile == 1:
            o_ref[0, :, t, :] = out
        else:
            o_ref[0, :, t * k1_per_tile:(t + 1) * k1_per_tile, :] = out.reshape(o_ref.shape[1], k1_per_tile, D_MODEL)


def _fourier_branch(f_in, w_out):
    bsz, s, _ = f_in.shape
    n2_len = s // DFT_N1
    k1_per_tile = DFT_N1 // n2_len
    f1, g, cs = _dft_tables(s)
    y = pl.pallas_call(
        _fft1_kernel,
        grid=(bsz, n2_len // SUBLANES),
        in_specs=[pl.BlockSpec((1, DFT_N1, SUBLANES, D_FNET), lambda b, j: (b, 0, j, 0)),
                  _full((2 * DFT_N1 * SUBLANES, DFT_N1 * SUBLANES))],
        out_specs=pl.BlockSpec((1, 2, DFT_N1, SUBLANES, D_FNET), lambda b, j: (b, 0, 0, j, 0)),
        out_shape=jax.ShapeDtypeStruct((bsz, 2, DFT_N1, n2_len, D_FNET), F32),
        compiler_params=_cparams("parallel", "parallel"),
        name="fft_stage1",
    )(f_in.reshape(bsz, DFT_N1, n2_len, D_FNET), f1)
    y = y.reshape(bsz, 2, s, D_FNET)
    scale = float((s * FNET_GROUP_DIM) ** -0.5)
    assert n2_len % SUBLANES == 0 and (k1_per_tile == 1 or k1_per_tile % SUBLANES == 0)
    tiles = SUBLANES if k1_per_tile == 1 else min(FFT2_TILES, n2_len)
    k1_per_step = tiles * k1_per_tile
    yb = pl.pallas_call(
        functools.partial(_fft2_kernel, scale=scale, tiles=tiles),
        grid=(bsz, s // (DFT_N1 * tiles)),
        in_specs=[
            pl.BlockSpec((1, 2, DFT_N1 * tiles, D_FNET), lambda b, t: (b, 0, t, 0)),
            pl.BlockSpec((tiles, 2 * DFT_N1, 2 * DFT_N1), lambda b, t: (t, 0, 0)),
            _full((2, DFT_N1, DFT_N1)),
            _full((D_FNET, D_MODEL)),
        ],
        out_specs=pl.BlockSpec((1, n2_len, k1_per_step, D_MODEL), lambda b, t: (b, 0, t, 0)),
        out_shape=jax.ShapeDtypeStruct((bsz, n2_len, DFT_N1, D_MODEL), F32),
        compiler_params=_cparams("parallel", "parallel"),
        name="fft_stage2",
    )(y, g, cs, w_out)
    return yb.reshape(bsz * s, D_MODEL)


def _mix_kernel(x_ref, ya_ref, yb_ref, cin_ref, ln0g_ref, ln0b_ref, wg_ref, bg_ref, wo_ref, bo_ref,
                ln1g_ref, ln1b_ref, wr_ref, br_ref,
                h1_ref, idx_ref, gate_ref, rank_ref, cnt_ref, carry_ref):
    tm = x_ref.shape[0]
    sub = min(SUB_MIX, tm)

    @pl.when(pl.program_id(0) == 0)
    def _():
        carry_ref[...] = cin_ref[...]

    carry = carry_ref[...]
    eio = lax.broadcasted_iota(I32, (N_EXPERTS, sub), 0)
    before = (lax.broadcasted_iota(I32, (sub, sub), 0) < lax.broadcasted_iota(I32, (sub, sub), 1)).astype(BF16)
    for r0 in range(0, tm, sub):
        rows = slice(r0, r0 + sub)
        h = _layer_norm(x_ref[rows, :], ln0g_ref[...], ln0b_ref[...])
        zg = jnp.dot(h.astype(BF16), wg_ref[...], preferred_element_type=F32) + bg_ref[...]
        m = (_sigmoid(zg[:, :D_MODEL]) * ya_ref[rows, :].astype(F32)
             + _sigmoid(zg[:, D_MODEL:]) * yb_ref[rows, :].astype(F32))
        mix = jnp.dot(m.astype(BF16), wo_ref[...], preferred_element_type=F32) + bo_ref[...]
        h1 = _layer_norm(DEEPNORM_ALPHA * h + mix, ln1g_ref[...], ln1b_ref[...])
        _to_row_tiles(h1_ref, r0, h1)

        h_hi = h1.astype(BF16)
        h_lo = (h1 - h_hi.astype(F32)).astype(BF16)
        hw = jnp.dot(h_hi, wr_ref[...], preferred_element_type=F32)
        lw = jnp.dot(h_lo, wr_ref[:, :LANES], preferred_element_type=F32)
        logits = (hw[:, :LANES] + hw[:, LANES:] + lw).T[:N_EXPERTS] + br_ref[...]
        vals, idxs = [], []
        cur = logits
        for _ in range(TOP_K):
            mx = jnp.max(cur, axis=0, keepdims=True)
            ik = jnp.min(jnp.where(cur == mx, eio, N_EXPERTS), axis=0, keepdims=True)
            vals.append(mx)
            idxs.append(ik)
            cur = jnp.where(eio == ik, -jnp.inf, cur)
        exps = [jnp.exp(v - vals[0]) for v in vals]
        den = exps[0] + exps[1] + exps[2] + exps[3]
        gate_ref[:, rows] = jnp.concatenate([e / den for e in exps], axis=0)
        idx_ref[:, rows] = jnp.concatenate(idxs, axis=0)

        hot = jnp.zeros((N_EXPERTS, sub), F32)
        for ik in idxs:
            hot = hot + (eio == ik).astype(F32)
        prior = jnp.dot(hot.astype(BF16), before, preferred_element_type=F32) + carry[:, 0:1]
        ranks = [jnp.sum(jnp.where(eio == ik, prior, 0.0), axis=0, keepdims=True) for ik in idxs]
        rank_ref[:, rows] = jnp.concatenate(ranks, axis=0).astype(I32)
        carry = carry + jnp.sum(hot, axis=1, keepdims=True)
    carry_ref[...] = carry
    cnt_ref[...] = carry


def _mix_route(x, ya, yb, carry_in, ln0_g, ln0_b, w_gate, b_gate, w_o, b_o, ln1_g, ln1_b, w_rt, b_rt):
    t = x.shape[0]
    tm = TM_MIX
    row = lambda d: pl.BlockSpec((tm, d), lambda i: (i, 0))
    col = lambda r: pl.BlockSpec((r, tm), lambda i: (0, i))
    return pl.pallas_call(
        _mix_kernel,
        grid=(t // tm,),
        in_specs=[
            row(D_MODEL), row(D_MODEL), row(D_MODEL), _full((N_EXPERTS, LANES)),
            _full((1, D_MODEL)), _full((1, D_MODEL)),
            _full((D_MODEL, 2 * D_MODEL)), _full((1, 2 * D_MODEL)),
            _full((D_MODEL, D_MODEL)), _full((1, D_MODEL)),
            _full((1, D_MODEL)), _full((1, D_MODEL)),
            _full((D_MODEL, 2 * LANES)), _full((N_EXPERTS, 1)),
        ],
        out_specs=[pl.BlockSpec((tm * SUBLANES, LANES), lambda i: (i, 0)), col(TOP_K), col(TOP_K), col(TOP_K),
                   _full((N_EXPERTS, LANES))],
        out_shape=[
            jax.ShapeDtypeStruct((t * SUBLANES, LANES), F32),
            jax.ShapeDtypeStruct((TOP_K, t), I32),
            jax.ShapeDtypeStruct((TOP_K, t), F32),
            jax.ShapeDtypeStruct((TOP_K, t), I32),
            jax.ShapeDtypeStruct((N_EXPERTS, LANES), F32),
        ],
        scratch_shapes=[pltpu.VMEM((N_EXPERTS, LANES), F32)],
        compiler_params=_cparams("arbitrary"),
        name="mix_route",
    )(x, ya, yb, carry_in, ln0_g, ln0_b, w_gate, b_gate, w_o, b_o, ln1_g, ln1_b, w_rt, b_rt)


def _row_copy(src_ref, src_row, dst_ref, dst_row, sem):
    return pltpu.make_async_copy(src_ref.at[src_row], dst_ref.at[dst_row], sem)


def _fetch_slots(slots_hbm, slots_smem, sem, step, slot):
    n = slots_hbm.shape[1]
    return pltpu.make_async_copy(slots_hbm.at[step], slots_smem.at[pl.ds(pl.multiple_of(slot * n, n), n)],
                                 sem.at[slot])


def _pad_fill(h_ref, xs_out, pad_smem, pad_sem, n_pad_runs, wait):
    def go(cp):
        if wait:
            cp.wait()
        else:
            cp.start()

    def run(j, carry):
        start = pad_smem[2 * j]
        length = pad_smem[2 * j + 1]
        for bit in range(BM_MOE.bit_length() - 1):
            size = 1 << bit

            @pl.when((length & size) != 0)
            def _():
                off = start + (length & ~(2 * size - 1))
                go(pltpu.make_async_copy(h_ref.at[pl.ds(0, size)], xs_out.at[pl.ds(off, size)], pad_sem))
        return carry

    lax.fori_loop(0, n_pad_runs, run, 0)

    piece = min(h_ref.shape[0], BM_MOE)

    def dead_block(j, carry):
        for part in range(BM_MOE // piece):
            off = j * BM_MOE + part * piece
            go(pltpu.make_async_copy(h_ref.at[pl.ds(0, piece)], xs_out.at[pl.ds(off, piece)], pad_sem))
        return carry

    lax.fori_loop(pad_smem[2 * n_pad_runs], xs_out.shape[0] // BM_MOE, dead_block, 0)


def _dispatch_kernel(*refs, tiles):
    n_sets = len(tiles)
    h_refs = refs[:n_sets]
    slots_hbm, pad_hbm, xs_out, slots_smem, pad_smem, slot_sem, pad_sem, row_sem = refs[n_sets:]
    i = pl.program_id(0)
    n = pl.num_programs(0)
    tm = h_refs[0].shape[0]
    cur = lax.rem(i, 2)

    @pl.when(i == 0)
    def _():
        _fetch_slots(slots_hbm, slots_smem, slot_sem, 0, 0).start()
        fetch = pltpu.make_async_copy(pad_hbm, pad_smem, pad_sem)
        fetch.start()
        fetch.wait()
        _pad_fill(h_refs[0], xs_out, pad_smem, pad_sem, N_EXPERTS, wait=False)
        _pad_fill(h_refs[0], xs_out, pad_smem, pad_sem, N_EXPERTS, wait=True)

    @pl.when(i + 1 < n)
    def _():
        _fetch_slots(slots_hbm, slots_smem, slot_sem, i + 1, 1 - cur).start()

    _fetch_slots(slots_hbm, slots_smem, slot_sem, i, cur).wait()
    base = cur * (TOP_K * tm)

    first = 0
    for h_ref, count in zip(h_refs, tiles):
        @pl.when(jnp.logical_and(i >= first, i < first + count))
        def _(h_ref=h_ref):
            def issue(r, carry):
                for k in range(TOP_K):
                    _row_copy(h_ref, r, xs_out, slots_smem[base + TOP_K * r + k], row_sem).start(priority=k % 2)
                return carry

            lax.fori_loop(0, tm, issue, 0, unroll=ISSUE_UNROLL)
            for _ in range(TOP_K):
                pltpu.make_async_copy(h_ref, xs_out.at[pl.ds(0, tm)], row_sem).wait()
        first += count


def _dispatch(h1s, slots, pad_runs, n_rows):
    tm = TM_ROWS
    h1s = [h.reshape(-1, SUBLANES, LANES) for h in h1s]
    tiles = [h.shape[0] // tm for h in h1s]
    firsts = [sum(tiles[:j]) for j in range(len(tiles))]

    def tile_of(first, count):
        return lambda i: (jnp.clip(i - first, 0, count - 1), 0, 0)

    return pl.pallas_call(
        functools.partial(_dispatch_kernel, tiles=tuple(tiles)),
        grid=(sum(tiles),),
        in_specs=[pl.BlockSpec((tm, SUBLANES, LANES), tile_of(f, c)) for f, c in zip(firsts, tiles)]
        + [pl.BlockSpec(memory_space=pl.ANY)] * 2,
        out_specs=pl.BlockSpec(memory_space=pl.ANY),
        out_shape=jax.ShapeDtypeStruct((n_rows, SUBLANES, LANES), F32),
        scratch_shapes=[
            pltpu.SMEM((2 * TOP_K * tm,), I32),
            pltpu.SMEM((pad_runs.shape[0],), I32),
            pltpu.SemaphoreType.DMA((2,)),
            pltpu.SemaphoreType.DMA,
            pltpu.SemaphoreType.DMA,
        ],
        compiler_params=_cparams("arbitrary"),
        name="dispatch",
    )(*h1s, slots, pad_runs)


def _moe_kernel(blk_e_ref, nb_ref, x_ref, wgu_ref, bgu_ref, wd_ref, bd_ref, o_ref, wgu_b, wd_b):
    i = pl.program_id(0)
    is_live = i < nb_ref[0]

    @pl.when(jnp.logical_not(is_live))
    def _():
        o_ref[...] = jnp.zeros(o_ref.shape, F32)

    @pl.when(jnp.logical_or(i == 0, blk_e_ref[i] != blk_e_ref[jnp.maximum(i - 1, 0)]))
    def _():
        for r in range(0, D_MODEL, CAST_ROWS):
            wgu_b[r:r + CAST_ROWS, :] = wgu_ref[0, r:r + CAST_ROWS, :].astype(BF16)
        for r in range(0, D_FF, CAST_ROWS):
            wd_b[r:r + CAST_ROWS, :] = wd_ref[0, r:r + CAST_ROWS, :].astype(BF16)

    @pl.when(is_live)
    def _():
        x = _from_row_tiles(x_ref, 0, BM_MOE).astype(BF16)
        acc = jnp.zeros((BM_MOE, D_MODEL), F32) + bd_ref[0]
        for c in range(D_FF // FF_CHUNK):
            lo = c * FF_CHUNK
            g = jnp.dot(x, wgu_b[:, lo:lo + FF_CHUNK], preferred_element_type=F32) + bgu_ref[0, :, lo:lo + FF_CHUNK]
            u = (jnp.dot(x, wgu_b[:, D_FF + lo:D_FF + lo + FF_CHUNK], preferred_element_type=F32)
                 + bgu_ref[0, :, D_FF + lo:D_FF + lo + FF_CHUNK])
            g = jnp.minimum(g, SWIGLU_LIMIT)
            u = jnp.clip(u, -SWIGLU_LIMIT, SWIGLU_LIMIT)
            act = (u + 1.0) * (g * _sigmoid(g * SWIGLU_ALPHA))
            acc = acc + jnp.dot(act.astype(BF16), wd_b[lo:lo + FF_CHUNK, :], preferred_element_type=F32)
        _to_row_tiles(o_ref, 0, acc)


def _moe_blocks(xs, blk_e, nb_used, w_gu, b_gu, w_down, b_down):
    p = xs.shape[0] // SUBLANES
    n_blocks = p // BM_MOE
    live = lambda i, nb: jnp.minimum(i, nb[0] - 1)
    grid_spec = pltpu.PrefetchScalarGridSpec(
        num_scalar_prefetch=2,
        grid=(n_blocks,),
        in_specs=[
            pl.BlockSpec((BM_MOE * SUBLANES, LANES), lambda i, be, nb: (live(i, nb), 0)),
            pl.BlockSpec((1, D_MODEL, 2 * D_FF), lambda i, be, nb: (be[i], 0, 0)),
            pl.BlockSpec((1, 1, 2 * D_FF), lambda i, be, nb: (be[i], 0, 0)),
            pl.BlockSpec((1, D_FF, D_MODEL), lambda i, be, nb: (be[i], 0, 0)),
            pl.BlockSpec((1, 1, D_MODEL), lambda i, be, nb: (be[i], 0, 0)),
        ],
        out_specs=pl.BlockSpec((BM_MOE * SUBLANES, LANES), lambda i, be, nb: (i, 0)),
        scratch_shapes=[pltpu.VMEM((D_MODEL, 2 * D_FF), BF16), pltpu.VMEM((D_FF, D_MODEL), BF16)],
    )
    return pl.pallas_call(
        _moe_kernel,
        grid_spec=grid_spec,
        out_shape=jax.ShapeDtypeStruct((p * SUBLANES, LANES), F32),
        compiler_params=pltpu.CompilerParams(dimension_semantics=("arbitrary",), vmem_limit_bytes=VMEM_LIMIT_MOE),
        name="moe_experts",
    )(blk_e, nb_used, xs, w_gu, b_gu, w_down, b_down)


def _final_kernel(h_ref, p_ref, gate_ref, slots_hbm, y_hbm, wpg_ref, bpg_ref, wpp_ref, g_ref, b_ref, o_ref,
                  slots_smem, rows_ref, slot_sem, row_sem):
    i = pl.program_id(0)
    n = pl.num_programs(0)
    tm = h_ref.shape[0] // SUBLANES
    cur = lax.rem(i, 2)
    nxt = 1 - cur

    def gather(r, buf, base):
        for k in range(TOP_K):
            slot = slots_smem[base + TOP_K * r + k]
            src = y_hbm.at[pl.ds(pl.multiple_of(slot * SUBLANES, SUBLANES), SUBLANES)]
            dst = rows_ref.at[buf, k, pl.ds(pl.multiple_of(r * SUBLANES, SUBLANES), SUBLANES)]
            pltpu.make_async_copy(src, dst, row_sem.at[buf]).start(priority=k % 2)

    def wait_rows(buf):
        for k in range(TOP_K):
            pltpu.make_async_copy(y_hbm.at[pl.ds(0, tm * SUBLANES)], rows_ref.at[buf, k], row_sem.at[buf]).wait()

    @pl.when(i == 0)
    def _():
        first = _fetch_slots(slots_hbm, slots_smem, slot_sem, 0, 0)
        first.start()
        first.wait()

        def issue(r, carry):
            gather(r, 0, 0)
            return carry

        lax.fori_loop(0, tm, issue, 0, unroll=ISSUE_UNROLL)
        _fetch_slots(slots_hbm, slots_smem, slot_sem, 1, 1).start()

    _fetch_slots(slots_hbm, slots_smem, slot_sem, i + 1, nxt).wait()

    @pl.when(i + 2 <= n)
    def _():
        _fetch_slots(slots_hbm, slots_smem, slot_sem, i + 2, cur).start()

    base = nxt * (TOP_K * tm)
    for r in range(tm):
        gather(r, nxt, base)

    h = _from_row_tiles(h_ref, 0, tm)
    ple = (_sigmoid(jnp.dot(h.astype(BF16), wpg_ref[...], preferred_element_type=F32) + bpg_ref[...])
           * jnp.dot(p_ref[...].astype(BF16), wpp_ref[...], preferred_element_type=F32))

    wait_rows(cur)
    gates = gate_ref[...]
    ffn = gates[:, 0:1] * _from_row_tiles(rows_ref.at[cur, 0], 0, tm)
    for k in range(1, TOP_K):
        ffn = ffn + gates[:, k:k + 1] * _from_row_tiles(rows_ref.at[cur, k], 0, tm)
    o_ref[...] = _layer_norm(DEEPNORM_ALPHA * h + (ffn + ple), g_ref[...], b_ref[...])

    @pl.when(i == n - 1)
    def _():
        wait_rows(nxt)


def _combine_final(h1, p, gates, slots, y, w_pg, b_pg, w_pp, ln_g, ln_b):
    t = h1.shape[0] // SUBLANES
    tm = TM_COMBINE
    slots = jnp.concatenate([slots, slots[-1:]], axis=0)
    return pl.pallas_call(
        _final_kernel,
        grid=(t // tm,),
        in_specs=[
            pl.BlockSpec((tm * SUBLANES, LANES), lambda i: (i, 0)),
            pl.BlockSpec((tm, D_PLE), lambda i: (i, 0)),
            pl.BlockSpec((tm, TOP_K), lambda i: (i, 0)),
            pl.BlockSpec(memory_space=pl.ANY),
            pl.BlockSpec(memory_space=pl.ANY),
            _full((D_MODEL, D_MODEL)), _full((1, D_MODEL)), _full((D_PLE, D_MODEL)),
            _full((1, D_MODEL)), _full((1, D_MODEL)),
        ],
        out_specs=pl.BlockSpec((tm, D_MODEL), lambda i: (i, 0)),
        out_shape=jax.ShapeDtypeStruct((t, D_MODEL), F32),
        scratch_shapes=[
            pltpu.SMEM((2 * TOP_K * tm,), I32),
            pltpu.VMEM((2, TOP_K, tm * SUBLANES, LANES), F32),
            pltpu.SemaphoreType.DMA((2,)),
            pltpu.SemaphoreType.DMA((2,)),
        ],
        compiler_params=_cparams("arbitrary"),
        name="combine_final",
    )(h1, p, gates, slots, y, w_pg, b_pg, w_pp, ln_g, ln_b)


def _tile_slots(dest, tm):
    t = dest.shape[1]
    return dest.T.reshape(t // tm, tm * TOP_K)


def kernel(x_prompt, x_sample, p_prompt, p_sample, ln0_g, ln0_b, w_in, b_in, conv_w, conv_b, lnc_g, lnc_b, w_conv_out, w_fnet_out, w_o, b_o, ln1_g, ln1_b, w_router, b_router, w_gu, b_gu, w_down, b_down, w_pg, b_pg, w_pp, ln2_g, ln2_b):
    row = lambda v: v.reshape(1, -1).astype(F32)
    n_branch = 2 * D_CONV + D_FNET
    w_in_b = w_in[0].astype(BF16)
    w_branch, w_gate = w_in_b[:, :n_branch], w_in_b[:, n_branch:]
    b_branch, b_gate = row(b_in[0, :n_branch]), row(b_in[0, n_branch:])
    w_conv_out_b = w_conv_out[0].astype(BF16)
    w_fnet_out_b = w_fnet_out[0].astype(BF16)
    w_o_b = w_o[0].astype(BF16)
    w_rt_f = jnp.pad(w_router[0].astype(F32), ((0, 0), (0, LANES - N_EXPERTS)))
    w_rt_hi = w_rt_f.astype(BF16)
    w_rt = jnp.concatenate([w_rt_hi, (w_rt_f - w_rt_hi.astype(F32)).astype(BF16)], axis=1)
    b_rt = b_router[0].reshape(N_EXPERTS, 1).astype(F32)
    b_gu_r = b_gu[0].reshape(N_EXPERTS, 1, 2 * D_FF).astype(F32)
    b_down_r = b_down[0].reshape(N_EXPERTS, 1, D_MODEL).astype(F32)
    w_pg_b = w_pg[0].astype(BF16)
    w_pp_b = w_pp[0].astype(BF16)

    carry = jnp.zeros((N_EXPERTS, LANES), F32)
    routed = []
    for x, p in ((x_prompt, p_prompt[0]), (x_sample, p_sample[0])):
        bsz, s, _ = x.shape
        xt = x.reshape(bsz * s, D_MODEL)
        u, f_in = _inproj(xt, row(ln0_g), row(ln0_b), w_branch, b_branch)
        ya = _conv_branch(u.reshape(bsz, s, D_CONV), conv_w[0].astype(F32), row(conv_b[0]), row(lnc_g[0]),
                          row(lnc_b[0]), w_conv_out_b).reshape(bsz * s, D_MODEL)
        yb = _fourier_branch(f_in.reshape(bsz, s, D_FNET), w_fnet_out_b)
        h1, idx, gates, rank, carry = _mix_route(
            xt, ya, yb, carry, row(ln0_g), row(ln0_b), w_gate, b_gate, w_o_b, row(b_o[0]),
            row(ln1_g[0]), row(ln1_b[0]), w_rt, b_rt)
        routed.append((h1, idx, gates, rank, p.reshape(bsz * s, D_PLE), (bsz, s)))

    n_assign = sum(r[1].shape[1] for r in routed) * TOP_K
    n_blocks = (n_assign + N_EXPERTS * (BM_MOE - 1)) // BM_MOE
    counts = carry[:, 0].astype(I32)
    padded = ((counts + BM_MOE - 1) // BM_MOE) * BM_MOE
    ends = jnp.cumsum(padded)
    start = ends - padded
    nb_used = (ends[-1] // BM_MOE).reshape(1).astype(I32)
    blk = jnp.minimum(jnp.arange(n_blocks, dtype=I32), nb_used[0] - 1)
    blk_e = jnp.minimum(jnp.sum((ends[None, :] <= (blk * BM_MOE)[:, None]).astype(I32), axis=1), N_EXPERTS - 1)
    pad_runs = jnp.concatenate([jnp.stack([start + counts, padded - counts], axis=1).reshape(-1), nb_used,
                                jnp.zeros((LANES - 2 * N_EXPERTS - 1,), I32)]).astype(I32)

    slot_tables = []
    dispatch_slots = []
    for _, idx, _, rank, _, _ in routed:
        first = jnp.sum(jnp.where(idx[None] == jnp.arange(N_EXPERTS, dtype=I32)[:, None, None],
                                  start[:, None, None], 0), axis=0)
        slot_tables.append(_tile_slots(first + rank, TM_COMBINE))
        dispatch_slots.append(_tile_slots(first + rank, TM_ROWS))
    xs = _dispatch([r[0] for r in routed], jnp.concatenate(dispatch_slots, axis=0), pad_runs, n_blocks * BM_MOE)

    y = _moe_blocks(xs.reshape(-1, LANES), blk_e, nb_used, w_gu.reshape(N_EXPERTS, D_MODEL, 2 * D_FF), b_gu_r,
                    w_down.reshape(N_EXPERTS, D_FF, D_MODEL), b_down_r)

    outs = []
    for (h1, _, gates, _, p, (bsz, s)), slots in zip(routed, slot_tables):
        o = _combine_final(h1, p, gates.T, slots, y, w_pg_b, row(b_pg[0]), w_pp_b, row(ln2_g[0]), row(ln2_b[0]))
        outs.append(o.reshape(bsz, s, D_MODEL))
    return tuple(outs)
```

```python
import functools

import jax
import jax.numpy as jnp
from jax import lax
from jax.experimental import pallas as pl
from jax.experimental.pallas import tpu as pltpu

F32 = jnp.float32
BF16 = jnp.bfloat16
I32 = jnp.int32

D_MODEL = 1024
D_CONV = D_MODEL // 2
CONV_WIDTH = 31
CONV_PAD = CONV_WIDTH // 2
D_FNET = D_MODEL // 2
FNET_GROUPS = 4
FNET_GROUP_DIM = D_FNET // FNET_GROUPS
D_PLE = 256
N_EXPERTS = 32
TOP_K = 4
D_FF = D_MODEL
SWIGLU_ALPHA = 1.702
SWIGLU_LIMIT = 7.0
LN_EPS = 1e-5
DEPTH = 1
DEEPNORM_ALPHA = (2 * DEPTH) ** 0.25

LANES = 128
SUBLANES = 8
DFT_N1 = 128
HALO = 16

TM_INPROJ = 512
TS_CONV = 512
RC_CONV = 128
TM_MIX = 512
SUB_MIX = 512
FFT2_TILES = 4
TM_ROWS = 1024
TM_COMBINE = 256
BM_MOE = 512
FF_CHUNK = 1024
ISSUE_UNROLL = 8
CAST_ROWS = 128
VMEM_LIMIT = 48 * 1024 * 1024
VMEM_LIMIT_MOE = 56 * 1024 * 1024


def _cparams(*sem):
    return pltpu.CompilerParams(dimension_semantics=sem, vmem_limit_bytes=VMEM_LIMIT)


def _layer_norm(x, g, b):
    mu = jnp.mean(x, axis=-1, keepdims=True)
    xc = x - mu
    var = jnp.mean(xc * xc, axis=-1, keepdims=True)
    return xc * lax.rsqrt(var + LN_EPS) * g + b


def _sigmoid(x):
    return 0.5 * jnp.tanh(0.5 * x) + 0.5


def _full(shape):
    return pl.BlockSpec(shape, lambda *_: (0,) * len(shape))


def _to_row_tiles(ref, row0, x):
    rows = x.shape[0]
    for j in range(D_MODEL // LANES):
        ref[pl.ds(row0 * SUBLANES + j, rows, stride=SUBLANES), :] = x[:, j * LANES:(j + 1) * LANES]


def _from_row_tiles(ref, row0, rows):
    return jnp.concatenate([ref[pl.ds(row0 * SUBLANES + j, rows, stride=SUBLANES), :]
                            for j in range(D_MODEL // LANES)], axis=1)


def _inproj_kernel(x_ref, g_ref, b_ref, w_ref, bias_ref, u_ref, f_ref):
    h = _layer_norm(x_ref[...], g_ref[...], b_ref[...])
    z = jnp.dot(h.astype(BF16), w_ref[...], preferred_element_type=F32) + bias_ref[...]
    u_ref[...] = (z[:, :D_CONV] * _sigmoid(z[:, D_CONV:2 * D_CONV])).astype(BF16)
    f_ref[...] = z[:, 2 * D_CONV:]


def _inproj(x, ln_g, ln_b, w, bias):
    t = x.shape[0]
    n_out = w.shape[1]
    return pl.pallas_call(
        _inproj_kernel,
        grid=(t // TM_INPROJ,),
        in_specs=[
            pl.BlockSpec((TM_INPROJ, D_MODEL), lambda i: (i, 0)),
            _full((1, D_MODEL)), _full((1, D_MODEL)),
            _full((D_MODEL, n_out)), _full((1, n_out)),
        ],
        out_specs=[
            pl.BlockSpec((TM_INPROJ, D_CONV), lambda i: (i, 0)),
            pl.BlockSpec((TM_INPROJ, D_FNET), lambda i: (i, 0)),
        ],
        out_shape=[jax.ShapeDtypeStruct((t, D_CONV), BF16), jax.ShapeDtypeStruct((t, D_FNET), F32)],
        compiler_params=_cparams("parallel"),
        name="inproj",
    )(x, ln_g, ln_b, w, bias)


def _conv_kernel(prev_ref, cur_ref, next_ref, cw_ref, cb_ref, g_ref, b_ref, wout_ref, ya_ref,
                 ext_ref, act_ref):
    i = pl.program_id(1)
    last = pl.num_programs(1) - 1
    ts = cur_ref.shape[1]
    ext_rows = ts + 2 * HALO
    zero = jnp.zeros((HALO, D_CONV), F32)
    ext_ref[0, 0:HALO, :] = jnp.where(i > 0, prev_ref[0].astype(F32), zero)
    ext_ref[0, HALO:HALO + ts, :] = cur_ref[0].astype(F32)
    ext_ref[0, HALO + ts:ext_rows, :] = jnp.where(i < last, next_ref[0].astype(F32), zero)
    for r in range(1, SUBLANES):
        ext_ref[r, 0:ext_rows - SUBLANES, :] = ext_ref[0, r:r + ext_rows - SUBLANES, :]

    cb = cb_ref[...]
    g = g_ref[...]
    b = b_ref[...]

    def chunk(c, carry):
        r0 = pl.multiple_of(c * RC_CONV, RC_CONV)
        acc = jnp.zeros((RC_CONV, D_CONV), F32) + cb
        for j in range(CONV_WIDTH):
            off = j + HALO - CONV_PAD
            q, r = divmod(off, SUBLANES)
            acc = acc + cw_ref[j:j + 1, :] * ext_ref[r, pl.ds(r0 + q * SUBLANES, RC_CONV), :]
        y = _layer_norm(acc, g, b)
        act_ref[pl.ds(r0, RC_CONV), :] = (y * _sigmoid(y)).astype(BF16)
        return carry

    lax.fori_loop(0, ts // RC_CONV, chunk, 0)
    ya_ref[0] = jnp.dot(act_ref[...], wout_ref[...], preferred_element_type=F32).astype(BF16)


def _conv_branch(u, conv_w, conv_b, ln_g, ln_b, w_out):
    bsz, s, _ = u.shape
    ts = TS_CONV
    nb = s // ts
    hb = ts // HALO
    return pl.pallas_call(
        _conv_kernel,
        grid=(bsz, nb),
        in_specs=[
            pl.BlockSpec((1, HALO, D_CONV), lambda b, i: (b, jnp.maximum(i * hb - 1, 0), 0)),
            pl.BlockSpec((1, ts, D_CONV), lambda b, i: (b, i, 0)),
            pl.BlockSpec((1, HALO, D_CONV), lambda b, i: (b, jnp.minimum((i + 1) * hb, s // HALO - 1), 0)),
            _full((CONV_WIDTH, D_CONV)), _full((1, D_CONV)), _full((1, D_CONV)), _full((1, D_CONV)),
            _full((D_CONV, D_MODEL)),
        ],
        out_specs=pl.BlockSpec((1, ts, D_MODEL), lambda b, i: (b, i, 0)),
        out_shape=jax.ShapeDtypeStruct((bsz, s, D_MODEL), BF16),
        scratch_shapes=[
            pltpu.VMEM((SUBLANES, ts + 2 * HALO, D_CONV), F32),
            pltpu.VMEM((ts, D_CONV), BF16),
        ],
        compiler_params=_cparams("parallel", "parallel"),
        name="conv_branch",
    )(u, u, u, conv_w, conv_b, ln_g, ln_b, w_out)


def _dft_tables(s):
    n2_len = s // DFT_N1
    k1_per_tile = DFT_N1 // n2_len
    two_pi = 2.0 * jnp.pi
    a = jnp.arange(DFT_N1, dtype=I32)
    ang1 = two_pi * ((a[:, None] * a[None, :]) % DFT_N1).astype(F32) / DFT_N1
    small = jnp.concatenate([jnp.cos(ang1), -jnp.sin(ang1)], axis=0)
    r = jnp.arange(2 * DFT_N1 * SUBLANES, dtype=I32)[:, None]
    c = jnp.arange(DFT_N1 * SUBLANES, dtype=I32)[None, :]
    pick_r = (r // SUBLANES == jnp.arange(2 * DFT_N1, dtype=I32)[None, :]).astype(F32)
    pick_c = (jnp.arange(DFT_N1, dtype=I32)[:, None] == c // SUBLANES).astype(F32)
    rep = jnp.dot(jnp.dot(pick_r, small, precision=lax.Precision.HIGHEST), pick_c, precision=lax.Precision.HIGHEST)
    f1 = jnp.where(r % SUBLANES == c % SUBLANES, rep, 0.0).astype(BF16)
    cs = jnp.stack([jnp.cos(ang1), jnp.sin(ang1)]).astype(BF16)
    n_tiles = s // DFT_N1
    t = jnp.arange(n_tiles, dtype=I32)[:, None, None]
    row = jnp.arange(DFT_N1, dtype=I32)[None, :, None]
    col = jnp.arange(DFT_N1, dtype=I32)[None, None, :]
    k2, k1l_out = row // k1_per_tile, row % k1_per_tile
    k1l_in, n2 = col // n2_len, col % n2_len
    k = t * k1_per_tile + k1l_out + DFT_N1 * k2
    ang2 = two_pi * ((n2 * k) % s).astype(F32) / s
    hit = k1l_in == k1l_out
    gr = jnp.where(hit, jnp.cos(ang2), 0.0)
    gi = jnp.where(hit, -jnp.sin(ang2), 0.0)
    g = jnp.concatenate([jnp.concatenate([gr, -gi], axis=2),
                         jnp.concatenate([gi, gr], axis=2)], axis=1).astype(BF16)
    return f1, g, cs


def _fft1_kernel(x_ref, f_ref, y_ref):
    nb = x_ref.shape[2]
    x = x_ref[0].reshape(DFT_N1 * nb, D_FNET).astype(BF16)
    r = jnp.dot(f_ref[...], x, preferred_element_type=F32)
    y_ref[0] = r.reshape(2, DFT_N1, nb, D_FNET)


def _fft2_kernel(y_ref, g_ref, cs_ref, w_ref, o_ref, *, scale, tiles):
    for t in range(tiles):
        rows = slice(t * DFT_N1, (t + 1) * DFT_N1)
        yb = jnp.concatenate([y_ref[0, 0, rows, :], y_ref[0, 1, rows, :]], axis=0).astype(BF16)
        z = jnp.dot(g_ref[t], yb, preferred_element_type=F32)
        zr = z[:DFT_N1].astype(BF16)
        zi = z[DFT_N1:].astype(BF16)
        parts = []
        for grp in range(FNET_GROUPS):
            sl = slice(grp * FNET_GROUP_DIM, (grp + 1) * FNET_GROUP_DIM)
            parts.append(jnp.dot(zr[:, sl], cs_ref[0], preferred_element_type=F32)
                         + jnp.dot(zi[:, sl], cs_ref[1], preferred_element_type=F32))
        fm = (jnp.concatenate(parts, axis=1) * scale).astype(BF16)
        out = jnp.dot(fm, w_ref[...], preferred_element_type=F32)
        k1_per_tile = o_ref.shape[2] // tiles
        if k1_per_tile == 1:
            o_ref[0, :, t, :] = out
        else:
            o_ref[0, :, t * k1_per_tile:(t + 1) * k1_per_tile, :] = out.reshape(o_ref.shape[1], k1_per_tile, D_MODEL)


def _fourier_branch(f_in, w_out):
    bsz, s, _ = f_in.shape
    n2_len = s // DFT_N1
    k1_per_tile = DFT_N1 // n2_len
    f1, g, cs = _dft_tables(s)
    y = pl.pallas_call(
        _fft1_kernel,
        grid=(bsz, n2_len // SUBLANES),
        in_specs=[pl.BlockSpec((1, DFT_N1, SUBLANES, D_FNET), lambda b, j: (b, 0, j, 0)),
                  _full((2 * DFT_N1 * SUBLANES, DFT_N1 * SUBLANES))],
        out_specs=pl.BlockSpec((1, 2, DFT_N1, SUBLANES, D_FNET), lambda b, j: (b, 0, 0, j, 0)),
        out_shape=jax.ShapeDtypeStruct((bsz, 2, DFT_N1, n2_len, D_FNET), F32),
        compiler_params=_cparams("parallel", "parallel"),
        name="fft_stage1",
    )(f_in.reshape(bsz, DFT_N1, n2_len, D_FNET), f1)
    y = y.reshape(bsz, 2, s, D_FNET)
    scale = float((s * FNET_GROUP_DIM) ** -0.5)
    assert n2_len % SUBLANES == 0 and (k1_per_tile == 1 or k1_per_tile % SUBLANES == 0)
    tiles = SUBLANES if k1_per_tile == 1 else min(FFT2_TILES, n2_len)
    k1_per_step = tiles * k1_per_tile
    yb = pl.pallas_call(
        functools.partial(_fft2_kernel, scale=scale, tiles=tiles),
        grid=(bsz, s // (DFT_N1 * tiles)),
        in_specs=[
            pl.BlockSpec((1, 2, DFT_N1 * tiles, D_FNET), lambda b, t: (b, 0, t, 0)),
            pl.BlockSpec((tiles, 2 * DFT_N1, 2 * DFT_N1), lambda b, t: (t, 0, 0)),
            _full((2, DFT_N1, DFT_N1)),
            _full((D_FNET, D_MODEL)),
        ],
        out_specs=pl.BlockSpec((1, n2_len, k1_per_step, D_MODEL), lambda b, t: (b, 0, t, 0)),
        out_shape=jax.ShapeDtypeStruct((bsz, n2_len, DFT_N1, D_MODEL), F32),
        compiler_params=_cparams("parallel", "parallel"),
        name="fft_stage2",
    )(y, g, cs, w_out)
    return yb.reshape(bsz * s, D_MODEL)


def _mix_kernel(x_ref, ya_ref, yb_ref, cin_ref, ln0g_ref, ln0b_ref, wg_ref, bg_ref, wo_ref, bo_ref,
                ln1g_ref, ln1b_ref, wr_ref, br_ref,
                h1_ref, idx_ref, gate_ref, rank_ref, cnt_ref, carry_ref):
    tm = x_ref.shape[0]
    sub = min(SUB_MIX, tm)

    @pl.when(pl.program_id(0) == 0)
    def _():
        carry_ref[...] = cin_ref[...]

    carry = carry_ref[...]
    eio = lax.broadcasted_iota(I32, (N_EXPERTS, sub), 0)
    before = (lax.broadcasted_iota(I32, (sub, sub), 0) < lax.broadcasted_iota(I32, (sub, sub), 1)).astype(BF16)
    for r0 in range(0, tm, sub):
        rows = slice(r0, r0 + sub)
        h = _layer_norm(x_ref[rows, :], ln0g_ref[...], ln0b_ref[...])
        zg = jnp.dot(h.astype(BF16), wg_ref[...], preferred_element_type=F32) + bg_ref[...]
        m = (_sigmoid(zg[:, :D_MODEL]) * ya_ref[rows, :].astype(F32)
             + _sigmoid(zg[:, D_MODEL:]) * yb_ref[rows, :].astype(F32))
        mix = jnp.dot(m.astype(BF16), wo_ref[...], preferred_element_type=F32) + bo_ref[...]
        h1 = _layer_norm(DEEPNORM_ALPHA * h + mix, ln1g_ref[...], ln1b_ref[...])
        _to_row_tiles(h1_ref, r0, h1)

        h_hi = h1.astype(BF16)
        h_lo = (h1 - h_hi.astype(F32)).astype(BF16)
        hw = jnp.dot(h_hi, wr_ref[...], preferred_element_type=F32)
        lw = jnp.dot(h_lo, wr_ref[:, :LANES], preferred_element_type=F32)
        logits = (hw[:, :LANES] + hw[:, LANES:] + lw).T[:N_EXPERTS] + br_ref[...]
        vals, idxs = [], []
        cur = logits
        for _ in range(TOP_K):
            mx = jnp.max(cur, axis=0, keepdims=True)
            ik = jnp.min(jnp.where(cur == mx, eio, N_EXPERTS), axis=0, keepdims=True)
            vals.append(mx)
            idxs.append(ik)
            cur = jnp.where(eio == ik, -jnp.inf, cur)
        exps = [jnp.exp(v - vals[0]) for v in vals]
        den = exps[0] + exps[1] + exps[2] + exps[3]
        gate_ref[:, rows] = jnp.concatenate([e / den for e in exps], axis=0)
        idx_ref[:, rows] = jnp.concatenate(idxs, axis=0)

        hot = jnp.zeros((N_EXPERTS, sub), F32)
        for ik in idxs:
            hot = hot + (eio == ik).astype(F32)
        prior = jnp.dot(hot.astype(BF16), before, preferred_element_type=F32) + carry[:, 0:1]
        ranks = [jnp.sum(jnp.where(eio == ik, prior, 0.0), axis=0, keepdims=True) for ik in idxs]
        rank_ref[:, rows] = jnp.concatenate(ranks, axis=0).astype(I32)
        carry = carry + jnp.sum(hot, axis=1, keepdims=True)
    carry_ref[...] = carry
    cnt_ref[...] = carry


def _mix_route(x, ya, yb, carry_in, ln0_g, ln0_b, w_gate, b_gate, w_o, b_o, ln1_g, ln1_b, w_rt, b_rt):
    t = x.shape[0]
    tm = TM_MIX
    row = lambda d: pl.BlockSpec((tm, d), lambda i: (i, 0))
    col = lambda r: pl.BlockSpec((r, tm), lambda i: (0, i))
    return pl.pallas_call(
        _mix_kernel,
        grid=(t // tm,),
        in_specs=[
            row(D_MODEL), row(D_MODEL), row(D_MODEL), _full((N_EXPERTS, LANES)),
            _full((1, D_MODEL)), _full((1, D_MODEL)),
            _full((D_MODEL, 2 * D_MODEL)), _full((1, 2 * D_MODEL)),
            _full((D_MODEL, D_MODEL)), _full((1, D_MODEL)),
            _full((1, D_MODEL)), _full((1, D_MODEL)),
            _full((D_MODEL, 2 * LANES)), _full((N_EXPERTS, 1)),
        ],
        out_specs=[pl.BlockSpec((tm * SUBLANES, LANES), lambda i: (i, 0)), col(TOP_K), col(TOP_K), col(TOP_K),
                   _full((N_EXPERTS, LANES))],
        out_shape=[
            jax.ShapeDtypeStruct((t * SUBLANES, LANES), F32),
            jax.ShapeDtypeStruct((TOP_K, t), I32),
            jax.ShapeDtypeStruct((TOP_K, t), F32),
            jax.ShapeDtypeStruct((TOP_K, t), I32),
            jax.ShapeDtypeStruct((N_EXPERTS, LANES), F32),
        ],
        scratch_shapes=[pltpu.VMEM((N_EXPERTS, LANES), F32)],
        compiler_params=_cparams("arbitrary"),
        name="mix_route",
    )(x, ya, yb, carry_in, ln0_g, ln0_b, w_gate, b_gate, w_o, b_o, ln1_g, ln1_b, w_rt, b_rt)


def _row_copy(src_ref, src_row, dst_ref, dst_row, sem):
    return pltpu.make_async_copy(src_ref.at[src_row], dst_ref.at[dst_row], sem)


def _fetch_slots(slots_hbm, slots_smem, sem, step, slot):
    n = slots_hbm.shape[1]
    return pltpu.make_async_copy(slots_hbm.at[step], slots_smem.at[pl.ds(pl.multiple_of(slot * n, n), n)],
                                 sem.at[slot])


def _pad_fill(h_ref, xs_out, pad_smem, pad_sem, n_pad_runs, wait):
    def go(cp):
        if wait:
            cp.wait()
        else:
            cp.start()

    def run(j, carry):
        start = pad_smem[2 * j]
        length = pad_smem[2 * j + 1]
        for bit in range(BM_MOE.bit_length() - 1):
            size = 1 << bit

            @pl.when((length & size) != 0)
            def _():
                off = start + (length & ~(2 * size - 1))
                go(pltpu.make_async_copy(h_ref.at[pl.ds(0, size)], xs_out.at[pl.ds(off, size)], pad_sem))
        return carry

    lax.fori_loop(0, n_pad_runs, run, 0)

    piece = min(h_ref.shape[0], BM_MOE)

    def dead_block(j, carry):
        for part in range(BM_MOE // piece):
            off = j * BM_MOE + part * piece
            go(pltpu.make_async_copy(h_ref.at[pl.ds(0, piece)], xs_out.at[pl.ds(off, piece)], pad_sem))
        return carry

    lax.fori_loop(pad_smem[2 * n_pad_runs], xs_out.shape[0] // BM_MOE, dead_block, 0)


def _dispatch_kernel(*refs, tiles):
    n_sets = len(tiles)
    h_refs = refs[:n_sets]
    slots_hbm, pad_hbm, xs_out, slots_smem, pad_smem, slot_sem, pad_sem, row_sem = refs[n_sets:]
    i = pl.program_id(0)
    n = pl.num_programs(0)
    tm = h_refs[0].shape[0]
    cur = lax.rem(i, 2)

    @pl.when(i == 0)
    def _():
        _fetch_slots(slots_hbm, slots_smem, slot_sem, 0, 0).start()
        fetch = pltpu.make_async_copy(pad_hbm, pad_smem, pad_sem)
        fetch.start()
        fetch.wait()
        _pad_fill(h_refs[0], xs_out, pad_smem, pad_sem, N_EXPERTS, wait=False)
        _pad_fill(h_refs[0], xs_out, pad_smem, pad_sem, N_EXPERTS, wait=True)

    @pl.when(i + 1 < n)
    def _():
        _fetch_slots(slots_hbm, slots_smem, slot_sem, i + 1, 1 - cur).start()

    _fetch_slots(slots_hbm, slots_smem, slot_sem, i, cur).wait()
    base = cur * (TOP_K * tm)

    first = 0
    for h_ref, count in zip(h_refs, tiles):
        @pl.when(jnp.logical_and(i >= first, i < first + count))
        def _(h_ref=h_ref):
            def issue(r, carry):
                for k in range(TOP_K):
                    _row_copy(h_ref, r, xs_out, slots_smem[base + TOP_K * r + k], row_sem).start(priority=k % 2)
                return carry

            lax.fori_loop(0, tm, issue, 0, unroll=ISSUE_UNROLL)
            for _ in range(TOP_K):
                pltpu.make_async_copy(h_ref, xs_out.at[pl.ds(0, tm)], row_sem).wait()
        first += count


def _dispatch(h1s, slots, pad_runs, n_rows):
    tm = TM_ROWS
    h1s = [h.reshape(-1, SUBLANES, LANES) for h in h1s]
    tiles = [h.shape[0] // tm for h in h1s]
    firsts = [sum(tiles[:j]) for j in range(len(tiles))]

    def tile_of(first, count):
        return lambda i: (jnp.clip(i - first, 0, count - 1), 0, 0)

    return pl.pallas_call(
        functools.partial(_dispatch_kernel, tiles=tuple(tiles)),
        grid=(sum(tiles),),
        in_specs=[pl.BlockSpec((tm, SUBLANES, LANES), tile_of(f, c)) for f, c in zip(firsts, tiles)]
        + [pl.BlockSpec(memory_space=pl.ANY)] * 2,
        out_specs=pl.BlockSpec(memory_space=pl.ANY),
        out_shape=jax.ShapeDtypeStruct((n_rows, SUBLANES, LANES), F32),
        scratch_shapes=[
            pltpu.SMEM((2 * TOP_K * tm,), I32),
            pltpu.SMEM((pad_runs.shape[0],), I32),
            pltpu.SemaphoreType.DMA((2,)),
            pltpu.SemaphoreType.DMA,
            pltpu.SemaphoreType.DMA,
        ],
        compiler_params=_cparams("arbitrary"),
        name="dispatch",
    )(*h1s, slots, pad_runs)


def _moe_kernel(blk_e_ref, nb_ref, x_ref, wgu_ref, bgu_ref, wd_ref, bd_ref, o_ref, wgu_b, wd_b):
    i = pl.program_id(0)
    is_live = i < nb_ref[0]

    @pl.when(jnp.logical_not(is_live))
    def _():
        o_ref[...] = jnp.zeros(o_ref.shape, F32)

    @pl.when(jnp.logical_or(i == 0, blk_e_ref[i] != blk_e_ref[jnp.maximum(i - 1, 0)]))
    def _():
        for r in range(0, D_MODEL, CAST_ROWS):
            wgu_b[r:r + CAST_ROWS, :] = wgu_ref[0, r:r + CAST_ROWS, :].astype(BF16)
        for r in range(0, D_FF, CAST_ROWS):
            wd_b[r:r + CAST_ROWS, :] = wd_ref[0, r:r + CAST_ROWS, :].astype(BF16)

    @pl.when(is_live)
    def _():
        x = _from_row_tiles(x_ref, 0, BM_MOE).astype(BF16)
        acc = jnp.zeros((BM_MOE, D_MODEL), F32) + bd_ref[0]
        for c in range(D_FF // FF_CHUNK):
            lo = c * FF_CHUNK
            g = jnp.dot(x, wgu_b[:, lo:lo + FF_CHUNK], preferred_element_type=F32) + bgu_ref[0, :, lo:lo + FF_CHUNK]
            u = (jnp.dot(x, wgu_b[:, D_FF + lo:D_FF + lo + FF_CHUNK], preferred_element_type=F32)
                 + bgu_ref[0, :, D_FF + lo:D_FF + lo + FF_CHUNK])
            g = jnp.minimum(g, SWIGLU_LIMIT)
            u = jnp.clip(u, -SWIGLU_LIMIT, SWIGLU_LIMIT)
            act = (u + 1.0) * (g * _sigmoid(g * SWIGLU_ALPHA))
            acc = acc + jnp.dot(act.astype(BF16), wd_b[lo:lo + FF_CHUNK, :], preferred_element_type=F32)
        _to_row_tiles(o_ref, 0, acc)


def _moe_blocks(xs, blk_e, nb_used, w_gu, b_gu, w_down, b_down):
    p = xs.shape[0] // SUBLANES
    n_blocks = p // BM_MOE
    live = lambda i, nb: jnp.minimum(i, nb[0] - 1)
    grid_spec = pltpu.PrefetchScalarGridSpec(
        num_scalar_prefetch=2,
        grid=(n_blocks,),
        in_specs=[
            pl.BlockSpec((BM_MOE * SUBLANES, LANES), lambda i, be, nb: (live(i, nb), 0)),
            pl.BlockSpec((1, D_MODEL, 2 * D_FF), lambda i, be, nb: (be[i], 0, 0)),
            pl.BlockSpec((1, 1, 2 * D_FF), lambda i, be, nb: (be[i], 0, 0)),
            pl.BlockSpec((1, D_FF, D_MODEL), lambda i, be, nb: (be[i], 0, 0)),
            pl.BlockSpec((1, 1, D_MODEL), lambda i, be, nb: (be[i], 0, 0)),
        ],
        out_specs=pl.BlockSpec((BM_MOE * SUBLANES, LANES), lambda i, be, nb: (i, 0)),
        scratch_shapes=[pltpu.VMEM((D_MODEL, 2 * D_FF), BF16), pltpu.VMEM((D_FF, D_MODEL), BF16)],
    )
    return pl.pallas_call(
        _moe_kernel,
        grid_spec=grid_spec,
        out_shape=jax.ShapeDtypeStruct((p * SUBLANES, LANES), F32),
        compiler_params=pltpu.CompilerParams(dimension_semantics=("arbitrary",), vmem_limit_bytes=VMEM_LIMIT_MOE),
        name="moe_experts",
    )(blk_e, nb_used, xs, w_gu, b_gu, w_down, b_down)


def _final_kernel(h_ref, p_ref, gate_ref, slots_hbm, y_hbm, wpg_ref, bpg_ref, wpp_ref, g_ref, b_ref, o_ref,
                  slots_smem, rows_ref, slot_sem, row_sem):
    i = pl.program_id(0)
    n = pl.num_programs(0)
    tm = h_ref.shape[0] // SUBLANES

    def gather(r, buf, base):
        for k in range(TOP_K):
            slot = slots_smem[base + TOP_K * r + k]
            src = y_hbm.at[pl.ds(pl.multiple_of(slot * SUBLANES, SUBLANES), SUBLANES)]
            dst = rows_ref.at[buf, k, pl.ds(pl.multiple_of(r * SUBLANES, SUBLANES), SUBLANES)]
            pltpu.make_async_copy(src, dst, row_sem.at[buf]).start(priority=k % 2)

    def wait_rows(buf):
        for k in range(TOP_K):
            pltpu.make_async_copy(y_hbm.at[pl.ds(0, tm * SUBLANES)], rows_ref.at[buf, k], row_sem.at[buf]).wait()

    @pl.when(i == 0)
    def _():
        first = _fetch_slots(slots_hbm, slots_smem, slot_sem, 0, 0)
        first.start()
        first.wait()

        def issue(r, carry):
            gather(r, 0, 0)
            return carry

        lax.fori_loop(0, tm, issue, 0, unroll=ISSUE_UNROLL)
        _fetch_slots(slots_hbm, slots_smem, slot_sem, 1, 1).start()

    def step(cur):
        nxt = 1 - cur
        _fetch_slots(slots_hbm, slots_smem, slot_sem, i + 1, nxt).wait()

        @pl.when(i + 2 <= n)
        def _():
            _fetch_slots(slots_hbm, slots_smem, slot_sem, i + 2, cur).start()

        for r in range(tm):
            gather(r, nxt, nxt * (TOP_K * tm))

        h = _from_row_tiles(h_ref, 0, tm)
        ple = (_sigmoid(jnp.dot(h.astype(BF16), wpg_ref[...], preferred_element_type=F32) + bpg_ref[...])
               * jnp.dot(p_ref[...].astype(BF16), wpp_ref[...], preferred_element_type=F32))

        wait_rows(cur)
        gates = gate_ref[...]
        ffn = gates[:, 0:1] * _from_row_tiles(rows_ref.at[cur, 0], 0, tm)
        for k in range(1, TOP_K):
            ffn = ffn + gates[:, k:k + 1] * _from_row_tiles(rows_ref.at[cur, k], 0, tm)
        o_ref[...] = _layer_norm(DEEPNORM_ALPHA * h + (ffn + ple), g_ref[...], b_ref[...])

        @pl.when(i == n - 1)
        def _():
            wait_rows(nxt)

    for parity in range(2):
        pl.when(lax.rem(i, 2) == parity)(functools.partial(step, parity))


def _combine_final(h1, p, gates, slots, y, w_pg, b_pg, w_pp, ln_g, ln_b):
    t = h1.shape[0] // SUBLANES
    tm = TM_COMBINE
    slots = jnp.concatenate([slots, slots[-1:]], axis=0)
    return pl.pallas_call(
        _final_kernel,
        grid=(t // tm,),
        in_specs=[
            pl.BlockSpec((tm * SUBLANES, LANES), lambda i: (i, 0)),
            pl.BlockSpec((tm, D_PLE), lambda i: (i, 0)),
            pl.BlockSpec((tm, TOP_K), lambda i: (i, 0)),
            pl.BlockSpec(memory_space=pl.ANY),
            pl.BlockSpec(memory_space=pl.ANY),
            _full((D_MODEL, D_MODEL)), _full((1, D_MODEL)), _full((D_PLE, D_MODEL)),
            _full((1, D_MODEL)), _full((1, D_MODEL)),
        ],
        out_specs=pl.BlockSpec((tm, D_MODEL), lambda i: (i, 0)),
        out_shape=jax.ShapeDtypeStruct((t, D_MODEL), F32),
        scratch_shapes=[
            pltpu.SMEM((2 * TOP_K * tm,), I32),
            pltpu.VMEM((2, TOP_K, tm * SUBLANES, LANES), F32),
            pltpu.SemaphoreType.DMA((2,)),
            pltpu.SemaphoreType.DMA((2,)),
        ],
        compiler_params=_cparams("arbitrary"),
        name="combine_final",
    )(h1, p, gates, slots, y, w_pg, b_pg, w_pp, ln_g, ln_b)


def _tile_slots(dest, tm):
    t = dest.shape[1]
    return dest.T.reshape(t // tm, tm * TOP_K)


def kernel(x_prompt, x_sample, p_prompt, p_sample, ln0_g, ln0_b, w_in, b_in, conv_w, conv_b, lnc_g, lnc_b, w_conv_out, w_fnet_out, w_o, b_o, ln1_g, ln1_b, w_router, b_router, w_gu, b_gu, w_down, b_down, w_pg, b_pg, w_pp, ln2_g, ln2_b):
    row = lambda v: v.reshape(1, -1).astype(F32)
    n_branch = 2 * D_CONV + D_FNET
    w_in_b = w_in[0].astype(BF16)
    w_branch, w_gate = w_in_b[:, :n_branch], w_in_b[:, n_branch:]
    b_branch, b_gate = row(b_in[0, :n_branch]), row(b_in[0, n_branch:])
    w_conv_out_b = w_conv_out[0].astype(BF16)
    w_fnet_out_b = w_fnet_out[0].astype(BF16)
    w_o_b = w_o[0].astype(BF16)
    w_rt_f = jnp.pad(w_router[0].astype(F32), ((0, 0), (0, LANES - N_EXPERTS)))
    w_rt_hi = w_rt_f.astype(BF16)
    w_rt = jnp.concatenate([w_rt_hi, (w_rt_f - w_rt_hi.astype(F32)).astype(BF16)], axis=1)
    b_rt = b_router[0].reshape(N_EXPERTS, 1).astype(F32)
    b_gu_r = b_gu[0].reshape(N_EXPERTS, 1, 2 * D_FF).astype(F32)
    b_down_r = b_down[0].reshape(N_EXPERTS, 1, D_MODEL).astype(F32)
    w_pg_b = w_pg[0].astype(BF16)
    w_pp_b = w_pp[0].astype(BF16)

    carry = jnp.zeros((N_EXPERTS, LANES), F32)
    routed = []
    for x, p in ((x_prompt, p_prompt[0]), (x_sample, p_sample[0])):
        bsz, s, _ = x.shape
        xt = x.reshape(bsz * s, D_MODEL)
        u, f_in = _inproj(xt, row(ln0_g), row(ln0_b), w_branch, b_branch)
        ya = _conv_branch(u.reshape(bsz, s, D_CONV), conv_w[0].astype(F32), row(conv_b[0]), row(lnc_g[0]),
                          row(lnc_b[0]), w_conv_out_b).reshape(bsz * s, D_MODEL)
        yb = _fourier_branch(f_in.reshape(bsz, s, D_FNET), w_fnet_out_b)
        h1, idx, gates, rank, carry = _mix_route(
            xt, ya, yb, carry, row(ln0_g), row(ln0_b), w_gate, b_gate, w_o_b, row(b_o[0]),
            row(ln1_g[0]), row(ln1_b[0]), w_rt, b_rt)
        routed.append((h1, idx, gates, rank, p.reshape(bsz * s, D_PLE), (bsz, s)))

    n_assign = sum(r[1].shape[1] for r in routed) * TOP_K
    n_blocks = (n_assign + N_EXPERTS * (BM_MOE - 1)) // BM_MOE
    counts = carry[:, 0].astype(I32)
    padded = ((counts + BM_MOE - 1) // BM_MOE) * BM_MOE
    ends = jnp.cumsum(padded)
    start = ends - padded
    nb_used = (ends[-1] // BM_MOE).reshape(1).astype(I32)
    blk = jnp.minimum(jnp.arange(n_blocks, dtype=I32), nb_used[0] - 1)
    blk_e = jnp.minimum(jnp.sum((ends[None, :] <= (blk * BM_MOE)[:, None]).astype(I32), axis=1), N_EXPERTS - 1)
    pad_runs = jnp.concatenate([jnp.stack([start + counts, padded - counts], axis=1).reshape(-1), nb_used,
                                jnp.zeros((LANES - 2 * N_EXPERTS - 1,), I32)]).astype(I32)

    slot_tables = []
    dispatch_slots = []
    for _, idx, _, rank, _, _ in routed:
        first = jnp.sum(jnp.where(idx[None] == jnp.arange(N_EXPERTS, dtype=I32)[:, None, None],
                                  start[:, None, None], 0), axis=0)
        slot_tables.append(_tile_slots(first + rank, TM_COMBINE))
        dispatch_slots.append(_tile_slots(first + rank, TM_ROWS))
    xs = _dispatch([r[0] for r in routed], jnp.concatenate(dispatch_slots, axis=0), pad_runs, n_blocks * BM_MOE)

    y = _moe_blocks(xs.reshape(-1, LANES), blk_e, nb_used, w_gu.reshape(N_EXPERTS, D_MODEL, 2 * D_FF), b_gu_r,
                    w_down.reshape(N_EXPERTS, D_FF, D_MODEL), b_down_r)

    outs = []
    for (h1, _, gates, _, p, (bsz, s)), slots in zip(routed, slot_tables):
        o = _combine_final(h1, p, gates.T, slots, y, w_pg_b, row(b_pg[0]), w_pp_b, row(ln2_g[0]), row(ln2_b[0]))
        outs.append(o.reshape(bsz, s, D_MODEL))
    return tuple(outs)
```

```python
import functools

import jax
import jax.numpy as jnp
from jax import lax
from jax.experimental import pallas as pl
from jax.experimental.pallas import tpu as pltpu

F32 = jnp.float32
BF16 = jnp.bfloat16
I32 = jnp.int32

D_MODEL = 1024
D_CONV = D_MODEL // 2
CONV_WIDTH = 31
CONV_PAD = CONV_WIDTH // 2
D_FNET = D_MODEL // 2
FNET_GROUPS = 4
FNET_GROUP_DIM = D_FNET // FNET_GROUPS
D_PLE = 256
N_EXPERTS = 32
TOP_K = 4
D_FF = D_MODEL
SWIGLU_ALPHA = 1.702
SWIGLU_LIMIT = 7.0
LN_EPS = 1e-5
DEPTH = 1
DEEPNORM_ALPHA = (2 * DEPTH) ** 0.25

LANES = 128
SUBLANES = 8
DFT_N1 = 128
HALO = 16

TM_INPROJ = 1024
TS_CONV = 512
RC_CONV = 128
TM_MIX = 512
SUB_MIX = 512
FFT2_TILES = 4
TM_ROWS = 1024
TM_COMBINE = 256
BM_MOE = 512
FF_CHUNK = 1024
ISSUE_UNROLL = 8
CAST_ROWS = 128
VMEM_LIMIT = 48 * 1024 * 1024
VMEM_LIMIT_MOE = 56 * 1024 * 1024


def _cparams(*sem):
    return pltpu.CompilerParams(dimension_semantics=sem, vmem_limit_bytes=VMEM_LIMIT)


def _layer_norm(x, g, b):
    mu = jnp.mean(x, axis=-1, keepdims=True)
    xc = x - mu
    var = jnp.mean(xc * xc, axis=-1, keepdims=True)
    return xc * lax.rsqrt(var + LN_EPS) * g + b


def _sigmoid(x):
    return 0.5 * jnp.tanh(0.5 * x) + 0.5


def _full(shape):
    return pl.BlockSpec(shape, lambda *_: (0,) * len(shape))


def _to_row_tiles(ref, row0, x):
    rows = x.shape[0]
    for j in range(D_MODEL // LANES):
        ref[pl.ds(row0 * SUBLANES + j, rows, stride=SUBLANES), :] = x[:, j * LANES:(j + 1) * LANES]


def _from_row_tiles(ref, row0, rows):
    return jnp.concatenate([ref[pl.ds(row0 * SUBLANES + j, rows, stride=SUBLANES), :]
                            for j in range(D_MODEL // LANES)], axis=1)


def _inproj_kernel(x_ref, g_ref, b_ref, w_ref, bias_ref, u_ref, f_ref):
    h = _layer_norm(x_ref[...], g_ref[...], b_ref[...])
    z = jnp.dot(h.astype(BF16), w_ref[...], preferred_element_type=F32) + bias_ref[...]
    u_ref[...] = (z[:, :D_CONV] * _sigmoid(z[:, D_CONV:2 * D_CONV])).astype(BF16)
    f_ref[...] = z[:, 2 * D_CONV:]


def _inproj(x, ln_g, ln_b, w, bias):
    t = x.shape[0]
    n_out = w.shape[1]
    return pl.pallas_call(
        _inproj_kernel,
        grid=(t // TM_INPROJ,),
        in_specs=[
            pl.BlockSpec((TM_INPROJ, D_MODEL), lambda i: (i, 0)),
            _full((1, D_MODEL)), _full((1, D_MODEL)),
            _full((D_MODEL, n_out)), _full((1, n_out)),
        ],
        out_specs=[
            pl.BlockSpec((TM_INPROJ, D_CONV), lambda i: (i, 0)),
            pl.BlockSpec((TM_INPROJ, D_FNET), lambda i: (i, 0)),
        ],
        out_shape=[jax.ShapeDtypeStruct((t, D_CONV), BF16), jax.ShapeDtypeStruct((t, D_FNET), F32)],
        compiler_params=_cparams("parallel"),
        name="inproj",
    )(x, ln_g, ln_b, w, bias)


def _conv_kernel(prev_ref, cur_ref, next_ref, cw_ref, cb_ref, g_ref, b_ref, wout_ref, ya_ref,
                 ext_ref, act_ref):
    i = pl.program_id(1)
    last = pl.num_programs(1) - 1
    ts = cur_ref.shape[1]
    ext_rows = ts + 2 * HALO
    zero = jnp.zeros((HALO, D_CONV), F32)
    ext_ref[0, 0:HALO, :] = jnp.where(i > 0, prev_ref[0].astype(F32), zero)
    ext_ref[0, HALO:HALO + ts, :] = cur_ref[0].astype(F32)
    ext_ref[0, HALO + ts:ext_rows, :] = jnp.where(i < last, next_ref[0].astype(F32), zero)
    for r in range(1, SUBLANES):
        ext_ref[r, 0:ext_rows - SUBLANES, :] = ext_ref[0, r:r + ext_rows - SUBLANES, :]

    cb = cb_ref[...]
    g = g_ref[...]
    b = b_ref[...]

    def chunk(c, carry):
        r0 = pl.multiple_of(c * RC_CONV, RC_CONV)
        acc = jnp.zeros((RC_CONV, D_CONV), F32) + cb
        for j in range(CONV_WIDTH):
            off = j + HALO - CONV_PAD
            q, r = divmod(off, SUBLANES)
            acc = acc + cw_ref[j:j + 1, :] * ext_ref[r, pl.ds(r0 + q * SUBLANES, RC_CONV), :]
        y = _layer_norm(acc, g, b)
        act_ref[pl.ds(r0, RC_CONV), :] = (y * _sigmoid(y)).astype(BF16)
        return carry

    lax.fori_loop(0, ts // RC_CONV, chunk, 0)
    ya_ref[0] = jnp.dot(act_ref[...], wout_ref[...], preferred_element_type=F32).astype(BF16)


def _conv_branch(u, conv_w, conv_b, ln_g, ln_b, w_out):
    bsz, s, _ = u.shape
    ts = TS_CONV
    nb = s // ts
    hb = ts // HALO
    return pl.pallas_call(
        _conv_kernel,
        grid=(bsz, nb),
        in_specs=[
            pl.BlockSpec((1, HALO, D_CONV), lambda b, i: (b, jnp.maximum(i * hb - 1, 0), 0)),
            pl.BlockSpec((1, ts, D_CONV), lambda b, i: (b, i, 0)),
            pl.BlockSpec((1, HALO, D_CONV), lambda b, i: (b, jnp.minimum((i + 1) * hb, s // HALO - 1), 0)),
            _full((CONV_WIDTH, D_CONV)), _full((1, D_CONV)), _full((1, D_CONV)), _full((1, D_CONV)),
            _full((D_CONV, D_MODEL)),
        ],
        out_specs=pl.BlockSpec((1, ts, D_MODEL), lambda b, i: (b, i, 0)),
        out_shape=jax.ShapeDtypeStruct((bsz, s, D_MODEL), BF16),
        scratch_shapes=[
            pltpu.VMEM((SUBLANES, ts + 2 * HALO, D_CONV), F32),
            pltpu.VMEM((ts, D_CONV), BF16),
        ],
        compiler_params=_cparams("parallel", "parallel"),
        name="conv_branch",
    )(u, u, u, conv_w, conv_b, ln_g, ln_b, w_out)


def _dft_tables(s):
    n2_len = s // DFT_N1
    k1_per_tile = DFT_N1 // n2_len
    two_pi = 2.0 * jnp.pi
    a = jnp.arange(DFT_N1, dtype=I32)
    ang1 = two_pi * ((a[:, None] * a[None, :]) % DFT_N1).astype(F32) / DFT_N1
    small = jnp.concatenate([jnp.cos(ang1), -jnp.sin(ang1)], axis=0)
    r = jnp.arange(2 * DFT_N1 * SUBLANES, dtype=I32)[:, None]
    c = jnp.arange(DFT_N1 * SUBLANES, dtype=I32)[None, :]
    pick_r = (r // SUBLANES == jnp.arange(2 * DFT_N1, dtype=I32)[None, :]).astype(F32)
    pick_c = (jnp.arange(DFT_N1, dtype=I32)[:, None] == c // SUBLANES).astype(F32)
    rep = jnp.dot(jnp.dot(pick_r, small, precision=lax.Precision.HIGHEST), pick_c, precision=lax.Precision.HIGHEST)
    f1 = jnp.where(r % SUBLANES == c % SUBLANES, rep, 0.0).astype(BF16)
    cs = jnp.stack([jnp.cos(ang1), jnp.sin(ang1)]).astype(BF16)
    n_tiles = s // DFT_N1
    t = jnp.arange(n_tiles, dtype=I32)[:, None, None]
    row = jnp.arange(DFT_N1, dtype=I32)[None, :, None]
    col = jnp.arange(DFT_N1, dtype=I32)[None, None, :]
    k2, k1l_out = row // k1_per_tile, row % k1_per_tile
    k1l_in, n2 = col // n2_len, col % n2_len
    k = t * k1_per_tile + k1l_out + DFT_N1 * k2
    hit = k1l_in == k1l_out
    if k1_per_tile == 1:
        ang_a = two_pi * ((n2 * t) % s).astype(F32) / s
        ang_b = two_pi * ((n2 * k2) % n2_len).astype(F32) / n2_len
        cos2 = jnp.cos(ang_a) * jnp.cos(ang_b) - jnp.sin(ang_a) * jnp.sin(ang_b)
        sin2 = jnp.sin(ang_a) * jnp.cos(ang_b) + jnp.cos(ang_a) * jnp.sin(ang_b)
    else:
        ang2 = two_pi * ((n2 * k) % s).astype(F32) / s
        cos2, sin2 = jnp.cos(ang2), jnp.sin(ang2)
    gr = jnp.where(hit, cos2, 0.0)
    gi = jnp.where(hit, -sin2, 0.0)
    g = jnp.concatenate([jnp.concatenate([gr, -gi], axis=2),
                         jnp.concatenate([gi, gr], axis=2)], axis=1).astype(BF16)
    return f1, g, cs


def _fft1_kernel(x_ref, f_ref, y_ref):
    nb = x_ref.shape[2]
    x = x_ref[0].reshape(DFT_N1 * nb, D_FNET).astype(BF16)
    r = jnp.dot(f_ref[...], x, preferred_element_type=F32)
    y_ref[0] = r.reshape(2, DFT_N1, nb, D_FNET)


def _fft2_kernel(y_ref, g_ref, cs_ref, w_ref, o_ref, *, scale, tiles):
    for t in range(tiles):
        rows = slice(t * DFT_N1, (t + 1) * DFT_N1)
        yb = jnp.concatenate([y_ref[0, 0, rows, :], y_ref[0, 1, rows, :]], axis=0).astype(BF16)
        z = jnp.dot(g_ref[t], yb, preferred_element_type=F32)
        zr = z[:DFT_N1].astype(BF16)
        zi = z[DFT_N1:].astype(BF16)
        parts = []
        for grp in range(FNET_GROUPS):
            sl = slice(grp * FNET_GROUP_DIM, (grp + 1) * FNET_GROUP_DIM)
            parts.append(jnp.dot(zr[:, sl], cs_ref[0], preferred_element_type=F32)
                         + jnp.dot(zi[:, sl], cs_ref[1], preferred_element_type=F32))
        fm = (jnp.concatenate(parts, axis=1) * scale).astype(BF16)
        out = jnp.dot(fm, w_ref[...], preferred_element_type=F32)
        k1_per_tile = o_ref.shape[2] // tiles
        if k1_per_tile == 1:
            o_ref[0, :, t, :] = out
        else:
            o_ref[0, :, t * k1_per_tile:(t + 1) * k1_per_tile, :] = out.reshape(o_ref.shape[1], k1_per_tile, D_MODEL)


def _fourier_branch(f_in, w_out):
    bsz, s, _ = f_in.shape
    n2_len = s // DFT_N1
    k1_per_tile = DFT_N1 // n2_len
    f1, g, cs = _dft_tables(s)
    y = pl.pallas_call(
        _fft1_kernel,
        grid=(bsz, n2_len // SUBLANES),
        in_specs=[pl.BlockSpec((1, DFT_N1, SUBLANES, D_FNET), lambda b, j: (b, 0, j, 0)),
                  _full((2 * DFT_N1 * SUBLANES, DFT_N1 * SUBLANES))],
        out_specs=pl.BlockSpec((1, 2, DFT_N1, SUBLANES, D_FNET), lambda b, j: (b, 0, 0, j, 0)),
        out_shape=jax.ShapeDtypeStruct((bsz, 2, DFT_N1, n2_len, D_FNET), F32),
        compiler_params=_cparams("parallel", "parallel"),
        name="fft_stage1",
    )(f_in.reshape(bsz, DFT_N1, n2_len, D_FNET), f1)
    y = y.reshape(bsz, 2, s, D_FNET)
    scale = float((s * FNET_GROUP_DIM) ** -0.5)
    assert n2_len % SUBLANES == 0 and (k1_per_tile == 1 or k1_per_tile % SUBLANES == 0)
    tiles = SUBLANES if k1_per_tile == 1 else min(FFT2_TILES, n2_len)
    k1_per_step = tiles * k1_per_tile
    yb = pl.pallas_call(
        functools.partial(_fft2_kernel, scale=scale, tiles=tiles),
        grid=(bsz, s // (DFT_N1 * tiles)),
        in_specs=[
            pl.BlockSpec((1, 2, DFT_N1 * tiles, D_FNET), lambda b, t: (b, 0, t, 0)),
            pl.BlockSpec((tiles, 2 * DFT_N1, 2 * DFT_N1), lambda b, t: (t, 0, 0)),
            _full((2, DFT_N1, DFT_N1)),
            _full((D_FNET, D_MODEL)),
        ],
        out_specs=pl.BlockSpec((1, n2_len, k1_per_step, D_MODEL), lambda b, t: (b, 0, t, 0)),
        out_shape=jax.ShapeDtypeStruct((bsz, n2_len, DFT_N1, D_MODEL), F32),
        compiler_params=_cparams("parallel", "parallel"),
        name="fft_stage2",
    )(y, g, cs, w_out)
    return yb.reshape(bsz * s, D_MODEL)


def _mix_kernel(x_ref, ya_ref, yb_ref, cin_ref, ln0g_ref, ln0b_ref, wg_ref, bg_ref, wo_ref, bo_ref,
                ln1g_ref, ln1b_ref, wr_ref, br_ref,
                h1_ref, idx_ref, gate_ref, rank_ref, cnt_ref, carry_ref):
    tm = x_ref.shape[0]
    sub = min(SUB_MIX, tm)

    @pl.when(pl.program_id(0) == 0)
    def _():
        carry_ref[...] = cin_ref[...]

    carry = carry_ref[...]
    eio = lax.broadcasted_iota(I32, (N_EXPERTS, sub), 0)
    before = (lax.broadcasted_iota(I32, (sub, sub), 0) < lax.broadcasted_iota(I32, (sub, sub), 1)).astype(BF16)
    for r0 in range(0, tm, sub):
        rows = slice(r0, r0 + sub)
        h = _layer_norm(x_ref[rows, :], ln0g_ref[...], ln0b_ref[...])
        zg = jnp.dot(h.astype(BF16), wg_ref[...], preferred_element_type=F32) + bg_ref[...]
        m = (_sigmoid(zg[:, :D_MODEL]) * ya_ref[rows, :].astype(F32)
             + _sigmoid(zg[:, D_MODEL:]) * yb_ref[rows, :].astype(F32))
        mix = jnp.dot(m.astype(BF16), wo_ref[...], preferred_element_type=F32) + bo_ref[...]
        h1 = _layer_norm(DEEPNORM_ALPHA * h + mix, ln1g_ref[...], ln1b_ref[...])
        _to_row_tiles(h1_ref, r0, h1)

        h_hi = h1.astype(BF16)
        h_lo = (h1 - h_hi.astype(F32)).astype(BF16)
        hw = jnp.dot(h_hi, wr_ref[...], preferred_element_type=F32)
        lw = jnp.dot(h_lo, wr_ref[:, :LANES], preferred_element_type=F32)
        logits = (hw[:, :LANES] + hw[:, LANES:] + lw).T[:N_EXPERTS] + br_ref[...]
        vals, idxs = [], []
        cur = logits
        for _ in range(TOP_K):
            mx = jnp.max(cur, axis=0, keepdims=True)
            ik = jnp.min(jnp.where(cur == mx, eio, N_EXPERTS), axis=0, keepdims=True)
            vals.append(mx)
            idxs.append(ik)
            cur = jnp.where(eio == ik, -jnp.inf, cur)
        exps = [jnp.exp(v - vals[0]) for v in vals]
        den = exps[0] + exps[1] + exps[2] + exps[3]
        gate_ref[:, rows] = jnp.concatenate([e / den for e in exps], axis=0)
        idx_ref[:, rows] = jnp.concatenate(idxs, axis=0)

        hot = jnp.zeros((N_EXPERTS, sub), F32)
        for ik in idxs:
            hot = hot + (eio == ik).astype(F32)
        prior = jnp.dot(hot.astype(BF16), before, preferred_element_type=F32) + carry[:, 0:1]
        ranks = [jnp.sum(jnp.where(eio == ik, prior, 0.0), axis=0, keepdims=True) for ik in idxs]
        rank_ref[:, rows] = jnp.concatenate(ranks, axis=0).astype(I32)
        carry = carry + jnp.sum(hot, axis=1, keepdims=True)
    carry_ref[...] = carry
    cnt_ref[...] = carry


def _mix_route(x, ya, yb, carry_in, ln0_g, ln0_b, w_gate, b_gate, w_o, b_o, ln1_g, ln1_b, w_rt, b_rt):
    t = x.shape[0]
    tm = TM_MIX
    row = lambda d: pl.BlockSpec((tm, d), lambda i: (i, 0))
    col = lambda r: pl.BlockSpec((r, tm), lambda i: (0, i))
    return pl.pallas_call(
        _mix_kernel,
        grid=(t // tm,),
        in_specs=[
            row(D_MODEL), row(D_MODEL), row(D_MODEL), _full((N_EXPERTS, LANES)),
            _full((1, D_MODEL)), _full((1, D_MODEL)),
            _full((D_MODEL, 2 * D_MODEL)), _full((1, 2 * D_MODEL)),
            _full((D_MODEL, D_MODEL)), _full((1, D_MODEL)),
            _full((1, D_MODEL)), _full((1, D_MODEL)),
            _full((D_MODEL, 2 * LANES)), _full((N_EXPERTS, 1)),
        ],
        out_specs=[pl.BlockSpec((tm * SUBLANES, LANES), lambda i: (i, 0)), col(TOP_K), col(TOP_K), col(TOP_K),
                   _full((N_EXPERTS, LANES))],
        out_shape=[
            jax.ShapeDtypeStruct((t * SUBLANES, LANES), F32),
            jax.ShapeDtypeStruct((TOP_K, t), I32),
            jax.ShapeDtypeStruct((TOP_K, t), F32),
            jax.ShapeDtypeStruct((TOP_K, t), I32),
            jax.ShapeDtypeStruct((N_EXPERTS, LANES), F32),
        ],
        scratch_shapes=[pltpu.VMEM((N_EXPERTS, LANES), F32)],
        compiler_params=_cparams("arbitrary"),
        name="mix_route",
    )(x, ya, yb, carry_in, ln0_g, ln0_b, w_gate, b_gate, w_o, b_o, ln1_g, ln1_b, w_rt, b_rt)


def _row_copy(src_ref, src_row, dst_ref, dst_row, sem):
    return pltpu.make_async_copy(src_ref.at[src_row], dst_ref.at[dst_row], sem)


def _fetch_slots(slots_hbm, slots_smem, sem, step, slot):
    n = slots_hbm.shape[1]
    return pltpu.make_async_copy(slots_hbm.at[step], slots_smem.at[pl.ds(pl.multiple_of(slot * n, n), n)],
                                 sem.at[slot])


def _pad_fill(h_ref, xs_out, pad_smem, pad_sem, n_pad_runs, wait):
    def go(cp):
        if wait:
            cp.wait()
        else:
            cp.start()

    def run(j, carry):
        start = pad_smem[2 * j]
        length = pad_smem[2 * j + 1]
        for bit in range(BM_MOE.bit_length() - 1):
            size = 1 << bit

            @pl.when((length & size) != 0)
            def _():
                off = start + (length & ~(2 * size - 1))
                go(pltpu.make_async_copy(h_ref.at[pl.ds(0, size)], xs_out.at[pl.ds(off, size)], pad_sem))
        return carry

    lax.fori_loop(0, n_pad_runs, run, 0)

    piece = min(h_ref.shape[0], BM_MOE)

    def dead_block(j, carry):
        for part in range(BM_MOE // piece):
            off = j * BM_MOE + part * piece
            go(pltpu.make_async_copy(h_ref.at[pl.ds(0, piece)], xs_out.at[pl.ds(off, piece)], pad_sem))
        return carry

    lax.fori_loop(pad_smem[2 * n_pad_runs], xs_out.shape[0] // BM_MOE, dead_block, 0)


def _dispatch_kernel(*refs, tiles):
    n_sets = len(tiles)
    h_refs = refs[:n_sets]
    slots_hbm, pad_hbm, xs_out, slots_smem, pad_smem, slot_sem, pad_sem, row_sem = refs[n_sets:]
    i = pl.program_id(0)
    n = pl.num_programs(0)
    tm = h_refs[0].shape[0]
    cur = lax.rem(i, 2)

    @pl.when(i == 0)
    def _():
        _fetch_slots(slots_hbm, slots_smem, slot_sem, 0, 0).start()
        fetch = pltpu.make_async_copy(pad_hbm, pad_smem, pad_sem)
        fetch.start()
        fetch.wait()
        _pad_fill(h_refs[0], xs_out, pad_smem, pad_sem, N_EXPERTS, wait=False)
        _pad_fill(h_refs[0], xs_out, pad_smem, pad_sem, N_EXPERTS, wait=True)

    @pl.when(i + 1 < n)
    def _():
        _fetch_slots(slots_hbm, slots_smem, slot_sem, i + 1, 1 - cur).start()

    _fetch_slots(slots_hbm, slots_smem, slot_sem, i, cur).wait()
    base = cur * (TOP_K * tm)

    first = 0
    for h_ref, count in zip(h_refs, tiles):
        @pl.when(jnp.logical_and(i >= first, i < first + count))
        def _(h_ref=h_ref):
            def issue(r, carry):
                for k in range(TOP_K):
                    _row_copy(h_ref, r, xs_out, slots_smem[base + TOP_K * r + k], row_sem).start(priority=k % 2)
                return carry

            lax.fori_loop(0, tm, issue, 0, unroll=ISSUE_UNROLL)
            for _ in range(TOP_K):
                pltpu.make_async_copy(h_ref, xs_out.at[pl.ds(0, tm)], row_sem).wait()
        first += count


def _dispatch(h1s, slots, pad_runs, n_rows):
    tm = TM_ROWS
    h1s = [h.reshape(-1, SUBLANES, LANES) for h in h1s]
    tiles = [h.shape[0] // tm for h in h1s]
    firsts = [sum(tiles[:j]) for j in range(len(tiles))]

    def tile_of(first, count):
        return lambda i: (jnp.clip(i - first, 0, count - 1), 0, 0)

    return pl.pallas_call(
        functools.partial(_dispatch_kernel, tiles=tuple(tiles)),
        grid=(sum(tiles),),
        in_specs=[pl.BlockSpec((tm, SUBLANES, LANES), tile_of(f, c)) for f, c in zip(firsts, tiles)]
        + [pl.BlockSpec(memory_space=pl.ANY)] * 2,
        out_specs=pl.BlockSpec(memory_space=pl.ANY),
        out_shape=jax.ShapeDtypeStruct((n_rows, SUBLANES, LANES), F32),
        scratch_shapes=[
            pltpu.SMEM((2 * TOP_K * tm,), I32),
            pltpu.SMEM((pad_runs.shape[0],), I32),
            pltpu.SemaphoreType.DMA((2,)),
            pltpu.SemaphoreType.DMA,
            pltpu.SemaphoreType.DMA,
        ],
        compiler_params=_cparams("arbitrary"),
        name="dispatch",
    )(*h1s, slots, pad_runs)


def _moe_kernel(blk_e_ref, nb_ref, x_ref, wgu_ref, bgu_ref, wd_ref, bd_ref, o_ref, wgu_b, wd_b):
    i = pl.program_id(0)
    is_live = i < nb_ref[0]

    @pl.when(jnp.logical_not(is_live))
    def _():
        o_ref[...] = jnp.zeros(o_ref.shape, F32)

    @pl.when(jnp.logical_or(i == 0, blk_e_ref[i] != blk_e_ref[jnp.maximum(i - 1, 0)]))
    def _():
        for r in range(0, D_MODEL, CAST_ROWS):
            wgu_b[r:r + CAST_ROWS, :] = wgu_ref[0, r:r + CAST_ROWS, :].astype(BF16)
        for r in range(0, D_FF, CAST_ROWS):
            wd_b[r:r + CAST_ROWS, :] = wd_ref[0, r:r + CAST_ROWS, :].astype(BF16)

    @pl.when(is_live)
    def _():
        x = _from_row_tiles(x_ref, 0, BM_MOE).astype(BF16)
        acc = jnp.zeros((BM_MOE, D_MODEL), F32) + bd_ref[0]
        for c in range(D_FF // FF_CHUNK):
            lo = c * FF_CHUNK
            g = jnp.dot(x, wgu_b[:, lo:lo + FF_CHUNK], preferred_element_type=F32) + bgu_ref[0, :, lo:lo + FF_CHUNK]
            u = (jnp.dot(x, wgu_b[:, D_FF + lo:D_FF + lo + FF_CHUNK], preferred_element_type=F32)
                 + bgu_ref[0, :, D_FF + lo:D_FF + lo + FF_CHUNK])
            g = jnp.minimum(g, SWIGLU_LIMIT)
            u = jnp.clip(u, -SWIGLU_LIMIT, SWIGLU_LIMIT)
            act = (u + 1.0) * (g * _sigmoid(g * SWIGLU_ALPHA))
            acc = acc + jnp.dot(act.astype(BF16), wd_b[lo:lo + FF_CHUNK, :], preferred_element_type=F32)
        _to_row_tiles(o_ref, 0, acc)


def _moe_blocks(xs, blk_e, nb_used, w_gu, b_gu, w_down, b_down):
    p = xs.shape[0] // SUBLANES
    n_blocks = p // BM_MOE
    live = lambda i, nb: jnp.minimum(i, nb[0] - 1)
    grid_spec = pltpu.PrefetchScalarGridSpec(
        num_scalar_prefetch=2,
        grid=(n_blocks,),
        in_specs=[
            pl.BlockSpec((BM_MOE * SUBLANES, LANES), lambda i, be, nb: (live(i, nb), 0)),
            pl.BlockSpec((1, D_MODEL, 2 * D_FF), lambda i, be, nb: (be[i], 0, 0)),
            pl.BlockSpec((1, 1, 2 * D_FF), lambda i, be, nb: (be[i], 0, 0)),
            pl.BlockSpec((1, D_FF, D_MODEL), lambda i, be, nb: (be[i], 0, 0)),
            pl.BlockSpec((1, 1, D_MODEL), lambda i, be, nb: (be[i], 0, 0)),
        ],
        out_specs=pl.BlockSpec((BM_MOE * SUBLANES, LANES), lambda i, be, nb: (i, 0)),
        scratch_shapes=[pltpu.VMEM((D_MODEL, 2 * D_FF), BF16), pltpu.VMEM((D_FF, D_MODEL), BF16)],
    )
    return pl.pallas_call(
        _moe_kernel,
        grid_spec=grid_spec,
        out_shape=jax.ShapeDtypeStruct((p * SUBLANES, LANES), F32),
        compiler_params=pltpu.CompilerParams(dimension_semantics=("arbitrary",), vmem_limit_bytes=VMEM_LIMIT_MOE),
        name="moe_experts",
    )(blk_e, nb_used, xs, w_gu, b_gu, w_down, b_down)


def _final_kernel(h_ref, p_ref, gate_ref, slots_hbm, y_hbm, wpg_ref, bpg_ref, wpp_ref, g_ref, b_ref, o_ref,
                  slots_smem, rows_ref, slot_sem, row_sem):
    i = pl.program_id(0)
    n = pl.num_programs(0)
    tm = h_ref.shape[0] // SUBLANES

    def gather(r, buf, base):
        for k in range(TOP_K):
            slot = slots_smem[base + TOP_K * r + k]
            src = y_hbm.at[pl.ds(pl.multiple_of(slot * SUBLANES, SUBLANES), SUBLANES)]
            dst = rows_ref.at[buf, k, pl.ds(pl.multiple_of(r * SUBLANES, SUBLANES), SUBLANES)]
            pltpu.make_async_copy(src, dst, row_sem.at[buf]).start(priority=k % 2)

    def wait_rows(buf):
        for k in range(TOP_K):
            pltpu.make_async_copy(y_hbm.at[pl.ds(0, tm * SUBLANES)], rows_ref.at[buf, k], row_sem.at[buf]).wait()

    @pl.when(i == 0)
    def _():
        first = _fetch_slots(slots_hbm, slots_smem, slot_sem, 0, 0)
        first.start()
        first.wait()

        def issue(r, carry):
            gather(r, 0, 0)
            return carry

        lax.fori_loop(0, tm, issue, 0, unroll=ISSUE_UNROLL)
        _fetch_slots(slots_hbm, slots_smem, slot_sem, 1, 1).start()

    def step(cur):
        nxt = 1 - cur
        _fetch_slots(slots_hbm, slots_smem, slot_sem, i + 1, nxt).wait()

        @pl.when(i + 2 <= n)
        def _():
            _fetch_slots(slots_hbm, slots_smem, slot_sem, i + 2, cur).start()

        for r in range(tm):
            gather(r, nxt, nxt * (TOP_K * tm))

        h = _from_row_tiles(h_ref, 0, tm)
        ple = (_sigmoid(jnp.dot(h.astype(BF16), wpg_ref[...], preferred_element_type=F32) + bpg_ref[...])
               * jnp.dot(p_ref[...].astype(BF16), wpp_ref[...], preferred_element_type=F32))

        wait_rows(cur)
        gates = gate_ref[...]
        ffn = gates[:, 0:1] * _from_row_tiles(rows_ref.at[cur, 0], 0, tm)
        for k in range(1, TOP_K):
            ffn = ffn + gates[:, k:k + 1] * _from_row_tiles(rows_ref.at[cur, k], 0, tm)
        o_ref[...] = _layer_norm(DEEPNORM_ALPHA * h + (ffn + ple), g_ref[...], b_ref[...])

        @pl.when(i == n - 1)
        def _():
            wait_rows(nxt)

    for parity in range(2):
        pl.when(lax.rem(i, 2) == parity)(functools.partial(step, parity))


def _combine_final(h1, p, gates, slots, y, w_pg, b_pg, w_pp, ln_g, ln_b):
    t = h1.shape[0] // SUBLANES
    tm = TM_COMBINE
    slots = jnp.concatenate([slots, slots[-1:]], axis=0)
    return pl.pallas_call(
        _final_kernel,
        grid=(t // tm,),
        in_specs=[
            pl.BlockSpec((tm * SUBLANES, LANES), lambda i: (i, 0)),
            pl.BlockSpec((tm, D_PLE), lambda i: (i, 0)),
            pl.BlockSpec((tm, TOP_K), lambda i: (i, 0)),
            pl.BlockSpec(memory_space=pl.ANY),
            pl.BlockSpec(memory_space=pl.ANY),
            _full((D_MODEL, D_MODEL)), _full((1, D_MODEL)), _full((D_PLE, D_MODEL)),
            _full((1, D_MODEL)), _full((1, D_MODEL)),
        ],
        out_specs=pl.BlockSpec((tm, D_MODEL), lambda i: (i, 0)),
        out_shape=jax.ShapeDtypeStruct((t, D_MODEL), F32),
        scratch_shapes=[
            pltpu.SMEM((2 * TOP_K * tm,), I32),
            pltpu.VMEM((2, TOP_K, tm * SUBLANES, LANES), F32),
            pltpu.SemaphoreType.DMA((2,)),
            pltpu.SemaphoreType.DMA((2,)),
        ],
        compiler_params=_cparams("arbitrary"),
        name="combine_final",
    )(h1, p, gates, slots, y, w_pg, b_pg, w_pp, ln_g, ln_b)


def _tile_slots(dest, tm):
    t = dest.shape[1]
    return dest.T.reshape(t // tm, tm * TOP_K)


def kernel(x_prompt, x_sample, p_prompt, p_sample, ln0_g, ln0_b, w_in, b_in, conv_w, conv_b, lnc_g, lnc_b, w_conv_out, w_fnet_out, w_o, b_o, ln1_g, ln1_b, w_router, b_router, w_gu, b_gu, w_down, b_down, w_pg, b_pg, w_pp, ln2_g, ln2_b):
    row = lambda v: v.reshape(1, -1).astype(F32)
    n_branch = 2 * D_CONV + D_FNET
    w_in_b = w_in[0].astype(BF16)
    w_branch, w_gate = w_in_b[:, :n_branch], w_in_b[:, n_branch:]
    b_branch, b_gate = row(b_in[0, :n_branch]), row(b_in[0, n_branch:])
    w_conv_out_b = w_conv_out[0].astype(BF16)
    w_fnet_out_b = w_fnet_out[0].astype(BF16)
    w_o_b = w_o[0].astype(BF16)
    w_rt_f = jnp.pad(w_router[0].astype(F32), ((0, 0), (0, LANES - N_EXPERTS)))
    w_rt_hi = w_rt_f.astype(BF16)
    w_rt = jnp.concatenate([w_rt_hi, (w_rt_f - w_rt_hi.astype(F32)).astype(BF16)], axis=1)
    b_rt = b_router[0].reshape(N_EXPERTS, 1).astype(F32)
    b_gu_r = b_gu[0].reshape(N_EXPERTS, 1, 2 * D_FF).astype(F32)
    b_down_r = b_down[0].reshape(N_EXPERTS, 1, D_MODEL).astype(F32)
    w_pg_b = w_pg[0].astype(BF16)
    w_pp_b = w_pp[0].astype(BF16)

    carry = jnp.zeros((N_EXPERTS, LANES), F32)
    routed = []
    for x, p in ((x_prompt, p_prompt[0]), (x_sample, p_sample[0])):
        bsz, s, _ = x.shape
        xt = x.reshape(bsz * s, D_MODEL)
        u, f_in = _inproj(xt, row(ln0_g), row(ln0_b), w_branch, b_branch)
        ya = _conv_branch(u.reshape(bsz, s, D_CONV), conv_w[0].astype(F32), row(conv_b[0]), row(lnc_g[0]),
                          row(lnc_b[0]), w_conv_out_b).reshape(bsz * s, D_MODEL)
        yb = _fourier_branch(f_in.reshape(bsz, s, D_FNET), w_fnet_out_b)
        h1, idx, gates, rank, carry = _mix_route(
            xt, ya, yb, carry, row(ln0_g), row(ln0_b), w_gate, b_gate, w_o_b, row(b_o[0]),
            row(ln1_g[0]), row(ln1_b[0]), w_rt, b_rt)
        routed.append((h1, idx, gates, rank, p.reshape(bsz * s, D_PLE), (bsz, s)))

    n_assign = sum(r[1].shape[1] for r in routed) * TOP_K
    n_blocks = (n_assign + N_EXPERTS * (BM_MOE - 1)) // BM_MOE
    counts = carry[:, 0].astype(I32)
    padded = ((counts + BM_MOE - 1) // BM_MOE) * BM_MOE
    ends = jnp.cumsum(padded)
    start = ends - padded
    nb_used = (ends[-1] // BM_MOE).reshape(1).astype(I32)
    blk = jnp.minimum(jnp.arange(n_blocks, dtype=I32), nb_used[0] - 1)
    blk_e = jnp.minimum(jnp.sum((ends[None, :] <= (blk * BM_MOE)[:, None]).astype(I32), axis=1), N_EXPERTS - 1)
    pad_runs = jnp.concatenate([jnp.stack([start + counts, padded - counts], axis=1).reshape(-1), nb_used,
                                jnp.zeros((LANES - 2 * N_EXPERTS - 1,), I32)]).astype(I32)

    slot_tables = []
    dispatch_slots = []
    for _, idx, _, rank, _, _ in routed:
        first = jnp.sum(jnp.where(idx[None] == jnp.arange(N_EXPERTS, dtype=I32)[:, None, None],
                                  start[:, None, None], 0), axis=0)
        slot_tables.append(_tile_slots(first + rank, TM_COMBINE))
        dispatch_slots.append(_tile_slots(first + rank, TM_ROWS))
    xs = _dispatch([r[0] for r in routed], jnp.concatenate(dispatch_slots, axis=0), pad_runs, n_blocks * BM_MOE)

    y = _moe_blocks(xs.reshape(-1, LANES), blk_e, nb_used, w_gu.reshape(N_EXPERTS, D_MODEL, 2 * D_FF), b_gu_r,
                    w_down.reshape(N_EXPERTS, D_FF, D_MODEL), b_down_r)

    outs = []
    for (h1, _, gates, _, p, (bsz, s)), slots in zip(routed, slot_tables):
        o = _combine_final(h1, p, gates.T, slots, y, w_pg_b, row(b_pg[0]), w_pp_b, row(ln2_g[0]), row(ln2_b[0]))
        outs.append(o.reshape(bsz, s, D_MODEL))
    return tuple(outs)
```

```python
import functools

import jax
import jax.numpy as jnp
from jax import lax
from jax.experimental import pallas as pl
from jax.experimental.pallas import tpu as pltpu

F32 = jnp.float32
BF16 = jnp.bfloat16
I32 = jnp.int32

D_MODEL = 1024
D_CONV = D_MODEL // 2
CONV_WIDTH = 31
CONV_PAD = CONV_WIDTH // 2
D_FNET = D_MODEL // 2
FNET_GROUPS = 4
FNET_GROUP_DIM = D_FNET // FNET_GROUPS
D_PLE = 256
N_EXPERTS = 32
TOP_K = 4
D_FF = D_MODEL
SWIGLU_ALPHA = 1.702
SWIGLU_LIMIT = 7.0
LN_EPS = 1e-5
DEPTH = 1
DEEPNORM_ALPHA = (2 * DEPTH) ** 0.25

LANES = 128
SUBLANES = 8
DFT_N1 = 128
HALO = 16

TM_INPROJ = 1024
TS_CONV = 512
RC_CONV = 128
TM_MIX = 512
SUB_MIX = 512
FFT2_TILES = 4
TM_ROWS = 1024
TM_COMBINE = 256
BM_MOE = 512
FF_CHUNK = 1024
ISSUE_UNROLL = 8
CAST_ROWS = 128
VMEM_LIMIT = 48 * 1024 * 1024
VMEM_LIMIT_MOE = 56 * 1024 * 1024


def _cparams(*sem):
    return pltpu.CompilerParams(dimension_semantics=sem, vmem_limit_bytes=VMEM_LIMIT)


def _layer_norm(x, g, b):
    mu = jnp.mean(x, axis=-1, keepdims=True)
    xc = x - mu
    var = jnp.mean(xc * xc, axis=-1, keepdims=True)
    return xc * lax.rsqrt(var + LN_EPS) * g + b


def _sigmoid(x):
    return 0.5 * jnp.tanh(0.5 * x) + 0.5


def _full(shape):
    return pl.BlockSpec(shape, lambda *_: (0,) * len(shape))


def _to_row_tiles(ref, row0, x):
    rows = x.shape[0]
    for j in range(D_MODEL // LANES):
        ref[pl.ds(row0 * SUBLANES + j, rows, stride=SUBLANES), :] = x[:, j * LANES:(j + 1) * LANES]


def _from_row_tiles(ref, row0, rows):
    return jnp.concatenate([ref[pl.ds(row0 * SUBLANES + j, rows, stride=SUBLANES), :]
                            for j in range(D_MODEL // LANES)], axis=1)


def _inproj_kernel(x_ref, g_ref, b_ref, w_ref, bias_ref, u_ref, f_ref):
    h = _layer_norm(x_ref[...], g_ref[...], b_ref[...])
    z = jnp.dot(h.astype(BF16), w_ref[...], preferred_element_type=F32) + bias_ref[...]
    u_ref[...] = (z[:, :D_CONV] * _sigmoid(z[:, D_CONV:2 * D_CONV])).astype(BF16)
    f_ref[...] = z[:, 2 * D_CONV:]


def _inproj(x, ln_g, ln_b, w, bias):
    t = x.shape[0]
    n_out = w.shape[1]
    return pl.pallas_call(
        _inproj_kernel,
        grid=(t // TM_INPROJ,),
        in_specs=[
            pl.BlockSpec((TM_INPROJ, D_MODEL), lambda i: (i, 0)),
            _full((1, D_MODEL)), _full((1, D_MODEL)),
            _full((D_MODEL, n_out)), _full((1, n_out)),
        ],
        out_specs=[
            pl.BlockSpec((TM_INPROJ, D_CONV), lambda i: (i, 0)),
            pl.BlockSpec((TM_INPROJ, D_FNET), lambda i: (i, 0)),
        ],
        out_shape=[jax.ShapeDtypeStruct((t, D_CONV), BF16), jax.ShapeDtypeStruct((t, D_FNET), F32)],
        compiler_params=_cparams("parallel"),
        name="inproj",
    )(x, ln_g, ln_b, w, bias)


def _conv_kernel(prev_ref, cur_ref, next_ref, cw_ref, cb_ref, g_ref, b_ref, wout_ref, ya_ref,
                 ext_ref, act_ref):
    i = pl.program_id(1)
    last = pl.num_programs(1) - 1
    ts = cur_ref.shape[1]
    ext_rows = ts + 2 * HALO
    zero = jnp.zeros((HALO, D_CONV), F32)
    ext_ref[0, 0:HALO, :] = jnp.where(i > 0, prev_ref[0].astype(F32), zero)
    ext_ref[0, HALO:HALO + ts, :] = cur_ref[0].astype(F32)
    ext_ref[0, HALO + ts:ext_rows, :] = jnp.where(i < last, next_ref[0].astype(F32), zero)
    for r in range(1, SUBLANES):
        ext_ref[r, 0:ext_rows - SUBLANES, :] = ext_ref[0, r:r + ext_rows - SUBLANES, :]

    cb = cb_ref[...]
    g = g_ref[...]
    b = b_ref[...]

    def chunk(c, carry):
        r0 = pl.multiple_of(c * RC_CONV, RC_CONV)
        acc = jnp.zeros((RC_CONV, D_CONV), F32) + cb
        for j in range(CONV_WIDTH):
            off = j + HALO - CONV_PAD
            q, r = divmod(off, SUBLANES)
            acc = acc + cw_ref[j:j + 1, :] * ext_ref[r, pl.ds(r0 + q * SUBLANES, RC_CONV), :]
        y = _layer_norm(acc, g, b)
        act_ref[pl.ds(r0, RC_CONV), :] = (y * _sigmoid(y)).astype(BF16)
        return carry

    lax.fori_loop(0, ts // RC_CONV, chunk, 0)
    ya_ref[0] = jnp.dot(act_ref[...], wout_ref[...], preferred_element_type=F32).astype(BF16)


def _conv_branch(u, conv_w, conv_b, ln_g, ln_b, w_out):
    bsz, s, _ = u.shape
    ts = TS_CONV
    nb = s // ts
    hb = ts // HALO
    return pl.pallas_call(
        _conv_kernel,
        grid=(bsz, nb),
        in_specs=[
            pl.BlockSpec((1, HALO, D_CONV), lambda b, i: (b, jnp.maximum(i * hb - 1, 0), 0)),
            pl.BlockSpec((1, ts, D_CONV), lambda b, i: (b, i, 0)),
            pl.BlockSpec((1, HALO, D_CONV), lambda b, i: (b, jnp.minimum((i + 1) * hb, s // HALO - 1), 0)),
            _full((CONV_WIDTH, D_CONV)), _full((1, D_CONV)), _full((1, D_CONV)), _full((1, D_CONV)),
            _full((D_CONV, D_MODEL)),
        ],
        out_specs=pl.BlockSpec((1, ts, D_MODEL), lambda b, i: (b, i, 0)),
        out_shape=jax.ShapeDtypeStruct((bsz, s, D_MODEL), BF16),
        scratch_shapes=[
            pltpu.VMEM((SUBLANES, ts + 2 * HALO, D_CONV), F32),
            pltpu.VMEM((ts, D_CONV), BF16),
        ],
        compiler_params=_cparams("parallel", "parallel"),
        name="conv_branch",
    )(u, u, u, conv_w, conv_b, ln_g, ln_b, w_out)


def _dft_tables(s):
    n2_len = s // DFT_N1
    k1_per_tile = DFT_N1 // n2_len
    two_pi = 2.0 * jnp.pi
    a = jnp.arange(DFT_N1, dtype=I32)
    ang1 = two_pi * ((a[:, None] * a[None, :]) % DFT_N1).astype(F32) / DFT_N1
    small = jnp.concatenate([jnp.cos(ang1), -jnp.sin(ang1)], axis=0)
    r = jnp.arange(2 * DFT_N1 * SUBLANES, dtype=I32)[:, None]
    c = jnp.arange(DFT_N1 * SUBLANES, dtype=I32)[None, :]
    pick_r = (r // SUBLANES == jnp.arange(2 * DFT_N1, dtype=I32)[None, :]).astype(F32)
    pick_c = (jnp.arange(DFT_N1, dtype=I32)[:, None] == c // SUBLANES).astype(F32)
    rep = jnp.dot(jnp.dot(pick_r, small, precision=lax.Precision.HIGHEST), pick_c, precision=lax.Precision.HIGHEST)
    f1 = jnp.where(r % SUBLANES == c % SUBLANES, rep, 0.0).astype(BF16)
    cs = jnp.stack([jnp.cos(ang1), jnp.sin(ang1)]).astype(BF16)
    n_tiles = s // DFT_N1
    t = jnp.arange(n_tiles, dtype=I32)[:, None, None]
    row = jnp.arange(DFT_N1, dtype=I32)[None, :, None]
    col = jnp.arange(DFT_N1, dtype=I32)[None, None, :]
    k2, k1l_out = row // k1_per_tile, row % k1_per_tile
    k1l_in, n2 = col // n2_len, col % n2_len
    k = t * k1_per_tile + k1l_out + DFT_N1 * k2
    hit = k1l_in == k1l_out
    if k1_per_tile == 1:
        ang_a = two_pi * ((n2 * t) % s).astype(F32) / s
        ang_b = two_pi * ((n2 * k2) % n2_len).astype(F32) / n2_len
        cos2 = jnp.cos(ang_a) * jnp.cos(ang_b) - jnp.sin(ang_a) * jnp.sin(ang_b)
        sin2 = jnp.sin(ang_a) * jnp.cos(ang_b) + jnp.cos(ang_a) * jnp.sin(ang_b)
    else:
        ang2 = two_pi * ((n2 * k) % s).astype(F32) / s
        cos2, sin2 = jnp.cos(ang2), jnp.sin(ang2)
    gr = jnp.where(hit, cos2, 0.0)
    gi = jnp.where(hit, -sin2, 0.0)
    g = jnp.concatenate([jnp.concatenate([gr, -gi], axis=2),
                         jnp.concatenate([gi, gr], axis=2)], axis=1).astype(BF16)
    return f1, g, cs


def _fft1_kernel(x_ref, f_ref, y_ref):
    nb = x_ref.shape[2]
    x = x_ref[0].reshape(DFT_N1 * nb, D_FNET).astype(BF16)
    r = jnp.dot(f_ref[...], x, preferred_element_type=F32)
    y_ref[0] = r.reshape(2, DFT_N1, nb, D_FNET)


def _fft2_kernel(y_ref, g_ref, cs_ref, w_ref, o_ref, *, scale, tiles):
    for t in range(tiles):
        rows = slice(t * DFT_N1, (t + 1) * DFT_N1)
        yb = jnp.concatenate([y_ref[0, 0, rows, :], y_ref[0, 1, rows, :]], axis=0).astype(BF16)
        z = jnp.dot(g_ref[t], yb, preferred_element_type=F32)
        zr = z[:DFT_N1].astype(BF16)
        zi = z[DFT_N1:].astype(BF16)
        parts = []
        for grp in range(FNET_GROUPS):
            sl = slice(grp * FNET_GROUP_DIM, (grp + 1) * FNET_GROUP_DIM)
            parts.append(jnp.dot(zr[:, sl], cs_ref[0], preferred_element_type=F32)
                         + jnp.dot(zi[:, sl], cs_ref[1], preferred_element_type=F32))
        fm = (jnp.concatenate(parts, axis=1) * scale).astype(BF16)
        out = jnp.dot(fm, w_ref[...], preferred_element_type=F32)
        k1_per_tile = o_ref.shape[2] // tiles
        if k1_per_tile == 1:
            o_ref[0, :, t, :] = out
        else:
            o_ref[0, :, t * k1_per_tile:(t + 1) * k1_per_tile, :] = out.reshape(o_ref.shape[1], k1_per_tile, D_MODEL)


def _fourier_branch(f_in, w_out):
    bsz, s, _ = f_in.shape
    n2_len = s // DFT_N1
    k1_per_tile = DFT_N1 // n2_len
    f1, g, cs = _dft_tables(s)
    y = pl.pallas_call(
        _fft1_kernel,
        grid=(bsz, n2_len // SUBLANES),
        in_specs=[pl.BlockSpec((1, DFT_N1, SUBLANES, D_FNET), lambda b, j: (b, 0, j, 0)),
                  _full((2 * DFT_N1 * SUBLANES, DFT_N1 * SUBLANES))],
        out_specs=pl.BlockSpec((1, 2, DFT_N1, SUBLANES, D_FNET), lambda b, j: (b, 0, 0, j, 0)),
        out_shape=jax.ShapeDtypeStruct((bsz, 2, DFT_N1, n2_len, D_FNET), F32),
        compiler_params=_cparams("parallel", "parallel"),
        name="fft_stage1",
    )(f_in.reshape(bsz, DFT_N1, n2_len, D_FNET), f1)
    y = y.reshape(bsz, 2, s, D_FNET)
    scale = float((s * FNET_GROUP_DIM) ** -0.5)
    assert n2_len % SUBLANES == 0 and (k1_per_tile == 1 or k1_per_tile % SUBLANES == 0)
    tiles = SUBLANES if k1_per_tile == 1 else min(FFT2_TILES, n2_len)
    k1_per_step = tiles * k1_per_tile
    yb = pl.pallas_call(
        functools.partial(_fft2_kernel, scale=scale, tiles=tiles),
        grid=(bsz, s // (DFT_N1 * tiles)),
        in_specs=[
            pl.BlockSpec((1, 2, DFT_N1 * tiles, D_FNET), lambda b, t: (b, 0, t, 0)),
            pl.BlockSpec((tiles, 2 * DFT_N1, 2 * DFT_N1), lambda b, t: (t, 0, 0)),
            _full((2, DFT_N1, DFT_N1)),
            _full((D_FNET, D_MODEL)),
        ],
        out_specs=pl.BlockSpec((1, n2_len, k1_per_step, D_MODEL), lambda b, t: (b, 0, t, 0)),
        out_shape=jax.ShapeDtypeStruct((bsz, n2_len, DFT_N1, D_MODEL), F32),
        compiler_params=_cparams("parallel", "parallel"),
        name="fft_stage2",
    )(y, g, cs, w_out)
    return yb.reshape(bsz * s, D_MODEL)


def _mix_kernel(x_ref, ya_ref, yb_ref, cin_ref, ln0g_ref, ln0b_ref, wg_ref, bg_ref, wo_ref, bo_ref,
                ln1g_ref, ln1b_ref, wr_ref, br_ref,
                h1_ref, idx_ref, gate_ref, rank_ref, cnt_ref, carry_ref):
    tm = x_ref.shape[0]
    sub = min(SUB_MIX, tm)

    @pl.when(pl.program_id(0) == 0)
    def _():
        carry_ref[...] = cin_ref[...]

    carry = carry_ref[...]
    eio = lax.broadcasted_iota(I32, (N_EXPERTS, sub), 0)
    before = (lax.broadcasted_iota(I32, (sub, sub), 0) < lax.broadcasted_iota(I32, (sub, sub), 1)).astype(BF16)
    for r0 in range(0, tm, sub):
        rows = slice(r0, r0 + sub)
        h = _layer_norm(x_ref[rows, :], ln0g_ref[...], ln0b_ref[...])
        zg = jnp.dot(h.astype(BF16), wg_ref[...], preferred_element_type=F32) + bg_ref[...]
        m = (_sigmoid(zg[:, :D_MODEL]) * ya_ref[rows, :].astype(F32)
             + _sigmoid(zg[:, D_MODEL:]) * yb_ref[rows, :].astype(F32))
        mix = jnp.dot(m.astype(BF16), wo_ref[...], preferred_element_type=F32) + bo_ref[...]
        h1 = _layer_norm(DEEPNORM_ALPHA * h + mix, ln1g_ref[...], ln1b_ref[...])
        _to_row_tiles(h1_ref, r0, h1)

        h_hi = h1.astype(BF16)
        h_lo = (h1 - h_hi.astype(F32)).astype(BF16)
        hw = jnp.dot(h_hi, wr_ref[...], preferred_element_type=F32)
        lw = jnp.dot(h_lo, wr_ref[:, :LANES], preferred_element_type=F32)
        logits = (hw[:, :LANES] + hw[:, LANES:] + lw).T[:N_EXPERTS] + br_ref[...]
        vals, idxs = [], []
        cur = logits
        for _ in range(TOP_K):
            mx = jnp.max(cur, axis=0, keepdims=True)
            ik = jnp.min(jnp.where(cur == mx, eio, N_EXPERTS), axis=0, keepdims=True)
            vals.append(mx)
            idxs.append(ik)
            cur = jnp.where(eio == ik, -jnp.inf, cur)
        exps = [jnp.exp(v - vals[0]) for v in vals]
        den = exps[0] + exps[1] + exps[2] + exps[3]
        gate_ref[:, rows] = jnp.concatenate([e / den for e in exps], axis=0)
        idx_ref[:, rows] = jnp.concatenate(idxs, axis=0)

        hot = jnp.zeros((N_EXPERTS, sub), F32)
        for ik in idxs:
            hot = hot + (eio == ik).astype(F32)
        prior = jnp.dot(hot.astype(BF16), before, preferred_element_type=F32) + carry[:, 0:1]
        ranks = [jnp.sum(jnp.where(eio == ik, prior, 0.0), axis=0, keepdims=True) for ik in idxs]
        rank_ref[:, rows] = jnp.concatenate(ranks, axis=0).astype(I32)
        carry = carry + jnp.sum(hot, axis=1, keepdims=True)
    carry_ref[...] = carry
    cnt_ref[...] = carry


def _mix_route(x, ya, yb, carry_in, ln0_g, ln0_b, w_gate, b_gate, w_o, b_o, ln1_g, ln1_b, w_rt, b_rt):
    t = x.shape[0]
    tm = TM_MIX
    row = lambda d: pl.BlockSpec((tm, d), lambda i: (i, 0))
    col = lambda r: pl.BlockSpec((r, tm), lambda i: (0, i))
    return pl.pallas_call(
        _mix_kernel,
        grid=(t // tm,),
        in_specs=[
            row(D_MODEL), row(D_MODEL), row(D_MODEL), _full((N_EXPERTS, LANES)),
            _full((1, D_MODEL)), _full((1, D_MODEL)),
            _full((D_MODEL, 2 * D_MODEL)), _full((1, 2 * D_MODEL)),
            _full((D_MODEL, D_MODEL)), _full((1, D_MODEL)),
            _full((1, D_MODEL)), _full((1, D_MODEL)),
            _full((D_MODEL, 2 * LANES)), _full((N_EXPERTS, 1)),
        ],
        out_specs=[pl.BlockSpec((tm * SUBLANES, LANES), lambda i: (i, 0)), col(TOP_K), col(TOP_K), col(TOP_K),
                   _full((N_EXPERTS, LANES))],
        out_shape=[
            jax.ShapeDtypeStruct((t * SUBLANES, LANES), F32),
            jax.ShapeDtypeStruct((TOP_K, t), I32),
            jax.ShapeDtypeStruct((TOP_K, t), F32),
            jax.ShapeDtypeStruct((TOP_K, t), I32),
            jax.ShapeDtypeStruct((N_EXPERTS, LANES), F32),
        ],
        scratch_shapes=[pltpu.VMEM((N_EXPERTS, LANES), F32)],
        compiler_params=_cparams("arbitrary"),
        name="mix_route",
    )(x, ya, yb, carry_in, ln0_g, ln0_b, w_gate, b_gate, w_o, b_o, ln1_g, ln1_b, w_rt, b_rt)


def _row_copy(src_ref, src_row, dst_ref, dst_row, sem):
    return pltpu.make_async_copy(src_ref.at[src_row], dst_ref.at[dst_row], sem)


def _fetch_slots(slots_hbm, slots_smem, sem, step, slot):
    n = slots_hbm.shape[1]
    return pltpu.make_async_copy(slots_hbm.at[step], slots_smem.at[pl.ds(pl.multiple_of(slot * n, n), n)],
                                 sem.at[slot])


def _pad_fill(h_ref, xs_out, pad_smem, pad_sem, n_pad_runs, wait):
    def go(cp):
        if wait:
            cp.wait()
        else:
            cp.start()

    def run(j, carry):
        start = pad_smem[2 * j]
        length = pad_smem[2 * j + 1]
        for bit in range(BM_MOE.bit_length() - 1):
            size = 1 << bit

            @pl.when((length & size) != 0)
            def _():
                off = start + (length & ~(2 * size - 1))
                go(pltpu.make_async_copy(h_ref.at[pl.ds(0, size)], xs_out.at[pl.ds(off, size)], pad_sem))
        return carry

    lax.fori_loop(0, n_pad_runs, run, 0)

    piece = min(h_ref.shape[0], BM_MOE)

    def dead_block(j, carry):
        for part in range(BM_MOE // piece):
            off = j * BM_MOE + part * piece
            go(pltpu.make_async_copy(h_ref.at[pl.ds(0, piece)], xs_out.at[pl.ds(off, piece)], pad_sem))
        return carry

    lax.fori_loop(pad_smem[2 * n_pad_runs], xs_out.shape[0] // BM_MOE, dead_block, 0)


def _dispatch_kernel(*refs, tiles):
    n_sets = len(tiles)
    h_refs = refs[:n_sets]
    slots_hbm, pad_hbm, xs_out, slots_smem, pad_smem, slot_sem, pad_sem, row_sem = refs[n_sets:]
    i = pl.program_id(0)
    n = pl.num_programs(0)
    tm = h_refs[0].shape[0]
    cur = lax.rem(i, 2)

    @pl.when(i == 0)
    def _():
        _fetch_slots(slots_hbm, slots_smem, slot_sem, 0, 0).start()
        fetch = pltpu.make_async_copy(pad_hbm, pad_smem, pad_sem)
        fetch.start()
        fetch.wait()
        _pad_fill(h_refs[0], xs_out, pad_smem, pad_sem, N_EXPERTS, wait=False)
        _pad_fill(h_refs[0], xs_out, pad_smem, pad_sem, N_EXPERTS, wait=True)

    @pl.when(i + 1 < n)
    def _():
        _fetch_slots(slots_hbm, slots_smem, slot_sem, i + 1, 1 - cur).start()

    _fetch_slots(slots_hbm, slots_smem, slot_sem, i, cur).wait()
    base = cur * (TOP_K * tm)

    first = 0
    for h_ref, count in zip(h_refs, tiles):
        @pl.when(jnp.logical_and(i >= first, i < first + count))
        def _(h_ref=h_ref):
            def issue(r, carry):
                for k in range(TOP_K):
                    _row_copy(h_ref, r, xs_out, slots_smem[base + k * tm + r], row_sem).start(priority=k % 2)
                return carry

            lax.fori_loop(0, tm, issue, 0, unroll=ISSUE_UNROLL)
            for _ in range(TOP_K):
                pltpu.make_async_copy(h_ref, xs_out.at[pl.ds(0, tm)], row_sem).wait()
        first += count


def _dispatch(h1s, slots, pad_runs, n_rows):
    tm = TM_ROWS
    h1s = [h.reshape(-1, SUBLANES, LANES) for h in h1s]
    tiles = [h.shape[0] // tm for h in h1s]
    firsts = [sum(tiles[:j]) for j in range(len(tiles))]

    def tile_of(first, count):
        return lambda i: (jnp.clip(i - first, 0, count - 1), 0, 0)

    return pl.pallas_call(
        functools.partial(_dispatch_kernel, tiles=tuple(tiles)),
        grid=(sum(tiles),),
        in_specs=[pl.BlockSpec((tm, SUBLANES, LANES), tile_of(f, c)) for f, c in zip(firsts, tiles)]
        + [pl.BlockSpec(memory_space=pl.ANY)] * 2,
        out_specs=pl.BlockSpec(memory_space=pl.ANY),
        out_shape=jax.ShapeDtypeStruct((n_rows, SUBLANES, LANES), F32),
        scratch_shapes=[
            pltpu.SMEM((2 * TOP_K * tm,), I32),
            pltpu.SMEM((pad_runs.shape[0],), I32),
            pltpu.SemaphoreType.DMA((2,)),
            pltpu.SemaphoreType.DMA,
            pltpu.SemaphoreType.DMA,
        ],
        compiler_params=_cparams("arbitrary"),
        name="dispatch",
    )(*h1s, slots, pad_runs)


def _moe_kernel(blk_e_ref, nb_ref, x_ref, wgu_ref, bgu_ref, wd_ref, bd_ref, o_ref, wgu_b, wd_b):
    i = pl.program_id(0)
    is_live = i < nb_ref[0]

    @pl.when(jnp.logical_not(is_live))
    def _():
        o_ref[...] = jnp.zeros(o_ref.shape, F32)

    @pl.when(jnp.logical_or(i == 0, blk_e_ref[i] != blk_e_ref[jnp.maximum(i - 1, 0)]))
    def _():
        for r in range(0, D_MODEL, CAST_ROWS):
            wgu_b[r:r + CAST_ROWS, :] = wgu_ref[0, r:r + CAST_ROWS, :].astype(BF16)
        for r in range(0, D_FF, CAST_ROWS):
            wd_b[r:r + CAST_ROWS, :] = wd_ref[0, r:r + CAST_ROWS, :].astype(BF16)

    @pl.when(is_live)
    def _():
        x = _from_row_tiles(x_ref, 0, BM_MOE).astype(BF16)
        acc = jnp.zeros((BM_MOE, D_MODEL), F32) + bd_ref[0]
        for c in range(D_FF // FF_CHUNK):
            lo = c * FF_CHUNK
            g = jnp.dot(x, wgu_b[:, lo:lo + FF_CHUNK], preferred_element_type=F32) + bgu_ref[0, :, lo:lo + FF_CHUNK]
            u = (jnp.dot(x, wgu_b[:, D_FF + lo:D_FF + lo + FF_CHUNK], preferred_element_type=F32)
                 + bgu_ref[0, :, D_FF + lo:D_FF + lo + FF_CHUNK])
            g = jnp.minimum(g, SWIGLU_LIMIT)
            u = jnp.clip(u, -SWIGLU_LIMIT, SWIGLU_LIMIT)
            act = (u + 1.0) * (g * _sigmoid(g * SWIGLU_ALPHA))
            acc = acc + jnp.dot(act.astype(BF16), wd_b[lo:lo + FF_CHUNK, :], preferred_element_type=F32)
        _to_row_tiles(o_ref, 0, acc)


def _moe_blocks(xs, blk_e, nb_used, w_gu, b_gu, w_down, b_down):
    p = xs.shape[0] // SUBLANES
    n_blocks = p // BM_MOE
    live = lambda i, nb: jnp.minimum(i, nb[0] - 1)
    grid_spec = pltpu.PrefetchScalarGridSpec(
        num_scalar_prefetch=2,
        grid=(n_blocks,),
        in_specs=[
            pl.BlockSpec((BM_MOE * SUBLANES, LANES), lambda i, be, nb: (live(i, nb), 0)),
            pl.BlockSpec((1, D_MODEL, 2 * D_FF), lambda i, be, nb: (be[i], 0, 0)),
            pl.BlockSpec((1, 1, 2 * D_FF), lambda i, be, nb: (be[i], 0, 0)),
            pl.BlockSpec((1, D_FF, D_MODEL), lambda i, be, nb: (be[i], 0, 0)),
            pl.BlockSpec((1, 1, D_MODEL), lambda i, be, nb: (be[i], 0, 0)),
        ],
        out_specs=pl.BlockSpec((BM_MOE * SUBLANES, LANES), lambda i, be, nb: (i, 0)),
        scratch_shapes=[pltpu.VMEM((D_MODEL, 2 * D_FF), BF16), pltpu.VMEM((D_FF, D_MODEL), BF16)],
    )
    return pl.pallas_call(
        _moe_kernel,
        grid_spec=grid_spec,
        out_shape=jax.ShapeDtypeStruct((p * SUBLANES, LANES), F32),
        compiler_params=pltpu.CompilerParams(dimension_semantics=("arbitrary",), vmem_limit_bytes=VMEM_LIMIT_MOE),
        name="moe_experts",
    )(blk_e, nb_used, xs, w_gu, b_gu, w_down, b_down)


def _final_kernel(h_ref, p_ref, gate_ref, slots_hbm, y_hbm, wpg_ref, bpg_ref, wpp_ref, g_ref, b_ref, o_ref,
                  slots_smem, rows_ref, slot_sem, row_sem):
    i = pl.program_id(0)
    n = pl.num_programs(0)
    tm = h_ref.shape[0] // SUBLANES

    def gather(r, buf, base):
        for k in range(TOP_K):
            slot = slots_smem[base + k * tm + r]
            src = y_hbm.at[pl.ds(pl.multiple_of(slot * SUBLANES, SUBLANES), SUBLANES)]
            dst = rows_ref.at[buf, k, pl.ds(pl.multiple_of(r * SUBLANES, SUBLANES), SUBLANES)]
            pltpu.make_async_copy(src, dst, row_sem.at[buf]).start(priority=k % 2)

    def wait_rows(buf):
        for k in range(TOP_K):
            pltpu.make_async_copy(y_hbm.at[pl.ds(0, tm * SUBLANES)], rows_ref.at[buf, k], row_sem.at[buf]).wait()

    @pl.when(i == 0)
    def _():
        first = _fetch_slots(slots_hbm, slots_smem, slot_sem, 0, 0)
        first.start()
        first.wait()

        def issue(r, carry):
            gather(r, 0, 0)
            return carry

        lax.fori_loop(0, tm, issue, 0, unroll=ISSUE_UNROLL)
        _fetch_slots(slots_hbm, slots_smem, slot_sem, 1, 1).start()

    def step(cur):
        nxt = 1 - cur
        _fetch_slots(slots_hbm, slots_smem, slot_sem, i + 1, nxt).wait()

        @pl.when(i + 2 <= n)
        def _():
            _fetch_slots(slots_hbm, slots_smem, slot_sem, i + 2, cur).start()

        for r in range(tm):
            gather(r, nxt, nxt * (TOP_K * tm))

        h = _from_row_tiles(h_ref, 0, tm)
        ple = (_sigmoid(jnp.dot(h.astype(BF16), wpg_ref[...], preferred_element_type=F32) + bpg_ref[...])
               * jnp.dot(p_ref[...].astype(BF16), wpp_ref[...], preferred_element_type=F32))

        wait_rows(cur)
        gates = gate_ref[...]
        ffn = gates[:, 0:1] * _from_row_tiles(rows_ref.at[cur, 0], 0, tm)
        for k in range(1, TOP_K):
            ffn = ffn + gates[:, k:k + 1] * _from_row_tiles(rows_ref.at[cur, k], 0, tm)
        o_ref[...] = _layer_norm(DEEPNORM_ALPHA * h + (ffn + ple), g_ref[...], b_ref[...])

        @pl.when(i == n - 1)
        def _():
            wait_rows(nxt)

    for parity in range(2):
        pl.when(lax.rem(i, 2) == parity)(functools.partial(step, parity))


def _combine_final(h1, p, gates, slots, y, w_pg, b_pg, w_pp, ln_g, ln_b):
    t = h1.shape[0] // SUBLANES
    tm = TM_COMBINE
    slots = jnp.concatenate([slots, slots[-1:]], axis=0)
    return pl.pallas_call(
        _final_kernel,
        grid=(t // tm,),
        in_specs=[
            pl.BlockSpec((tm * SUBLANES, LANES), lambda i: (i, 0)),
            pl.BlockSpec((tm, D_PLE), lambda i: (i, 0)),
            pl.BlockSpec((tm, TOP_K), lambda i: (i, 0)),
            pl.BlockSpec(memory_space=pl.ANY),
            pl.BlockSpec(memory_space=pl.ANY),
            _full((D_MODEL, D_MODEL)), _full((1, D_MODEL)), _full((D_PLE, D_MODEL)),
            _full((1, D_MODEL)), _full((1, D_MODEL)),
        ],
        out_specs=pl.BlockSpec((tm, D_MODEL), lambda i: (i, 0)),
        out_shape=jax.ShapeDtypeStruct((t, D_MODEL), F32),
        scratch_shapes=[
            pltpu.SMEM((2 * TOP_K * tm,), I32),
            pltpu.VMEM((2, TOP_K, tm * SUBLANES, LANES), F32),
            pltpu.SemaphoreType.DMA((2,)),
            pltpu.SemaphoreType.DMA((2,)),
        ],
        compiler_params=_cparams("arbitrary"),
        name="combine_final",
    )(h1, p, gates, slots, y, w_pg, b_pg, w_pp, ln_g, ln_b)


def _tile_slots(dest, tm):
    t = dest.shape[1]
    return dest.reshape(TOP_K, t // tm, tm).transpose(1, 0, 2).reshape(t // tm, TOP_K * tm)


def kernel(x_prompt, x_sample, p_prompt, p_sample, ln0_g, ln0_b, w_in, b_in, conv_w, conv_b, lnc_g, lnc_b, w_conv_out, w_fnet_out, w_o, b_o, ln1_g, ln1_b, w_router, b_router, w_gu, b_gu, w_down, b_down, w_pg, b_pg, w_pp, ln2_g, ln2_b):
    row = lambda v: v.reshape(1, -1).astype(F32)
    n_branch = 2 * D_CONV + D_FNET
    w_in_b = w_in[0].astype(BF16)
    w_branch, w_gate = w_in_b[:, :n_branch], w_in_b[:, n_branch:]
    b_branch, b_gate = row(b_in[0, :n_branch]), row(b_in[0, n_branch:])
    w_conv_out_b = w_conv_out[0].astype(BF16)
    w_fnet_out_b = w_fnet_out[0].astype(BF16)
    w_o_b = w_o[0].astype(BF16)
    w_rt_f = jnp.pad(w_router[0].astype(F32), ((0, 0), (0, LANES - N_EXPERTS)))
    w_rt_hi = w_rt_f.astype(BF16)
    w_rt = jnp.concatenate([w_rt_hi, (w_rt_f - w_rt_hi.astype(F32)).astype(BF16)], axis=1)
    b_rt = b_router[0].reshape(N_EXPERTS, 1).astype(F32)
    b_gu_r = b_gu[0].reshape(N_EXPERTS, 1, 2 * D_FF).astype(F32)
    b_down_r = b_down[0].reshape(N_EXPERTS, 1, D_MODEL).astype(F32)
    w_pg_b = w_pg[0].astype(BF16)
    w_pp_b = w_pp[0].astype(BF16)

    carry = jnp.zeros((N_EXPERTS, LANES), F32)
    routed = []
    for x, p in ((x_prompt, p_prompt[0]), (x_sample, p_sample[0])):
        bsz, s, _ = x.shape
        xt = x.reshape(bsz * s, D_MODEL)
        u, f_in = _inproj(xt, row(ln0_g), row(ln0_b), w_branch, b_branch)
        ya = _conv_branch(u.reshape(bsz, s, D_CONV), conv_w[0].astype(F32), row(conv_b[0]), row(lnc_g[0]),
                          row(lnc_b[0]), w_conv_out_b).reshape(bsz * s, D_MODEL)
        yb = _fourier_branch(f_in.reshape(bsz, s, D_FNET), w_fnet_out_b)
        h1, idx, gates, rank, carry = _mix_route(
            xt, ya, yb, carry, row(ln0_g), row(ln0_b), w_gate, b_gate, w_o_b, row(b_o[0]),
            row(ln1_g[0]), row(ln1_b[0]), w_rt, b_rt)
        routed.append((h1, idx, gates, rank, p.reshape(bsz * s, D_PLE), (bsz, s)))

    n_assign = sum(r[1].shape[1] for r in routed) * TOP_K
    n_blocks = (n_assign + N_EXPERTS * (BM_MOE - 1)) // BM_MOE
    counts = carry[:, 0].astype(I32)
    padded = ((counts + BM_MOE - 1) // BM_MOE) * BM_MOE
    ends = jnp.cumsum(padded)
    start = ends - padded
    nb_used = (ends[-1] // BM_MOE).reshape(1).astype(I32)
    blk = jnp.minimum(jnp.arange(n_blocks, dtype=I32), nb_used[0] - 1)
    blk_e = jnp.minimum(jnp.sum((ends[None, :] <= (blk * BM_MOE)[:, None]).astype(I32), axis=1), N_EXPERTS - 1)
    pad_runs = jnp.concatenate([jnp.stack([start + counts, padded - counts], axis=1).reshape(-1), nb_used,
                                jnp.zeros((LANES - 2 * N_EXPERTS - 1,), I32)]).astype(I32)

    slot_tables = []
    dispatch_slots = []
    for _, idx, _, rank, _, _ in routed:
        first = jnp.sum(jnp.where(idx[None] == jnp.arange(N_EXPERTS, dtype=I32)[:, None, None],
                                  start[:, None, None], 0), axis=0)
        slot_tables.append(_tile_slots(first + rank, TM_COMBINE))
        dispatch_slots.append(_tile_slots(first + rank, TM_ROWS))
    xs = _dispatch([r[0] for r in routed], jnp.concatenate(dispatch_slots, axis=0), pad_runs, n_blocks * BM_MOE)

    y = _moe_blocks(xs.reshape(-1, LANES), blk_e, nb_used, w_gu.reshape(N_EXPERTS, D_MODEL, 2 * D_FF), b_gu_r,
                    w_down.reshape(N_EXPERTS, D_FF, D_MODEL), b_down_r)

    outs = []
    for (h1, _, gates, _, p, (bsz, s)), slots in zip(routed, slot_tables):
        o = _combine_final(h1, p, gates.T, slots, y, w_pg_b, row(b_pg[0]), w_pp_b, row(ln2_g[0]), row(ln2_b[0]))
        outs.append(o.reshape(bsz, s, D_MODEL))
    return tuple(outs)
```

```python
import functools

import jax
import jax.numpy as jnp
from jax import lax
from jax.experimental import pallas as pl
from jax.experimental.pallas import tpu as pltpu

F32 = jnp.float32
BF16 = jnp.bfloat16
I32 = jnp.int32

D_MODEL = 1024
D_CONV = D_MODEL // 2
CONV_WIDTH = 31
CONV_PAD = CONV_WIDTH // 2
D_FNET = D_MODEL // 2
FNET_GROUPS = 4
FNET_GROUP_DIM = D_FNET // FNET_GROUPS
D_PLE = 256
N_EXPERTS = 32
TOP_K = 4
D_FF = D_MODEL
SWIGLU_ALPHA = 1.702
SWIGLU_LIMIT = 7.0
LN_EPS = 1e-5
DEPTH = 1
DEEPNORM_ALPHA = (2 * DEPTH) ** 0.25

LANES = 128
SUBLANES = 8
DFT_N1 = 128
HALO = 16

TM_INPROJ = 1024
TS_CONV = 512
RC_CONV = 128
TM_MIX = 512
SUB_MIX = 512
FFT2_TILES = 4
TM_ROWS = 1024
TM_COMBINE = 512
BM_MOE = 512
FF_CHUNK = 1024
ISSUE_UNROLL = 8
CAST_ROWS = 128
VMEM_LIMIT = 48 * 1024 * 1024
VMEM_LIMIT_MOE = 56 * 1024 * 1024


def _cparams(*sem):
    return pltpu.CompilerParams(dimension_semantics=sem, vmem_limit_bytes=VMEM_LIMIT)


def _layer_norm(x, g, b):
    mu = jnp.mean(x, axis=-1, keepdims=True)
    xc = x - mu
    var = jnp.mean(xc * xc, axis=-1, keepdims=True)
    return xc * lax.rsqrt(var + LN_EPS) * g + b


def _sigmoid(x):
    return 0.5 * jnp.tanh(0.5 * x) + 0.5


def _full(shape):
    return pl.BlockSpec(shape, lambda *_: (0,) * len(shape))


def _to_row_tiles(ref, row0, x):
    rows = x.shape[0]
    for j in range(D_MODEL // LANES):
        ref[pl.ds(row0 * SUBLANES + j, rows, stride=SUBLANES), :] = x[:, j * LANES:(j + 1) * LANES]


def _from_row_tiles(ref, row0, rows):
    return jnp.concatenate([ref[pl.ds(row0 * SUBLANES + j, rows, stride=SUBLANES), :]
                            for j in range(D_MODEL // LANES)], axis=1)


def _inproj_kernel(x_ref, g_ref, b_ref, w_ref, bias_ref, u_ref, f_ref):
    h = _layer_norm(x_ref[...], g_ref[...], b_ref[...])
    z = jnp.dot(h.astype(BF16), w_ref[...], preferred_element_type=F32) + bias_ref[...]
    u_ref[...] = (z[:, :D_CONV] * _sigmoid(z[:, D_CONV:2 * D_CONV])).astype(BF16)
    f_ref[...] = z[:, 2 * D_CONV:]


def _inproj(x, ln_g, ln_b, w, bias):
    t = x.shape[0]
    n_out = w.shape[1]
    return pl.pallas_call(
        _inproj_kernel,
        grid=(t // TM_INPROJ,),
        in_specs=[
            pl.BlockSpec((TM_INPROJ, D_MODEL), lambda i: (i, 0)),
            _full((1, D_MODEL)), _full((1, D_MODEL)),
            _full((D_MODEL, n_out)), _full((1, n_out)),
        ],
        out_specs=[
            pl.BlockSpec((TM_INPROJ, D_CONV), lambda i: (i, 0)),
            pl.BlockSpec((TM_INPROJ, D_FNET), lambda i: (i, 0)),
        ],
        out_shape=[jax.ShapeDtypeStruct((t, D_CONV), BF16), jax.ShapeDtypeStruct((t, D_FNET), F32)],
        compiler_params=_cparams("parallel"),
        name="inproj",
    )(x, ln_g, ln_b, w, bias)


def _conv_kernel(prev_ref, cur_ref, next_ref, cw_ref, cb_ref, g_ref, b_ref, wout_ref, ya_ref,
                 ext_ref, act_ref):
    i = pl.program_id(1)
    last = pl.num_programs(1) - 1
    ts = cur_ref.shape[1]
    ext_rows = ts + 2 * HALO
    zero = jnp.zeros((HALO, D_CONV), F32)
    ext_ref[0, 0:HALO, :] = jnp.where(i > 0, prev_ref[0].astype(F32), zero)
    ext_ref[0, HALO:HALO + ts, :] = cur_ref[0].astype(F32)
    ext_ref[0, HALO + ts:ext_rows, :] = jnp.where(i < last, next_ref[0].astype(F32), zero)
    for r in range(1, SUBLANES):
        ext_ref[r, 0:ext_rows - SUBLANES, :] = ext_ref[0, r:r + ext_rows - SUBLANES, :]

    cb = cb_ref[...]
    g = g_ref[...]
    b = b_ref[...]

    def chunk(c, carry):
        r0 = pl.multiple_of(c * RC_CONV, RC_CONV)
        acc = jnp.zeros((RC_CONV, D_CONV), F32) + cb
        for j in range(CONV_WIDTH):
            off = j + HALO - CONV_PAD
            q, r = divmod(off, SUBLANES)
            acc = acc + cw_ref[j:j + 1, :] * ext_ref[r, pl.ds(r0 + q * SUBLANES, RC_CONV), :]
        y = _layer_norm(acc, g, b)
        act_ref[pl.ds(r0, RC_CONV), :] = (y * _sigmoid(y)).astype(BF16)
        return carry

    lax.fori_loop(0, ts // RC_CONV, chunk, 0)
    ya_ref[0] = jnp.dot(act_ref[...], wout_ref[...], preferred_element_type=F32).astype(BF16)


def _conv_branch(u, conv_w, conv_b, ln_g, ln_b, w_out):
    bsz, s, _ = u.shape
    ts = TS_CONV
    nb = s // ts
    hb = ts // HALO
    return pl.pallas_call(
        _conv_kernel,
        grid=(bsz, nb),
        in_specs=[
            pl.BlockSpec((1, HALO, D_CONV), lambda b, i: (b, jnp.maximum(i * hb - 1, 0), 0)),
            pl.BlockSpec((1, ts, D_CONV), lambda b, i: (b, i, 0)),
            pl.BlockSpec((1, HALO, D_CONV), lambda b, i: (b, jnp.minimum((i + 1) * hb, s // HALO - 1), 0)),
            _full((CONV_WIDTH, D_CONV)), _full((1, D_CONV)), _full((1, D_CONV)), _full((1, D_CONV)),
            _full((D_CONV, D_MODEL)),
        ],
        out_specs=pl.BlockSpec((1, ts, D_MODEL), lambda b, i: (b, i, 0)),
        out_shape=jax.ShapeDtypeStruct((bsz, s, D_MODEL), BF16),
        scratch_shapes=[
            pltpu.VMEM((SUBLANES, ts + 2 * HALO, D_CONV), F32),
            pltpu.VMEM((ts, D_CONV), BF16),
        ],
        compiler_params=_cparams("parallel", "parallel"),
        name="conv_branch",
    )(u, u, u, conv_w, conv_b, ln_g, ln_b, w_out)


def _dft_tables(s):
    n2_len = s // DFT_N1
    k1_per_tile = DFT_N1 // n2_len
    two_pi = 2.0 * jnp.pi
    a = jnp.arange(DFT_N1, dtype=I32)
    ang1 = two_pi * ((a[:, None] * a[None, :]) % DFT_N1).astype(F32) / DFT_N1
    small = jnp.concatenate([jnp.cos(ang1), -jnp.sin(ang1)], axis=0)
    r = jnp.arange(2 * DFT_N1 * SUBLANES, dtype=I32)[:, None]
    c = jnp.arange(DFT_N1 * SUBLANES, dtype=I32)[None, :]
    pick_r = (r // SUBLANES == jnp.arange(2 * DFT_N1, dtype=I32)[None, :]).astype(F32)
    pick_c = (jnp.arange(DFT_N1, dtype=I32)[:, None] == c // SUBLANES).astype(F32)
    rep = jnp.dot(jnp.dot(pick_r, small, precision=lax.Precision.HIGHEST), pick_c, precision=lax.Precision.HIGHEST)
    f1 = jnp.where(r % SUBLANES == c % SUBLANES, rep, 0.0).astype(BF16)
    cs = jnp.stack([jnp.cos(ang1), jnp.sin(ang1)]).astype(BF16)
    n_tiles = s // DFT_N1
    t = jnp.arange(n_tiles, dtype=I32)[:, None, None]
    row = jnp.arange(DFT_N1, dtype=I32)[None, :, None]
    col = jnp.arange(DFT_N1, dtype=I32)[None, None, :]
    k2, k1l_out = row // k1_per_tile, row % k1_per_tile
    k1l_in, n2 = col // n2_len, col % n2_len
    k = t * k1_per_tile + k1l_out + DFT_N1 * k2
    hit = k1l_in == k1l_out
    if k1_per_tile == 1:
        ang_a = two_pi * ((n2 * t) % s).astype(F32) / s
        ang_b = two_pi * ((n2 * k2) % n2_len).astype(F32) / n2_len
        cos2 = jnp.cos(ang_a) * jnp.cos(ang_b) - jnp.sin(ang_a) * jnp.sin(ang_b)
        sin2 = jnp.sin(ang_a) * jnp.cos(ang_b) + jnp.cos(ang_a) * jnp.sin(ang_b)
    else:
        ang2 = two_pi * ((n2 * k) % s).astype(F32) / s
        cos2, sin2 = jnp.cos(ang2), jnp.sin(ang2)
    gr = jnp.where(hit, cos2, 0.0)
    gi = jnp.where(hit, -sin2, 0.0)
    g = jnp.concatenate([jnp.concatenate([gr, -gi], axis=2),
                         jnp.concatenate([gi, gr], axis=2)], axis=1).astype(BF16)
    return f1, g, cs


def _fft1_kernel(x_ref, f_ref, y_ref):
    nb = x_ref.shape[2]
    x = x_ref[0].reshape(DFT_N1 * nb, D_FNET).astype(BF16)
    r = jnp.dot(f_ref[...], x, preferred_element_type=F32)
    y_ref[0] = r.reshape(2, DFT_N1, nb, D_FNET)


def _fft2_kernel(y_ref, g_ref, cs_ref, w_ref, o_ref, *, scale, tiles):
    for t in range(tiles):
        rows = slice(t * DFT_N1, (t + 1) * DFT_N1)
        yb = jnp.concatenate([y_ref[0, 0, rows, :], y_ref[0, 1, rows, :]], axis=0).astype(BF16)
        z = jnp.dot(g_ref[t], yb, preferred_element_type=F32)
        zr = z[:DFT_N1].astype(BF16)
        zi = z[DFT_N1:].astype(BF16)
        parts = []
        cs2 = cs_ref[...].reshape(2 * DFT_N1, DFT_N1)
        for grp in range(FNET_GROUPS):
            sl = slice(grp * FNET_GROUP_DIM, (grp + 1) * FNET_GROUP_DIM)
            parts.append(jnp.dot(jnp.concatenate([zr[:, sl], zi[:, sl]], axis=1), cs2, preferred_element_type=F32))
        fm = (jnp.concatenate(parts, axis=1) * scale).astype(BF16)
        out = jnp.dot(fm, w_ref[...], preferred_element_type=F32)
        k1_per_tile = o_ref.shape[2] // tiles
        if k1_per_tile == 1:
            o_ref[0, :, t, :] = out
        else:
            o_ref[0, :, t * k1_per_tile:(t + 1) * k1_per_tile, :] = out.reshape(o_ref.shape[1], k1_per_tile, D_MODEL)


def _fourier_branch(f_in, w_out):
    bsz, s, _ = f_in.shape
    n2_len = s // DFT_N1
    k1_per_tile = DFT_N1 // n2_len
    f1, g, cs = _dft_tables(s)
    y = pl.pallas_call(
        _fft1_kernel,
        grid=(bsz, n2_len // SUBLANES),
        in_specs=[pl.BlockSpec((1, DFT_N1, SUBLANES, D_FNET), lambda b, j: (b, 0, j, 0)),
                  _full((2 * DFT_N1 * SUBLANES, DFT_N1 * SUBLANES))],
        out_specs=pl.BlockSpec((1, 2, DFT_N1, SUBLANES, D_FNET), lambda b, j: (b, 0, 0, j, 0)),
        out_shape=jax.ShapeDtypeStruct((bsz, 2, DFT_N1, n2_len, D_FNET), F32),
        compiler_params=_cparams("parallel", "parallel"),
        name="fft_stage1",
    )(f_in.reshape(bsz, DFT_N1, n2_len, D_FNET), f1)
    y = y.reshape(bsz, 2, s, D_FNET)
    scale = float((s * FNET_GROUP_DIM) ** -0.5)
    assert n2_len % SUBLANES == 0 and (k1_per_tile == 1 or k1_per_tile % SUBLANES == 0)
    tiles = SUBLANES if k1_per_tile == 1 else min(FFT2_TILES, n2_len)
    k1_per_step = tiles * k1_per_tile
    yb = pl.pallas_call(
        functools.partial(_fft2_kernel, scale=scale, tiles=tiles),
        grid=(bsz, s // (DFT_N1 * tiles)),
        in_specs=[
            pl.BlockSpec((1, 2, DFT_N1 * tiles, D_FNET), lambda b, t: (b, 0, t, 0)),
            pl.BlockSpec((tiles, 2 * DFT_N1, 2 * DFT_N1), lambda b, t: (t, 0, 0)),
            _full((2, DFT_N1, DFT_N1)),
            _full((D_FNET, D_MODEL)),
        ],
        out_specs=pl.BlockSpec((1, n2_len, k1_per_step, D_MODEL), lambda b, t: (b, 0, t, 0)),
        out_shape=jax.ShapeDtypeStruct((bsz, n2_len, DFT_N1, D_MODEL), F32),
        compiler_params=_cparams("parallel", "parallel"),
        name="fft_stage2",
    )(y, g, cs, w_out)
    return yb.reshape(bsz * s, D_MODEL)


def _mix_kernel(x_ref, ya_ref, yb_ref, cin_ref, ln0g_ref, ln0b_ref, wg_ref, bg_ref, wo_ref, bo_ref,
                ln1g_ref, ln1b_ref, wr_ref, br_ref,
                h1_ref, idx_ref, gate_ref, rank_ref, cnt_ref, carry_ref):
    tm = x_ref.shape[0]
    sub = min(SUB_MIX, tm)

    @pl.when(pl.program_id(0) == 0)
    def _():
        carry_ref[...] = cin_ref[...]

    carry = carry_ref[...]
    eio = lax.broadcasted_iota(I32, (N_EXPERTS, sub), 0)
    before = (lax.broadcasted_iota(I32, (sub, sub), 0) < lax.broadcasted_iota(I32, (sub, sub), 1)).astype(BF16)
    for r0 in range(0, tm, sub):
        rows = slice(r0, r0 + sub)
        h = _layer_norm(x_ref[rows, :], ln0g_ref[...], ln0b_ref[...])
        zg = jnp.dot(h.astype(BF16), wg_ref[...], preferred_element_type=F32) + bg_ref[...]
        m = (_sigmoid(zg[:, :D_MODEL]) * ya_ref[rows, :].astype(F32)
             + _sigmoid(zg[:, D_MODEL:]) * yb_ref[rows, :].astype(F32))
        mix = jnp.dot(m.astype(BF16), wo_ref[...], preferred_element_type=F32) + bo_ref[...]
        h1 = _layer_norm(DEEPNORM_ALPHA * h + mix, ln1g_ref[...], ln1b_ref[...])
        _to_row_tiles(h1_ref, r0, h1)

        h_hi = h1.astype(BF16)
        h_lo = (h1 - h_hi.astype(F32)).astype(BF16)
        hw = jnp.dot(h_hi, wr_ref[...], preferred_element_type=F32)
        lw = jnp.dot(h_lo, wr_ref[:, :LANES], preferred_element_type=F32)
        logits = (hw[:, :LANES] + hw[:, LANES:] + lw).T[:N_EXPERTS] + br_ref[...]
        vals, idxs = [], []
        cur = logits
        for _ in range(TOP_K):
            mx = jnp.max(cur, axis=0, keepdims=True)
            ik = jnp.min(jnp.where(cur == mx, eio, N_EXPERTS), axis=0, keepdims=True)
            vals.append(mx)
            idxs.append(ik)
            cur = jnp.where(eio == ik, -jnp.inf, cur)
        exps = [jnp.exp(v - vals[0]) for v in vals]
        den = exps[0] + exps[1] + exps[2] + exps[3]
        gate_ref[:, rows] = jnp.concatenate([e / den for e in exps], axis=0)
        idx_ref[:, rows] = jnp.concatenate(idxs, axis=0)

        hot = jnp.zeros((N_EXPERTS, sub), F32)
        for ik in idxs:
            hot = hot + (eio == ik).astype(F32)
        prior = jnp.dot(hot.astype(BF16), before, preferred_element_type=F32) + carry[:, 0:1]
        ranks = [jnp.sum(jnp.where(eio == ik, prior, 0.0), axis=0, keepdims=True) for ik in idxs]
        rank_ref[:, rows] = jnp.concatenate(ranks, axis=0).astype(I32)
        carry = carry + jnp.sum(hot, axis=1, keepdims=True)
    carry_ref[...] = carry
    cnt_ref[...] = carry


def _mix_route(x, ya, yb, carry_in, ln0_g, ln0_b, w_gate, b_gate, w_o, b_o, ln1_g, ln1_b, w_rt, b_rt):
    t = x.shape[0]
    tm = TM_MIX
    row = lambda d: pl.BlockSpec((tm, d), lambda i: (i, 0))
    col = lambda r: pl.BlockSpec((r, tm), lambda i: (0, i))
    return pl.pallas_call(
        _mix_kernel,
        grid=(t // tm,),
        in_specs=[
            row(D_MODEL), row(D_MODEL), row(D_MODEL), _full((N_EXPERTS, LANES)),
            _full((1, D_MODEL)), _full((1, D_MODEL)),
            _full((D_MODEL, 2 * D_MODEL)), _full((1, 2 * D_MODEL)),
            _full((D_MODEL, D_MODEL)), _full((1, D_MODEL)),
            _full((1, D_MODEL)), _full((1, D_MODEL)),
            _full((D_MODEL, 2 * LANES)), _full((N_EXPERTS, 1)),
        ],
        out_specs=[pl.BlockSpec((tm * SUBLANES, LANES), lambda i: (i, 0)), col(TOP_K), col(TOP_K), col(TOP_K),
                   _full((N_EXPERTS, LANES))],
        out_shape=[
            jax.ShapeDtypeStruct((t * SUBLANES, LANES), F32),
            jax.ShapeDtypeStruct((TOP_K, t), I32),
            jax.ShapeDtypeStruct((TOP_K, t), F32),
            jax.ShapeDtypeStruct((TOP_K, t), I32),
            jax.ShapeDtypeStruct((N_EXPERTS, LANES), F32),
        ],
        scratch_shapes=[pltpu.VMEM((N_EXPERTS, LANES), F32)],
        compiler_params=_cparams("arbitrary"),
        name="mix_route",
    )(x, ya, yb, carry_in, ln0_g, ln0_b, w_gate, b_gate, w_o, b_o, ln1_g, ln1_b, w_rt, b_rt)


def _row_copy(src_ref, src_row, dst_ref, dst_row, sem):
    return pltpu.make_async_copy(src_ref.at[src_row], dst_ref.at[dst_row], sem)


def _fetch_slots(slots_hbm, slots_smem, sem, step, slot):
    n = slots_hbm.shape[1]
    return pltpu.make_async_copy(slots_hbm.at[step], slots_smem.at[pl.ds(pl.multiple_of(slot * n, n), n)],
                                 sem.at[slot])


def _pad_fill(h_ref, xs_out, pad_smem, pad_sem, n_pad_runs, wait):
    def go(cp):
        if wait:
            cp.wait()
        else:
            cp.start()

    def run(j, carry):
        start = pad_smem[2 * j]
        length = pad_smem[2 * j + 1]
        for bit in range(BM_MOE.bit_length() - 1):
            size = 1 << bit

            @pl.when((length & size) != 0)
            def _():
                off = start + (length & ~(2 * size - 1))
                go(pltpu.make_async_copy(h_ref.at[pl.ds(0, size)], xs_out.at[pl.ds(off, size)], pad_sem))
        return carry

    lax.fori_loop(0, n_pad_runs, run, 0)

    piece = min(h_ref.shape[0], BM_MOE)

    def dead_block(j, carry):
        for part in range(BM_MOE // piece):
            off = j * BM_MOE + part * piece
            go(pltpu.make_async_copy(h_ref.at[pl.ds(0, piece)], xs_out.at[pl.ds(off, piece)], pad_sem))
        return carry

    lax.fori_loop(pad_smem[2 * n_pad_runs], xs_out.shape[0] // BM_MOE, dead_block, 0)


def _dispatch_kernel(*refs, tiles):
    n_sets = len(tiles)
    h_refs = refs[:n_sets]
    slots_hbm, pad_hbm, xs_out, slots_smem, pad_smem, slot_sem, pad_sem, row_sem = refs[n_sets:]
    i = pl.program_id(0)
    n = pl.num_programs(0)
    tm = h_refs[0].shape[0]
    cur = lax.rem(i, 2)

    @pl.when(i == 0)
    def _():
        _fetch_slots(slots_hbm, slots_smem, slot_sem, 0, 0).start()
        fetch = pltpu.make_async_copy(pad_hbm, pad_smem, pad_sem)
        fetch.start()
        fetch.wait()
        _pad_fill(h_refs[0], xs_out, pad_smem, pad_sem, N_EXPERTS, wait=False)
        _pad_fill(h_refs[0], xs_out, pad_smem, pad_sem, N_EXPERTS, wait=True)

    @pl.when(i + 1 < n)
    def _():
        _fetch_slots(slots_hbm, slots_smem, slot_sem, i + 1, 1 - cur).start()

    _fetch_slots(slots_hbm, slots_smem, slot_sem, i, cur).wait()
    base = cur * (TOP_K * tm)

    first = 0
    for h_ref, count in zip(h_refs, tiles):
        @pl.when(jnp.logical_and(i >= first, i < first + count))
        def _(h_ref=h_ref):
            def issue(r, carry):
                for k in range(TOP_K):
                    _row_copy(h_ref, r, xs_out, slots_smem[base + k * tm + r], row_sem).start(priority=k % 2)
                return carry

            lax.fori_loop(0, tm, issue, 0, unroll=ISSUE_UNROLL)
            for _ in range(TOP_K):
                pltpu.make_async_copy(h_ref, xs_out.at[pl.ds(0, tm)], row_sem).wait()
        first += count


def _dispatch(h1s, slots, pad_runs, n_rows):
    tm = TM_ROWS
    h1s = [h.reshape(-1, SUBLANES, LANES) for h in h1s]
    tiles = [h.shape[0] // tm for h in h1s]
    firsts = [sum(tiles[:j]) for j in range(len(tiles))]

    def tile_of(first, count):
        return lambda i: (jnp.clip(i - first, 0, count - 1), 0, 0)

    return pl.pallas_call(
        functools.partial(_dispatch_kernel, tiles=tuple(tiles)),
        grid=(sum(tiles),),
        in_specs=[pl.BlockSpec((tm, SUBLANES, LANES), tile_of(f, c)) for f, c in zip(firsts, tiles)]
        + [pl.BlockSpec(memory_space=pl.ANY)] * 2,
        out_specs=pl.BlockSpec(memory_space=pl.ANY),
        out_shape=jax.ShapeDtypeStruct((n_rows, SUBLANES, LANES), F32),
        scratch_shapes=[
            pltpu.SMEM((2 * TOP_K * tm,), I32),
            pltpu.SMEM((pad_runs.shape[0],), I32),
            pltpu.SemaphoreType.DMA((2,)),
            pltpu.SemaphoreType.DMA,
            pltpu.SemaphoreType.DMA,
        ],
        compiler_params=_cparams("arbitrary"),
        name="dispatch",
    )(*h1s, slots, pad_runs)


def _moe_kernel(blk_e_ref, nb_ref, x_ref, wgu_ref, bgu_ref, wd_ref, bd_ref, o_ref, wgu_b, wd_b):
    i = pl.program_id(0)
    is_live = i < nb_ref[0]

    @pl.when(jnp.logical_not(is_live))
    def _():
        o_ref[...] = jnp.zeros(o_ref.shape, F32)

    @pl.when(jnp.logical_or(i == 0, blk_e_ref[i] != blk_e_ref[jnp.maximum(i - 1, 0)]))
    def _():
        for r in range(0, D_MODEL, CAST_ROWS):
            wgu_b[r:r + CAST_ROWS, :] = wgu_ref[0, r:r + CAST_ROWS, :].astype(BF16)
        for r in range(0, D_FF, CAST_ROWS):
            wd_b[r:r + CAST_ROWS, :] = wd_ref[0, r:r + CAST_ROWS, :].astype(BF16)

    @pl.when(is_live)
    def _():
        x = _from_row_tiles(x_ref, 0, BM_MOE).astype(BF16)
        acc = jnp.zeros((BM_MOE, D_MODEL), F32) + bd_ref[0]
        for c in range(D_FF // FF_CHUNK):
            lo = c * FF_CHUNK
            g = jnp.dot(x, wgu_b[:, lo:lo + FF_CHUNK], preferred_element_type=F32) + bgu_ref[0, :, lo:lo + FF_CHUNK]
            u = (jnp.dot(x, wgu_b[:, D_FF + lo:D_FF + lo + FF_CHUNK], preferred_element_type=F32)
                 + bgu_ref[0, :, D_FF + lo:D_FF + lo + FF_CHUNK])
            g = jnp.minimum(g, SWIGLU_LIMIT)
            u = jnp.clip(u, -SWIGLU_LIMIT, SWIGLU_LIMIT)
            act = (u + 1.0) * (g * _sigmoid(g * SWIGLU_ALPHA))
            acc = acc + jnp.dot(act.astype(BF16), wd_b[lo:lo + FF_CHUNK, :], preferred_element_type=F32)
        _to_row_tiles(o_ref, 0, acc)


def _moe_blocks(xs, blk_e, nb_used, w_gu, b_gu, w_down, b_down):
    p = xs.shape[0] // SUBLANES
    n_blocks = p // BM_MOE
    live = lambda i, nb: jnp.minimum(i, nb[0] - 1)
    grid_spec = pltpu.PrefetchScalarGridSpec(
        num_scalar_prefetch=2,
        grid=(n_blocks,),
        in_specs=[
            pl.BlockSpec((BM_MOE * SUBLANES, LANES), lambda i, be, nb: (live(i, nb), 0)),
            pl.BlockSpec((1, D_MODEL, 2 * D_FF), lambda i, be, nb: (be[i], 0, 0)),
            pl.BlockSpec((1, 1, 2 * D_FF), lambda i, be, nb: (be[i], 0, 0)),
            pl.BlockSpec((1, D_FF, D_MODEL), lambda i, be, nb: (be[i], 0, 0)),
            pl.BlockSpec((1, 1, D_MODEL), lambda i, be, nb: (be[i], 0, 0)),
        ],
        out_specs=pl.BlockSpec((BM_MOE * SUBLANES, LANES), lambda i, be, nb: (i, 0)),
        scratch_shapes=[pltpu.VMEM((D_MODEL, 2 * D_FF), BF16), pltpu.VMEM((D_FF, D_MODEL), BF16)],
    )
    return pl.pallas_call(
        _moe_kernel,
        grid_spec=grid_spec,
        out_shape=jax.ShapeDtypeStruct((p * SUBLANES, LANES), F32),
        compiler_params=pltpu.CompilerParams(dimension_semantics=("arbitrary",), vmem_limit_bytes=VMEM_LIMIT_MOE),
        name="moe_experts",
    )(blk_e, nb_used, xs, w_gu, b_gu, w_down, b_down)


def _final_kernel(h_ref, p_ref, gate_ref, slots_hbm, y_hbm, wpg_ref, bpg_ref, wpp_ref, g_ref, b_ref, o_ref,
                  slots_smem, rows_ref, slot_sem, row_sem):
    i = pl.program_id(0)
    n = pl.num_programs(0)
    tm = h_ref.shape[0] // SUBLANES

    def gather(r, buf, base):
        for k in range(TOP_K):
            slot = slots_smem[base + k * tm + r]
            src = y_hbm.at[pl.ds(pl.multiple_of(slot * SUBLANES, SUBLANES), SUBLANES)]
            dst = rows_ref.at[buf, k, pl.ds(pl.multiple_of(r * SUBLANES, SUBLANES), SUBLANES)]
            pltpu.make_async_copy(src, dst, row_sem.at[buf]).start(priority=k % 2)

    def wait_rows(buf):
        for k in range(TOP_K):
            pltpu.make_async_copy(y_hbm.at[pl.ds(0, tm * SUBLANES)], rows_ref.at[buf, k], row_sem.at[buf]).wait()

    @pl.when(i == 0)
    def _():
        first = _fetch_slots(slots_hbm, slots_smem, slot_sem, 0, 0)
        first.start()
        first.wait()

        def issue(r, carry):
            gather(r, 0, 0)
            return carry

        lax.fori_loop(0, tm, issue, 0, unroll=ISSUE_UNROLL)
        _fetch_slots(slots_hbm, slots_smem, slot_sem, 1, 1).start()

    def step(cur):
        nxt = 1 - cur
        _fetch_slots(slots_hbm, slots_smem, slot_sem, i + 1, nxt).wait()

        @pl.when(i + 2 <= n)
        def _():
            _fetch_slots(slots_hbm, slots_smem, slot_sem, i + 2, cur).start()

        for r in range(tm):
            gather(r, nxt, nxt * (TOP_K * tm))

        h = _from_row_tiles(h_ref, 0, tm)
        ple = (_sigmoid(jnp.dot(h.astype(BF16), wpg_ref[...], preferred_element_type=F32) + bpg_ref[...])
               * jnp.dot(p_ref[...].astype(BF16), wpp_ref[...], preferred_element_type=F32))

        wait_rows(cur)
        gates = gate_ref[...]
        ffn = gates[:, 0:1] * _from_row_tiles(rows_ref.at[cur, 0], 0, tm)
        for k in range(1, TOP_K):
            ffn = ffn + gates[:, k:k + 1] * _from_row_tiles(rows_ref.at[cur, k], 0, tm)
        o_ref[...] = _layer_norm(DEEPNORM_ALPHA * h + (ffn + ple), g_ref[...], b_ref[...])

        @pl.when(i == n - 1)
        def _():
            wait_rows(nxt)

    for parity in range(2):
        pl.when(lax.rem(i, 2) == parity)(functools.partial(step, parity))


def _combine_final(h1, p, gates, slots, y, w_pg, b_pg, w_pp, ln_g, ln_b):
    t = h1.shape[0] // SUBLANES
    tm = TM_COMBINE
    slots = jnp.concatenate([slots, slots[-1:]], axis=0)
    return pl.pallas_call(
        _final_kernel,
        grid=(t // tm,),
        in_specs=[
            pl.BlockSpec((tm * SUBLANES, LANES), lambda i: (i, 0)),
            pl.BlockSpec((tm, D_PLE), lambda i: (i, 0)),
            pl.BlockSpec((tm, TOP_K), lambda i: (i, 0)),
            pl.BlockSpec(memory_space=pl.ANY),
            pl.BlockSpec(memory_space=pl.ANY),
            _full((D_MODEL, D_MODEL)), _full((1, D_MODEL)), _full((D_PLE, D_MODEL)),
            _full((1, D_MODEL)), _full((1, D_MODEL)),
        ],
        out_specs=pl.BlockSpec((tm, D_MODEL), lambda i: (i, 0)),
        out_shape=jax.ShapeDtypeStruct((t, D_MODEL), F32),
        scratch_shapes=[
            pltpu.SMEM((2 * TOP_K * tm,), I32),
            pltpu.VMEM((2, TOP_K, tm * SUBLANES, LANES), F32),
            pltpu.SemaphoreType.DMA((2,)),
            pltpu.SemaphoreType.DMA((2,)),
        ],
        compiler_params=_cparams("arbitrary"),
        name="combine_final",
    )(h1, p, gates, slots, y, w_pg, b_pg, w_pp, ln_g, ln_b)


def _tile_slots(dest, tm):
    t = dest.shape[1]
    return dest.reshape(TOP_K, t // tm, tm).transpose(1, 0, 2).reshape(t // tm, TOP_K * tm)


def kernel(x_prompt, x_sample, p_prompt, p_sample, ln0_g, ln0_b, w_in, b_in, conv_w, conv_b, lnc_g, lnc_b, w_conv_out, w_fnet_out, w_o, b_o, ln1_g, ln1_b, w_router, b_router, w_gu, b_gu, w_down, b_down, w_pg, b_pg, w_pp, ln2_g, ln2_b):
    row = lambda v: v.reshape(1, -1).astype(F32)
    n_branch = 2 * D_CONV + D_FNET
    w_in_b = w_in[0].astype(BF16)
    w_branch, w_gate = w_in_b[:, :n_branch], w_in_b[:, n_branch:]
    b_branch, b_gate = row(b_in[0, :n_branch]), row(b_in[0, n_branch:])
    w_conv_out_b = w_conv_out[0].astype(BF16)
    w_fnet_out_b = w_fnet_out[0].astype(BF16)
    w_o_b = w_o[0].astype(BF16)
    w_rt_f = jnp.pad(w_router[0].astype(F32), ((0, 0), (0, LANES - N_EXPERTS)))
    w_rt_hi = w_rt_f.astype(BF16)
    w_rt = jnp.concatenate([w_rt_hi, (w_rt_f - w_rt_hi.astype(F32)).astype(BF16)], axis=1)
    b_rt = b_router[0].reshape(N_EXPERTS, 1).astype(F32)
    b_gu_r = b_gu[0].reshape(N_EXPERTS, 1, 2 * D_FF).astype(F32)
    b_down_r = b_down[0].reshape(N_EXPERTS, 1, D_MODEL).astype(F32)
    w_pg_b = w_pg[0].astype(BF16)
    w_pp_b = w_pp[0].astype(BF16)

    carry = jnp.zeros((N_EXPERTS, LANES), F32)
    routed = []
    for x, p in ((x_prompt, p_prompt[0]), (x_sample, p_sample[0])):
        bsz, s, _ = x.shape
        xt = x.reshape(bsz * s, D_MODEL)
        u, f_in = _inproj(xt, row(ln0_g), row(ln0_b), w_branch, b_branch)
        ya = _conv_branch(u.reshape(bsz, s, D_CONV), conv_w[0].astype(F32), row(conv_b[0]), row(lnc_g[0]),
                          row(lnc_b[0]), w_conv_out_b).reshape(bsz * s, D_MODEL)
        yb = _fourier_branch(f_in.reshape(bsz, s, D_FNET), w_fnet_out_b)
        h1, idx, gates, rank, carry = _mix_route(
            xt, ya, yb, carry, row(ln0_g), row(ln0_b), w_gate, b_gate, w_o_b, row(b_o[0]),
            row(ln1_g[0]), row(ln1_b[0]), w_rt, b_rt)
        routed.append((h1, idx, gates, rank, p.reshape(bsz * s, D_PLE), (bsz, s)))

    n_assign = sum(r[1].shape[1] for r in routed) * TOP_K
    n_blocks = (n_assign + N_EXPERTS * (BM_MOE - 1)) // BM_MOE
    counts = carry[:, 0].astype(I32)
    padded = ((counts + BM_MOE - 1) // BM_MOE) * BM_MOE
    ends = jnp.cumsum(padded)
    start = ends - padded
    nb_used = (ends[-1] // BM_MOE).reshape(1).astype(I32)
    blk = jnp.minimum(jnp.arange(n_blocks, dtype=I32), nb_used[0] - 1)
    blk_e = jnp.minimum(jnp.sum((ends[None, :] <= (blk * BM_MOE)[:, None]).astype(I32), axis=1), N_EXPERTS - 1)
    pad_runs = jnp.concatenate([jnp.stack([start + counts, padded - counts], axis=1).reshape(-1), nb_used,
                                jnp.zeros((LANES - 2 * N_EXPERTS - 1,), I32)]).astype(I32)

    slot_tables = []
    dispatch_slots = []
    for _, idx, _, rank, _, _ in routed:
        first = jnp.sum(jnp.where(idx[None] == jnp.arange(N_EXPERTS, dtype=I32)[:, None, None],
                                  start[:, None, None], 0), axis=0)
        slot_tables.append(_tile_slots(first + rank, TM_COMBINE))
        dispatch_slots.append(_tile_slots(first + rank, TM_ROWS))
    xs = _dispatch([r[0] for r in routed], jnp.concatenate(dispatch_slots, axis=0), pad_runs, n_blocks * BM_MOE)

    y = _moe_blocks(xs.reshape(-1, LANES), blk_e, nb_used, w_gu.reshape(N_EXPERTS, D_MODEL, 2 * D_FF), b_gu_r,
                    w_down.reshape(N_EXPERTS, D_FF, D_MODEL), b_down_r)

    outs = []
    for (h1, _, gates, _, p, (bsz, s)), slots in zip(routed, slot_tables):
        o = _combine_final(h1, p, gates.T, slots, y, w_pg_b, row(b_pg[0]), w_pp_b, row(ln2_g[0]), row(ln2_b[0]))
        outs.append(o.reshape(bsz, s, D_MODEL))
    return tuple(outs)
```

```python
import functools

import jax
import jax.numpy as jnp
from jax import lax
from jax.experimental import pallas as pl
from jax.experimental.pallas import tpu as pltpu

F32 = jnp.float32
BF16 = jnp.bfloat16
I32 = jnp.int32

D_MODEL = 1024
D_CONV = D_MODEL // 2
CONV_WIDTH = 31
CONV_PAD = CONV_WIDTH // 2
D_FNET = D_MODEL // 2
FNET_GROUPS = 4
FNET_GROUP_DIM = D_FNET // FNET_GROUPS
D_PLE = 256
N_EXPERTS = 32
TOP_K = 4
D_FF = D_MODEL
SWIGLU_ALPHA = 1.702
SWIGLU_LIMIT = 7.0
LN_EPS = 1e-5
DEPTH = 1
DEEPNORM_ALPHA = (2 * DEPTH) ** 0.25

LANES = 128
SUBLANES = 8
DFT_N1 = 128
HALO = 16

TM_INPROJ = 1024
TS_CONV = 512
RC_CONV = 128
TM_MIX = 512
SUB_MIX = 512
FFT2_TILES = 4
TM_ROWS = 1024
TM_COMBINE = 256
BM_MOE = 512
FF_CHUNK = 1024
ISSUE_UNROLL = 8
CAST_ROWS = 128
VMEM_LIMIT = 48 * 1024 * 1024
VMEM_LIMIT_MOE = 56 * 1024 * 1024


def _cparams(*sem):
    return pltpu.CompilerParams(dimension_semantics=sem, vmem_limit_bytes=VMEM_LIMIT)


def _layer_norm(x, g, b):
    mu = jnp.mean(x, axis=-1, keepdims=True)
    xc = x - mu
    var = jnp.mean(xc * xc, axis=-1, keepdims=True)
    return xc * lax.rsqrt(var + LN_EPS) * g + b


def _sigmoid(x):
    return 0.5 * jnp.tanh(0.5 * x) + 0.5


def _full(shape):
    return pl.BlockSpec(shape, lambda *_: (0,) * len(shape))


def _to_row_tiles(ref, row0, x):
    rows = x.shape[0]
    for j in range(D_MODEL // LANES):
        ref[pl.ds(row0 * SUBLANES + j, rows, stride=SUBLANES), :] = x[:, j * LANES:(j + 1) * LANES]


def _from_row_tiles(ref, row0, rows):
    return jnp.concatenate([ref[pl.ds(row0 * SUBLANES + j, rows, stride=SUBLANES), :]
                            for j in range(D_MODEL // LANES)], axis=1)


def _inproj_kernel(x_ref, g_ref, b_ref, w_ref, bias_ref, u_ref, f_ref):
    h = _layer_norm(x_ref[...], g_ref[...], b_ref[...])
    z = jnp.dot(h.astype(BF16), w_ref[...], preferred_element_type=F32) + bias_ref[...]
    u_ref[...] = (z[:, :D_CONV] * _sigmoid(z[:, D_CONV:2 * D_CONV])).astype(BF16)
    f_ref[...] = z[:, 2 * D_CONV:]


def _inproj(x, ln_g, ln_b, w, bias):
    t = x.shape[0]
    n_out = w.shape[1]
    return pl.pallas_call(
        _inproj_kernel,
        grid=(t // TM_INPROJ,),
        in_specs=[
            pl.BlockSpec((TM_INPROJ, D_MODEL), lambda i: (i, 0)),
            _full((1, D_MODEL)), _full((1, D_MODEL)),
            _full((D_MODEL, n_out)), _full((1, n_out)),
        ],
        out_specs=[
            pl.BlockSpec((TM_INPROJ, D_CONV), lambda i: (i, 0)),
            pl.BlockSpec((TM_INPROJ, D_FNET), lambda i: (i, 0)),
        ],
        out_shape=[jax.ShapeDtypeStruct((t, D_CONV), BF16), jax.ShapeDtypeStruct((t, D_FNET), F32)],
        compiler_params=_cparams("parallel"),
        name="inproj",
    )(x, ln_g, ln_b, w, bias)


def _conv_kernel(prev_ref, cur_ref, next_ref, cw_ref, cb_ref, g_ref, b_ref, wout_ref, ya_ref,
                 ext_ref, act_ref):
    i = pl.program_id(1)
    last = pl.num_programs(1) - 1
    ts = cur_ref.shape[1]
    ext_rows = ts + 2 * HALO
    zero = jnp.zeros((HALO, D_CONV), F32)
    ext_ref[0, 0:HALO, :] = jnp.where(i > 0, prev_ref[0].astype(F32), zero)
    ext_ref[0, HALO:HALO + ts, :] = cur_ref[0].astype(F32)
    ext_ref[0, HALO + ts:ext_rows, :] = jnp.where(i < last, next_ref[0].astype(F32), zero)
    for r in range(1, SUBLANES):
        ext_ref[r, 0:ext_rows - SUBLANES, :] = ext_ref[0, r:r + ext_rows - SUBLANES, :]

    cb = cb_ref[...]
    g = g_ref[...]
    b = b_ref[...]

    def chunk(c, carry):
        r0 = pl.multiple_of(c * RC_CONV, RC_CONV)
        acc = jnp.zeros((RC_CONV, D_CONV), F32) + cb
        for j in range(CONV_WIDTH):
            off = j + HALO - CONV_PAD
            q, r = divmod(off, SUBLANES)
            acc = acc + cw_ref[j:j + 1, :] * ext_ref[r, pl.ds(r0 + q * SUBLANES, RC_CONV), :]
        y = _layer_norm(acc, g, b)
        act_ref[pl.ds(r0, RC_CONV), :] = (y * _sigmoid(y)).astype(BF16)
        return carry

    lax.fori_loop(0, ts // RC_CONV, chunk, 0)
    ya_ref[0] = jnp.dot(act_ref[...], wout_ref[...], preferred_element_type=F32).astype(BF16)


def _conv_branch(u, conv_w, conv_b, ln_g, ln_b, w_out):
    bsz, s, _ = u.shape
    ts = TS_CONV
    nb = s // ts
    hb = ts // HALO
    return pl.pallas_call(
        _conv_kernel,
        grid=(bsz, nb),
        in_specs=[
            pl.BlockSpec((1, HALO, D_CONV), lambda b, i: (b, jnp.maximum(i * hb - 1, 0), 0)),
            pl.BlockSpec((1, ts, D_CONV), lambda b, i: (b, i, 0)),
            pl.BlockSpec((1, HALO, D_CONV), lambda b, i: (b, jnp.minimum((i + 1) * hb, s // HALO - 1), 0)),
            _full((CONV_WIDTH, D_CONV)), _full((1, D_CONV)), _full((1, D_CONV)), _full((1, D_CONV)),
            _full((D_CONV, D_MODEL)),
        ],
        out_specs=pl.BlockSpec((1, ts, D_MODEL), lambda b, i: (b, i, 0)),
        out_shape=jax.ShapeDtypeStruct((bsz, s, D_MODEL), BF16),
        scratch_shapes=[
            pltpu.VMEM((SUBLANES, ts + 2 * HALO, D_CONV), F32),
            pltpu.VMEM((ts, D_CONV), BF16),
        ],
        compiler_params=_cparams("parallel", "parallel"),
        name="conv_branch",
    )(u, u, u, conv_w, conv_b, ln_g, ln_b, w_out)


def _dft_tables(s):
    n2_len = s // DFT_N1
    k1_per_tile = DFT_N1 // n2_len
    two_pi = 2.0 * jnp.pi
    a = jnp.arange(DFT_N1, dtype=I32)
    ang1 = two_pi * ((a[:, None] * a[None, :]) % DFT_N1).astype(F32) / DFT_N1
    small = jnp.concatenate([jnp.cos(ang1), -jnp.sin(ang1)], axis=0)
    r = jnp.arange(2 * DFT_N1 * SUBLANES, dtype=I32)[:, None]
    c = jnp.arange(DFT_N1 * SUBLANES, dtype=I32)[None, :]
    pick_r = (r // SUBLANES == jnp.arange(2 * DFT_N1, dtype=I32)[None, :]).astype(F32)
    pick_c = (jnp.arange(DFT_N1, dtype=I32)[:, None] == c // SUBLANES).astype(F32)
    rep = jnp.dot(jnp.dot(pick_r, small, precision=lax.Precision.HIGHEST), pick_c, precision=lax.Precision.HIGHEST)
    f1 = jnp.where(r % SUBLANES == c % SUBLANES, rep, 0.0).astype(BF16)
    cs = jnp.stack([jnp.cos(ang1), jnp.sin(ang1)]).astype(BF16)
    n_tiles = s // DFT_N1
    t = jnp.arange(n_tiles, dtype=I32)[:, None, None]
    row = jnp.arange(DFT_N1, dtype=I32)[None, :, None]
    col = jnp.arange(DFT_N1, dtype=I32)[None, None, :]
    k2, k1l_out = row // k1_per_tile, row % k1_per_tile
    k1l_in, n2 = col // n2_len, col % n2_len
    k = t * k1_per_tile + k1l_out + DFT_N1 * k2
    hit = k1l_in == k1l_out
    if k1_per_tile == 1:
        ang_a = two_pi * ((n2 * t) % s).astype(F32) / s
        ang_b = two_pi * ((n2 * k2) % n2_len).astype(F32) / n2_len
        cos2 = jnp.cos(ang_a) * jnp.cos(ang_b) - jnp.sin(ang_a) * jnp.sin(ang_b)
        sin2 = jnp.sin(ang_a) * jnp.cos(ang_b) + jnp.cos(ang_a) * jnp.sin(ang_b)
    else:
        ang2 = two_pi * ((n2 * k) % s).astype(F32) / s
        cos2, sin2 = jnp.cos(ang2), jnp.sin(ang2)
    gr = jnp.where(hit, cos2, 0.0)
    gi = jnp.where(hit, -sin2, 0.0)
    g = jnp.concatenate([jnp.concatenate([gr, -gi], axis=2),
                         jnp.concatenate([gi, gr], axis=2)], axis=1).astype(BF16)
    return f1, g, cs


def _fft1_kernel(x_ref, f_ref, y_ref):
    nb = x_ref.shape[2]
    x = x_ref[0].reshape(DFT_N1 * nb, D_FNET).astype(BF16)
    r = jnp.dot(f_ref[...], x, preferred_element_type=F32)
    y_ref[0] = r.reshape(2, DFT_N1, nb, D_FNET)


def _fft2_kernel(y_ref, g_ref, cs_ref, w_ref, o_ref, *, scale, tiles):
    for t in range(tiles):
        rows = slice(t * DFT_N1, (t + 1) * DFT_N1)
        yb = jnp.concatenate([y_ref[0, 0, rows, :], y_ref[0, 1, rows, :]], axis=0).astype(BF16)
        z = jnp.dot(g_ref[t], yb, preferred_element_type=F32)
        zr = z[:DFT_N1].astype(BF16)
        zi = z[DFT_N1:].astype(BF16)
        parts = []
        cs2 = cs_ref[...].reshape(2 * DFT_N1, DFT_N1)
        for grp in range(FNET_GROUPS):
            sl = slice(grp * FNET_GROUP_DIM, (grp + 1) * FNET_GROUP_DIM)
            parts.append(jnp.dot(jnp.concatenate([zr[:, sl], zi[:, sl]], axis=1), cs2, preferred_element_type=F32))
        fm = (jnp.concatenate(parts, axis=1) * scale).astype(BF16)
        out = jnp.dot(fm, w_ref[...], preferred_element_type=F32)
        k1_per_tile = o_ref.shape[2] // tiles
        if k1_per_tile == 1:
            o_ref[0, :, t, :] = out
        else:
            o_ref[0, :, t * k1_per_tile:(t + 1) * k1_per_tile, :] = out.reshape(o_ref.shape[1], k1_per_tile, D_MODEL)


def _fourier_branch(f_in, w_out):
    bsz, s, _ = f_in.shape
    n2_len = s // DFT_N1
    k1_per_tile = DFT_N1 // n2_len
    f1, g, cs = _dft_tables(s)
    y = pl.pallas_call(
        _fft1_kernel,
        grid=(bsz, n2_len // SUBLANES),
        in_specs=[pl.BlockSpec((1, DFT_N1, SUBLANES, D_FNET), lambda b, j: (b, 0, j, 0)),
                  _full((2 * DFT_N1 * SUBLANES, DFT_N1 * SUBLANES))],
        out_specs=pl.BlockSpec((1, 2, DFT_N1, SUBLANES, D_FNET), lambda b, j: (b, 0, 0, j, 0)),
        out_shape=jax.ShapeDtypeStruct((bsz, 2, DFT_N1, n2_len, D_FNET), F32),
        compiler_params=_cparams("parallel", "parallel"),
        name="fft_stage1",
    )(f_in.reshape(bsz, DFT_N1, n2_len, D_FNET), f1)
    y = y.reshape(bsz, 2, s, D_FNET)
    scale = float((s * FNET_GROUP_DIM) ** -0.5)
    assert n2_len % SUBLANES == 0 and (k1_per_tile == 1 or k1_per_tile % SUBLANES == 0)
    tiles = SUBLANES if k1_per_tile == 1 else min(FFT2_TILES, n2_len)
    k1_per_step = tiles * k1_per_tile
    yb = pl.pallas_call(
        functools.partial(_fft2_kernel, scale=scale, tiles=tiles),
        grid=(bsz, s // (DFT_N1 * tiles)),
        in_specs=[
            pl.BlockSpec((1, 2, DFT_N1 * tiles, D_FNET), lambda b, t: (b, 0, t, 0)),
            pl.BlockSpec((tiles, 2 * DFT_N1, 2 * DFT_N1), lambda b, t: (t, 0, 0)),
            _full((2, DFT_N1, DFT_N1)),
            _full((D_FNET, D_MODEL)),
        ],
        out_specs=pl.BlockSpec((1, n2_len, k1_per_step, D_MODEL), lambda b, t: (b, 0, t, 0)),
        out_shape=jax.ShapeDtypeStruct((bsz, n2_len, DFT_N1, D_MODEL), F32),
        compiler_params=_cparams("parallel", "parallel"),
        name="fft_stage2",
    )(y, g, cs, w_out)
    return yb.reshape(bsz * s, D_MODEL)


def _mix_kernel(x_ref, ya_ref, yb_ref, cin_ref, ln0g_ref, ln0b_ref, wg_ref, bg_ref, wo_ref, bo_ref,
                ln1g_ref, ln1b_ref, wr_ref, br_ref,
                h1_ref, idx_ref, gate_ref, rank_ref, cnt_ref, carry_ref):
    tm = x_ref.shape[0]
    sub = min(SUB_MIX, tm)

    @pl.when(pl.program_id(0) == 0)
    def _():
        carry_ref[...] = cin_ref[...]

    carry = carry_ref[...]
    eio = lax.broadcasted_iota(I32, (N_EXPERTS, sub), 0)
    before = (lax.broadcasted_iota(I32, (sub, sub), 0) < lax.broadcasted_iota(I32, (sub, sub), 1)).astype(BF16)
    for r0 in range(0, tm, sub):
        rows = slice(r0, r0 + sub)
        h = _layer_norm(x_ref[rows, :], ln0g_ref[...], ln0b_ref[...])
        zg = jnp.dot(h.astype(BF16), wg_ref[...], preferred_element_type=F32) + bg_ref[...]
        m = (_sigmoid(zg[:, :D_MODEL]) * ya_ref[rows, :].astype(F32)
             + _sigmoid(zg[:, D_MODEL:]) * yb_ref[rows, :].astype(F32))
        mix = jnp.dot(m.astype(BF16), wo_ref[...], preferred_element_type=F32) + bo_ref[...]
        h1 = _layer_norm(DEEPNORM_ALPHA * h + mix, ln1g_ref[...], ln1b_ref[...])
        _to_row_tiles(h1_ref, r0, h1)

        h_hi = h1.astype(BF16)
        h_lo = (h1 - h_hi.astype(F32)).astype(BF16)
        hw = jnp.dot(h_hi, wr_ref[...], preferred_element_type=F32)
        lw = jnp.dot(h_lo, wr_ref[:, :LANES], preferred_element_type=F32)
        logits = (hw[:, :LANES] + hw[:, LANES:] + lw).T[:N_EXPERTS] + br_ref[...]
        vals, idxs = [], []
        cur = logits
        for _ in range(TOP_K):
            mx = jnp.max(cur, axis=0, keepdims=True)
            ik = jnp.min(jnp.where(cur == mx, eio, N_EXPERTS), axis=0, keepdims=True)
            vals.append(mx)
            idxs.append(ik)
            cur = jnp.where(eio == ik, -jnp.inf, cur)
        exps = [jnp.exp(v - vals[0]) for v in vals]
        den = exps[0] + exps[1] + exps[2] + exps[3]
        gate_ref[:, rows] = jnp.concatenate([e / den for e in exps], axis=0)
        idx_ref[:, rows] = jnp.concatenate(idxs, axis=0)

        hot = jnp.zeros((N_EXPERTS, sub), F32)
        for ik in idxs:
            hot = hot + (eio == ik).astype(F32)
        prior = jnp.dot(hot.astype(BF16), before, preferred_element_type=F32) + carry[:, 0:1]
        ranks = [jnp.sum(jnp.where(eio == ik, prior, 0.0), axis=0, keepdims=True) for ik in idxs]
        rank_ref[:, rows] = jnp.concatenate(ranks, axis=0).astype(I32)
        carry = carry + jnp.sum(hot, axis=1, keepdims=True)
    carry_ref[...] = carry
    cnt_ref[...] = carry


def _mix_route(x, ya, yb, carry_in, ln0_g, ln0_b, w_gate, b_gate, w_o, b_o, ln1_g, ln1_b, w_rt, b_rt):
    t = x.shape[0]
    tm = TM_MIX
    row = lambda d: pl.BlockSpec((tm, d), lambda i: (i, 0))
    col = lambda r: pl.BlockSpec((r, tm), lambda i: (0, i))
    return pl.pallas_call(
        _mix_kernel,
        grid=(t // tm,),
        in_specs=[
            row(D_MODEL), row(D_MODEL), row(D_MODEL), _full((N_EXPERTS, LANES)),
            _full((1, D_MODEL)), _full((1, D_MODEL)),
            _full((D_MODEL, 2 * D_MODEL)), _full((1, 2 * D_MODEL)),
            _full((D_MODEL, D_MODEL)), _full((1, D_MODEL)),
            _full((1, D_MODEL)), _full((1, D_MODEL)),
            _full((D_MODEL, 2 * LANES)), _full((N_EXPERTS, 1)),
        ],
        out_specs=[pl.BlockSpec((tm * SUBLANES, LANES), lambda i: (i, 0)), col(TOP_K), col(TOP_K), col(TOP_K),
                   _full((N_EXPERTS, LANES))],
        out_shape=[
            jax.ShapeDtypeStruct((t * SUBLANES, LANES), F32),
            jax.ShapeDtypeStruct((TOP_K, t), I32),
            jax.ShapeDtypeStruct((TOP_K, t), F32),
            jax.ShapeDtypeStruct((TOP_K, t), I32),
            jax.ShapeDtypeStruct((N_EXPERTS, LANES), F32),
        ],
        scratch_shapes=[pltpu.VMEM((N_EXPERTS, LANES), F32)],
        compiler_params=_cparams("arbitrary"),
        name="mix_route",
    )(x, ya, yb, carry_in, ln0_g, ln0_b, w_gate, b_gate, w_o, b_o, ln1_g, ln1_b, w_rt, b_rt)


def _row_copy(src_ref, src_row, dst_ref, dst_row, sem):
    return pltpu.make_async_copy(src_ref.at[src_row], dst_ref.at[dst_row], sem)


def _fetch_slots(slots_hbm, slots_smem, sem, step, slot):
    n = slots_hbm.shape[1]
    return pltpu.make_async_copy(slots_hbm.at[step], slots_smem.at[pl.ds(pl.multiple_of(slot * n, n), n)],
                                 sem.at[slot])


def _pad_fill(h_ref, xs_out, pad_smem, pad_sem, n_pad_runs, wait):
    def go(cp):
        if wait:
            cp.wait()
        else:
            cp.start()

    def run(j, carry):
        start = pad_smem[2 * j]
        length = pad_smem[2 * j + 1]
        for bit in range(BM_MOE.bit_length() - 1):
            size = 1 << bit

            @pl.when((length & size) != 0)
            def _():
                off = start + (length & ~(2 * size - 1))
                go(pltpu.make_async_copy(h_ref.at[pl.ds(0, size)], xs_out.at[pl.ds(off, size)], pad_sem))
        return carry

    lax.fori_loop(0, n_pad_runs, run, 0)

    piece = min(h_ref.shape[0], BM_MOE)

    def dead_block(j, carry):
        for part in range(BM_MOE // piece):
            off = j * BM_MOE + part * piece
            go(pltpu.make_async_copy(h_ref.at[pl.ds(0, piece)], xs_out.at[pl.ds(off, piece)], pad_sem))
        return carry

    lax.fori_loop(pad_smem[2 * n_pad_runs], xs_out.shape[0] // BM_MOE, dead_block, 0)


def _dispatch_kernel(*refs, tiles):
    n_sets = len(tiles)
    h_refs = refs[:n_sets]
    slots_hbm, pad_hbm, xs_out, slots_smem, pad_smem, slot_sem, pad_sem, row_sem = refs[n_sets:]
    i = pl.program_id(0)
    n = pl.num_programs(0)
    tm = h_refs[0].shape[0]
    cur = lax.rem(i, 2)

    @pl.when(i == 0)
    def _():
        _fetch_slots(slots_hbm, slots_smem, slot_sem, 0, 0).start()
        fetch = pltpu.make_async_copy(pad_hbm, pad_smem, pad_sem)
        fetch.start()
        fetch.wait()
        _pad_fill(h_refs[0], xs_out, pad_smem, pad_sem, N_EXPERTS, wait=False)
        _pad_fill(h_refs[0], xs_out, pad_smem, pad_sem, N_EXPERTS, wait=True)

    @pl.when(i + 1 < n)
    def _():
        _fetch_slots(slots_hbm, slots_smem, slot_sem, i + 1, 1 - cur).start()

    _fetch_slots(slots_hbm, slots_smem, slot_sem, i, cur).wait()
    base = cur * (TOP_K * tm)

    first = 0
    for h_ref, count in zip(h_refs, tiles):
        @pl.when(jnp.logical_and(i >= first, i < first + count))
        def _(h_ref=h_ref):
            def issue(r, carry):
                for k in range(TOP_K):
                    _row_copy(h_ref, r, xs_out, slots_smem[base + k * tm + r], row_sem).start(priority=k % 2)
                return carry

            lax.fori_loop(0, tm, issue, 0, unroll=ISSUE_UNROLL)
            for _ in range(TOP_K):
                pltpu.make_async_copy(h_ref, xs_out.at[pl.ds(0, tm)], row_sem).wait()
        first += count


def _dispatch(h1s, slots, pad_runs, n_rows):
    tm = TM_ROWS
    h1s = [h.reshape(-1, SUBLANES, LANES) for h in h1s]
    tiles = [h.shape[0] // tm for h in h1s]
    firsts = [sum(tiles[:j]) for j in range(len(tiles))]

    def tile_of(first, count):
        return lambda i: (jnp.clip(i - first, 0, count - 1), 0, 0)

    return pl.pallas_call(
        functools.partial(_dispatch_kernel, tiles=tuple(tiles)),
        grid=(sum(tiles),),
        in_specs=[pl.BlockSpec((tm, SUBLANES, LANES), tile_of(f, c)) for f, c in zip(firsts, tiles)]
        + [pl.BlockSpec(memory_space=pl.ANY)] * 2,
        out_specs=pl.BlockSpec(memory_space=pl.ANY),
        out_shape=jax.ShapeDtypeStruct((n_rows, SUBLANES, LANES), F32),
        scratch_shapes=[
            pltpu.SMEM((2 * TOP_K * tm,), I32),
            pltpu.SMEM((pad_runs.shape[0],), I32),
            pltpu.SemaphoreType.DMA((2,)),
            pltpu.SemaphoreType.DMA,
            pltpu.SemaphoreType.DMA,
        ],
        compiler_params=_cparams("arbitrary"),
        name="dispatch",
    )(*h1s, slots, pad_runs)


def _moe_kernel(blk_e_ref, nb_ref, x_ref, wgu_ref, bgu_ref, wd_ref, bd_ref, o_ref, wgu_b, wd_b):
    i = pl.program_id(0)
    is_live = i < nb_ref[0]

    @pl.when(jnp.logical_not(is_live))
    def _():
        o_ref[...] = jnp.zeros(o_ref.shape, F32)

    @pl.when(jnp.logical_or(i == 0, blk_e_ref[i] != blk_e_ref[jnp.maximum(i - 1, 0)]))
    def _():
        for r in range(0, D_MODEL, CAST_ROWS):
            wgu_b[r:r + CAST_ROWS, :] = wgu_ref[0, r:r + CAST_ROWS, :].astype(BF16)
        for r in range(0, D_FF, CAST_ROWS):
            wd_b[r:r + CAST_ROWS, :] = wd_ref[0, r:r + CAST_ROWS, :].astype(BF16)

    @pl.when(is_live)
    def _():
        x = _from_row_tiles(x_ref, 0, BM_MOE).astype(BF16)
        acc = jnp.zeros((BM_MOE, D_MODEL), F32) + bd_ref[0]
        for c in range(D_FF // FF_CHUNK):
            lo = c * FF_CHUNK
            g = jnp.dot(x, wgu_b[:, lo:lo + FF_CHUNK], preferred_element_type=F32) + bgu_ref[0, :, lo:lo + FF_CHUNK]
            u = (jnp.dot(x, wgu_b[:, D_FF + lo:D_FF + lo + FF_CHUNK], preferred_element_type=F32)
                 + bgu_ref[0, :, D_FF + lo:D_FF + lo + FF_CHUNK])
            g = jnp.minimum(g, SWIGLU_LIMIT)
            u = jnp.clip(u, -SWIGLU_LIMIT, SWIGLU_LIMIT)
            act = (u + 1.0) * (g * _sigmoid(g * SWIGLU_ALPHA))
            acc = acc + jnp.dot(act.astype(BF16), wd_b[lo:lo + FF_CHUNK, :], preferred_element_type=F32)
        _to_row_tiles(o_ref, 0, acc)


def _moe_blocks(xs, blk_e, nb_used, w_gu, b_gu, w_down, b_down):
    p = xs.shape[0] // SUBLANES
    n_blocks = p // BM_MOE
    live = lambda i, nb: jnp.minimum(i, nb[0] - 1)
    grid_spec = pltpu.PrefetchScalarGridSpec(
        num_scalar_prefetch=2,
        grid=(n_blocks,),
        in_specs=[
            pl.BlockSpec((BM_MOE * SUBLANES, LANES), lambda i, be, nb: (live(i, nb), 0)),
            pl.BlockSpec((1, D_MODEL, 2 * D_FF), lambda i, be, nb: (be[i], 0, 0)),
            pl.BlockSpec((1, 1, 2 * D_FF), lambda i, be, nb: (be[i], 0, 0)),
            pl.BlockSpec((1, D_FF, D_MODEL), lambda i, be, nb: (be[i], 0, 0)),
            pl.BlockSpec((1, 1, D_MODEL), lambda i, be, nb: (be[i], 0, 0)),
        ],
        out_specs=pl.BlockSpec((BM_MOE * SUBLANES, LANES), lambda i, be, nb: (i, 0)),
        scratch_shapes=[pltpu.VMEM((D_MODEL, 2 * D_FF), BF16), pltpu.VMEM((D_FF, D_MODEL), BF16)],
    )
    return pl.pallas_call(
        _moe_kernel,
        grid_spec=grid_spec,
        out_shape=jax.ShapeDtypeStruct((p * SUBLANES, LANES), F32),
        compiler_params=pltpu.CompilerParams(dimension_semantics=("arbitrary",), vmem_limit_bytes=VMEM_LIMIT_MOE),
        name="moe_experts",
    )(blk_e, nb_used, xs, w_gu, b_gu, w_down, b_down)


def _final_kernel(h_ref, p_ref, gate_ref, slots_hbm, y_hbm, wpg_ref, bpg_ref, wpp_ref, g_ref, b_ref, o_ref,
                  slots_smem, rows_ref, slot_sem, row_sem):
    i = pl.program_id(0)
    n = pl.num_programs(0)
    tm = h_ref.shape[0] // SUBLANES

    def gather(r, buf, base):
        for k in range(TOP_K):
            slot = slots_smem[base + k * tm + r]
            src = y_hbm.at[pl.ds(pl.multiple_of(slot * SUBLANES, SUBLANES), SUBLANES)]
            dst = rows_ref.at[buf, k, pl.ds(pl.multiple_of(r * SUBLANES, SUBLANES), SUBLANES)]
            pltpu.make_async_copy(src, dst, row_sem.at[buf]).start(priority=k % 2)

    def wait_rows(buf):
        for k in range(TOP_K):
            pltpu.make_async_copy(y_hbm.at[pl.ds(0, tm * SUBLANES)], rows_ref.at[buf, k], row_sem.at[buf]).wait()

    @pl.when(i == 0)
    def _():
        first = _fetch_slots(slots_hbm, slots_smem, slot_sem, 0, 0)
        first.start()
        first.wait()

        def issue(r, carry):
            gather(r, 0, 0)
            return carry

        lax.fori_loop(0, tm, issue, 0, unroll=ISSUE_UNROLL)
        _fetch_slots(slots_hbm, slots_smem, slot_sem, 1, 1).start()

    def step(cur):
        nxt = 1 - cur
        _fetch_slots(slots_hbm, slots_smem, slot_sem, i + 1, nxt).wait()

        @pl.when(i + 2 <= n)
        def _():
            _fetch_slots(slots_hbm, slots_smem, slot_sem, i + 2, cur).start()

        for r in range(tm):
            gather(r, nxt, nxt * (TOP_K * tm))

        h = _from_row_tiles(h_ref, 0, tm)
        ple = (_sigmoid(jnp.dot(h.astype(BF16), wpg_ref[...], preferred_element_type=F32) + bpg_ref[...])
               * jnp.dot(p_ref[...].astype(BF16), wpp_ref[...], preferred_element_type=F32))

        wait_rows(cur)
        gates = gate_ref[...]
        ffn = gates[:, 0:1] * _from_row_tiles(rows_ref.at[cur, 0], 0, tm)
        for k in range(1, TOP_K):
            ffn = ffn + gates[:, k:k + 1] * _from_row_tiles(rows_ref.at[cur, k], 0, tm)
        o_ref[...] = _layer_norm(DEEPNORM_ALPHA * h + (ffn + ple), g_ref[...], b_ref[...])

        @pl.when(i == n - 1)
        def _():
            wait_rows(nxt)

    for parity in range(2):
        pl.when(lax.rem(i, 2) == parity)(functools.partial(step, parity))


def _combine_final(h1, p, gates, slots, y, w_pg, b_pg, w_pp, ln_g, ln_b):
    t = h1.shape[0] // SUBLANES
    tm = TM_COMBINE
    slots = jnp.concatenate([slots, slots[-1:]], axis=0)
    return pl.pallas_call(
        _final_kernel,
        grid=(t // tm,),
        in_specs=[
            pl.BlockSpec((tm * SUBLANES, LANES), lambda i: (i, 0)),
            pl.BlockSpec((tm, D_PLE), lambda i: (i, 0)),
            pl.BlockSpec((tm, TOP_K), lambda i: (i, 0)),
            pl.BlockSpec(memory_space=pl.ANY),
            pl.BlockSpec(memory_space=pl.ANY),
            _full((D_MODEL, D_MODEL)), _full((1, D_MODEL)), _full((D_PLE, D_MODEL)),
            _full((1, D_MODEL)), _full((1, D_MODEL)),
        ],
        out_specs=pl.BlockSpec((tm, D_MODEL), lambda i: (i, 0)),
        out_shape=jax.ShapeDtypeStruct((t, D_MODEL), F32),
        scratch_shapes=[
            pltpu.SMEM((2 * TOP_K * tm,), I32),
            pltpu.VMEM((2, TOP_K, tm * SUBLANES, LANES), F32),
            pltpu.SemaphoreType.DMA((2,)),
            pltpu.SemaphoreType.DMA((2,)),
        ],
        compiler_params=_cparams("arbitrary"),
        name="combine_final",
    )(h1, p, gates, slots, y, w_pg, b_pg, w_pp, ln_g, ln_b)


def _tile_slots(dest, tm):
    t = dest.shape[1]
    return dest.reshape(TOP_K, t // tm, tm).transpose(1, 0, 2).reshape(t // tm, TOP_K * tm)


def kernel(x_prompt, x_sample, p_prompt, p_sample, ln0_g, ln0_b, w_in, b_in, conv_w, conv_b, lnc_g, lnc_b, w_conv_out, w_fnet_out, w_o, b_o, ln1_g, ln1_b, w_router, b_router, w_gu, b_gu, w_down, b_down, w_pg, b_pg, w_pp, ln2_g, ln2_b):
    row = lambda v: v.reshape(1, -1).astype(F32)
    n_branch = 2 * D_CONV + D_FNET
    w_in_b = w_in[0].astype(BF16)
    w_branch, w_gate = w_in_b[:, :n_branch], w_in_b[:, n_branch:]
    b_branch, b_gate = row(b_in[0, :n_branch]), row(b_in[0, n_branch:])
    w_conv_out_b = w_conv_out[0].astype(BF16)
    w_fnet_out_b = w_fnet_out[0].astype(BF16)
    w_o_b = w_o[0].astype(BF16)
    w_rt_f = jnp.pad(w_router[0].astype(F32), ((0, 0), (0, LANES - N_EXPERTS)))
    w_rt_hi = w_rt_f.astype(BF16)
    w_rt = jnp.concatenate([w_rt_hi, (w_rt_f - w_rt_hi.astype(F32)).astype(BF16)], axis=1)
    b_rt = b_router[0].reshape(N_EXPERTS, 1).astype(F32)
    b_gu_r = b_gu[0].reshape(N_EXPERTS, 1, 2 * D_FF).astype(F32)
    b_down_r = b_down[0].reshape(N_EXPERTS, 1, D_MODEL).astype(F32)
    w_pg_b = w_pg[0].astype(BF16)
    w_pp_b = w_pp[0].astype(BF16)

    carry = jnp.zeros((N_EXPERTS, LANES), F32)
    routed = []
    for x, p in ((x_prompt, p_prompt[0]), (x_sample, p_sample[0])):
        bsz, s, _ = x.shape
        xt = x.reshape(bsz * s, D_MODEL)
        u, f_in = _inproj(xt, row(ln0_g), row(ln0_b), w_branch, b_branch)
        ya = _conv_branch(u.reshape(bsz, s, D_CONV), conv_w[0].astype(F32), row(conv_b[0]), row(lnc_g[0]),
                          row(lnc_b[0]), w_conv_out_b).reshape(bsz * s, D_MODEL)
        yb = _fourier_branch(f_in.reshape(bsz, s, D_FNET), w_fnet_out_b)
        h1, idx, gates, rank, carry = _mix_route(
            xt, ya, yb, carry, row(ln0_g), row(ln0_b), w_gate, b_gate, w_o_b, row(b_o[0]),
            row(ln1_g[0]), row(ln1_b[0]), w_rt, b_rt)
        routed.append((h1, idx, gates, rank, p.reshape(bsz * s, D_PLE), (bsz, s)))

    n_assign = sum(r[1].shape[1] for r in routed) * TOP_K
    n_blocks = (n_assign + N_EXPERTS * (BM_MOE - 1)) // BM_MOE
    counts = carry[:, 0].astype(I32)
    padded = ((counts + BM_MOE - 1) // BM_MOE) * BM_MOE
    ends = jnp.cumsum(padded)
    start = ends - padded
    nb_used = (ends[-1] // BM_MOE).reshape(1).astype(I32)
    blk = jnp.minimum(jnp.arange(n_blocks, dtype=I32), nb_used[0] - 1)
    blk_e = jnp.minimum(jnp.sum((ends[None, :] <= (blk * BM_MOE)[:, None]).astype(I32), axis=1), N_EXPERTS - 1)
    pad_runs = jnp.concatenate([jnp.stack([start + counts, padded - counts], axis=1).reshape(-1), nb_used,
                                jnp.zeros((LANES - 2 * N_EXPERTS - 1,), I32)]).astype(I32)

    slot_tables = []
    dispatch_slots = []
    for _, idx, _, rank, _, _ in routed:
        first = jnp.sum(jnp.where(idx[None] == jnp.arange(N_EXPERTS, dtype=I32)[:, None, None],
                                  start[:, None, None], 0), axis=0)
        slot_tables.append(_tile_slots(first + rank, TM_COMBINE))
        dispatch_slots.append(_tile_slots(first + rank, TM_ROWS))
    xs = _dispatch([r[0] for r in routed], jnp.concatenate(dispatch_slots, axis=0), pad_runs, n_blocks * BM_MOE)

    y = _moe_blocks(xs.reshape(-1, LANES), blk_e, nb_used, w_gu.reshape(N_EXPERTS, D_MODEL, 2 * D_FF), b_gu_r,
                    w_down.reshape(N_EXPERTS, D_FF, D_MODEL), b_down_r)

    outs = []
    for (h1, _, gates, _, p, (bsz, s)), slots in zip(routed, slot_tables):
        o = _combine_final(h1, p, gates.T, slots, y, w_pg_b, row(b_pg[0]), w_pp_b, row(ln2_g[0]), row(ln2_b[0]))
        outs.append(o.reshape(bsz, s, D_MODEL))
    return tuple(outs)
```

```python
import functools

import jax
import jax.numpy as jnp
from jax import lax
from jax.experimental import pallas as pl
from jax.experimental.pallas import tpu as pltpu

F32 = jnp.float32
BF16 = jnp.bfloat16
I32 = jnp.int32

D_MODEL = 1024
D_CONV = D_MODEL // 2
CONV_WIDTH = 31
CONV_PAD = CONV_WIDTH // 2
D_FNET = D_MODEL // 2
FNET_GROUPS = 4
FNET_GROUP_DIM = D_FNET // FNET_GROUPS
D_PLE = 256
N_EXPERTS = 32
TOP_K = 4
D_FF = D_MODEL
SWIGLU_ALPHA = 1.702
SWIGLU_LIMIT = 7.0
LN_EPS = 1e-5
DEPTH = 1
DEEPNORM_ALPHA = (2 * DEPTH) ** 0.25

LANES = 128
SUBLANES = 8
DFT_N1 = 128
HALO = 16

TM_INPROJ = 1024
TS_CONV = 512
RC_CONV = 256
TM_MIX = 512
SUB_MIX = 512
FFT2_TILES = 4
TM_ROWS = 1024
TM_COMBINE = 256
BM_MOE = 512
FF_CHUNK = 1024
ISSUE_UNROLL = 8
CAST_ROWS = 128
VMEM_LIMIT = 48 * 1024 * 1024
VMEM_LIMIT_MOE = 56 * 1024 * 1024


def _cparams(*sem):
    return pltpu.CompilerParams(dimension_semantics=sem, vmem_limit_bytes=VMEM_LIMIT)


def _layer_norm(x, g, b):
    mu = jnp.mean(x, axis=-1, keepdims=True)
    xc = x - mu
    var = jnp.mean(xc * xc, axis=-1, keepdims=True)
    return xc * lax.rsqrt(var + LN_EPS) * g + b


def _sigmoid(x):
    return 0.5 * jnp.tanh(0.5 * x) + 0.5


def _full(shape):
    return pl.BlockSpec(shape, lambda *_: (0,) * len(shape))


def _to_row_tiles(ref, row0, x):
    rows = x.shape[0]
    for j in range(D_MODEL // LANES):
        ref[pl.ds(row0 * SUBLANES + j, rows, stride=SUBLANES), :] = x[:, j * LANES:(j + 1) * LANES]


def _from_row_tiles(ref, row0, rows):
    return jnp.concatenate([ref[pl.ds(row0 * SUBLANES + j, rows, stride=SUBLANES), :]
                            for j in range(D_MODEL // LANES)], axis=1)


def _inproj_kernel(x_ref, g_ref, b_ref, w_ref, bias_ref, u_ref, f_ref):
    h = _layer_norm(x_ref[...], g_ref[...], b_ref[...])
    z = jnp.dot(h.astype(BF16), w_ref[...], preferred_element_type=F32) + bias_ref[...]
    u_ref[...] = (z[:, :D_CONV] * _sigmoid(z[:, D_CONV:2 * D_CONV])).astype(BF16)
    f_ref[...] = z[:, 2 * D_CONV:]


def _inproj(x, ln_g, ln_b, w, bias):
    t = x.shape[0]
    n_out = w.shape[1]
    return pl.pallas_call(
        _inproj_kernel,
        grid=(t // TM_INPROJ,),
        in_specs=[
            pl.BlockSpec((TM_INPROJ, D_MODEL), lambda i: (i, 0)),
            _full((1, D_MODEL)), _full((1, D_MODEL)),
            _full((D_MODEL, n_out)), _full((1, n_out)),
        ],
        out_specs=[
            pl.BlockSpec((TM_INPROJ, D_CONV), lambda i: (i, 0)),
            pl.BlockSpec((TM_INPROJ, D_FNET), lambda i: (i, 0)),
        ],
        out_shape=[jax.ShapeDtypeStruct((t, D_CONV), BF16), jax.ShapeDtypeStruct((t, D_FNET), F32)],
        compiler_params=_cparams("parallel"),
        name="inproj",
    )(x, ln_g, ln_b, w, bias)


def _conv_kernel(prev_ref, cur_ref, next_ref, cw_ref, cb_ref, g_ref, b_ref, wout_ref, ya_ref,
                 ext_ref, act_ref):
    i = pl.program_id(1)
    last = pl.num_programs(1) - 1
    ts = cur_ref.shape[1]
    ext_rows = ts + 2 * HALO
    zero = jnp.zeros((HALO, D_CONV), F32)
    ext_ref[0, 0:HALO, :] = jnp.where(i > 0, prev_ref[0].astype(F32), zero)
    ext_ref[0, HALO:HALO + ts, :] = cur_ref[0].astype(F32)
    ext_ref[0, HALO + ts:ext_rows, :] = jnp.where(i < last, next_ref[0].astype(F32), zero)
    for r in range(1, SUBLANES):
        ext_ref[r, 0:ext_rows - SUBLANES, :] = ext_ref[0, r:r + ext_rows - SUBLANES, :]

    cb = cb_ref[...]
    g = g_ref[...]
    b = b_ref[...]

    def chunk(c, carry):
        r0 = pl.multiple_of(c * RC_CONV, RC_CONV)
        acc = jnp.zeros((RC_CONV, D_CONV), F32) + cb
        for j in range(CONV_WIDTH):
            off = j + HALO - CONV_PAD
            q, r = divmod(off, SUBLANES)
            acc = acc + cw_ref[j:j + 1, :] * ext_ref[r, pl.ds(r0 + q * SUBLANES, RC_CONV), :]
        y = _layer_norm(acc, g, b)
        act_ref[pl.ds(r0, RC_CONV), :] = (y * _sigmoid(y)).astype(BF16)
        return carry

    lax.fori_loop(0, ts // RC_CONV, chunk, 0)
    ya_ref[0] = jnp.dot(act_ref[...], wout_ref[...], preferred_element_type=F32).astype(BF16)


def _conv_branch(u, conv_w, conv_b, ln_g, ln_b, w_out):
    bsz, s, _ = u.shape
    ts = TS_CONV
    nb = s // ts
    hb = ts // HALO
    return pl.pallas_call(
        _conv_kernel,
        grid=(bsz, nb),
        in_specs=[
            pl.BlockSpec((1, HALO, D_CONV), lambda b, i: (b, jnp.maximum(i * hb - 1, 0), 0)),
            pl.BlockSpec((1, ts, D_CONV), lambda b, i: (b, i, 0)),
            pl.BlockSpec((1, HALO, D_CONV), lambda b, i: (b, jnp.minimum((i + 1) * hb, s // HALO - 1), 0)),
            _full((CONV_WIDTH, D_CONV)), _full((1, D_CONV)), _full((1, D_CONV)), _full((1, D_CONV)),
            _full((D_CONV, D_MODEL)),
        ],
        out_specs=pl.BlockSpec((1, ts, D_MODEL), lambda b, i: (b, i, 0)),
        out_shape=jax.ShapeDtypeStruct((bsz, s, D_MODEL), BF16),
        scratch_shapes=[
            pltpu.VMEM((SUBLANES, ts + 2 * HALO, D_CONV), F32),
            pltpu.VMEM((ts, D_CONV), BF16),
        ],
        compiler_params=_cparams("parallel", "parallel"),
        name="conv_branch",
    )(u, u, u, conv_w, conv_b, ln_g, ln_b, w_out)


def _dft_tables(s):
    n2_len = s // DFT_N1
    k1_per_tile = DFT_N1 // n2_len
    two_pi = 2.0 * jnp.pi
    a = jnp.arange(DFT_N1, dtype=I32)
    ang1 = two_pi * ((a[:, None] * a[None, :]) % DFT_N1).astype(F32) / DFT_N1
    small = jnp.concatenate([jnp.cos(ang1), -jnp.sin(ang1)], axis=0)
    r = jnp.arange(2 * DFT_N1 * SUBLANES, dtype=I32)[:, None]
    c = jnp.arange(DFT_N1 * SUBLANES, dtype=I32)[None, :]
    pick_r = (r // SUBLANES == jnp.arange(2 * DFT_N1, dtype=I32)[None, :]).astype(F32)
    pick_c = (jnp.arange(DFT_N1, dtype=I32)[:, None] == c // SUBLANES).astype(F32)
    rep = jnp.dot(jnp.dot(pick_r, small, precision=lax.Precision.HIGHEST), pick_c, precision=lax.Precision.HIGHEST)
    f1 = jnp.where(r % SUBLANES == c % SUBLANES, rep, 0.0).astype(BF16)
    cs = jnp.stack([jnp.cos(ang1), jnp.sin(ang1)]).astype(BF16)
    n_tiles = s // DFT_N1
    t = jnp.arange(n_tiles, dtype=I32)[:, None, None]
    row = jnp.arange(DFT_N1, dtype=I32)[None, :, None]
    col = jnp.arange(DFT_N1, dtype=I32)[None, None, :]
    k2, k1l_out = row // k1_per_tile, row % k1_per_tile
    k1l_in, n2 = col // n2_len, col % n2_len
    k = t * k1_per_tile + k1l_out + DFT_N1 * k2
    hit = k1l_in == k1l_out
    if k1_per_tile == 1:
        ang_a = two_pi * ((n2 * t) % s).astype(F32) / s
        ang_b = two_pi * ((n2 * k2) % n2_len).astype(F32) / n2_len
        cos2 = jnp.cos(ang_a) * jnp.cos(ang_b) - jnp.sin(ang_a) * jnp.sin(ang_b)
        sin2 = jnp.sin(ang_a) * jnp.cos(ang_b) + jnp.cos(ang_a) * jnp.sin(ang_b)
    else:
        ang2 = two_pi * ((n2 * k) % s).astype(F32) / s
        cos2, sin2 = jnp.cos(ang2), jnp.sin(ang2)
    gr = jnp.where(hit, cos2, 0.0)
    gi = jnp.where(hit, -sin2, 0.0)
    g = jnp.concatenate([jnp.concatenate([gr, -gi], axis=2),
                         jnp.concatenate([gi, gr], axis=2)], axis=1).astype(BF16)
    return f1, g, cs


def _fft1_kernel(x_ref, f_ref, y_ref):
    nb = x_ref.shape[2]
    x = x_ref[0].reshape(DFT_N1 * nb, D_FNET).astype(BF16)
    r = jnp.dot(f_ref[...], x, preferred_element_type=F32)
    y_ref[0] = r.reshape(2, DFT_N1, nb, D_FNET)


def _fft2_kernel(y_ref, g_ref, cs_ref, w_ref, o_ref, *, scale, tiles):
    for t in range(tiles):
        rows = slice(t * DFT_N1, (t + 1) * DFT_N1)
        yb = jnp.concatenate([y_ref[0, 0, rows, :], y_ref[0, 1, rows, :]], axis=0).astype(BF16)
        z = jnp.dot(g_ref[t], yb, preferred_element_type=F32)
        zr = z[:DFT_N1].astype(BF16)
        zi = z[DFT_N1:].astype(BF16)
        parts = []
        cs2 = cs_ref[...].reshape(2 * DFT_N1, DFT_N1)
        for grp in range(FNET_GROUPS):
            sl = slice(grp * FNET_GROUP_DIM, (grp + 1) * FNET_GROUP_DIM)
            parts.append(jnp.dot(jnp.concatenate([zr[:, sl], zi[:, sl]], axis=1), cs2, preferred_element_type=F32))
        fm = (jnp.concatenate(parts, axis=1) * scale).astype(BF16)
        out = jnp.dot(fm, w_ref[...], preferred_element_type=F32)
        k1_per_tile = o_ref.shape[2] // tiles
        if k1_per_tile == 1:
            o_ref[0, :, t, :] = out
        else:
            o_ref[0, :, t * k1_per_tile:(t + 1) * k1_per_tile, :] = out.reshape(o_ref.shape[1], k1_per_tile, D_MODEL)


def _fourier_branch(f_in, w_out):
    bsz, s, _ = f_in.shape
    n2_len = s // DFT_N1
    k1_per_tile = DFT_N1 // n2_len
    f1, g, cs = _dft_tables(s)
    y = pl.pallas_call(
        _fft1_kernel,
        grid=(bsz, n2_len // SUBLANES),
        in_specs=[pl.BlockSpec((1, DFT_N1, SUBLANES, D_FNET), lambda b, j: (b, 0, j, 0)),
                  _full((2 * DFT_N1 * SUBLANES, DFT_N1 * SUBLANES))],
        out_specs=pl.BlockSpec((1, 2, DFT_N1, SUBLANES, D_FNET), lambda b, j: (b, 0, 0, j, 0)),
        out_shape=jax.ShapeDtypeStruct((bsz, 2, DFT_N1, n2_len, D_FNET), F32),
        compiler_params=_cparams("parallel", "parallel"),
        name="fft_stage1",
    )(f_in.reshape(bsz, DFT_N1, n2_len, D_FNET), f1)
    y = y.reshape(bsz, 2, s, D_FNET)
    scale = float((s * FNET_GROUP_DIM) ** -0.5)
    assert n2_len % SUBLANES == 0 and (k1_per_tile == 1 or k1_per_tile % SUBLANES == 0)
    tiles = SUBLANES if k1_per_tile == 1 else min(FFT2_TILES, n2_len)
    k1_per_step = tiles * k1_per_tile
    yb = pl.pallas_call(
        functools.partial(_fft2_kernel, scale=scale, tiles=tiles),
        grid=(bsz, s // (DFT_N1 * tiles)),
        in_specs=[
            pl.BlockSpec((1, 2, DFT_N1 * tiles, D_FNET), lambda b, t: (b, 0, t, 0)),
            pl.BlockSpec((tiles, 2 * DFT_N1, 2 * DFT_N1), lambda b, t: (t, 0, 0)),
            _full((2, DFT_N1, DFT_N1)),
            _full((D_FNET, D_MODEL)),
        ],
        out_specs=pl.BlockSpec((1, n2_len, k1_per_step, D_MODEL), lambda b, t: (b, 0, t, 0)),
        out_shape=jax.ShapeDtypeStruct((bsz, n2_len, DFT_N1, D_MODEL), F32),
        compiler_params=_cparams("parallel", "parallel"),
        name="fft_stage2",
    )(y, g, cs, w_out)
    return yb.reshape(bsz * s, D_MODEL)


def _mix_kernel(x_ref, ya_ref, yb_ref, cin_ref, ln0g_ref, ln0b_ref, wg_ref, bg_ref, wo_ref, bo_ref,
                ln1g_ref, ln1b_ref, wr_ref, br_ref,
                h1_ref, idx_ref, gate_ref, rank_ref, cnt_ref, carry_ref):
    tm = x_ref.shape[0]
    sub = min(SUB_MIX, tm)

    @pl.when(pl.program_id(0) == 0)
    def _():
        carry_ref[...] = cin_ref[...]

    carry = carry_ref[...]
    eio = lax.broadcasted_iota(I32, (N_EXPERTS, sub), 0)
    before = (lax.broadcasted_iota(I32, (sub, sub), 0) < lax.broadcasted_iota(I32, (sub, sub), 1)).astype(BF16)
    for r0 in range(0, tm, sub):
        rows = slice(r0, r0 + sub)
        h = _layer_norm(x_ref[rows, :], ln0g_ref[...], ln0b_ref[...])
        zg = jnp.dot(h.astype(BF16), wg_ref[...], preferred_element_type=F32) + bg_ref[...]
        m = (_sigmoid(zg[:, :D_MODEL]) * ya_ref[rows, :].astype(F32)
             + _sigmoid(zg[:, D_MODEL:]) * yb_ref[rows, :].astype(F32))
        mix = jnp.dot(m.astype(BF16), wo_ref[...], preferred_element_type=F32) + bo_ref[...]
        h1 = _layer_norm(DEEPNORM_ALPHA * h + mix, ln1g_ref[...], ln1b_ref[...])
        _to_row_tiles(h1_ref, r0, h1)

        h_hi = h1.astype(BF16)
        h_lo = (h1 - h_hi.astype(F32)).astype(BF16)
        hw = jnp.dot(h_hi, wr_ref[...], preferred_element_type=F32)
        lw = jnp.dot(h_lo, wr_ref[:, :LANES], preferred_element_type=F32)
        logits = (hw[:, :LANES] + hw[:, LANES:] + lw).T[:N_EXPERTS] + br_ref[...]
        vals, idxs = [], []
        cur = logits
        for _ in range(TOP_K):
            mx = jnp.max(cur, axis=0, keepdims=True)
            ik = jnp.min(jnp.where(cur == mx, eio, N_EXPERTS), axis=0, keepdims=True)
            vals.append(mx)
            idxs.append(ik)
            cur = jnp.where(eio == ik, -jnp.inf, cur)
        exps = [jnp.exp(v - vals[0]) for v in vals]
        den = exps[0] + exps[1] + exps[2] + exps[3]
        gate_ref[:, rows] = jnp.concatenate([e / den for e in exps], axis=0)
        idx_ref[:, rows] = jnp.concatenate(idxs, axis=0)

        hot = jnp.zeros((N_EXPERTS, sub), F32)
        for ik in idxs:
            hot = hot + (eio == ik).astype(F32)
        prior = jnp.dot(hot.astype(BF16), before, preferred_element_type=F32) + carry[:, 0:1]
        ranks = [jnp.sum(jnp.where(eio == ik, prior, 0.0), axis=0, keepdims=True) for ik in idxs]
        rank_ref[:, rows] = jnp.concatenate(ranks, axis=0).astype(I32)
        carry = carry + jnp.sum(hot, axis=1, keepdims=True)
    carry_ref[...] = carry
    cnt_ref[...] = carry


def _mix_route(x, ya, yb, carry_in, ln0_g, ln0_b, w_gate, b_gate, w_o, b_o, ln1_g, ln1_b, w_rt, b_rt):
    t = x.shape[0]
    tm = TM_MIX
    row = lambda d: pl.BlockSpec((tm, d), lambda i: (i, 0))
    col = lambda r: pl.BlockSpec((r, tm), lambda i: (0, i))
    return pl.pallas_call(
        _mix_kernel,
        grid=(t // tm,),
        in_specs=[
            row(D_MODEL), row(D_MODEL), row(D_MODEL), _full((N_EXPERTS, LANES)),
            _full((1, D_MODEL)), _full((1, D_MODEL)),
            _full((D_MODEL, 2 * D_MODEL)), _full((1, 2 * D_MODEL)),
            _full((D_MODEL, D_MODEL)), _full((1, D_MODEL)),
            _full((1, D_MODEL)), _full((1, D_MODEL)),
            _full((D_MODEL, 2 * LANES)), _full((N_EXPERTS, 1)),
        ],
        out_specs=[pl.BlockSpec((tm * SUBLANES, LANES), lambda i: (i, 0)), col(TOP_K), col(TOP_K), col(TOP_K),
                   _full((N_EXPERTS, LANES))],
        out_shape=[
            jax.ShapeDtypeStruct((t * SUBLANES, LANES), F32),
            jax.ShapeDtypeStruct((TOP_K, t), I32),
            jax.ShapeDtypeStruct((TOP_K, t), F32),
            jax.ShapeDtypeStruct((TOP_K, t), I32),
            jax.ShapeDtypeStruct((N_EXPERTS, LANES), F32),
        ],
        scratch_shapes=[pltpu.VMEM((N_EXPERTS, LANES), F32)],
        compiler_params=_cparams("arbitrary"),
        name="mix_route",
    )(x, ya, yb, carry_in, ln0_g, ln0_b, w_gate, b_gate, w_o, b_o, ln1_g, ln1_b, w_rt, b_rt)


def _row_copy(src_ref, src_row, dst_ref, dst_row, sem):
    return pltpu.make_async_copy(src_ref.at[src_row], dst_ref.at[dst_row], sem)


def _fetch_slots(slots_hbm, slots_smem, sem, step, slot):
    n = slots_hbm.shape[1]
    return pltpu.make_async_copy(slots_hbm.at[step], slots_smem.at[pl.ds(pl.multiple_of(slot * n, n), n)],
                                 sem.at[slot])


def _pad_fill(h_ref, xs_out, pad_smem, pad_sem, n_pad_runs, wait):
    def go(cp):
        if wait:
            cp.wait()
        else:
            cp.start()

    def run(j, carry):
        start = pad_smem[2 * j]
        length = pad_smem[2 * j + 1]
        for bit in range(BM_MOE.bit_length() - 1):
            size = 1 << bit

            @pl.when((length & size) != 0)
            def _():
                off = start + (length & ~(2 * size - 1))
                go(pltpu.make_async_copy(h_ref.at[pl.ds(0, size)], xs_out.at[pl.ds(off, size)], pad_sem))
        return carry

    lax.fori_loop(0, n_pad_runs, run, 0)

    piece = min(h_ref.shape[0], BM_MOE)

    def dead_block(j, carry):
        for part in range(BM_MOE // piece):
            off = j * BM_MOE + part * piece
            go(pltpu.make_async_copy(h_ref.at[pl.ds(0, piece)], xs_out.at[pl.ds(off, piece)], pad_sem))
        return carry

    lax.fori_loop(pad_smem[2 * n_pad_runs], xs_out.shape[0] // BM_MOE, dead_block, 0)


def _dispatch_kernel(*refs, tiles):
    n_sets = len(tiles)
    h_refs = refs[:n_sets]
    slots_hbm, pad_hbm, xs_out, slots_smem, pad_smem, slot_sem, pad_sem, row_sem = refs[n_sets:]
    i = pl.program_id(0)
    n = pl.num_programs(0)
    tm = h_refs[0].shape[0]
    cur = lax.rem(i, 2)

    @pl.when(i == 0)
    def _():
        _fetch_slots(slots_hbm, slots_smem, slot_sem, 0, 0).start()
        fetch = pltpu.make_async_copy(pad_hbm, pad_smem, pad_sem)
        fetch.start()
        fetch.wait()
        _pad_fill(h_refs[0], xs_out, pad_smem, pad_sem, N_EXPERTS, wait=False)
        _pad_fill(h_refs[0], xs_out, pad_smem, pad_sem, N_EXPERTS, wait=True)

    @pl.when(i + 1 < n)
    def _():
        _fetch_slots(slots_hbm, slots_smem, slot_sem, i + 1, 1 - cur).start()

    _fetch_slots(slots_hbm, slots_smem, slot_sem, i, cur).wait()
    base = cur * (TOP_K * tm)

    first = 0
    for h_ref, count in zip(h_refs, tiles):
        @pl.when(jnp.logical_and(i >= first, i < first + count))
        def _(h_ref=h_ref):
            def issue(r, carry):
                for k in range(TOP_K):
                    _row_copy(h_ref, r, xs_out, slots_smem[base + k * tm + r], row_sem).start(priority=k % 2)
                return carry

            lax.fori_loop(0, tm, issue, 0, unroll=ISSUE_UNROLL)
            for _ in range(TOP_K):
                pltpu.make_async_copy(h_ref, xs_out.at[pl.ds(0, tm)], row_sem).wait()
        first += count


def _dispatch(h1s, slots, pad_runs, n_rows):
    tm = TM_ROWS
    h1s = [h.reshape(-1, SUBLANES, LANES) for h in h1s]
    tiles = [h.shape[0] // tm for h in h1s]
    firsts = [sum(tiles[:j]) for j in range(len(tiles))]

    def tile_of(first, count):
        return lambda i: (jnp.clip(i - first, 0, count - 1), 0, 0)

    return pl.pallas_call(
        functools.partial(_dispatch_kernel, tiles=tuple(tiles)),
        grid=(sum(tiles),),
        in_specs=[pl.BlockSpec((tm, SUBLANES, LANES), tile_of(f, c)) for f, c in zip(firsts, tiles)]
        + [pl.BlockSpec(memory_space=pl.ANY)] * 2,
        out_specs=pl.BlockSpec(memory_space=pl.ANY),
        out_shape=jax.ShapeDtypeStruct((n_rows, SUBLANES, LANES), F32),
        scratch_shapes=[
            pltpu.SMEM((2 * TOP_K * tm,), I32),
            pltpu.SMEM((pad_runs.shape[0],), I32),
            pltpu.SemaphoreType.DMA((2,)),
            pltpu.SemaphoreType.DMA,
            pltpu.SemaphoreType.DMA,
        ],
        compiler_params=_cparams("arbitrary"),
        name="dispatch",
    )(*h1s, slots, pad_runs)


def _moe_kernel(blk_e_ref, nb_ref, x_ref, wgu_ref, bgu_ref, wd_ref, bd_ref, o_ref, wgu_b, wd_b):
    i = pl.program_id(0)
    is_live = i < nb_ref[0]

    @pl.when(jnp.logical_not(is_live))
    def _():
        o_ref[...] = jnp.zeros(o_ref.shape, F32)

    @pl.when(jnp.logical_or(i == 0, blk_e_ref[i] != blk_e_ref[jnp.maximum(i - 1, 0)]))
    def _():
        for r in range(0, D_MODEL, CAST_ROWS):
            wgu_b[r:r + CAST_ROWS, :] = wgu_ref[0, r:r + CAST_ROWS, :].astype(BF16)
        for r in range(0, D_FF, CAST_ROWS):
            wd_b[r:r + CAST_ROWS, :] = wd_ref[0, r:r + CAST_ROWS, :].astype(BF16)

    @pl.when(is_live)
    def _():
        x = _from_row_tiles(x_ref, 0, BM_MOE).astype(BF16)
        acc = jnp.zeros((BM_MOE, D_MODEL), F32) + bd_ref[0]
        for c in range(D_FF // FF_CHUNK):
            lo = c * FF_CHUNK
            g = jnp.dot(x, wgu_b[:, lo:lo + FF_CHUNK], preferred_element_type=F32) + bgu_ref[0, :, lo:lo + FF_CHUNK]
            u = (jnp.dot(x, wgu_b[:, D_FF + lo:D_FF + lo + FF_CHUNK], preferred_element_type=F32)
                 + bgu_ref[0, :, D_FF + lo:D_FF + lo + FF_CHUNK])
            g = jnp.minimum(g, SWIGLU_LIMIT)
            u = jnp.clip(u, -SWIGLU_LIMIT, SWIGLU_LIMIT)
            act = (u + 1.0) * (g * _sigmoid(g * SWIGLU_ALPHA))
            acc = acc + jnp.dot(act.astype(BF16), wd_b[lo:lo + FF_CHUNK, :], preferred_element_type=F32)
        _to_row_tiles(o_ref, 0, acc)


def _moe_blocks(xs, blk_e, nb_used, w_gu, b_gu, w_down, b_down):
    p = xs.shape[0] // SUBLANES
    n_blocks = p // BM_MOE
    live = lambda i, nb: jnp.minimum(i, nb[0] - 1)
    grid_spec = pltpu.PrefetchScalarGridSpec(
        num_scalar_prefetch=2,
        grid=(n_blocks,),
        in_specs=[
            pl.BlockSpec((BM_MOE * SUBLANES, LANES), lambda i, be, nb: (live(i, nb), 0)),
            pl.BlockSpec((1, D_MODEL, 2 * D_FF), lambda i, be, nb: (be[i], 0, 0)),
            pl.BlockSpec((1, 1, 2 * D_FF), lambda i, be, nb: (be[i], 0, 0)),
            pl.BlockSpec((1, D_FF, D_MODEL), lambda i, be, nb: (be[i], 0, 0)),
            pl.BlockSpec((1, 1, D_MODEL), lambda i, be, nb: (be[i], 0, 0)),
        ],
        out_specs=pl.BlockSpec((BM_MOE * SUBLANES, LANES), lambda i, be, nb: (i, 0)),
        scratch_shapes=[pltpu.VMEM((D_MODEL, 2 * D_FF), BF16), pltpu.VMEM((D_FF, D_MODEL), BF16)],
    )
    return pl.pallas_call(
        _moe_kernel,
        grid_spec=grid_spec,
        out_shape=jax.ShapeDtypeStruct((p * SUBLANES, LANES), F32),
        compiler_params=pltpu.CompilerParams(dimension_semantics=("arbitrary",), vmem_limit_bytes=VMEM_LIMIT_MOE),
        name="moe_experts",
    )(blk_e, nb_used, xs, w_gu, b_gu, w_down, b_down)


def _final_kernel(h_ref, p_ref, gate_ref, slots_hbm, y_hbm, wpg_ref, bpg_ref, wpp_ref, g_ref, b_ref, o_ref,
                  slots_smem, rows_ref, slot_sem, row_sem):
    i = pl.program_id(0)
    n = pl.num_programs(0)
    tm = h_ref.shape[0] // SUBLANES

    def gather(r, buf, base):
        for k in range(TOP_K):
            slot = slots_smem[base + k * tm + r]
            src = y_hbm.at[pl.ds(pl.multiple_of(slot * SUBLANES, SUBLANES), SUBLANES)]
            dst = rows_ref.at[buf, k, pl.ds(pl.multiple_of(r * SUBLANES, SUBLANES), SUBLANES)]
            pltpu.make_async_copy(src, dst, row_sem.at[buf]).start(priority=k % 2)

    def wait_rows(buf):
        for k in range(TOP_K):
            pltpu.make_async_copy(y_hbm.at[pl.ds(0, tm * SUBLANES)], rows_ref.at[buf, k], row_sem.at[buf]).wait()

    @pl.when(i == 0)
    def _():
        first = _fetch_slots(slots_hbm, slots_smem, slot_sem, 0, 0)
        first.start()
        first.wait()

        def issue(r, carry):
            gather(r, 0, 0)
            return carry

        lax.fori_loop(0, tm, issue, 0, unroll=ISSUE_UNROLL)
        _fetch_slots(slots_hbm, slots_smem, slot_sem, 1, 1).start()

    def step(cur):
        nxt = 1 - cur
        _fetch_slots(slots_hbm, slots_smem, slot_sem, i + 1, nxt).wait()

        @pl.when(i + 2 <= n)
        def _():
            _fetch_slots(slots_hbm, slots_smem, slot_sem, i + 2, cur).start()

        for r in range(tm):
            gather(r, nxt, nxt * (TOP_K * tm))

        h = _from_row_tiles(h_ref, 0, tm)
        ple = (_sigmoid(jnp.dot(h.astype(BF16), wpg_ref[...], preferred_element_type=F32) + bpg_ref[...])
               * jnp.dot(p_ref[...].astype(BF16), wpp_ref[...], preferred_element_type=F32))

        wait_rows(cur)
        gates = gate_ref[...]
        ffn = gates[:, 0:1] * _from_row_tiles(rows_ref.at[cur, 0], 0, tm)
        for k in range(1, TOP_K):
            ffn = ffn + gates[:, k:k + 1] * _from_row_tiles(rows_ref.at[cur, k], 0, tm)
        o_ref[...] = _layer_norm(DEEPNORM_ALPHA * h + (ffn + ple), g_ref[...], b_ref[...])

        @pl.when(i == n - 1)
        def _():
            wait_rows(nxt)

    for parity in range(2):
        pl.when(lax.rem(i, 2) == parity)(functools.partial(step, parity))


def _combine_final(h1, p, gates, slots, y, w_pg, b_pg, w_pp, ln_g, ln_b):
    t = h1.shape[0] // SUBLANES
    tm = TM_COMBINE
    slots = jnp.concatenate([slots, slots[-1:]], axis=0)
    return pl.pallas_call(
        _final_kernel,
        grid=(t // tm,),
        in_specs=[
            pl.BlockSpec((tm * SUBLANES, LANES), lambda i: (i, 0)),
            pl.BlockSpec((tm, D_PLE), lambda i: (i, 0)),
            pl.BlockSpec((tm, TOP_K), lambda i: (i, 0)),
            pl.BlockSpec(memory_space=pl.ANY),
            pl.BlockSpec(memory_space=pl.ANY),
            _full((D_MODEL, D_MODEL)), _full((1, D_MODEL)), _full((D_PLE, D_MODEL)),
            _full((1, D_MODEL)), _full((1, D_MODEL)),
        ],
        out_specs=pl.BlockSpec((tm, D_MODEL), lambda i: (i, 0)),
        out_shape=jax.ShapeDtypeStruct((t, D_MODEL), F32),
        scratch_shapes=[
            pltpu.SMEM((2 * TOP_K * tm,), I32),
            pltpu.VMEM((2, TOP_K, tm * SUBLANES, LANES), F32),
            pltpu.SemaphoreType.DMA((2,)),
            pltpu.SemaphoreType.DMA((2,)),
        ],
        compiler_params=_cparams("arbitrary"),
        name="combine_final",
    )(h1, p, gates, slots, y, w_pg, b_pg, w_pp, ln_g, ln_b)


def _tile_slots(dest, tm):
    t = dest.shape[1]
    return dest.reshape(TOP_K, t // tm, tm).transpose(1, 0, 2).reshape(t // tm, TOP_K * tm)


def kernel(x_prompt, x_sample, p_prompt, p_sample, ln0_g, ln0_b, w_in, b_in, conv_w, conv_b, lnc_g, lnc_b, w_conv_out, w_fnet_out, w_o, b_o, ln1_g, ln1_b, w_router, b_router, w_gu, b_gu, w_down, b_down, w_pg, b_pg, w_pp, ln2_g, ln2_b):
    row = lambda v: v.reshape(1, -1).astype(F32)
    n_branch = 2 * D_CONV + D_FNET
    w_in_b = w_in[0].astype(BF16)
    w_branch, w_gate = w_in_b[:, :n_branch], w_in_b[:, n_branch:]
    b_branch, b_gate = row(b_in[0, :n_branch]), row(b_in[0, n_branch:])
    w_conv_out_b = w_conv_out[0].astype(BF16)
    w_fnet_out_b = w_fnet_out[0].astype(BF16)
    w_o_b = w_o[0].astype(BF16)
    w_rt_f = jnp.pad(w_router[0].astype(F32), ((0, 0), (0, LANES - N_EXPERTS)))
    w_rt_hi = w_rt_f.astype(BF16)
    w_rt = jnp.concatenate([w_rt_hi, (w_rt_f - w_rt_hi.astype(F32)).astype(BF16)], axis=1)
    b_rt = b_router[0].reshape(N_EXPERTS, 1).astype(F32)
    b_gu_r = b_gu[0].reshape(N_EXPERTS, 1, 2 * D_FF).astype(F32)
    b_down_r = b_down[0].reshape(N_EXPERTS, 1, D_MODEL).astype(F32)
    w_pg_b = w_pg[0].astype(BF16)
    w_pp_b = w_pp[0].astype(BF16)

    carry = jnp.zeros((N_EXPERTS, LANES), F32)
    routed = []
    for x, p in ((x_prompt, p_prompt[0]), (x_sample, p_sample[0])):
        bsz, s, _ = x.shape
        xt = x.reshape(bsz * s, D_MODEL)
        u, f_in = _inproj(xt, row(ln0_g), row(ln0_b), w_branch, b_branch)
        ya = _conv_branch(u.reshape(bsz, s, D_CONV), conv_w[0].astype(F32), row(conv_b[0]), row(lnc_g[0]),
                          row(lnc_b[0]), w_conv_out_b).reshape(bsz * s, D_MODEL)
        yb = _fourier_branch(f_in.reshape(bsz, s, D_FNET), w_fnet_out_b)
        h1, idx, gates, rank, carry = _mix_route(
            xt, ya, yb, carry, row(ln0_g), row(ln0_b), w_gate, b_gate, w_o_b, row(b_o[0]),
            row(ln1_g[0]), row(ln1_b[0]), w_rt, b_rt)
        routed.append((h1, idx, gates, rank, p.reshape(bsz * s, D_PLE), (bsz, s)))

    n_assign = sum(r[1].shape[1] for r in routed) * TOP_K
    n_blocks = (n_assign + N_EXPERTS * (BM_MOE - 1)) // BM_MOE
    counts = carry[:, 0].astype(I32)
    padded = ((counts + BM_MOE - 1) // BM_MOE) * BM_MOE
    ends = jnp.cumsum(padded)
    start = ends - padded
    nb_used = (ends[-1] // BM_MOE).reshape(1).astype(I32)
    blk = jnp.minimum(jnp.arange(n_blocks, dtype=I32), nb_used[0] - 1)
    blk_e = jnp.minimum(jnp.sum((ends[None, :] <= (blk * BM_MOE)[:, None]).astype(I32), axis=1), N_EXPERTS - 1)
    pad_runs = jnp.concatenate([jnp.stack([start + counts, padded - counts], axis=1).reshape(-1), nb_used,
                                jnp.zeros((LANES - 2 * N_EXPERTS - 1,), I32)]).astype(I32)

    slot_tables = []
    dispatch_slots = []
    for _, idx, _, rank, _, _ in routed:
        first = jnp.sum(jnp.where(idx[None] == jnp.arange(N_EXPERTS, dtype=I32)[:, None, None],
                                  start[:, None, None], 0), axis=0)
        slot_tables.append(_tile_slots(first + rank, TM_COMBINE))
        dispatch_slots.append(_tile_slots(first + rank, TM_ROWS))
    xs = _dispatch([r[0] for r in routed], jnp.concatenate(dispatch_slots, axis=0), pad_runs, n_blocks * BM_MOE)

    y = _moe_blocks(xs.reshape(-1, LANES), blk_e, nb_used, w_gu.reshape(N_EXPERTS, D_MODEL, 2 * D_FF), b_gu_r,
                    w_down.reshape(N_EXPERTS, D_FF, D_MODEL), b_down_r)

    outs = []
    for (h1, _, gates, _, p, (bsz, s)), slots in zip(routed, slot_tables):
        o = _combine_final(h1, p, gates.T, slots, y, w_pg_b, row(b_pg[0]), w_pp_b, row(ln2_g[0]), row(ln2_b[0]))
        outs.append(o.reshape(bsz, s, D_MODEL))
    return tuple(outs)
```

```python
import functools

import jax
import jax.numpy as jnp
from jax import lax
from jax.experimental import pallas as pl
from jax.experimental.pallas import tpu as pltpu

F32 = jnp.float32
BF16 = jnp.bfloat16
I32 = jnp.int32

D_MODEL = 1024
D_CONV = D_MODEL // 2
CONV_WIDTH = 31
CONV_PAD = CONV_WIDTH // 2
D_FNET = D_MODEL // 2
FNET_GROUPS = 4
FNET_GROUP_DIM = D_FNET // FNET_GROUPS
D_PLE = 256
N_EXPERTS = 32
TOP_K = 4
D_FF = D_MODEL
SWIGLU_ALPHA = 1.702
SWIGLU_LIMIT = 7.0
LN_EPS = 1e-5
DEPTH = 1
DEEPNORM_ALPHA = (2 * DEPTH) ** 0.25

LANES = 128
SUBLANES = 8
DFT_N1 = 128
HALO = 16

TM_INPROJ = 1024
TS_CONV = 512
RC_CONV = 256
TM_MIX = 512
SUB_MIX = 512
FFT2_TILES = 4
TM_ROWS = 1024
TM_COMBINE = 256
BM_MOE = 512
FF_CHUNK = 1024
ISSUE_UNROLL = 8
RING = 3
CAST_ROWS = 128
VMEM_LIMIT = 48 * 1024 * 1024
VMEM_LIMIT_MOE = 56 * 1024 * 1024


def _cparams(*sem):
    return pltpu.CompilerParams(dimension_semantics=sem, vmem_limit_bytes=VMEM_LIMIT)


def _layer_norm(x, g, b):
    mu = jnp.mean(x, axis=-1, keepdims=True)
    xc = x - mu
    var = jnp.mean(xc * xc, axis=-1, keepdims=True)
    return xc * lax.rsqrt(var + LN_EPS) * g + b


def _sigmoid(x):
    return 0.5 * jnp.tanh(0.5 * x) + 0.5


def _full(shape):
    return pl.BlockSpec(shape, lambda *_: (0,) * len(shape))


def _to_row_tiles(ref, row0, x):
    rows = x.shape[0]
    for j in range(D_MODEL // LANES):
        ref[pl.ds(row0 * SUBLANES + j, rows, stride=SUBLANES), :] = x[:, j * LANES:(j + 1) * LANES]


def _from_row_tiles(ref, row0, rows):
    return jnp.concatenate([ref[pl.ds(row0 * SUBLANES + j, rows, stride=SUBLANES), :]
                            for j in range(D_MODEL // LANES)], axis=1)


def _inproj_kernel(x_ref, g_ref, b_ref, w_ref, bias_ref, u_ref, f_ref):
    h = _layer_norm(x_ref[...], g_ref[...], b_ref[...])
    z = jnp.dot(h.astype(BF16), w_ref[...], preferred_element_type=F32) + bias_ref[...]
    u_ref[...] = (z[:, :D_CONV] * _sigmoid(z[:, D_CONV:2 * D_CONV])).astype(BF16)
    f_ref[...] = z[:, 2 * D_CONV:]


def _inproj(x, ln_g, ln_b, w, bias):
    t = x.shape[0]
    n_out = w.shape[1]
    return pl.pallas_call(
        _inproj_kernel,
        grid=(t // TM_INPROJ,),
        in_specs=[
            pl.BlockSpec((TM_INPROJ, D_MODEL), lambda i: (i, 0)),
            _full((1, D_MODEL)), _full((1, D_MODEL)),
            _full((D_MODEL, n_out)), _full((1, n_out)),
        ],
        out_specs=[
            pl.BlockSpec((TM_INPROJ, D_CONV), lambda i: (i, 0)),
            pl.BlockSpec((TM_INPROJ, D_FNET), lambda i: (i, 0)),
        ],
        out_shape=[jax.ShapeDtypeStruct((t, D_CONV), BF16), jax.ShapeDtypeStruct((t, D_FNET), F32)],
        compiler_params=_cparams("parallel"),
        name="inproj",
    )(x, ln_g, ln_b, w, bias)


def _conv_kernel(prev_ref, cur_ref, next_ref, cw_ref, cb_ref, g_ref, b_ref, wout_ref, ya_ref,
                 ext_ref, act_ref):
    i = pl.program_id(1)
    last = pl.num_programs(1) - 1
    ts = cur_ref.shape[1]
    ext_rows = ts + 2 * HALO
    zero = jnp.zeros((HALO, D_CONV), F32)
    ext_ref[0, 0:HALO, :] = jnp.where(i > 0, prev_ref[0].astype(F32), zero)
    ext_ref[0, HALO:HALO + ts, :] = cur_ref[0].astype(F32)
    ext_ref[0, HALO + ts:ext_rows, :] = jnp.where(i < last, next_ref[0].astype(F32), zero)
    for r in range(1, SUBLANES):
        ext_ref[r, 0:ext_rows - SUBLANES, :] = ext_ref[0, r:r + ext_rows - SUBLANES, :]

    cb = cb_ref[...]
    g = g_ref[...]
    b = b_ref[...]

    def chunk(c, carry):
        r0 = pl.multiple_of(c * RC_CONV, RC_CONV)
        acc = jnp.zeros((RC_CONV, D_CONV), F32) + cb
        for j in range(CONV_WIDTH):
            off = j + HALO - CONV_PAD
            q, r = divmod(off, SUBLANES)
            acc = acc + cw_ref[j:j + 1, :] * ext_ref[r, pl.ds(r0 + q * SUBLANES, RC_CONV), :]
        y = _layer_norm(acc, g, b)
        act_ref[pl.ds(r0, RC_CONV), :] = (y * _sigmoid(y)).astype(BF16)
        return carry

    lax.fori_loop(0, ts // RC_CONV, chunk, 0)
    ya_ref[0] = jnp.dot(act_ref[...], wout_ref[...], preferred_element_type=F32).astype(BF16)


def _conv_branch(u, conv_w, conv_b, ln_g, ln_b, w_out):
    bsz, s, _ = u.shape
    ts = TS_CONV
    nb = s // ts
    hb = ts // HALO
    return pl.pallas_call(
        _conv_kernel,
        grid=(bsz, nb),
        in_specs=[
            pl.BlockSpec((1, HALO, D_CONV), lambda b, i: (b, jnp.maximum(i * hb - 1, 0), 0)),
            pl.BlockSpec((1, ts, D_CONV), lambda b, i: (b, i, 0)),
            pl.BlockSpec((1, HALO, D_CONV), lambda b, i: (b, jnp.minimum((i + 1) * hb, s // HALO - 1), 0)),
            _full((CONV_WIDTH, D_CONV)), _full((1, D_CONV)), _full((1, D_CONV)), _full((1, D_CONV)),
            _full((D_CONV, D_MODEL)),
        ],
        out_specs=pl.BlockSpec((1, ts, D_MODEL), lambda b, i: (b, i, 0)),
        out_shape=jax.ShapeDtypeStruct((bsz, s, D_MODEL), BF16),
        scratch_shapes=[
            pltpu.VMEM((SUBLANES, ts + 2 * HALO, D_CONV), F32),
            pltpu.VMEM((ts, D_CONV), BF16),
        ],
        compiler_params=_cparams("parallel", "parallel"),
        name="conv_branch",
    )(u, u, u, conv_w, conv_b, ln_g, ln_b, w_out)


def _dft_tables(s):
    n2_len = s // DFT_N1
    k1_per_tile = DFT_N1 // n2_len
    two_pi = 2.0 * jnp.pi
    a = jnp.arange(DFT_N1, dtype=I32)
    ang1 = two_pi * ((a[:, None] * a[None, :]) % DFT_N1).astype(F32) / DFT_N1
    small = jnp.concatenate([jnp.cos(ang1), -jnp.sin(ang1)], axis=0)
    r = jnp.arange(2 * DFT_N1 * SUBLANES, dtype=I32)[:, None]
    c = jnp.arange(DFT_N1 * SUBLANES, dtype=I32)[None, :]
    pick_r = (r // SUBLANES == jnp.arange(2 * DFT_N1, dtype=I32)[None, :]).astype(F32)
    pick_c = (jnp.arange(DFT_N1, dtype=I32)[:, None] == c // SUBLANES).astype(F32)
    rep = jnp.dot(jnp.dot(pick_r, small, precision=lax.Precision.HIGHEST), pick_c, precision=lax.Precision.HIGHEST)
    f1 = jnp.where(r % SUBLANES == c % SUBLANES, rep, 0.0).astype(BF16)
    cs = jnp.stack([jnp.cos(ang1), jnp.sin(ang1)]).astype(BF16)
    n_tiles = s // DFT_N1
    t = jnp.arange(n_tiles, dtype=I32)[:, None, None]
    row = jnp.arange(DFT_N1, dtype=I32)[None, :, None]
    col = jnp.arange(DFT_N1, dtype=I32)[None, None, :]
    k2, k1l_out = row // k1_per_tile, row % k1_per_tile
    k1l_in, n2 = col // n2_len, col % n2_len
    k = t * k1_per_tile + k1l_out + DFT_N1 * k2
    hit = k1l_in == k1l_out
    if k1_per_tile == 1:
        ang_a = two_pi * ((n2 * t) % s).astype(F32) / s
        ang_b = two_pi * ((n2 * k2) % n2_len).astype(F32) / n2_len
        cos2 = jnp.cos(ang_a) * jnp.cos(ang_b) - jnp.sin(ang_a) * jnp.sin(ang_b)
        sin2 = jnp.sin(ang_a) * jnp.cos(ang_b) + jnp.cos(ang_a) * jnp.sin(ang_b)
    else:
        ang2 = two_pi * ((n2 * k) % s).astype(F32) / s
        cos2, sin2 = jnp.cos(ang2), jnp.sin(ang2)
    gr = jnp.where(hit, cos2, 0.0)
    gi = jnp.where(hit, -sin2, 0.0)
    g = jnp.concatenate([jnp.concatenate([gr, -gi], axis=2),
                         jnp.concatenate([gi, gr], axis=2)], axis=1).astype(BF16)
    return f1, g, cs


def _fft1_kernel(x_ref, f_ref, y_ref):
    nb = x_ref.shape[2]
    x = x_ref[0].reshape(DFT_N1 * nb, D_FNET).astype(BF16)
    r = jnp.dot(f_ref[...], x, preferred_element_type=F32)
    y_ref[0] = r.reshape(2, DFT_N1, nb, D_FNET)


def _fft2_kernel(y_ref, g_ref, cs_ref, w_ref, o_ref, *, scale, tiles):
    for t in range(tiles):
        rows = slice(t * DFT_N1, (t + 1) * DFT_N1)
        yb = jnp.concatenate([y_ref[0, 0, rows, :], y_ref[0, 1, rows, :]], axis=0).astype(BF16)
        z = jnp.dot(g_ref[t], yb, preferred_element_type=F32)
        zr = z[:DFT_N1].astype(BF16)
        zi = z[DFT_N1:].astype(BF16)
        parts = []
        cs2 = cs_ref[...].reshape(2 * DFT_N1, DFT_N1)
        for grp in range(FNET_GROUPS):
            sl = slice(grp * FNET_GROUP_DIM, (grp + 1) * FNET_GROUP_DIM)
            parts.append(jnp.dot(jnp.concatenate([zr[:, sl], zi[:, sl]], axis=1), cs2, preferred_element_type=F32))
        fm = (jnp.concatenate(parts, axis=1) * scale).astype(BF16)
        out = jnp.dot(fm, w_ref[...], preferred_element_type=F32)
        k1_per_tile = o_ref.shape[2] // tiles
        if k1_per_tile == 1:
            o_ref[0, :, t, :] = out
        else:
            o_ref[0, :, t * k1_per_tile:(t + 1) * k1_per_tile, :] = out.reshape(o_ref.shape[1], k1_per_tile, D_MODEL)


def _fourier_branch(f_in, w_out):
    bsz, s, _ = f_in.shape
    n2_len = s // DFT_N1
    k1_per_tile = DFT_N1 // n2_len
    f1, g, cs = _dft_tables(s)
    y = pl.pallas_call(
        _fft1_kernel,
        grid=(bsz, n2_len // SUBLANES),
        in_specs=[pl.BlockSpec((1, DFT_N1, SUBLANES, D_FNET), lambda b, j: (b, 0, j, 0)),
                  _full((2 * DFT_N1 * SUBLANES, DFT_N1 * SUBLANES))],
        out_specs=pl.BlockSpec((1, 2, DFT_N1, SUBLANES, D_FNET), lambda b, j: (b, 0, 0, j, 0)),
        out_shape=jax.ShapeDtypeStruct((bsz, 2, DFT_N1, n2_len, D_FNET), F32),
        compiler_params=_cparams("parallel", "parallel"),
        name="fft_stage1",
    )(f_in.reshape(bsz, DFT_N1, n2_len, D_FNET), f1)
    y = y.reshape(bsz, 2, s, D_FNET)
    scale = float((s * FNET_GROUP_DIM) ** -0.5)
    assert n2_len % SUBLANES == 0 and (k1_per_tile == 1 or k1_per_tile % SUBLANES == 0)
    tiles = SUBLANES if k1_per_tile == 1 else min(FFT2_TILES, n2_len)
    k1_per_step = tiles * k1_per_tile
    yb = pl.pallas_call(
        functools.partial(_fft2_kernel, scale=scale, tiles=tiles),
        grid=(bsz, s // (DFT_N1 * tiles)),
        in_specs=[
            pl.BlockSpec((1, 2, DFT_N1 * tiles, D_FNET), lambda b, t: (b, 0, t, 0)),
            pl.BlockSpec((tiles, 2 * DFT_N1, 2 * DFT_N1), lambda b, t: (t, 0, 0)),
            _full((2, DFT_N1, DFT_N1)),
            _full((D_FNET, D_MODEL)),
        ],
        out_specs=pl.BlockSpec((1, n2_len, k1_per_step, D_MODEL), lambda b, t: (b, 0, t, 0)),
        out_shape=jax.ShapeDtypeStruct((bsz, n2_len, DFT_N1, D_MODEL), F32),
        compiler_params=_cparams("parallel", "parallel"),
        name="fft_stage2",
    )(y, g, cs, w_out)
    return yb.reshape(bsz * s, D_MODEL)


def _mix_kernel(x_ref, ya_ref, yb_ref, cin_ref, ln0g_ref, ln0b_ref, wg_ref, bg_ref, wo_ref, bo_ref,
                ln1g_ref, ln1b_ref, wr_ref, br_ref,
                h1_ref, idx_ref, gate_ref, rank_ref, cnt_ref, carry_ref):
    tm = x_ref.shape[0]
    sub = min(SUB_MIX, tm)

    @pl.when(pl.program_id(0) == 0)
    def _():
        carry_ref[...] = cin_ref[...]

    carry = carry_ref[...]
    eio = lax.broadcasted_iota(I32, (N_EXPERTS, sub), 0)
    before = (lax.broadcasted_iota(I32, (sub, sub), 0) < lax.broadcasted_iota(I32, (sub, sub), 1)).astype(BF16)
    for r0 in range(0, tm, sub):
        rows = slice(r0, r0 + sub)
        h = _layer_norm(x_ref[rows, :], ln0g_ref[...], ln0b_ref[...])
        zg = jnp.dot(h.astype(BF16), wg_ref[...], preferred_element_type=F32) + bg_ref[...]
        m = (_sigmoid(zg[:, :D_MODEL]) * ya_ref[rows, :].astype(F32)
             + _sigmoid(zg[:, D_MODEL:]) * yb_ref[rows, :].astype(F32))
        mix = jnp.dot(m.astype(BF16), wo_ref[...], preferred_element_type=F32) + bo_ref[...]
        h1 = _layer_norm(DEEPNORM_ALPHA * h + mix, ln1g_ref[...], ln1b_ref[...])
        _to_row_tiles(h1_ref, r0, h1)

        h_hi = h1.astype(BF16)
        h_lo = (h1 - h_hi.astype(F32)).astype(BF16)
        hw = jnp.dot(h_hi, wr_ref[...], preferred_element_type=F32)
        lw = jnp.dot(h_lo, wr_ref[:, :LANES], preferred_element_type=F32)
        logits = (hw[:, :LANES] + hw[:, LANES:] + lw).T[:N_EXPERTS] + br_ref[...]
        vals, idxs = [], []
        cur = logits
        for _ in range(TOP_K):
            mx = jnp.max(cur, axis=0, keepdims=True)
            ik = jnp.min(jnp.where(cur == mx, eio, N_EXPERTS), axis=0, keepdims=True)
            vals.append(mx)
            idxs.append(ik)
            cur = jnp.where(eio == ik, -jnp.inf, cur)
        exps = [jnp.exp(v - vals[0]) for v in vals]
        den = exps[0] + exps[1] + exps[2] + exps[3]
        gate_ref[:, rows] = jnp.concatenate([e / den for e in exps], axis=0)
        idx_ref[:, rows] = jnp.concatenate(idxs, axis=0)

        hot = jnp.zeros((N_EXPERTS, sub), F32)
        for ik in idxs:
            hot = hot + (eio == ik).astype(F32)
        prior = jnp.dot(hot.astype(BF16), before, preferred_element_type=F32) + carry[:, 0:1]
        ranks = [jnp.sum(jnp.where(eio == ik, prior, 0.0), axis=0, keepdims=True) for ik in idxs]
        rank_ref[:, rows] = jnp.concatenate(ranks, axis=0).astype(I32)
        carry = carry + jnp.sum(hot, axis=1, keepdims=True)
    carry_ref[...] = carry
    cnt_ref[...] = carry


def _mix_route(x, ya, yb, carry_in, ln0_g, ln0_b, w_gate, b_gate, w_o, b_o, ln1_g, ln1_b, w_rt, b_rt):
    t = x.shape[0]
    tm = TM_MIX
    row = lambda d: pl.BlockSpec((tm, d), lambda i: (i, 0))
    col = lambda r: pl.BlockSpec((r, tm), lambda i: (0, i))
    return pl.pallas_call(
        _mix_kernel,
        grid=(t // tm,),
        in_specs=[
            row(D_MODEL), row(D_MODEL), row(D_MODEL), _full((N_EXPERTS, LANES)),
            _full((1, D_MODEL)), _full((1, D_MODEL)),
            _full((D_MODEL, 2 * D_MODEL)), _full((1, 2 * D_MODEL)),
            _full((D_MODEL, D_MODEL)), _full((1, D_MODEL)),
            _full((1, D_MODEL)), _full((1, D_MODEL)),
            _full((D_MODEL, 2 * LANES)), _full((N_EXPERTS, 1)),
        ],
        out_specs=[pl.BlockSpec((tm * SUBLANES, LANES), lambda i: (i, 0)), col(TOP_K), col(TOP_K), col(TOP_K),
                   _full((N_EXPERTS, LANES))],
        out_shape=[
            jax.ShapeDtypeStruct((t * SUBLANES, LANES), F32),
            jax.ShapeDtypeStruct((TOP_K, t), I32),
            jax.ShapeDtypeStruct((TOP_K, t), F32),
            jax.ShapeDtypeStruct((TOP_K, t), I32),
            jax.ShapeDtypeStruct((N_EXPERTS, LANES), F32),
        ],
        scratch_shapes=[pltpu.VMEM((N_EXPERTS, LANES), F32)],
        compiler_params=_cparams("arbitrary"),
        name="mix_route",
    )(x, ya, yb, carry_in, ln0_g, ln0_b, w_gate, b_gate, w_o, b_o, ln1_g, ln1_b, w_rt, b_rt)


def _row_copy(src_ref, src_row, dst_ref, dst_row, sem):
    return pltpu.make_async_copy(src_ref.at[src_row], dst_ref.at[dst_row], sem)


def _fetch_slots(slots_hbm, slots_smem, sem, step, slot):
    n = slots_hbm.shape[1]
    return pltpu.make_async_copy(slots_hbm.at[step], slots_smem.at[pl.ds(pl.multiple_of(slot * n, n), n)],
                                 sem.at[slot])


def _pad_fill(h_ref, xs_out, pad_smem, pad_sem, n_pad_runs, wait):
    def go(cp):
        if wait:
            cp.wait()
        else:
            cp.start()

    def run(j, carry):
        start = pad_smem[2 * j]
        length = pad_smem[2 * j + 1]
        for bit in range(BM_MOE.bit_length() - 1):
            size = 1 << bit

            @pl.when((length & size) != 0)
            def _():
                off = start + (length & ~(2 * size - 1))
                go(pltpu.make_async_copy(h_ref.at[pl.ds(0, size)], xs_out.at[pl.ds(off, size)], pad_sem))
        return carry

    lax.fori_loop(0, n_pad_runs, run, 0)

    piece = min(h_ref.shape[0], BM_MOE)

    def dead_block(j, carry):
        for part in range(BM_MOE // piece):
            off = j * BM_MOE + part * piece
            go(pltpu.make_async_copy(h_ref.at[pl.ds(0, piece)], xs_out.at[pl.ds(off, piece)], pad_sem))
        return carry

    lax.fori_loop(pad_smem[2 * n_pad_runs], xs_out.shape[0] // BM_MOE, dead_block, 0)


def _dispatch_kernel(*refs, tiles):
    n_sets = len(tiles)
    h_hbm = refs[:n_sets]
    slots_hbm, pad_hbm, xs_out, hbuf, slots_smem, pad_smem, load_sem, slot_sem, pad_sem, row_sem = refs[n_sets:]
    i = pl.program_id(0)
    n = pl.num_programs(0)
    tm = hbuf.shape[1]
    cur = lax.rem(i, 2)

    def load(g, wait):
        slot = lax.rem(g, RING)
        first = 0
        for h, count in zip(h_hbm, tiles):
            @pl.when(jnp.logical_and(g >= first, g < first + count))
            def _(h=h, first=first):
                cp = pltpu.make_async_copy(h.at[pl.ds(pl.multiple_of((g - first) * tm, tm), tm)], hbuf.at[slot],
                                           load_sem.at[slot])
                if wait:
                    cp.wait()
                else:
                    cp.start()
            first += count

    def wait_rows(parity):
        for _ in range(TOP_K):
            pltpu.make_async_copy(hbuf.at[0], xs_out.at[pl.ds(0, tm)], row_sem.at[parity]).wait()

    @pl.when(i == 0)
    def _():
        load(0, wait=False)
        _fetch_slots(slots_hbm, slots_smem, slot_sem, 0, 0).start()

    @pl.when(jnp.logical_and(i == 0, n > 1))
    def _():
        load(1, wait=False)

    @pl.when(i + 1 < n)
    def _():
        _fetch_slots(slots_hbm, slots_smem, slot_sem, i + 1, 1 - cur).start()

    load(i, wait=True)

    @pl.when(i == 0)
    def _():
        fetch = pltpu.make_async_copy(pad_hbm, pad_smem, pad_sem)
        fetch.start()
        fetch.wait()
        _pad_fill(hbuf.at[0], xs_out, pad_smem, pad_sem, N_EXPERTS, wait=False)
        _pad_fill(hbuf.at[0], xs_out, pad_smem, pad_sem, N_EXPERTS, wait=True)

    _fetch_slots(slots_hbm, slots_smem, slot_sem, i, cur).wait()
    base = cur * (TOP_K * tm)
    h_ref = hbuf.at[lax.rem(i, RING)]

    def issue(r, carry):
        for k in range(TOP_K):
            _row_copy(h_ref, r, xs_out, slots_smem[base + k * tm + r], row_sem.at[cur]).start(priority=k % 2)
        return carry

    lax.fori_loop(0, tm, issue, 0, unroll=ISSUE_UNROLL)

    @pl.when(i > 0)
    def _():
        wait_rows(1 - cur)

    @pl.when(i + 2 < n)
    def _():
        load(i + 2, wait=False)

    @pl.when(i == n - 1)
    def _():
        wait_rows(cur)


def _dispatch(h1s, slots, pad_runs, n_rows):
    tm = TM_ROWS
    h1s = [h.reshape(-1, SUBLANES, LANES) for h in h1s]
    tiles = [h.shape[0] // tm for h in h1s]
    return pl.pallas_call(
        functools.partial(_dispatch_kernel, tiles=tuple(tiles)),
        grid=(sum(tiles),),
        in_specs=[pl.BlockSpec(memory_space=pl.ANY)] * (len(h1s) + 2),
        out_specs=pl.BlockSpec(memory_space=pl.ANY),
        out_shape=jax.ShapeDtypeStruct((n_rows, SUBLANES, LANES), F32),
        scratch_shapes=[
            pltpu.VMEM((RING, tm, SUBLANES, LANES), F32),
            pltpu.SMEM((2 * TOP_K * tm,), I32),
            pltpu.SMEM((pad_runs.shape[0],), I32),
            pltpu.SemaphoreType.DMA((RING,)),
            pltpu.SemaphoreType.DMA((2,)),
            pltpu.SemaphoreType.DMA,
            pltpu.SemaphoreType.DMA((2,)),
        ],
        compiler_params=_cparams("arbitrary"),
        name="dispatch",
    )(*h1s, slots, pad_runs)


def _moe_kernel(blk_e_ref, nb_ref, x_ref, wgu_ref, bgu_ref, wd_ref, bd_ref, o_ref, wgu_b, wd_b):
    i = pl.program_id(0)
    is_live = i < nb_ref[0]

    @pl.when(jnp.logical_not(is_live))
    def _():
        o_ref[...] = jnp.zeros(o_ref.shape, F32)

    @pl.when(jnp.logical_or(i == 0, blk_e_ref[i] != blk_e_ref[jnp.maximum(i - 1, 0)]))
    def _():
        for r in range(0, D_MODEL, CAST_ROWS):
            wgu_b[r:r + CAST_ROWS, :] = wgu_ref[0, r:r + CAST_ROWS, :].astype(BF16)
        for r in range(0, D_FF, CAST_ROWS):
            wd_b[r:r + CAST_ROWS, :] = wd_ref[0, r:r + CAST_ROWS, :].astype(BF16)

    @pl.when(is_live)
    def _():
        x = _from_row_tiles(x_ref, 0, BM_MOE).astype(BF16)
        acc = jnp.zeros((BM_MOE, D_MODEL), F32) + bd_ref[0]
        for c in range(D_FF // FF_CHUNK):
            lo = c * FF_CHUNK
            g = jnp.dot(x, wgu_b[:, lo:lo + FF_CHUNK], preferred_element_type=F32) + bgu_ref[0, :, lo:lo + FF_CHUNK]
            u = (jnp.dot(x, wgu_b[:, D_FF + lo:D_FF + lo + FF_CHUNK], preferred_element_type=F32)
                 + bgu_ref[0, :, D_FF + lo:D_FF + lo + FF_CHUNK])
            g = jnp.minimum(g, SWIGLU_LIMIT)
            u = jnp.clip(u, -SWIGLU_LIMIT, SWIGLU_LIMIT)
            act = (u + 1.0) * (g * _sigmoid(g * SWIGLU_ALPHA))
            acc = acc + jnp.dot(act.astype(BF16), wd_b[lo:lo + FF_CHUNK, :], preferred_element_type=F32)
        _to_row_tiles(o_ref, 0, acc)


def _moe_blocks(xs, blk_e, nb_used, w_gu, b_gu, w_down, b_down):
    p = xs.shape[0] // SUBLANES
    n_blocks = p // BM_MOE
    live = lambda i, nb: jnp.minimum(i, nb[0] - 1)
    grid_spec = pltpu.PrefetchScalarGridSpec(
        num_scalar_prefetch=2,
        grid=(n_blocks,),
        in_specs=[
            pl.BlockSpec((BM_MOE * SUBLANES, LANES), lambda i, be, nb: (live(i, nb), 0)),
            pl.BlockSpec((1, D_MODEL, 2 * D_FF), lambda i, be, nb: (be[i], 0, 0)),
            pl.BlockSpec((1, 1, 2 * D_FF), lambda i, be, nb: (be[i], 0, 0)),
            pl.BlockSpec((1, D_FF, D_MODEL), lambda i, be, nb: (be[i], 0, 0)),
            pl.BlockSpec((1, 1, D_MODEL), lambda i, be, nb: (be[i], 0, 0)),
        ],
        out_specs=pl.BlockSpec((BM_MOE * SUBLANES, LANES), lambda i, be, nb: (i, 0)),
        scratch_shapes=[pltpu.VMEM((D_MODEL, 2 * D_FF), BF16), pltpu.VMEM((D_FF, D_MODEL), BF16)],
    )
    return pl.pallas_call(
        _moe_kernel,
        grid_spec=grid_spec,
        out_shape=jax.ShapeDtypeStruct((p * SUBLANES, LANES), F32),
        compiler_params=pltpu.CompilerParams(dimension_semantics=("arbitrary",), vmem_limit_bytes=VMEM_LIMIT_MOE),
        name="moe_experts",
    )(blk_e, nb_used, xs, w_gu, b_gu, w_down, b_down)


def _final_kernel(h_ref, p_ref, gate_ref, slots_hbm, y_hbm, wpg_ref, bpg_ref, wpp_ref, g_ref, b_ref, o_ref,
                  slots_smem, rows_ref, slot_sem, row_sem):
    i = pl.program_id(0)
    n = pl.num_programs(0)
    tm = h_ref.shape[0] // SUBLANES

    def gather(r, buf, base):
        for k in range(TOP_K):
            slot = slots_smem[base + k * tm + r]
            src = y_hbm.at[pl.ds(pl.multiple_of(slot * SUBLANES, SUBLANES), SUBLANES)]
            dst = rows_ref.at[buf, k, pl.ds(pl.multiple_of(r * SUBLANES, SUBLANES), SUBLANES)]
            pltpu.make_async_copy(src, dst, row_sem.at[buf]).start(priority=k % 2)

    def wait_rows(buf):
        for k in range(TOP_K):
            pltpu.make_async_copy(y_hbm.at[pl.ds(0, tm * SUBLANES)], rows_ref.at[buf, k], row_sem.at[buf]).wait()

    @pl.when(i == 0)
    def _():
        first = _fetch_slots(slots_hbm, slots_smem, slot_sem, 0, 0)
        first.start()
        first.wait()

        def issue(r, carry):
            gather(r, 0, 0)
            return carry

        lax.fori_loop(0, tm, issue, 0, unroll=ISSUE_UNROLL)
        _fetch_slots(slots_hbm, slots_smem, slot_sem, 1, 1).start()

    def step(cur):
        nxt = 1 - cur
        _fetch_slots(slots_hbm, slots_smem, slot_sem, i + 1, nxt).wait()

        @pl.when(i + 2 <= n)
        def _():
            _fetch_slots(slots_hbm, slots_smem, slot_sem, i + 2, cur).start()

        for r in range(tm):
            gather(r, nxt, nxt * (TOP_K * tm))

        h = _from_row_tiles(h_ref, 0, tm)
        ple = (_sigmoid(jnp.dot(h.astype(BF16), wpg_ref[...], preferred_element_type=F32) + bpg_ref[...])
               * jnp.dot(p_ref[...].astype(BF16), wpp_ref[...], preferred_element_type=F32))

        wait_rows(cur)
        gates = gate_ref[...]
        ffn = gates[:, 0:1] * _from_row_tiles(rows_ref.at[cur, 0], 0, tm)
        for k in range(1, TOP_K):
            ffn = ffn + gates[:, k:k + 1] * _from_row_tiles(rows_ref.at[cur, k], 0, tm)
        o_ref[...] = _layer_norm(DEEPNORM_ALPHA * h + (ffn + ple), g_ref[...], b_ref[...])

        @pl.when(i == n - 1)
        def _():
            wait_rows(nxt)

    for parity in range(2):
        pl.when(lax.rem(i, 2) == parity)(functools.partial(step, parity))


def _combine_final(h1, p, gates, slots, y, w_pg, b_pg, w_pp, ln_g, ln_b):
    t = h1.shape[0] // SUBLANES
    tm = TM_COMBINE
    slots = jnp.concatenate([slots, slots[-1:]], axis=0)
    return pl.pallas_call(
        _final_kernel,
        grid=(t // tm,),
        in_specs=[
            pl.BlockSpec((tm * SUBLANES, LANES), lambda i: (i, 0)),
            pl.BlockSpec((tm, D_PLE), lambda i: (i, 0)),
            pl.BlockSpec((tm, TOP_K), lambda i: (i, 0)),
            pl.BlockSpec(memory_space=pl.ANY),
            pl.BlockSpec(memory_space=pl.ANY),
            _full((D_MODEL, D_MODEL)), _full((1, D_MODEL)), _full((D_PLE, D_MODEL)),
            _full((1, D_MODEL)), _full((1, D_MODEL)),
        ],
        out_specs=pl.BlockSpec((tm, D_MODEL), lambda i: (i, 0)),
        out_shape=jax.ShapeDtypeStruct((t, D_MODEL), F32),
        scratch_shapes=[
            pltpu.SMEM((2 * TOP_K * tm,), I32),
            pltpu.VMEM((2, TOP_K, tm * SUBLANES, LANES), F32),
            pltpu.SemaphoreType.DMA((2,)),
            pltpu.SemaphoreType.DMA((2,)),
        ],
        compiler_params=_cparams("arbitrary"),
        name="combine_final",
    )(h1, p, gates, slots, y, w_pg, b_pg, w_pp, ln_g, ln_b)


def _tile_slots(dest, tm):
    t = dest.shape[1]
    return dest.reshape(TOP_K, t // tm, tm).transpose(1, 0, 2).reshape(t // tm, TOP_K * tm)


def kernel(x_prompt, x_sample, p_prompt, p_sample, ln0_g, ln0_b, w_in, b_in, conv_w, conv_b, lnc_g, lnc_b, w_conv_out, w_fnet_out, w_o, b_o, ln1_g, ln1_b, w_router, b_router, w_gu, b_gu, w_down, b_down, w_pg, b_pg, w_pp, ln2_g, ln2_b):
    row = lambda v: v.reshape(1, -1).astype(F32)
    n_branch = 2 * D_CONV + D_FNET
    w_in_b = w_in[0].astype(BF16)
    w_branch, w_gate = w_in_b[:, :n_branch], w_in_b[:, n_branch:]
    b_branch, b_gate = row(b_in[0, :n_branch]), row(b_in[0, n_branch:])
    w_conv_out_b = w_conv_out[0].astype(BF16)
    w_fnet_out_b = w_fnet_out[0].astype(BF16)
    w_o_b = w_o[0].astype(BF16)
    w_rt_f = jnp.pad(w_router[0].astype(F32), ((0, 0), (0, LANES - N_EXPERTS)))
    w_rt_hi = w_rt_f.astype(BF16)
    w_rt = jnp.concatenate([w_rt_hi, (w_rt_f - w_rt_hi.astype(F32)).astype(BF16)], axis=1)
    b_rt = b_router[0].reshape(N_EXPERTS, 1).astype(F32)
    b_gu_r = b_gu[0].reshape(N_EXPERTS, 1, 2 * D_FF).astype(F32)
    b_down_r = b_down[0].reshape(N_EXPERTS, 1, D_MODEL).astype(F32)
    w_pg_b = w_pg[0].astype(BF16)
    w_pp_b = w_pp[0].astype(BF16)

    carry = jnp.zeros((N_EXPERTS, LANES), F32)
    routed = []
    for x, p in ((x_prompt, p_prompt[0]), (x_sample, p_sample[0])):
        bsz, s, _ = x.shape
        xt = x.reshape(bsz * s, D_MODEL)
        u, f_in = _inproj(xt, row(ln0_g), row(ln0_b), w_branch, b_branch)
        ya = _conv_branch(u.reshape(bsz, s, D_CONV), conv_w[0].astype(F32), row(conv_b[0]), row(lnc_g[0]),
                          row(lnc_b[0]), w_conv_out_b).reshape(bsz * s, D_MODEL)
        yb = _fourier_branch(f_in.reshape(bsz, s, D_FNET), w_fnet_out_b)
        h1, idx, gates, rank, carry = _mix_route(
            xt, ya, yb, carry, row(ln0_g), row(ln0_b), w_gate, b_gate, w_o_b, row(b_o[0]),
            row(ln1_g[0]), row(ln1_b[0]), w_rt, b_rt)
        routed.append((h1, idx, gates, rank, p.reshape(bsz * s, D_PLE), (bsz, s)))

    n_assign = sum(r[1].shape[1] for r in routed) * TOP_K
    n_blocks = (n_assign + N_EXPERTS * (BM_MOE - 1)) // BM_MOE
    counts = carry[:, 0].astype(I32)
    padded = ((counts + BM_MOE - 1) // BM_MOE) * BM_MOE
    ends = jnp.cumsum(padded)
    start = ends - padded
    nb_used = (ends[-1] // BM_MOE).reshape(1).astype(I32)
    blk = jnp.minimum(jnp.arange(n_blocks, dtype=I32), nb_used[0] - 1)
    blk_e = jnp.minimum(jnp.sum((ends[None, :] <= (blk * BM_MOE)[:, None]).astype(I32), axis=1), N_EXPERTS - 1)
    pad_runs = jnp.concatenate([jnp.stack([start + counts, padded - counts], axis=1).reshape(-1), nb_used,
                                jnp.zeros((LANES - 2 * N_EXPERTS - 1,), I32)]).astype(I32)

    slot_tables = []
    dispatch_slots = []
    for _, idx, _, rank, _, _ in routed:
        first = jnp.sum(jnp.where(idx[None] == jnp.arange(N_EXPERTS, dtype=I32)[:, None, None],
                                  start[:, None, None], 0), axis=0)
        slot_tables.append(_tile_slots(first + rank, TM_COMBINE))
        dispatch_slots.append(_tile_slots(first + rank, TM_ROWS))
    xs = _dispatch([r[0] for r in routed], jnp.concatenate(dispatch_slots, axis=0), pad_runs, n_blocks * BM_MOE)

    y = _moe_blocks(xs.reshape(-1, LANES), blk_e, nb_used, w_gu.reshape(N_EXPERTS, D_MODEL, 2 * D_FF), b_gu_r,
                    w_down.reshape(N_EXPERTS, D_FF, D_MODEL), b_down_r)

    outs = []
    for (h1, _, gates, _, p, (bsz, s)), slots in zip(routed, slot_tables):
        o = _combine_final(h1, p, gates.T, slots, y, w_pg_b, row(b_pg[0]), w_pp_b, row(ln2_g[0]), row(ln2_b[0]))
        outs.append(o.reshape(bsz, s, D_MODEL))
    return tuple(outs)
```
